```python
import jax, jax.numpy as jnp
from jax import lax
import numpy as np

D_MODEL = 2048
BATCH = 4
SEQ = 2048
DEPTH = 4
DEC_BATCH = 8
DEC_SEQ = 1
PAST_LEN = 16384
PAGE_SIZE = 128

HEAD_DIM = 128
HEADS_PER_GROUP = D_MODEL // HEAD_DIM
DILATION_PAIRS = ((128, 1), (512, 4), (2048, 16))
N_GROUPS = len(DILATION_PAIRS)
N_SUBHEADS = N_GROUPS * HEADS_PER_GROUP
ATT_WIDTH = HEADS_PER_GROUP * HEAD_DIM
QKV_WIDTH = N_GROUPS * ATT_WIDTH
ATT_IN_COLS = 3 * QKV_WIDTH + ATT_WIDTH
BAND = 128
ATT_SCALE = HEAD_DIM ** -0.5
POOL_WINDOWS = (2, 4, 8, 16)
POOL_WIDTH = 2 * D_MODEL
POOL_GROUP = POOL_WIDTH // len(POOL_WINDOWS)
POOL_BUF = max(POOL_WINDOWS) - 1
N_BUCKETS = 32
T5_MAX_DIST = 2048
N_POOL_LAYERS = (DEPTH + 1) // 2
N_ATT_LAYERS = DEPTH // 2
RMS_EPS = 1e-6
NEG_INF = -1e30

kernel_name = "hybrid_pool_dilated_attn_decoder_step"


def rmsnorm(x, g):
    xf = x.astype(jnp.float32)
    y = xf * lax.rsqrt(jnp.mean(xf * xf, axis=-1, keepdims=True) + RMS_EPS)
    return (y * g.astype(jnp.float32)).astype(x.dtype)


def adaln(c, w, b):
    mod = jax.nn.silu(c) @ w + b
    shift, scale, gate = jnp.split(mod, 3, axis=-1)
    return shift[:, None, :], scale[:, None, :], gate[:, None, :]


def t5_bucket(dist):
    dist = np.asarray(dist, dtype=np.int64)
    max_exact = N_BUCKETS // 2
    ratio = np.log(np.maximum(dist, 1) / max_exact) / np.log(T5_MAX_DIST / max_exact)
    large = np.minimum(max_exact + (ratio * (N_BUCKETS - max_exact)).astype(np.int64), N_BUCKETS - 1)
    return np.where(dist < max_exact, dist, large).astype(np.int32)


def pool_mix(u_ext, n_prev, start_pos, w_grp, scale):
    B, T, E = u_ext.shape
    S = T - n_prev
    uf = u_ext.astype(jnp.float32)
    cs = jnp.concatenate([jnp.zeros((B, 1, E), jnp.float32), jnp.cumsum(uf, axis=1)], axis=1)
    j = np.arange(n_prev, T)
    pos = start_pos + np.arange(S)
    res = []
    for g, w in enumerate(POOL_WINDOWS):
        sl = slice(g * POOL_GROUP, (g + 1) * POOL_GROUP)
        lo = np.maximum(j + 1 - w, 0)
        cnt = np.minimum(w, pos + 1).astype(np.float32)
        csg = cs[:, :, sl]
        res.append((csg[:, j + 1] - csg[:, lo]) / cnt[None, :, None] - uf[:, n_prev:, sl])
    r = jnp.stack(res, axis=2)
    y = jnp.einsum('bsge,gef->bsgf', r, w_grp.astype(jnp.float32)).reshape(B, S, E)
    return (y * scale.astype(jnp.float32)).astype(u_ext.dtype)


def pool_layer(h, buf, w_in, w_grp, scale, w_out):
    proj = h @ w_in
    u, z = proj[..., :POOL_WIDTH], proj[..., POOL_WIDTH:]
    if buf is None:
        u_ext, n_prev, start = u, 0, 0
    else:
        u_ext = jnp.concatenate([buf.astype(u.dtype), u], axis=1)
        n_prev, start = buf.shape[1], PAST_LEN
    r = pool_mix(u_ext, n_prev, start, w_grp, scale)
    y = (r * jax.nn.silu(z)) @ w_out
    return y, u_ext[:, -POOL_BUF:]


def dilated_prompt(q, k, v, dil, tab):
    B, S, H, dh = q.shape
    n = S // dil
    nb = -(-n // BAND)
    pad = nb * BAND - n

    def sub(x):
        x = x.reshape(B, n, dil, H, dh).transpose(0, 2, 1, 3, 4)
        x = jnp.pad(x, ((0, 0), (0, 0), (0, pad), (0, 0), (0, 0)))
        return x.reshape(B, dil, nb, BAND, H, dh)

    def band(x):
        prev = jnp.pad(x, ((0, 0), (0, 0), (1, 0), (0, 0), (0, 0), (0, 0)))[:, :, :-1]
        return jnp.concatenate([prev, x], axis=3)

    qs = sub(q)
    kb, vb = band(sub(k)), band(sub(v))
    rel = np.arange(BAND)[:, None] + BAND - np.arange(2 * BAND)[None, :]
    inband = (rel >= 0) & (rel <= BAND)
    valid = inband[None] & ((np.arange(nb)[:, None, None] > 0) | (np.arange(2 * BAND)[None, None, :] >= BAND))
    bias = tab[t5_bucket(np.clip(rel, 0, BAND) * dil)].astype(jnp.float32).transpose(2, 0, 1)
    s = jnp.einsum('brcqhe,brckhe->brchqk', qs, kb).astype(jnp.float32) * ATT_SCALE + bias
    s = jnp.where(valid[:, None], s, NEG_INF)
    m = jnp.max(s, axis=-1, keepdims=True)
    p = jnp.exp(s - m)
    l = jnp.sum(p, axis=-1, keepdims=True)
    o = jnp.einsum('brchqk,brckhe->brcqhe', p / l, vb.astype(jnp.float32))
    lse = jnp.swapaxes((m + jnp.log(l))[..., 0], 3, 4)
    o = o.reshape(B, dil, nb * BAND, H, dh)[:, :, :n].transpose(0, 2, 1, 3, 4).reshape(B, S, H, dh)
    lse = lse.reshape(B, dil, nb * BAND, H)[:, :, :n].transpose(0, 2, 1, 3).reshape(B, S, H)
    return o, lse


def dilated_sample(q, k_ext, v_ext, n_buf, dil, tab):
    B, S, H, dh = q.shape
    offs = np.arange(BAND + 1)
    idx = n_buf + np.arange(S)[:, None] - offs[None, :] * dil
    valid = idx >= 0
    idx = np.maximum(idx, 0)
    kg, vg = k_ext[:, idx], v_ext[:, idx]
    bias = tab[t5_bucket(offs * dil)].astype(jnp.float32).T
    s = jnp.einsum('bshe,bskhe->bhsk', q, kg).astype(jnp.float32) * ATT_SCALE + bias[None, :, None, :]
    s = jnp.where(valid[None, None], s, NEG_INF)
    m = jnp.max(s, axis=-1, keepdims=True)
    p = jnp.exp(s - m)
    l = jnp.sum(p, axis=-1, keepdims=True)
    o = jnp.einsum('bhsk,bskhe->bshe', p / l, vg.astype(jnp.float32))
    lse = jnp.swapaxes((m + jnp.log(l))[..., 0], 1, 2)
    return o, lse


def att_layer(h, kv_bufs, w_in, w_out, t5_bias):
    B, T, _ = h.shape
    proj = h @ w_in
    shp = (B, T, N_GROUPS, HEADS_PER_GROUP, HEAD_DIM)
    q = proj[..., :QKV_WIDTH].reshape(shp)
    k = proj[..., QKV_WIDTH:2 * QKV_WIDTH].reshape(shp)
    v = proj[..., 2 * QKV_WIDTH:3 * QKV_WIDTH].reshape(shp)
    z = proj[..., 3 * QKV_WIDTH:]
    outs, lses, new_kv = [], [], []
    for g, (win, dil) in enumerate(DILATION_PAIRS):
        tab = t5_bias[:, g * HEADS_PER_GROUP:(g + 1) * HEADS_PER_GROUP]
        qg, kg, vg = q[:, :, g], k[:, :, g], v[:, :, g]
        if kv_bufs is None:
            o, lse = dilated_prompt(qg, kg, vg, dil, tab)
            keep = min(win, T)
            new_kv.append(jnp.stack([kg[:, -keep:], vg[:, -keep:]], axis=2))
        else:
            buf = kv_bufs[g]
            k_ext = jnp.concatenate([buf[:, :, 0].astype(kg.dtype), kg], axis=1)
            v_ext = jnp.concatenate([buf[:, :, 1].astype(vg.dtype), vg], axis=1)
            o, lse = dilated_sample(qg, k_ext, v_ext, buf.shape[1], dil, tab)
            new_kv.append(jnp.stack([kg, vg], axis=2))
        outs.append(o)
        lses.append(lse)
    wgt = jax.nn.softmax(jnp.stack(lses, axis=0), axis=0)
    o = jnp.sum(wgt[..., None] * jnp.stack(outs, axis=0), axis=0)
    o = o.reshape(B, T, ATT_WIDTH).astype(h.dtype)
    y = (o * jax.nn.silu(z)) @ w_out
    return y, new_kv


def setup_inputs(seed: int = 0) -> dict:
    key = jax.random.key(seed)
    ks = jax.random.split(key, 20)
    D = D_MODEL

    def nrm(k, shape, s):
        return jax.random.normal(k, shape, jnp.float32) * s

    kv_shape = lambda w: (N_ATT_LAYERS, DEC_BATCH, min(w, PAST_LEN), 2, HEADS_PER_GROUP, HEAD_DIM)
    return {
        "x_prompt": nrm(ks[0], (BATCH, SEQ, D), 1.0),
        "x_sample": nrm(ks[1], (DEC_BATCH, DEC_SEQ, D), 1.0),
        "c_prompt": nrm(ks[2], (BATCH, D), 1.0),
        "c_sample": nrm(ks[3], (DEC_BATCH, D), 1.0),
        "cache_kv0": nrm(ks[4], kv_shape(DILATION_PAIRS[0][0]), 1.0),
        "cache_kv1": nrm(ks[5], kv_shape(DILATION_PAIRS[1][0]), 1.0),
        "cache_kv2": nrm(ks[6], kv_shape(DILATION_PAIRS[2][0]), 1.0),
        "state_pool": nrm(ks[7], (N_POOL_LAYERS, DEC_BATCH, POOL_BUF, POOL_WIDTH), 1.0),
        "norm_pre": 1.0 + nrm(ks[8], (DEPTH, D), 0.05),
        "norm_post": 1.0 + nrm(ks[9], (DEPTH, D), 0.05),
        "ada_w": nrm(ks[10], (DEPTH, D, 3 * D), 0.5 * D ** -0.5),
        "ada_b": nrm(ks[11], (DEPTH, 3 * D), 0.01),
        "t5_bias": nrm(ks[12], (N_BUCKETS, N_SUBHEADS), 0.5),
        "pool_w_in": nrm(ks[13], (N_POOL_LAYERS, D, 2 * POOL_WIDTH), D ** -0.5),
        "pool_w_grp": nrm(ks[14], (N_POOL_LAYERS, len(POOL_WINDOWS), POOL_GROUP, POOL_GROUP), POOL_GROUP ** -0.5),
        "pool_scale": 1.0 + nrm(ks[15], (N_POOL_LAYERS, POOL_WIDTH), 0.1),
        "pool_w_out": nrm(ks[16], (N_POOL_LAYERS, POOL_WIDTH, D), POOL_WIDTH ** -0.5),
        "att_w_in": nrm(ks[17], (N_ATT_LAYERS, D, ATT_IN_COLS), D ** -0.5),
        "att_w_out": nrm(ks[18], (N_ATT_LAYERS, ATT_WIDTH, D), ATT_WIDTH ** -0.5),
    }


def reference(x_prompt, x_sample, c_prompt, c_sample, cache_kv0, cache_kv1, cache_kv2, state_pool,
              norm_pre, norm_post, ada_w, ada_b, t5_bias, pool_w_in, pool_w_grp, pool_scale,
              pool_w_out, att_w_in, att_w_out):
    caches = (cache_kv0, cache_kv1, cache_kv2)
    xp, xs = x_prompt, x_sample
    kv_p = [[] for _ in range(N_GROUPS)]
    kv_s = [[] for _ in range(N_GROUPS)]
    pool_p, pool_s = [], []
    for i in range(DEPTH):
        li = i // 2
        sh_p, sc_p, gt_p = adaln(c_prompt, ada_w[i], ada_b[i])
        sh_s, sc_s, gt_s = adaln(c_sample, ada_w[i], ada_b[i])
        hp = rmsnorm(xp, norm_pre[i]) * (1.0 + sc_p) + sh_p
        hs = rmsnorm(xs, norm_pre[i]) * (1.0 + sc_s) + sh_s
        if i % 2 == 0:
            yp, st_p = pool_layer(hp, None, pool_w_in[li], pool_w_grp[li], pool_scale[li], pool_w_out[li])
            ys, st_s = pool_layer(hs, state_pool[li], pool_w_in[li], pool_w_grp[li], pool_scale[li], pool_w_out[li])
            pool_p.append(st_p)
            pool_s.append(st_s)
        else:
            yp, st_p = att_layer(hp, None, att_w_in[li], att_w_out[li], t5_bias)
            ys, st_s = att_layer(hs, [c[li] for c in caches], att_w_in[li], att_w_out[li], t5_bias)
            for g in range(N_GROUPS):
                kv_p[g].append(st_p[g])
                kv_s[g].append(st_s[g])
        xp = xp + gt_p * rmsnorm(yp, norm_post[i])
        xs = xs + gt_s * rmsnorm(ys, norm_post[i])
    return (xp, xs,
            jnp.stack(kv_p[0]), jnp.stack(kv_p[1]), jnp.stack(kv_p[2]), jnp.stack(pool_p),
            jnp.stack(kv_s[0]), jnp.stack(kv_s[1]), jnp.stack(kv_s[2]), jnp.stack(pool_s))
```

```python
import functools

import numpy as np
import jax
import jax.numpy as jnp
from jax import lax
from jax.experimental import pallas as pl
from jax.experimental.pallas import tpu as pltpu

D_MODEL = 2048
BATCH = 4
SEQ = 2048
DEPTH = 4
DEC_BATCH = 8
HEAD_DIM = 128
N_HEADS = 16
DILATIONS = (1, 4, 16)
WINDOWS = (128, 512, 2048)
N_GROUPS = 3
QKV_WIDTH = N_GROUPS * D_MODEL
ATT_IN_COLS = 3 * QKV_WIDTH + D_MODEL
BAND = 128
ATT_SCALE = HEAD_DIM ** -0.5
POOL_WINDOWS = (2, 4, 8, 16)
POOL_WIDTH = 2 * D_MODEL
POOL_GROUP = POOL_WIDTH // 4
POOL_BUF = 15
N_BUCKETS = 32
T5_MAX_DIST = 2048
RMS_EPS = 1e-6
NEG_INF = -1e30

M_PROMPT = BATCH * SEQ
M_ALL = M_PROMPT + 16
SAMPLE_ROWS = 16
C_ROWS = 32
C_SAMPLE_ROW0 = 16

TM_MATMUL = 912
TE_NORM = 256
TP_POOL = 512
POOL_HALO = 24
VMEM_LIMIT = 56 * 1024 * 1024

F32 = jnp.float32
BF16 = jnp.bfloat16


def _silu(x):
    return x * (1.0 / (1.0 + jnp.exp(-x)))


def _cast_rows_to_bf16(src_ref, dst_ref, rows, chunk=256):
    def body(i, c):
        r = pl.multiple_of(i * chunk, chunk)
        dst_ref[pl.ds(r, chunk), :] = src_ref[pl.ds(r, chunk), :].astype(BF16)
        return c
    lax.fori_loop(0, rows // chunk, body, 0)


def _ada_kernel(c_ref, w_ref, b_ref, o_ref):
    a = _silu(c_ref[...]).astype(BF16)
    kc = 512
    acc = jnp.zeros(o_ref.shape, F32)
    for k0 in range(0, D_MODEL, kc):
        acc = acc + jnp.dot(a[:, k0:k0 + kc], w_ref[k0:k0 + kc, :].astype(BF16),
                            preferred_element_type=F32)
    o_ref[...] = acc + b_ref[...]


def _ada_all(c_all, ada_w, ada_b):
    tn = 1024
    n = 3 * D_MODEL
    return pl.pallas_call(
        _ada_kernel,
        grid=(DEPTH, n // tn),
        in_specs=[pl.BlockSpec((C_ROWS, D_MODEL), lambda l, j: (0, 0)),
                  pl.BlockSpec((None, D_MODEL, tn), lambda l, j: (l, 0, j)),
                  pl.BlockSpec((None, 1, tn), lambda l, j: (l, 0, j))],
        out_specs=pl.BlockSpec((None, C_ROWS, tn), lambda l, j: (l, 0, j)),
        out_shape=jax.ShapeDtypeStruct((DEPTH, C_ROWS, n), F32),
        compiler_params=pltpu.CompilerParams(
            dimension_semantics=("arbitrary", "arbitrary"), vmem_limit_bytes=VMEM_LIMIT),
        name="ada_mod",
    )(c_all, ada_w, ada_b.reshape(DEPTH, 1, n))


def _t5_bucket(dist):
    dist = np.asarray(dist, dtype=np.int64)
    max_exact = N_BUCKETS // 2
    ratio = np.log(np.maximum(dist, 1) / max_exact) / np.log(T5_MAX_DIST / max_exact)
    large = np.minimum(max_exact + (ratio * (N_BUCKETS - max_exact)).astype(np.int64), N_BUCKETS - 1)
    return np.where(dist < max_exact, dist, large).astype(np.int32)


def _bucket_index_table():
    rel = np.arange(BAND)[:, None] + BAND - np.arange(2 * BAND)[None, :]
    inband = (rel >= 0) & (rel <= BAND)
    out = []
    for dil in DILATIONS:
        bucket = _t5_bucket(np.clip(rel, 0, BAND) * dil)
        out.append(np.where(inband, bucket, -1))
    return np.stack(out).astype(np.int32)


def _bias_kernel(tab_ref, idx_ref, o_ref):
    col = pl.program_id(0)
    idx = idx_ref[...]
    acc = jnp.full(idx.shape, NEG_INF, F32)
    for b in range(N_BUCKETS):
        acc = jnp.where(idx == b, tab_ref[b, col], acc)
    o_ref[...] = acc


def _bias_matrices(t5_bias):
    idx = jnp.asarray(_bucket_index_table())
    n_sub = N_GROUPS * N_HEADS
    return pl.pallas_call(
        _bias_kernel,
        grid=(n_sub,),
        in_specs=[pl.BlockSpec(memory_space=pltpu.SMEM),
                  pl.BlockSpec((None, BAND, 2 * BAND), lambda c: (c // N_HEADS, 0, 0))],
        out_specs=pl.BlockSpec((None, BAND, 2 * BAND), lambda c: (c, 0, 0)),
        out_shape=jax.ShapeDtypeStruct((n_sub, BAND, 2 * BAND), F32),
        name="t5_bias_mats",
    )(t5_bias, idx)


def _norm_kernel(*refs, has_post, has_pre, n_prompt_tiles):
    refs = list(refs)
    x_ref = refs.pop(0)
    if has_post:
        y_ref, gpost_ref, gate_p_ref, gate_s_ref = refs[:4]
        refs = refs[4:]
    if has_pre:
        gpre_ref, shift_p_ref, scale_p_ref, shift_s_ref, scale_s_ref = refs[:5]
        refs = refs[5:]
    if has_post:
        xo_ref = refs.pop(0)
    if has_pre:
        h_ref = refs.pop(0)

    def rms(v, g):
        return v * lax.rsqrt(jnp.mean(v * v, axis=-1, keepdims=True) + RMS_EPS) * g

    def body(rows, gate, shift, scale):
        x = x_ref[rows, :]
        if has_post:
            x = x + gate * rms(y_ref[rows, :], gpost_ref[...])
            xo_ref[rows, :] = x
        if has_pre:
            h = rms(x, gpre_ref[...]) * (1.0 + scale) + shift
            h_ref[rows, :] = h.astype(BF16)

    t = pl.program_id(0)

    @pl.when(t < n_prompt_tiles)
    def _():
        body(slice(None),
             gate_p_ref[...] if has_post else None,
             shift_p_ref[...] if has_pre else None,
             scale_p_ref[...] if has_pre else None)

    @pl.when(t == n_prompt_tiles)
    def _():
        body(slice(0, SAMPLE_ROWS),
             gate_s_ref[...] if has_post else None,
             shift_s_ref[...] if has_pre else None,
             scale_s_ref[...] if has_pre else None)


def _norm_step(x, y, mod, mod4, norm_post, norm_pre, post_layer, pre_layer):
    has_post = post_layer is not None
    has_pre = pre_layer is not None
    te = TE_NORM
    npt = M_PROMPT // te
    tiles_per_batch = SEQ // te
    row_spec = pl.BlockSpec((te, D_MODEL), lambda t: (t, 0))

    def mod_p_spec(layer, part):
        return pl.BlockSpec((None, None, 1, D_MODEL),
                            lambda t: (layer, jnp.minimum(t // tiles_per_batch, BATCH - 1), 0, part))

    def mod_s_spec(layer, part):
        return pl.BlockSpec((None, SAMPLE_ROWS, D_MODEL),
                            lambda t: (layer, C_SAMPLE_ROW0 // SAMPLE_ROWS, part))

    def gain_spec(layer):
        return pl.BlockSpec((None, 1, D_MODEL), lambda t: (layer, 0, 0))

    args, in_specs, out_shapes, out_specs = [x], [row_spec], [], []
    if has_post:
        args += [y, norm_post, mod4, mod]
        in_specs += [row_spec, gain_spec(post_layer), mod_p_spec(post_layer, 2), mod_s_spec(post_layer, 2)]
        out_shapes.append(jax.ShapeDtypeStruct((M_ALL, D_MODEL), F32))
        out_specs.append(row_spec)
    if has_pre:
        args += [norm_pre, mod4, mod4, mod, mod]
        in_specs += [gain_spec(pre_layer), mod_p_spec(pre_layer, 0), mod_p_spec(pre_layer, 1),
                     mod_s_spec(pre_layer, 0), mod_s_spec(pre_layer, 1)]
        out_shapes.append(jax.ShapeDtypeStruct((M_ALL, D_MODEL), BF16))
        out_specs.append(row_spec)
    outs = pl.pallas_call(
        functools.partial(_norm_kernel, has_post=has_post, has_pre=has_pre, n_prompt_tiles=npt),
        grid=(npt + 1,),
        in_specs=in_specs,
        out_specs=out_specs,
        out_shape=out_shapes,
        compiler_params=pltpu.CompilerParams(
            dimension_semantics=("arbitrary",), vmem_limit_bytes=VMEM_LIMIT),
        name="norm_step",
    )(*args)
    x_new = outs[0] if has_post else x
    h = outs[-1] if has_pre else None
    return x_new, h


def _mm_kernel(a_ref, w_ref, o_ref, wb_ref, *, k_rows):
    @pl.when(pl.program_id(1) == 0)
    def _():
        _cast_rows_to_bf16(w_ref, wb_ref, k_rows)

    o_ref[...] = jnp.dot(a_ref[...], wb_ref[...], preferred_element_type=F32).astype(o_ref.dtype)


def _matmul(a, w, layer, n_out, tn, out_dtype, name):
    m, k = a.shape
    tm = TM_MATMUL
    return pl.pallas_call(
        functools.partial(_mm_kernel, k_rows=k),
        grid=(n_out // tn, m // tm),
        in_specs=[pl.BlockSpec((tm, k), lambda j, i: (i, 0)),
                  pl.BlockSpec((None, k, tn), lambda j, i: (layer, 0, j))],
        out_specs=pl.BlockSpec((tm, tn), lambda j, i: (i, j)),
        out_shape=jax.ShapeDtypeStruct((m, n_out), out_dtype),
        scratch_shapes=[pltpu.VMEM((k, tn), BF16)],
        compiler_params=pltpu.CompilerParams(
            dimension_semantics=("arbitrary", "arbitrary"), vmem_limit_bytes=VMEM_LIMIT),
        name=name,
    )(a, w)


def _pool_kernel(u_ref, halo_ref, z_ref, st_ref, w_ref, sc_ref,
                 a_ref, pp_ref, ps_ref, wb_ref, buf_a, buf_b, *, n_prompt_tiles, tiles_per_batch):
    g = pl.program_id(0)
    t = pl.program_id(1)
    tp = TP_POOL
    h0 = POOL_HALO

    @pl.when(t == 0)
    def _():
        _cast_rows_to_bf16(w_ref, wb_ref, POOL_GROUP)

    def finish(r, z, rows):
        y = jnp.dot(r.astype(BF16), wb_ref[...], preferred_element_type=F32) * sc_ref[...]
        a_ref[rows, :] = (y * _silu(z)).astype(BF16)

    def prompt_tile(n_steps):
        w = 2 ** n_steps
        first = (t % tiles_per_batch) == 0
        buf_a[0:8, :] = jnp.zeros((8, POOL_GROUP), F32)
        buf_b[0:8, :] = jnp.zeros((8, POOL_GROUP), F32)
        buf_a[8:h0, :] = jnp.where(first, 0.0, halo_ref[...])
        buf_a[h0:h0 + tp, :] = u_ref[...]
        src, dst = buf_a, buf_b
        n = tp + h0 - 8
        for s in range(n_steps):
            sh = 2 ** s
            dst[8:8 + n, :] = src[8:8 + n, :] + src[8 - sh:8 - sh + n, :]
            src, dst = dst, src
        pos = (t % tiles_per_batch) * tp + lax.broadcasted_iota(jnp.int32, (tp, 1), 0)
        inv_cnt = 1.0 / jnp.minimum(pos + 1, w).astype(F32)
        u = u_ref[...]
        r = src[h0:h0 + tp, :] * inv_cnt - u
        finish(r, z_ref[...], slice(None))

        @pl.when((t % tiles_per_batch) == tiles_per_batch - 1)
        def _():
            pp_ref[...] = u_ref[tp - POOL_BUF:tp, :]

    def sample_tile(n_steps):
        w = 2 ** n_steps
        u_new = u_ref[0:DEC_BATCH, :]
        acc = u_new
        for k in range(1, w):
            acc = acc + st_ref[POOL_BUF - k]
        r = acc / float(w) - u_new
        buf_a[0:DEC_BATCH, :] = r
        buf_a[DEC_BATCH:SAMPLE_ROWS, :] = jnp.zeros((SAMPLE_ROWS - DEC_BATCH, POOL_GROUP), F32)
        finish(buf_a[0:SAMPLE_ROWS, :], z_ref[0:SAMPLE_ROWS, :], slice(0, SAMPLE_ROWS))
        for k in range(POOL_BUF - 1):
            ps_ref[k] = st_ref[k + 1]
        ps_ref[POOL_BUF - 1] = u_new

    for gi in range(len(POOL_WINDOWS)):
        @pl.when((g == gi) & (t < n_prompt_tiles))
        def _(gi=gi):
            prompt_tile(gi + 1)

        @pl.when((g == gi) & (t == n_prompt_tiles))
        def _(gi=gi):
            sample_tile(gi + 1)


def _pool_mix(proj, state_t, w_grp, scale, layer):
    tp = TP_POOL
    npt = M_PROMPT // tp
    tpb = SEQ // tp
    ng = len(POOL_WINDOWS)
    halo_blocks = tp // 16
    outs = pl.pallas_call(
        functools.partial(_pool_kernel, n_prompt_tiles=npt, tiles_per_batch=tpb),
        grid=(ng, npt + 1),
        in_specs=[
            pl.BlockSpec((tp, POOL_GROUP), lambda g, t: (t, g)),
            pl.BlockSpec((16, POOL_GROUP), lambda g, t: (jnp.maximum(t * halo_blocks - 1, 0), g)),
            pl.BlockSpec((tp, POOL_GROUP), lambda g, t: (t, ng + g)),
            pl.BlockSpec((None, POOL_BUF, DEC_BATCH, POOL_GROUP), lambda g, t: (layer, 0, 0, g)),
            pl.BlockSpec((None, None, POOL_GROUP, POOL_GROUP), lambda g, t: (layer, g, 0, 0)),
            pl.BlockSpec((None, 1, POOL_GROUP), lambda g, t: (layer, 0, g)),
        ],
        out_specs=[
            pl.BlockSpec((tp, POOL_GROUP), lambda g, t: (t, g)),
            pl.BlockSpec((None, POOL_BUF, POOL_GROUP),
                         lambda g, t: (jnp.minimum(t // tpb, BATCH - 1), 0, g)),
            pl.BlockSpec((POOL_BUF, DEC_BATCH, POOL_GROUP), lambda g, t: (0, 0, g)),
        ],
        out_shape=[
            jax.ShapeDtypeStruct((M_ALL, POOL_WIDTH), BF16),
            jax.ShapeDtypeStruct((BATCH, POOL_BUF, POOL_WIDTH), F32),
            jax.ShapeDtypeStruct((POOL_BUF, DEC_BATCH, POOL_WIDTH), F32),
        ],
        scratch_shapes=[pltpu.VMEM((POOL_GROUP, POOL_GROUP), BF16),
                        pltpu.VMEM((tp + POOL_HALO, POOL_GROUP), F32),
                        pltpu.VMEM((tp + POOL_HALO, POOL_GROUP), F32)],
        compiler_params=pltpu.CompilerParams(
            dimension_semantics=("arbitrary", "arbitrary"), vmem_limit_bytes=VMEM_LIMIT),
        name="pool_mix",
    )(proj, proj, proj, state_t, w_grp, scale.reshape(-1, 1, POOL_WIDTH))
    return outs


def _att_prompt_kernel(q0, k0, v0, q1, k1, v1, q2, k2, v2, z_ref, b0, b1, b2,
                       a_ref, ko0, vo0, ko1, vo1, ko2, vo2,
                       qd, kd, vd, od, ld, on, ln):
    q_refs, k_refs, v_refs = (q0, q1, q2), (k0, k1, k2), (v0, v1, v2)
    bias_refs = (b0, b1, b2)
    nt = (((1,), (1,)), ((), ()))

    for k_ref, v_ref, ko, vo, keep in ((k0, v0, ko0, vo0, WINDOWS[0]),
                                       (k1, v1, ko1, vo1, WINDOWS[1]),
                                       (k2, v2, ko2, vo2, WINDOWS[2])):
        ko[...] = k_ref[SEQ - keep:SEQ, :]
        vo[...] = v_ref[SEQ - keep:SEQ, :]

    for g, dil in enumerate(DILATIONS):
        n = SEQ // dil
        for src, dst in ((q_refs[g], qd), (k_refs[g], kd), (v_refs[g], vd)):
            if dil == 1:
                def cp(i, c, src=src, dst=dst, g=g):
                    r = pl.multiple_of(i * 256, 256)
                    dst[g, pl.ds(r, 256), :] = src[pl.ds(r, 256), :].astype(BF16)
                    return c
                lax.fori_loop(0, SEQ // 256, cp, 0)
            else:
                for r in range(dil):
                    dst[g, r * n:(r + 1) * n, :] = src[pl.ds(r, n, stride=dil), :].astype(BF16)

    def softmax_unit(g, u, with_prev):
        row = pl.multiple_of(u * BAND, BAND)
        q = qd[g, pl.ds(row, BAND), :]
        kc = kd[g, pl.ds(row, BAND), :]
        vc = vd[g, pl.ds(row, BAND), :]
        bias = bias_refs[g]
        s_c = lax.dot_general(q, kc, nt, preferred_element_type=F32) * ATT_SCALE + bias[:, BAND:2 * BAND]
        m = jnp.max(s_c, axis=-1, keepdims=True)
        if with_prev:
            rowp = pl.multiple_of(row - BAND, BAND)
            kp = kd[g, pl.ds(rowp, BAND), :]
            vp = vd[g, pl.ds(rowp, BAND), :]
            s_p = lax.dot_general(q, kp, nt, preferred_element_type=F32) * ATT_SCALE + bias[:, 0:BAND]
            m = jnp.maximum(m, jnp.max(s_p, axis=-1, keepdims=True))
        p_c = jnp.exp(s_c - m)
        l = jnp.sum(p_c, axis=-1, keepdims=True)
        acc = jnp.dot(p_c.astype(BF16), vc, preferred_element_type=F32)
        if with_prev:
            p_p = jnp.exp(s_p - m)
            l = l + jnp.sum(p_p, axis=-1, keepdims=True)
            acc = acc + jnp.dot(p_p.astype(BF16), vp, preferred_element_type=F32)
        od[g, pl.ds(row, BAND), :] = acc / l
        ld[g, pl.ds(row, BAND), :] = jnp.broadcast_to(m + jnp.log(l), (BAND, HEAD_DIM))

    for g, dil in enumerate(DILATIONS):
        nb = (SEQ // dil) // BAND

        def seq_body(r, c, g=g, nb=nb):
            u0 = r * nb
            softmax_unit(g, u0, False)
            if nb > 1:
                def blk(cb, c2):
                    softmax_unit(g, u0 + cb, True)
                    return c2
                lax.fori_loop(1, nb, blk, 0)
            return c
        lax.fori_loop(0, dil, seq_body, 0)

    for g, dil in enumerate(DILATIONS):
        if dil == 1:
            continue
        n = SEQ // dil
        for r in range(dil):
            on[g - 1, pl.ds(r, n, stride=dil), :] = od[g, r * n:(r + 1) * n, :]
            ln[g - 1, pl.ds(r, n, stride=dil), :] = ld[g, r * n:(r + 1) * n, :]

    def comb(i, c):
        rs = pl.ds(pl.multiple_of(i * 256, 256), 256)
        l0, l1, l2 = ld[0, rs, :], ln[0, rs, :], ln[1, rs, :]
        mx = jnp.maximum(jnp.maximum(l0, l1), l2)
        w0, w1, w2 = jnp.exp(l0 - mx), jnp.exp(l1 - mx), jnp.exp(l2 - mx)
        o = (w0 * od[0, rs, :] + w1 * on[0, rs, :] + w2 * on[1, rs, :]) / (w0 + w1 + w2)
        a_ref[rs, :] = (o * _silu(z_ref[rs, :])).astype(BF16)
        return c
    lax.fori_loop(0, SEQ // 256, comb, 0)


def _att_prompt(proj, bias_mats):
    hb = N_HEADS
    in_specs = []
    for g in range(N_GROUPS):
        for part in range(3):
            off = part * N_GROUPS * hb + g * hb
            in_specs.append(pl.BlockSpec((SEQ, HEAD_DIM), lambda b, h, off=off: (b, off + h)))
    in_specs.append(pl.BlockSpec((SEQ, HEAD_DIM), lambda b, h: (b, 9 * hb + h)))
    for g in range(N_GROUPS):
        in_specs.append(pl.BlockSpec((None, BAND, 2 * BAND), lambda b, h, g=g: (g * hb + h, 0, 0)))
    out_specs = [pl.BlockSpec((SEQ, HEAD_DIM), lambda b, h: (b, h))]
    out_shapes = [jax.ShapeDtypeStruct((M_ALL, D_MODEL), BF16)]
    for g in range(N_GROUPS):
        keep = WINDOWS[g]
        for part in range(2):
            out_specs.append(pl.BlockSpec((None, keep, HEAD_DIM), lambda b, h: (b, 0, h)))
            out_shapes.append(jax.ShapeDtypeStruct((BATCH, keep, D_MODEL), F32))
    scratch = [pltpu.VMEM((N_GROUPS, SEQ, HEAD_DIM), BF16)] * 3 + \
              [pltpu.VMEM((N_GROUPS, SEQ, HEAD_DIM), F32)] * 2 + \
              [pltpu.VMEM((N_GROUPS - 1, SEQ, HEAD_DIM), F32)] * 2
    return pl.pallas_call(
        _att_prompt_kernel,
        grid=(BATCH, N_HEADS),
        in_specs=in_specs,
        out_specs=out_specs,
        out_shape=out_shapes,
        scratch_shapes=scratch,
        compiler_params=pltpu.CompilerParams(
            dimension_semantics=("arbitrary", "arbitrary"), vmem_limit_bytes=VMEM_LIMIT),
        name="att_prompt",
    )(*([proj] * 10), bias_mats, bias_mats, bias_mats)


def _att_sample_kernel(x_ref, c0, c1, c2, bias_ref, a_ref, kv0, kv1, kv2):
    caches = (c0, c1, c2)
    kv_outs = (kv0, kv1, kv2)
    hb = N_HEADS
    outs, lses = [], []
    for g in range(N_GROUPS):
        q = x_ref[g * hb:(g + 1) * hb, :]
        kn = x_ref[(3 + g) * hb:(4 + g) * hb, :]
        vn = x_ref[(6 + g) * hb:(7 + g) * hb, :]
        kv_outs[g][0] = kn
        kv_outs[g][1] = vn
        kc = caches[g][:, 0]
        vc = caches[g][:, 1]
        s_c = jnp.sum(kc * q[None], axis=-1, keepdims=True) * ATT_SCALE + bias_ref[g, 0:BAND]
        s_n = jnp.sum(kn * q, axis=-1, keepdims=True) * ATT_SCALE + bias_ref[g, BAND]
        m = jnp.maximum(jnp.max(s_c, axis=0), s_n)
        p_c = jnp.exp(s_c - m[None])
        p_n = jnp.exp(s_n - m)
        l = jnp.sum(p_c, axis=0) + p_n
        o = (jnp.sum(p_c * vc, axis=0) + p_n * vn) / l
        outs.append(o)
        lses.append(m + jnp.log(l))
    mx = jnp.maximum(jnp.maximum(lses[0], lses[1]), lses[2])
    ws = [jnp.exp(ls - mx) for ls in lses]
    o = (ws[0] * outs[0] + ws[1] * outs[1] + ws[2] * outs[2]) / (ws[0] + ws[1] + ws[2])
    z = x_ref[9 * hb:10 * hb, :]
    a_ref[...] = o * _silu(z)


def _att_sample(proj_s, caches, bias_s, layer):
    hb = N_HEADS
    in_specs = [pl.BlockSpec((None, ATT_IN_COLS // HEAD_DIM, HEAD_DIM), lambda b: (b, 0, 0))]
    for g in range(N_GROUPS):
        in_specs.append(pl.BlockSpec((None, None, BAND, None, 2, hb, HEAD_DIM),
                                     lambda b: (layer, b, 0, 0, 0, 0, 0)))
    in_specs.append(pl.BlockSpec((N_GROUPS, BAND + 1, hb, 1), lambda b: (0, 0, 0, 0)))
    out_specs = [pl.BlockSpec((None, hb, HEAD_DIM), lambda b: (b, 0, 0))]
    out_shapes = [jax.ShapeDtypeStruct((DEC_BATCH, hb, HEAD_DIM), F32)]
    for g in range(N_GROUPS):
        out_specs.append(pl.BlockSpec((None, 2, hb, HEAD_DIM), lambda b: (b, 0, 0, 0)))
        out_shapes.append(jax.ShapeDtypeStruct((DEC_BATCH, 2, hb, HEAD_DIM), F32))
    return pl.pallas_call(
        _att_sample_kernel,
        grid=(DEC_BATCH,),
        in_specs=in_specs,
        out_specs=out_specs,
        out_shape=out_shapes,
        compiler_params=pltpu.CompilerParams(
            dimension_semantics=("arbitrary",), vmem_limit_bytes=VMEM_LIMIT),
        name="att_sample",
    )(proj_s, *caches, bias_s)


def kernel(x_prompt, x_sample, c_prompt, c_sample, cache_kv0, cache_kv1, cache_kv2, state_pool,
           norm_pre, norm_post, ada_w, ada_b, t5_bias, pool_w_in, pool_w_grp, pool_scale,
           pool_w_out, att_w_in, att_w_out):
    n_att = DEPTH // 2
    x = jnp.concatenate([x_prompt.reshape(M_PROMPT, D_MODEL),
                         x_sample.reshape(DEC_BATCH, D_MODEL),
                         jnp.zeros((M_ALL - M_PROMPT - DEC_BATCH, D_MODEL), F32)], axis=0)
    c_all = jnp.zeros((C_ROWS, D_MODEL), F32)
    c_all = c_all.at[0:BATCH].set(c_prompt).at[C_SAMPLE_ROW0:C_SAMPLE_ROW0 + DEC_BATCH].set(c_sample)

    mod = _ada_all(c_all, ada_w, ada_b)
    mod4 = mod.reshape(DEPTH, C_ROWS, 1, 3 * D_MODEL)
    gains_pre = norm_pre.reshape(DEPTH, 1, D_MODEL)
    gains_post = norm_post.reshape(DEPTH, 1, D_MODEL)

    bias_mats = _bias_matrices(t5_bias)
    bias_s = bias_mats[:, 0, 0:BAND + 1].reshape(N_GROUPS, N_HEADS, BAND + 1)
    bias_s = jnp.transpose(bias_s, (0, 2, 1))[..., None]
    caches = [c.reshape(n_att, DEC_BATCH, BAND, dil, 2, N_HEADS, HEAD_DIM)
              for c, dil in zip((cache_kv0, cache_kv1, cache_kv2), DILATIONS)]
    state_t = jnp.transpose(state_pool, (0, 2, 1, 3))

    kv_p = [[] for _ in range(N_GROUPS)]
    kv_s = [[] for _ in range(N_GROUPS)]
    pool_p, pool_s = [], []

    _, h = _norm_step(x, None, mod, mod4, gains_post, gains_pre, None, 0)
    for i in range(DEPTH):
        li = i // 2
        if i % 2 == 0:
            proj = _matmul(h, pool_w_in, li, 2 * POOL_WIDTH, 1024, F32, "pool_in_proj")
            a, pp, ps = _pool_mix(proj, state_t, pool_w_grp, pool_scale, li)
            pool_p.append(pp)
            pool_s.append(jnp.transpose(ps, (1, 0, 2)))
            y = _matmul(a, pool_w_out, li, D_MODEL, 512, F32, "pool_out_proj")
        else:
            proj = _matmul(h, att_w_in, li, ATT_IN_COLS, 1024, F32, "att_in_proj")
            outs = _att_prompt(proj, bias_mats)
            a = outs[0]
            for g in range(N_GROUPS):
                keep = WINDOWS[g]
                k_new = outs[1 + 2 * g].reshape(BATCH, keep, 1, N_HEADS, HEAD_DIM)
                v_new = outs[2 + 2 * g].reshape(BATCH, keep, 1, N_HEADS, HEAD_DIM)
                kv_p[g].append(jnp.concatenate([k_new, v_new], axis=2))
            proj_s = proj[M_PROMPT:M_PROMPT + DEC_BATCH].reshape(DEC_BATCH, ATT_IN_COLS // HEAD_DIM, HEAD_DIM)
            souts = _att_sample(proj_s, caches, bias_s, li)
            a_s = jnp.zeros((SAMPLE_ROWS, D_MODEL), F32).at[0:DEC_BATCH].set(souts[0].reshape(DEC_BATCH, D_MODEL))
            a = lax.dynamic_update_slice(a, a_s.astype(BF16), (M_PROMPT, 0))
            for g in range(N_GROUPS):
                kv_s[g].append(souts[1 + g].reshape(DEC_BATCH, 1, 2, N_HEADS, HEAD_DIM))
            y = _matmul(a, att_w_out, li, D_MODEL, 512, F32, "att_out_proj")
        x, h = _norm_step(x, y, mod, mod4, gains_post, gains_pre, i, i + 1 if i + 1 < DEPTH else None)

    y_prompt = x[0:M_PROMPT].reshape(BATCH, SEQ, D_MODEL)
    y_sample = x[M_PROMPT:M_PROMPT + DEC_BATCH].reshape(DEC_BATCH, 1, D_MODEL)
    return (y_prompt, y_sample,
            jnp.stack(kv_p[0]), jnp.stack(kv_p[1]), jnp.stack(kv_p[2]), jnp.stack(pool_p),
            jnp.stack(kv_s[0]), jnp.stack(kv_s[1]), jnp.stack(kv_s[2]), jnp.stack(pool_s))
```

```python
import functools

import numpy as np
import jax
import jax.numpy as jnp
from jax import lax
from jax.experimental import pallas as pl
from jax.experimental.pallas import tpu as pltpu

D_MODEL = 2048
BATCH = 4
SEQ = 2048
DEPTH = 4
DEC_BATCH = 8
HEAD_DIM = 128
N_HEADS = 16
DILATIONS = (1, 4, 16)
WINDOWS = (128, 512, 2048)
N_GROUPS = 3
QKV_WIDTH = N_GROUPS * D_MODEL
ATT_IN_COLS = 3 * QKV_WIDTH + D_MODEL
BAND = 128
ATT_SCALE = HEAD_DIM ** -0.5
POOL_WINDOWS = (2, 4, 8, 16)
POOL_WIDTH = 2 * D_MODEL
POOL_GROUP = POOL_WIDTH // 4
POOL_BUF = 15
N_BUCKETS = 32
T5_MAX_DIST = 2048
RMS_EPS = 1e-6
NEG_INF = -1e30

M_PROMPT = BATCH * SEQ
M_ALL = M_PROMPT + 16
SAMPLE_ROWS = 16
C_ROWS = 32
C_SAMPLE_ROW0 = 16

TM_MATMUL = 912
TE_NORM = 256
TP_POOL = 512
POOL_HALO = 24
ATT_UNITS = 8
N_BLOCKS = SEQ // BAND
VMEM_LIMIT = 56 * 1024 * 1024

F32 = jnp.float32
BF16 = jnp.bfloat16


def _silu(x):
    return x * (1.0 / (1.0 + jnp.exp(-x)))


def _cast_rows_to_bf16(src_ref, dst_ref, rows, chunk=256):
    def body(i, c):
        r = pl.multiple_of(i * chunk, chunk)
        dst_ref[pl.ds(r, chunk), :] = src_ref[pl.ds(r, chunk), :].astype(BF16)
        return c
    lax.fori_loop(0, rows // chunk, body, 0)


def _ada_kernel(c_ref, w_ref, b_ref, o_ref):
    a = _silu(c_ref[...]).astype(BF16)
    kc = 512
    acc = jnp.zeros(o_ref.shape, F32)
    for k0 in range(0, D_MODEL, kc):
        acc = acc + jnp.dot(a[:, k0:k0 + kc], w_ref[k0:k0 + kc, :].astype(BF16),
                            preferred_element_type=F32)
    o_ref[...] = acc + b_ref[...]


def _ada_all(c_all, ada_w, ada_b):
    tn = 1024
    n = 3 * D_MODEL
    return pl.pallas_call(
        _ada_kernel,
        grid=(DEPTH, n // tn),
        in_specs=[pl.BlockSpec((C_ROWS, D_MODEL), lambda l, j: (0, 0)),
                  pl.BlockSpec((None, D_MODEL, tn), lambda l, j: (l, 0, j)),
                  pl.BlockSpec((None, 1, tn), lambda l, j: (l, 0, j))],
        out_specs=pl.BlockSpec((None, C_ROWS, tn), lambda l, j: (l, 0, j)),
        out_shape=jax.ShapeDtypeStruct((DEPTH, C_ROWS, n), F32),
        compiler_params=pltpu.CompilerParams(
            dimension_semantics=("arbitrary", "arbitrary"), vmem_limit_bytes=VMEM_LIMIT),
        name="ada_mod",
    )(c_all, ada_w, ada_b.reshape(DEPTH, 1, n))


def _t5_bucket(dist):
    dist = np.asarray(dist, dtype=np.int64)
    max_exact = N_BUCKETS // 2
    ratio = np.log(np.maximum(dist, 1) / max_exact) / np.log(T5_MAX_DIST / max_exact)
    large = np.minimum(max_exact + (ratio * (N_BUCKETS - max_exact)).astype(np.int64), N_BUCKETS - 1)
    return np.where(dist < max_exact, dist, large).astype(np.int32)


def _bucket_index_table():
    rel = np.arange(BAND)[:, None] + BAND - np.arange(2 * BAND)[None, :]
    inband = (rel >= 0) & (rel <= BAND)
    out = []
    for dil in DILATIONS:
        bucket = _t5_bucket(np.clip(rel, 0, BAND) * dil)
        out.append(np.where(inband, bucket, -1))
    return np.stack(out).astype(np.int32)


def _bias_kernel(tab_ref, idx_ref, o_ref):
    col = pl.program_id(0)
    idx = idx_ref[...]
    acc = jnp.full(idx.shape, NEG_INF, F32)
    for b in range(N_BUCKETS):
        acc = jnp.where(idx == b, tab_ref[b, col], acc)
    o_ref[...] = acc


def _bias_matrices(t5_bias):
    idx = jnp.asarray(_bucket_index_table())
    n_sub = N_GROUPS * N_HEADS
    return pl.pallas_call(
        _bias_kernel,
        grid=(n_sub,),
        in_specs=[pl.BlockSpec(memory_space=pltpu.SMEM),
                  pl.BlockSpec((None, BAND, 2 * BAND), lambda c: (c // N_HEADS, 0, 0))],
        out_specs=pl.BlockSpec((None, BAND, 2 * BAND), lambda c: (c, 0, 0)),
        out_shape=jax.ShapeDtypeStruct((n_sub, BAND, 2 * BAND), F32),
        name="t5_bias_mats",
    )(t5_bias, idx)


def _norm_kernel(*refs, has_post, has_pre, split_in, split_out, n_prompt_tiles):
    refs = list(refs)
    xp_ref = refs.pop(0)
    xs_ref = refs.pop(0) if split_in else xp_ref
    if has_post:
        y_ref, gpost_ref, gate_p_ref, gate_s_ref = refs[:4]
        refs = refs[4:]
    if has_pre:
        gpre_ref, shift_p_ref, scale_p_ref, shift_s_ref, scale_s_ref = refs[:5]
        refs = refs[5:]
    if has_post:
        xop_ref = refs.pop(0)
        xos_ref = refs.pop(0) if split_out else xop_ref
    if has_pre:
        h_ref = refs.pop(0)

    def rms(v, g):
        return v * lax.rsqrt(jnp.mean(v * v, axis=-1, keepdims=True) + RMS_EPS) * g

    def body(x_ref, xo_ref, rows, gate, shift, scale):
        x = x_ref[rows, :]
        if has_post:
            x = x + gate * rms(y_ref[rows, :], gpost_ref[...])
            xo_ref[rows, :] = x
        if has_pre:
            h = rms(x, gpre_ref[...]) * (1.0 + scale) + shift
            h_ref[rows, :] = h.astype(BF16)

    t = pl.program_id(0)

    @pl.when(t < n_prompt_tiles)
    def _():
        body(xp_ref, xop_ref if has_post else None, slice(None),
             gate_p_ref[...] if has_post else None,
             shift_p_ref[...] if has_pre else None,
             scale_p_ref[...] if has_pre else None)

    @pl.when(t == n_prompt_tiles)
    def _():
        body(xs_ref, xos_ref if has_post else None, slice(0, SAMPLE_ROWS),
             gate_s_ref[...] if has_post else None,
             shift_s_ref[...] if has_pre else None,
             scale_s_ref[...] if has_pre else None)


def _norm_step(x, y, mod, mod4, norm_post, norm_pre, post_layer, pre_layer, x_sample=None, split_out=False):
    has_post = post_layer is not None
    has_pre = pre_layer is not None
    split_in = x_sample is not None
    te = TE_NORM
    npt = M_PROMPT // te
    tiles_per_batch = SEQ // te
    row_spec = pl.BlockSpec((te, D_MODEL), lambda t: (t, 0))
    prompt_row_spec = pl.BlockSpec((te, D_MODEL), lambda t: (jnp.minimum(t, npt - 1), 0))
    sample_row_spec = pl.BlockSpec((SAMPLE_ROWS, D_MODEL), lambda t: (0, 0))

    def mod_p_spec(layer, part):
        return pl.BlockSpec((None, None, 1, D_MODEL),
                            lambda t: (layer, jnp.minimum(t // tiles_per_batch, BATCH - 1), 0, part))

    def mod_s_spec(layer, part):
        return pl.BlockSpec((None, SAMPLE_ROWS, D_MODEL),
                            lambda t: (layer, C_SAMPLE_ROW0 // SAMPLE_ROWS, part))

    def gain_spec(layer):
        return pl.BlockSpec((None, 1, D_MODEL), lambda t: (layer, 0, 0))

    if split_in:
        args, in_specs = [x, x_sample], [prompt_row_spec, sample_row_spec]
    else:
        args, in_specs = [x], [row_spec]
    out_shapes, out_specs = [], []
    if has_post:
        args += [y, norm_post, mod4, mod]
        in_specs += [row_spec, gain_spec(post_layer), mod_p_spec(post_layer, 2), mod_s_spec(post_layer, 2)]
        if split_out:
            out_shapes += [jax.ShapeDtypeStruct((M_PROMPT, D_MODEL), F32),
                           jax.ShapeDtypeStruct((SAMPLE_ROWS, D_MODEL), F32)]
            out_specs += [prompt_row_spec, sample_row_spec]
        else:
            out_shapes.append(jax.ShapeDtypeStruct((M_ALL, D_MODEL), F32))
            out_specs.append(row_spec)
    if has_pre:
        args += [norm_pre, mod4, mod4, mod, mod]
        in_specs += [gain_spec(pre_layer), mod_p_spec(pre_layer, 0), mod_p_spec(pre_layer, 1),
                     mod_s_spec(pre_layer, 0), mod_s_spec(pre_layer, 1)]
        out_shapes.append(jax.ShapeDtypeStruct((M_ALL, D_MODEL), BF16))
        out_specs.append(row_spec)
    outs = pl.pallas_call(
        functools.partial(_norm_kernel, has_post=has_post, has_pre=has_pre, split_in=split_in,
                          split_out=split_out, n_prompt_tiles=npt),
        grid=(npt + 1,),
        in_specs=in_specs,
        out_specs=out_specs,
        out_shape=out_shapes,
        compiler_params=pltpu.CompilerParams(
            dimension_semantics=("arbitrary",), vmem_limit_bytes=VMEM_LIMIT),
        name="norm_step",
    )(*args)
    return outs


def _mm_kernel(a_ref, w_ref, o_ref, wb_ref, *, k_rows):
    @pl.when(pl.program_id(1) == 0)
    def _():
        _cast_rows_to_bf16(w_ref, wb_ref, k_rows)

    o_ref[...] = jnp.dot(a_ref[...], wb_ref[...], preferred_element_type=F32).astype(o_ref.dtype)


def _matmul(a, w, layer, n_out, tn, out_dtype, name):
    m, k = a.shape
    tm = TM_MATMUL
    return pl.pallas_call(
        functools.partial(_mm_kernel, k_rows=k),
        grid=(n_out // tn, m // tm),
        in_specs=[pl.BlockSpec((tm, k), lambda j, i: (i, 0)),
                  pl.BlockSpec((None, k, tn), lambda j, i: (layer, 0, j))],
        out_specs=pl.BlockSpec((tm, tn), lambda j, i: (i, j)),
        out_shape=jax.ShapeDtypeStruct((m, n_out), out_dtype),
        scratch_shapes=[pltpu.VMEM((k, tn), BF16)],
        compiler_params=pltpu.CompilerParams(
            dimension_semantics=("arbitrary", "arbitrary"), vmem_limit_bytes=VMEM_LIMIT),
        name=name,
    )(a, w)


def _pool_kernel(u_ref, halo_ref, z_ref, st_ref, w_ref, sc_ref,
                 a_ref, pp_ref, ps_ref, wb_ref, buf_a, buf_b, *, n_prompt_tiles, tiles_per_batch):
    g = pl.program_id(0)
    t = pl.program_id(1)
    tp = TP_POOL
    h0 = POOL_HALO

    @pl.when(t == 0)
    def _():
        _cast_rows_to_bf16(w_ref, wb_ref, POOL_GROUP)

    def finish(r, z, rows):
        y = jnp.dot(r.astype(BF16), wb_ref[...], preferred_element_type=F32) * sc_ref[...]
        a_ref[rows, :] = (y * _silu(z)).astype(BF16)

    def prompt_tile(n_steps):
        w = 2 ** n_steps
        first = (t % tiles_per_batch) == 0
        buf_a[0:8, :] = jnp.zeros((8, POOL_GROUP), F32)
        buf_b[0:8, :] = jnp.zeros((8, POOL_GROUP), F32)
        buf_a[8:h0, :] = jnp.where(first, 0.0, halo_ref[...])
        buf_a[h0:h0 + tp, :] = u_ref[...]
        src, dst = buf_a, buf_b
        n = tp + h0 - 8
        for s in range(n_steps):
            sh = 2 ** s
            dst[8:8 + n, :] = src[8:8 + n, :] + src[8 - sh:8 - sh + n, :]
            src, dst = dst, src
        pos = (t % tiles_per_batch) * tp + lax.broadcasted_iota(jnp.int32, (tp, 1), 0)
        inv_cnt = 1.0 / jnp.minimum(pos + 1, w).astype(F32)
        u = u_ref[...]
        r = src[h0:h0 + tp, :] * inv_cnt - u
        finish(r, z_ref[...], slice(None))

        @pl.when((t % tiles_per_batch) == tiles_per_batch - 1)
        def _():
            pp_ref[...] = u_ref[tp - POOL_BUF:tp, :]

    def sample_tile(n_steps):
        w = 2 ** n_steps
        u_new = u_ref[0:DEC_BATCH, :]
        acc = u_new
        for k in range(1, w):
            acc = acc + st_ref[POOL_BUF - k]
        r = acc / float(w) - u_new
        buf_a[0:DEC_BATCH, :] = r
        buf_a[DEC_BATCH:SAMPLE_ROWS, :] = jnp.zeros((SAMPLE_ROWS - DEC_BATCH, POOL_GROUP), F32)
        finish(buf_a[0:SAMPLE_ROWS, :], z_ref[0:SAMPLE_ROWS, :], slice(0, SAMPLE_ROWS))
        for k in range(POOL_BUF - 1):
            ps_ref[k] = st_ref[k + 1]
        ps_ref[POOL_BUF - 1] = u_new

    for gi in range(len(POOL_WINDOWS)):
        @pl.when((g == gi) & (t < n_prompt_tiles))
        def _(gi=gi):
            prompt_tile(gi + 1)

        @pl.when((g == gi) & (t == n_prompt_tiles))
        def _(gi=gi):
            sample_tile(gi + 1)


def _pool_mix(proj, state_t, w_grp, scale, layer):
    tp = TP_POOL
    npt = M_PROMPT // tp
    tpb = SEQ // tp
    ng = len(POOL_WINDOWS)
    halo_blocks = tp // 16
    outs = pl.pallas_call(
        functools.partial(_pool_kernel, n_prompt_tiles=npt, tiles_per_batch=tpb),
        grid=(ng, npt + 1),
        in_specs=[
            pl.BlockSpec((tp, POOL_GROUP), lambda g, t: (t, g)),
            pl.BlockSpec((16, POOL_GROUP), lambda g, t: (jnp.maximum(t * halo_blocks - 1, 0), g)),
            pl.BlockSpec((tp, POOL_GROUP), lambda g, t: (t, ng + g)),
            pl.BlockSpec((None, POOL_BUF, DEC_BATCH, POOL_GROUP), lambda g, t: (layer, 0, 0, g)),
            pl.BlockSpec((None, None, POOL_GROUP, POOL_GROUP), lambda g, t: (layer, g, 0, 0)),
            pl.BlockSpec((None, 1, POOL_GROUP), lambda g, t: (layer, 0, g)),
        ],
        out_specs=[
            pl.BlockSpec((tp, POOL_GROUP), lambda g, t: (t, g)),
            pl.BlockSpec((None, POOL_BUF, POOL_GROUP),
                         lambda g, t: (jnp.minimum(t // tpb, BATCH - 1), 0, g)),
            pl.BlockSpec((POOL_BUF, DEC_BATCH, POOL_GROUP), lambda g, t: (0, 0, g)),
        ],
        out_shape=[
            jax.ShapeDtypeStruct((M_ALL, POOL_WIDTH), BF16),
            jax.ShapeDtypeStruct((BATCH, POOL_BUF, POOL_WIDTH), F32),
            jax.ShapeDtypeStruct((POOL_BUF, DEC_BATCH, POOL_WIDTH), F32),
        ],
        scratch_shapes=[pltpu.VMEM((POOL_GROUP, POOL_GROUP), BF16),
                        pltpu.VMEM((tp + POOL_HALO, POOL_GROUP), F32),
                        pltpu.VMEM((tp + POOL_HALO, POOL_GROUP), F32)],
        compiler_params=pltpu.CompilerParams(
            dimension_semantics=("arbitrary", "arbitrary"), vmem_limit_bytes=VMEM_LIMIT),
        name="pool_mix",
    )(proj, proj, proj, state_t, w_grp, scale.reshape(-1, 1, POOL_WIDTH))
    return outs


def _att_prompt_kernel(q0, k0, v0, q1, k1, v1, q2, k2, v2, z_ref, b0, b1, b2,
                       a_ref, qd, kd, vd, od, ld, on, ln):
    q_refs, k_refs, v_refs = (q0, q1, q2), (k0, k1, k2), (v0, v1, v2)
    bias_refs = (b0, b1, b2)
    nu = ATT_UNITS

    for g, dil in enumerate(DILATIONS):
        n = SEQ // dil
        nb = n // BAND
        zero_blk = jnp.zeros((BAND, HEAD_DIM), BF16)
        kd[g, 0:BAND, :] = zero_blk
        vd[g, 0:BAND, :] = zero_blk
        for src, dst, off in ((q_refs[g], qd, 0), (k_refs[g], kd, BAND), (v_refs[g], vd, BAND)):
            if dil == 1:
                def cp(i, c, src=src, dst=dst, g=g, off=off):
                    r = pl.multiple_of(i * 256, 256)
                    dst[g, pl.ds(off + r, 256), :] = src[pl.ds(r, 256), :].astype(BF16)
                    return c
                lax.fori_loop(0, SEQ // 256, cp, 0)
            else:
                for r in range(dil):
                    dst[g, off + r * n:off + (r + 1) * n, :] = src[pl.ds(r, n, stride=dil), :].astype(BF16)

    def unit_batch(g, u0, seq_blocks):
        with_prev = seq_blocks > 1
        rows = nu * BAND
        row = pl.multiple_of(u0 * BAND, rows)
        shape3 = (nu, BAND, HEAD_DIM)
        q = qd[g, pl.ds(row, rows), :].reshape(shape3)
        kc = kd[g, pl.ds(row + BAND, rows), :].reshape(shape3)
        vc = vd[g, pl.ds(row + BAND, rows), :].reshape(shape3)
        bias = bias_refs[g]
        s_c = jnp.einsum('uqe,uke->uqk', q, kc, preferred_element_type=F32) * ATT_SCALE \
            + bias[:, BAND:2 * BAND][None]
        if with_prev:
            kp = kd[g, pl.ds(row, rows), :].reshape(shape3)
            vp = vd[g, pl.ds(row, rows), :].reshape(shape3)
            bias_p = jnp.broadcast_to(bias[:, 0:BAND][None], (nu, BAND, BAND))
            blk = u0 + lax.broadcasted_iota(jnp.int32, (nu, BAND, BAND), 0)
            bias_p = jnp.where((blk & (seq_blocks - 1)) == 0, NEG_INF, bias_p)
            s_p = jnp.einsum('uqe,uke->uqk', q, kp, preferred_element_type=F32) * ATT_SCALE + bias_p
            m = jnp.max(jnp.maximum(s_c, s_p), axis=-1, keepdims=True)
            p_c = jnp.exp(s_c - m)
            p_p = jnp.exp(s_p - m)
            l = jnp.sum(p_c + p_p, axis=-1, keepdims=True)
            acc = jnp.einsum('uqk,uke->uqe', p_c.astype(BF16), vc, preferred_element_type=F32) \
                + jnp.einsum('uqk,uke->uqe', p_p.astype(BF16), vp, preferred_element_type=F32)
        else:
            m = jnp.max(s_c, axis=-1, keepdims=True)
            p_c = jnp.exp(s_c - m)
            l = jnp.sum(p_c, axis=-1, keepdims=True)
            acc = jnp.einsum('uqk,uke->uqe', p_c.astype(BF16), vc, preferred_element_type=F32)
        od[g, pl.ds(row, rows), :] = (acc / l).reshape(rows, HEAD_DIM)
        ld[g, pl.ds(row, rows), :] = jnp.broadcast_to(m + jnp.log(l), shape3).reshape(rows, HEAD_DIM)

    for g, dil in enumerate(DILATIONS):
        nb = (SEQ // dil) // BAND

        def batch_body(i, c, g=g, nb=nb):
            unit_batch(g, i * nu, nb)
            return c
        lax.fori_loop(0, N_BLOCKS // nu, batch_body, 0)

    for g, dil in enumerate(DILATIONS):
        if dil == 1:
            continue
        n = SEQ // dil
        for r in range(dil):
            on[g - 1, pl.ds(r, n, stride=dil), :] = od[g, r * n:(r + 1) * n, :]
            ln[g - 1, pl.ds(r, n, stride=dil), :] = ld[g, r * n:(r + 1) * n, :]

    def comb(i, c):
        rs = pl.ds(pl.multiple_of(i * 256, 256), 256)
        l0, l1, l2 = ld[0, rs, :], ln[0, rs, :], ln[1, rs, :]
        mx = jnp.maximum(jnp.maximum(l0, l1), l2)
        w0, w1, w2 = jnp.exp(l0 - mx), jnp.exp(l1 - mx), jnp.exp(l2 - mx)
        o = (w0 * od[0, rs, :] + w1 * on[0, rs, :] + w2 * on[1, rs, :]) / (w0 + w1 + w2)
        a_ref[rs, :] = (o * _silu(z_ref[rs, :])).astype(BF16)
        return c
    lax.fori_loop(0, SEQ // 256, comb, 0)


def _att_prompt(proj, bias_mats):
    hb = N_HEADS
    in_specs = []
    for g in range(N_GROUPS):
        for part in range(3):
            off = part * N_GROUPS * hb + g * hb
            in_specs.append(pl.BlockSpec((SEQ, HEAD_DIM), lambda b, h, off=off: (b, off + h)))
    in_specs.append(pl.BlockSpec((SEQ, HEAD_DIM), lambda b, h: (b, 9 * hb + h)))
    for g in range(N_GROUPS):
        in_specs.append(pl.BlockSpec((None, BAND, 2 * BAND), lambda b, h, g=g: (g * hb + h, 0, 0)))
    scratch = [pltpu.VMEM((N_GROUPS, SEQ, HEAD_DIM), BF16)] + \
              [pltpu.VMEM((N_GROUPS, SEQ + BAND, HEAD_DIM), BF16)] * 2 + \
              [pltpu.VMEM((N_GROUPS, SEQ, HEAD_DIM), F32)] * 2 + \
              [pltpu.VMEM((N_GROUPS - 1, SEQ, HEAD_DIM), F32)] * 2
    return pl.pallas_call(
        _att_prompt_kernel,
        grid=(BATCH, N_HEADS),
        in_specs=in_specs,
        out_specs=pl.BlockSpec((SEQ, HEAD_DIM), lambda b, h: (b, h)),
        out_shape=jax.ShapeDtypeStruct((M_ALL, D_MODEL), BF16),
        scratch_shapes=scratch,
        compiler_params=pltpu.CompilerParams(
            dimension_semantics=("arbitrary", "arbitrary"), vmem_limit_bytes=VMEM_LIMIT),
        name="att_prompt",
    )(*([proj] * 10), bias_mats, bias_mats, bias_mats)


def _kv_rows_kernel(src_a, src_b, o_ref):
    def emit(src):
        for h in range(N_HEADS):
            o_ref[:, h, :] = src[:, h * HEAD_DIM:(h + 1) * HEAD_DIM]

    @pl.when(pl.program_id(0) == 0)
    def _():
        emit(src_a)

    @pl.when(pl.program_id(0) == 1)
    def _():
        emit(src_b)


def _kv_rows(proj_a, proj_b, g):
    keep = WINDOWS[g]
    tr = 256 if keep >= 256 else keep
    nt = keep // tr
    first_blk = (SEQ - keep) // tr
    blks_per_batch = SEQ // tr

    def row_blk(b, t):
        return b * blks_per_batch + first_blk + t

    def col_blk(part):
        return (1 + part) * N_GROUPS + g

    def spec(layer):
        parked = (row_blk(BATCH - 1, nt - 1), col_blk(1)) if layer == 0 else (row_blk(0, 0), col_blk(0))
        return pl.BlockSpec((tr, D_MODEL),
                            lambda l, b, t, p: (jnp.where(l == layer, row_blk(b, t), parked[0]),
                                                jnp.where(l == layer, col_blk(p), parked[1])))

    return pl.pallas_call(
        _kv_rows_kernel,
        grid=(2, BATCH, nt, 2),
        in_specs=[spec(0), spec(1)],
        out_specs=pl.BlockSpec((None, None, tr, None, N_HEADS, HEAD_DIM), lambda l, b, t, p: (l, b, t, p, 0, 0)),
        out_shape=jax.ShapeDtypeStruct((2, BATCH, keep, 2, N_HEADS, HEAD_DIM), F32),
        compiler_params=pltpu.CompilerParams(
            dimension_semantics=("arbitrary",) * 4, vmem_limit_bytes=VMEM_LIMIT),
        name="kv_rows",
    )(proj_a, proj_b)


def _att_sample_kernel(x_ref, c0, c1, c2, bias_ref, a_ref, kv0, kv1, kv2):
    caches = (c0, c1, c2)
    kv_outs = (kv0, kv1, kv2)
    hb = N_HEADS
    outs, lses = [], []
    for g in range(N_GROUPS):
        q = x_ref[g * hb:(g + 1) * hb, :]
        kn = x_ref[(3 + g) * hb:(4 + g) * hb, :]
        vn = x_ref[(6 + g) * hb:(7 + g) * hb, :]
        kv_outs[g][0] = kn
        kv_outs[g][1] = vn
        kc = caches[g][:, 0]
        vc = caches[g][:, 1]
        s_c = jnp.sum(kc * q[None], axis=-1, keepdims=True) * ATT_SCALE + bias_ref[g, 0:BAND]
        s_n = jnp.sum(kn * q, axis=-1, keepdims=True) * ATT_SCALE + bias_ref[g, BAND]
        m = jnp.maximum(jnp.max(s_c, axis=0), s_n)
        p_c = jnp.exp(s_c - m[None])
        p_n = jnp.exp(s_n - m)
        l = jnp.sum(p_c, axis=0) + p_n
        o = (jnp.sum(p_c * vc, axis=0) + p_n * vn) / l
        outs.append(o)
        lses.append(m + jnp.log(l))
    mx = jnp.maximum(jnp.maximum(lses[0], lses[1]), lses[2])
    ws = [jnp.exp(ls - mx) for ls in lses]
    o = (ws[0] * outs[0] + ws[1] * outs[1] + ws[2] * outs[2]) / (ws[0] + ws[1] + ws[2])
    z = x_ref[9 * hb:10 * hb, :]
    a_ref[...] = o * _silu(z)


def _att_sample(proj_s, caches, bias_s, layer):
    hb = N_HEADS
    in_specs = [pl.BlockSpec((None, ATT_IN_COLS // HEAD_DIM, HEAD_DIM), lambda b: (b, 0, 0))]
    for g in range(N_GROUPS):
        in_specs.append(pl.BlockSpec((None, None, BAND, None, 2, hb, HEAD_DIM),
                                     lambda b: (layer, b, 0, 0, 0, 0, 0)))
    in_specs.append(pl.BlockSpec((N_GROUPS, BAND + 1, hb, 1), lambda b: (0, 0, 0, 0)))
    out_specs = [pl.BlockSpec((None, hb, HEAD_DIM), lambda b: (b, 0, 0))]
    out_shapes = [jax.ShapeDtypeStruct((DEC_BATCH, hb, HEAD_DIM), F32)]
    for g in range(N_GROUPS):
        out_specs.append(pl.BlockSpec((None, 2, hb, HEAD_DIM), lambda b: (b, 0, 0, 0)))
        out_shapes.append(jax.ShapeDtypeStruct((DEC_BATCH, 2, hb, HEAD_DIM), F32))
    return pl.pallas_call(
        _att_sample_kernel,
        grid=(DEC_BATCH,),
        in_specs=in_specs,
        out_specs=out_specs,
        out_shape=out_shapes,
        compiler_params=pltpu.CompilerParams(
            dimension_semantics=("arbitrary",), vmem_limit_bytes=VMEM_LIMIT),
        name="att_sample",
    )(proj_s, *caches, bias_s)


def kernel(x_prompt, x_sample, c_prompt, c_sample, cache_kv0, cache_kv1, cache_kv2, state_pool,
           norm_pre, norm_post, ada_w, ada_b, t5_bias, pool_w_in, pool_w_grp, pool_scale,
           pool_w_out, att_w_in, att_w_out):
    n_att = DEPTH // 2
    xp0 = x_prompt.reshape(M_PROMPT, D_MODEL)
    xs0 = jnp.zeros((SAMPLE_ROWS, D_MODEL), F32).at[0:DEC_BATCH].set(x_sample.reshape(DEC_BATCH, D_MODEL))
    c_all = jnp.zeros((C_ROWS, D_MODEL), F32)
    c_all = c_all.at[0:BATCH].set(c_prompt).at[C_SAMPLE_ROW0:C_SAMPLE_ROW0 + DEC_BATCH].set(c_sample)

    mod = _ada_all(c_all, ada_w, ada_b)
    mod4 = mod.reshape(DEPTH, C_ROWS, 1, 3 * D_MODEL)
    gains_pre = norm_pre.reshape(DEPTH, 1, D_MODEL)
    gains_post = norm_post.reshape(DEPTH, 1, D_MODEL)

    bias_mats = _bias_matrices(t5_bias)
    bias_s = bias_mats[:, 0, 0:BAND + 1].reshape(N_GROUPS, N_HEADS, BAND + 1)
    bias_s = jnp.transpose(bias_s, (0, 2, 1))[..., None]
    caches = [c.reshape(n_att, DEC_BATCH, BAND, dil, 2, N_HEADS, HEAD_DIM)
              for c, dil in zip((cache_kv0, cache_kv1, cache_kv2), DILATIONS)]
    state_t = jnp.transpose(state_pool, (0, 2, 1, 3))

    kv_s = [[] for _ in range(N_GROUPS)]
    pool_p, pool_s, att_projs = [], [], []

    (h,) = _norm_step(xp0, None, mod, mod4, gains_post, gains_pre, None, 0, x_sample=xs0)
    x = None
    for i in range(DEPTH):
        li = i // 2
        if i % 2 == 0:
            proj = _matmul(h, pool_w_in, li, 2 * POOL_WIDTH, 1024, F32, "pool_in_proj")
            a, pp, ps = _pool_mix(proj, state_t, pool_w_grp, pool_scale, li)
            pool_p.append(pp)
            pool_s.append(jnp.transpose(ps, (1, 0, 2)))
            y = _matmul(a, pool_w_out, li, D_MODEL, 512, F32, "pool_out_proj")
        else:
            proj = _matmul(h, att_w_in, li, ATT_IN_COLS, 1024, F32, "att_in_proj")
            att_projs.append(proj)
            a = _att_prompt(proj, bias_mats)
            proj_s = proj[M_PROMPT:M_PROMPT + DEC_BATCH].reshape(DEC_BATCH, ATT_IN_COLS // HEAD_DIM, HEAD_DIM)
            souts = _att_sample(proj_s, caches, bias_s, li)
            a_s = jnp.zeros((SAMPLE_ROWS, D_MODEL), F32).at[0:DEC_BATCH].set(souts[0].reshape(DEC_BATCH, D_MODEL))
            a = lax.dynamic_update_slice(a, a_s.astype(BF16), (M_PROMPT, 0))
            for g in range(N_GROUPS):
                kv_s[g].append(souts[1 + g].reshape(DEC_BATCH, 1, 2, N_HEADS, HEAD_DIM))
            y = _matmul(a, att_w_out, li, D_MODEL, 512, F32, "att_out_proj")
        last = i + 1 == DEPTH
        if i == 0:
            x, h = _norm_step(xp0, y, mod, mod4, gains_post, gains_pre, i, i + 1, x_sample=xs0)
        elif not last:
            x, h = _norm_step(x, y, mod, mod4, gains_post, gains_pre, i, i + 1)
        else:
            y_p, y_s = _norm_step(x, y, mod, mod4, gains_post, gains_pre, i, None, split_out=True)

    kv_p = [_kv_rows(att_projs[0], att_projs[1], g) for g in range(N_GROUPS)]
    y_prompt = y_p.reshape(BATCH, SEQ, D_MODEL)
    y_sample = y_s[0:DEC_BATCH].reshape(DEC_BATCH, 1, D_MODEL)
    return (y_prompt, y_sample, kv_p[0], kv_p[1], kv_p[2], jnp.stack(pool_p),
            jnp.stack(kv_s[0]), jnp.stack(kv_s[1]), jnp.stack(kv_s[2]), jnp.stack(pool_s))
```

```python
import functools

import numpy as np
import jax
import jax.numpy as jnp
from jax import lax
from jax.experimental import pallas as pl
from jax.experimental.pallas import tpu as pltpu

D_MODEL = 2048
BATCH = 4
SEQ = 2048
DEPTH = 4
DEC_BATCH = 8
HEAD_DIM = 128
N_HEADS = 16
DILATIONS = (1, 4, 16)
WINDOWS = (128, 512, 2048)
N_GROUPS = 3
QKV_WIDTH = N_GROUPS * D_MODEL
ATT_IN_COLS = 3 * QKV_WIDTH + D_MODEL
BAND = 128
ATT_SCALE = HEAD_DIM ** -0.5
POOL_WINDOWS = (2, 4, 8, 16)
POOL_WIDTH = 2 * D_MODEL
POOL_GROUP = POOL_WIDTH // 4
POOL_BUF = 15
N_BUCKETS = 32
T5_MAX_DIST = 2048
RMS_EPS = 1e-6
NEG_INF = -1e30

M_PROMPT = BATCH * SEQ
M_ALL = M_PROMPT + 16
SAMPLE_ROWS = 16
C_ROWS = 32
C_SAMPLE_ROW0 = 16

TM_MATMUL = 912
TE_NORM = 256
TP_POOL = 512
POOL_HALO = 24
ATT_UNITS = 8
N_BLOCKS = SEQ // BAND
VMEM_LIMIT = 56 * 1024 * 1024

F32 = jnp.float32
BF16 = jnp.bfloat16


def _silu(x):
    return x * (1.0 / (1.0 + jnp.exp(-x)))


def _cast_rows_to_bf16(src_ref, dst_ref, rows, chunk=256):
    def body(i, c):
        r = pl.multiple_of(i * chunk, chunk)
        dst_ref[pl.ds(r, chunk), :] = src_ref[pl.ds(r, chunk), :].astype(BF16)
        return c
    lax.fori_loop(0, rows // chunk, body, 0)


def _ada_kernel(c_ref, w_ref, b_ref, o_ref):
    a = _silu(c_ref[...]).astype(BF16)
    kc = 512
    acc = jnp.zeros(o_ref.shape, F32)
    for k0 in range(0, D_MODEL, kc):
        acc = acc + jnp.dot(a[:, k0:k0 + kc], w_ref[k0:k0 + kc, :].astype(BF16),
                            preferred_element_type=F32)
    o_ref[...] = acc + b_ref[...]


def _ada_all(c_all, ada_w, ada_b):
    tn = 1024
    n = 3 * D_MODEL
    return pl.pallas_call(
        _ada_kernel,
        grid=(DEPTH, n // tn),
        in_specs=[pl.BlockSpec((C_ROWS, D_MODEL), lambda l, j: (0, 0)),
                  pl.BlockSpec((None, D_MODEL, tn), lambda l, j: (l, 0, j)),
                  pl.BlockSpec((None, 1, tn), lambda l, j: (l, 0, j))],
        out_specs=pl.BlockSpec((None, C_ROWS, tn), lambda l, j: (l, 0, j)),
        out_shape=jax.ShapeDtypeStruct((DEPTH, C_ROWS, n), F32),
        compiler_params=pltpu.CompilerParams(
            dimension_semantics=("arbitrary", "arbitrary"), vmem_limit_bytes=VMEM_LIMIT),
        name="ada_mod",
    )(c_all, ada_w, ada_b.reshape(DEPTH, 1, n))


def _t5_bucket(dist):
    dist = np.asarray(dist, dtype=np.int64)
    max_exact = N_BUCKETS // 2
    ratio = np.log(np.maximum(dist, 1) / max_exact) / np.log(T5_MAX_DIST / max_exact)
    large = np.minimum(max_exact + (ratio * (N_BUCKETS - max_exact)).astype(np.int64), N_BUCKETS - 1)
    return np.where(dist < max_exact, dist, large).astype(np.int32)


def _bucket_index_table():
    rel = np.arange(BAND)[:, None] + BAND - np.arange(2 * BAND)[None, :]
    inband = (rel >= 0) & (rel <= BAND)
    out = []
    for dil in DILATIONS:
        bucket = _t5_bucket(np.clip(rel, 0, BAND) * dil)
        out.append(np.where(inband, bucket, -1))
    return np.stack(out).astype(np.int32)


def _bias_kernel(tab_ref, idx_ref, o_ref):
    col = pl.program_id(0)
    idx = idx_ref[...]
    acc = jnp.full(idx.shape, NEG_INF, F32)
    for b in range(N_BUCKETS):
        acc = jnp.where(idx == b, tab_ref[b, col], acc)
    o_ref[...] = acc


def _bias_matrices(t5_bias):
    idx = jnp.asarray(_bucket_index_table())
    n_sub = N_GROUPS * N_HEADS
    return pl.pallas_call(
        _bias_kernel,
        grid=(n_sub,),
        in_specs=[pl.BlockSpec(memory_space=pltpu.SMEM),
                  pl.BlockSpec((None, BAND, 2 * BAND), lambda c: (c // N_HEADS, 0, 0))],
        out_specs=pl.BlockSpec((None, BAND, 2 * BAND), lambda c: (c, 0, 0)),
        out_shape=jax.ShapeDtypeStruct((n_sub, BAND, 2 * BAND), F32),
        name="t5_bias_mats",
    )(t5_bias, idx)


def _norm_kernel(*refs, has_post, has_pre, split_in, split_out, n_prompt_tiles):
    refs = list(refs)
    xp_ref = refs.pop(0)
    xs_ref = refs.pop(0) if split_in else xp_ref
    if has_post:
        y_ref, gpost_ref, gate_p_ref, gate_s_ref = refs[:4]
        refs = refs[4:]
    if has_pre:
        gpre_ref, shift_p_ref, scale_p_ref, shift_s_ref, scale_s_ref = refs[:5]
        refs = refs[5:]
    if has_post:
        xop_ref = refs.pop(0)
        xos_ref = refs.pop(0) if split_out else xop_ref
    if has_pre:
        h_ref = refs.pop(0)

    def rms(v, g):
        return v * lax.rsqrt(jnp.mean(v * v, axis=-1, keepdims=True) + RMS_EPS) * g

    def body(x_ref, xo_ref, rows, gate, shift, scale):
        x = x_ref[rows, :]
        if has_post:
            x = x + gate * rms(y_ref[rows, :], gpost_ref[...])
            xo_ref[rows, :] = x
        if has_pre:
            h = rms(x, gpre_ref[...]) * (1.0 + scale) + shift
            h_ref[rows, :] = h.astype(BF16)

    t = pl.program_id(0)

    @pl.when(t < n_prompt_tiles)
    def _():
        body(xp_ref, xop_ref if has_post else None, slice(None),
             gate_p_ref[...] if has_post else None,
             shift_p_ref[...] if has_pre else None,
             scale_p_ref[...] if has_pre else None)

    @pl.when(t == n_prompt_tiles)
    def _():
        body(xs_ref, xos_ref if has_post else None, slice(0, SAMPLE_ROWS),
             gate_s_ref[...] if has_post else None,
             shift_s_ref[...] if has_pre else None,
             scale_s_ref[...] if has_pre else None)


def _norm_step(x, y, mod, mod4, norm_post, norm_pre, post_layer, pre_layer, x_sample=None, split_out=False):
    has_post = post_layer is not None
    has_pre = pre_layer is not None
    split_in = x_sample is not None
    te = TE_NORM
    npt = M_PROMPT // te
    tiles_per_batch = SEQ // te
    row_spec = pl.BlockSpec((te, D_MODEL), lambda t: (t, 0))
    prompt_row_spec = pl.BlockSpec((te, D_MODEL), lambda t: (jnp.minimum(t, npt - 1), 0))
    sample_row_spec = pl.BlockSpec((SAMPLE_ROWS, D_MODEL), lambda t: (0, 0))

    def mod_p_spec(layer, part):
        return pl.BlockSpec((None, None, 1, D_MODEL),
                            lambda t: (layer, jnp.minimum(t // tiles_per_batch, BATCH - 1), 0, part))

    def mod_s_spec(layer, part):
        return pl.BlockSpec((None, SAMPLE_ROWS, D_MODEL),
                            lambda t: (layer, C_SAMPLE_ROW0 // SAMPLE_ROWS, part))

    def gain_spec(layer):
        return pl.BlockSpec((None, 1, D_MODEL), lambda t: (layer, 0, 0))

    if split_in:
        args, in_specs = [x, x_sample], [prompt_row_spec, sample_row_spec]
    else:
        args, in_specs = [x], [row_spec]
    out_shapes, out_specs = [], []
    if has_post:
        args += [y, norm_post, mod4, mod]
        in_specs += [row_spec, gain_spec(post_layer), mod_p_spec(post_layer, 2), mod_s_spec(post_layer, 2)]
        if split_out:
            out_shapes += [jax.ShapeDtypeStruct((M_PROMPT, D_MODEL), F32),
                           jax.ShapeDtypeStruct((SAMPLE_ROWS, D_MODEL), F32)]
            out_specs += [prompt_row_spec, sample_row_spec]
        else:
            out_shapes.append(jax.ShapeDtypeStruct((M_ALL, D_MODEL), F32))
            out_specs.append(row_spec)
    if has_pre:
        args += [norm_pre, mod4, mod4, mod, mod]
        in_specs += [gain_spec(pre_layer), mod_p_spec(pre_layer, 0), mod_p_spec(pre_layer, 1),
                     mod_s_spec(pre_layer, 0), mod_s_spec(pre_layer, 1)]
        out_shapes.append(jax.ShapeDtypeStruct((M_ALL, D_MODEL), BF16))
        out_specs.append(row_spec)
    outs = pl.pallas_call(
        functools.partial(_norm_kernel, has_post=has_post, has_pre=has_pre, split_in=split_in,
                          split_out=split_out, n_prompt_tiles=npt),
        grid=(npt + 1,),
        in_specs=in_specs,
        out_specs=out_specs,
        out_shape=out_shapes,
        compiler_params=pltpu.CompilerParams(
            dimension_semantics=("arbitrary",), vmem_limit_bytes=VMEM_LIMIT),
        name="norm_step",
    )(*args)
    return outs


def _mm_kernel(a_ref, w_ref, o_ref, wb_ref, *, k_rows, slabs):
    @pl.when(pl.program_id(1) == 0)
    def _():
        _cast_rows_to_bf16(w_ref, wb_ref, k_rows)

    acc = jnp.dot(a_ref[...], wb_ref[...], preferred_element_type=F32).astype(o_ref.dtype)
    if slabs:
        for c in range(o_ref.shape[0]):
            o_ref[c] = acc[:, c * HEAD_DIM:(c + 1) * HEAD_DIM]
    else:
        o_ref[...] = acc


def _matmul(a, w, layer, n_out, tn, out_dtype, name, slabs=False):
    m, k = a.shape
    tm = TM_MATMUL
    if slabs:
        out_spec = pl.BlockSpec((tn // HEAD_DIM, tm, HEAD_DIM), lambda j, i: (j, i, 0))
        out_shape = jax.ShapeDtypeStruct((n_out // HEAD_DIM, m, HEAD_DIM), out_dtype)
    else:
        out_spec = pl.BlockSpec((tm, tn), lambda j, i: (i, j))
        out_shape = jax.ShapeDtypeStruct((m, n_out), out_dtype)
    return pl.pallas_call(
        functools.partial(_mm_kernel, k_rows=k, slabs=slabs),
        grid=(n_out // tn, m // tm),
        in_specs=[pl.BlockSpec((tm, k), lambda j, i: (i, 0)),
                  pl.BlockSpec((None, k, tn), lambda j, i: (layer, 0, j))],
        out_specs=out_spec,
        out_shape=out_shape,
        scratch_shapes=[pltpu.VMEM((k, tn), BF16)],
        compiler_params=pltpu.CompilerParams(
            dimension_semantics=("arbitrary", "arbitrary"), vmem_limit_bytes=VMEM_LIMIT),
        name=name,
    )(a, w)


def _pool_kernel(u_ref, halo_ref, z_ref, st_ref, w_ref, sc_ref,
                 a_ref, pp_ref, ps_ref, wb_ref, buf_a, buf_b, *, n_prompt_tiles, tiles_per_batch):
    g = pl.program_id(0)
    t = pl.program_id(1)
    tp = TP_POOL
    h0 = POOL_HALO

    @pl.when(t == 0)
    def _():
        _cast_rows_to_bf16(w_ref, wb_ref, POOL_GROUP)

    def finish(r, z, rows):
        y = jnp.dot(r.astype(BF16), wb_ref[...], preferred_element_type=F32) * sc_ref[...]
        a_ref[rows, :] = (y * _silu(z)).astype(BF16)

    def prompt_tile(n_steps):
        w = 2 ** n_steps
        first = (t % tiles_per_batch) == 0
        buf_a[0:8, :] = jnp.zeros((8, POOL_GROUP), F32)
        buf_b[0:8, :] = jnp.zeros((8, POOL_GROUP), F32)
        buf_a[8:h0, :] = jnp.where(first, 0.0, halo_ref[...])
        buf_a[h0:h0 + tp, :] = u_ref[...]
        src, dst = buf_a, buf_b
        n = tp + h0 - 8
        for s in range(n_steps):
            sh = 2 ** s
            dst[8:8 + n, :] = src[8:8 + n, :] + src[8 - sh:8 - sh + n, :]
            src, dst = dst, src
        pos = (t % tiles_per_batch) * tp + lax.broadcasted_iota(jnp.int32, (tp, 1), 0)
        inv_cnt = 1.0 / jnp.minimum(pos + 1, w).astype(F32)
        u = u_ref[...]
        r = src[h0:h0 + tp, :] * inv_cnt - u
        finish(r, z_ref[...], slice(None))

        @pl.when((t % tiles_per_batch) == tiles_per_batch - 1)
        def _():
            pp_ref[...] = u_ref[tp - POOL_BUF:tp, :]

    def sample_tile(n_steps):
        w = 2 ** n_steps
        u_new = u_ref[0:DEC_BATCH, :]
        acc = u_new
        for k in range(1, w):
            acc = acc + st_ref[POOL_BUF - k]
        r = acc / float(w) - u_new
        buf_a[0:DEC_BATCH, :] = r
        buf_a[DEC_BATCH:SAMPLE_ROWS, :] = jnp.zeros((SAMPLE_ROWS - DEC_BATCH, POOL_GROUP), F32)
        finish(buf_a[0:SAMPLE_ROWS, :], z_ref[0:SAMPLE_ROWS, :], slice(0, SAMPLE_ROWS))
        for k in range(POOL_BUF - 1):
            ps_ref[k] = st_ref[k + 1]
        ps_ref[POOL_BUF - 1] = u_new

    for gi in range(len(POOL_WINDOWS)):
        @pl.when((g == gi) & (t < n_prompt_tiles))
        def _(gi=gi):
            prompt_tile(gi + 1)

        @pl.when((g == gi) & (t == n_prompt_tiles))
        def _(gi=gi):
            sample_tile(gi + 1)


def _pool_mix(proj, state_t, w_grp, scale, layer):
    tp = TP_POOL
    npt = M_PROMPT // tp
    tpb = SEQ // tp
    ng = len(POOL_WINDOWS)
    halo_blocks = tp // 16
    outs = pl.pallas_call(
        functools.partial(_pool_kernel, n_prompt_tiles=npt, tiles_per_batch=tpb),
        grid=(ng, npt + 1),
        in_specs=[
            pl.BlockSpec((tp, POOL_GROUP), lambda g, t: (t, g)),
            pl.BlockSpec((16, POOL_GROUP), lambda g, t: (jnp.maximum(t * halo_blocks - 1, 0), g)),
            pl.BlockSpec((tp, POOL_GROUP), lambda g, t: (t, ng + g)),
            pl.BlockSpec((None, POOL_BUF, DEC_BATCH, POOL_GROUP), lambda g, t: (layer, 0, 0, g)),
            pl.BlockSpec((None, None, POOL_GROUP, POOL_GROUP), lambda g, t: (layer, g, 0, 0)),
            pl.BlockSpec((None, 1, POOL_GROUP), lambda g, t: (layer, 0, g)),
        ],
        out_specs=[
            pl.BlockSpec((tp, POOL_GROUP), lambda g, t: (t, g)),
            pl.BlockSpec((None, POOL_BUF, POOL_GROUP),
                         lambda g, t: (jnp.minimum(t // tpb, BATCH - 1), 0, g)),
            pl.BlockSpec((POOL_BUF, DEC_BATCH, POOL_GROUP), lambda g, t: (0, 0, g)),
        ],
        out_shape=[
            jax.ShapeDtypeStruct((M_ALL, POOL_WIDTH), BF16),
            jax.ShapeDtypeStruct((BATCH, POOL_BUF, POOL_WIDTH), F32),
            jax.ShapeDtypeStruct((POOL_BUF, DEC_BATCH, POOL_WIDTH), F32),
        ],
        scratch_shapes=[pltpu.VMEM((POOL_GROUP, POOL_GROUP), BF16),
                        pltpu.VMEM((tp + POOL_HALO, POOL_GROUP), F32),
                        pltpu.VMEM((tp + POOL_HALO, POOL_GROUP), F32)],
        compiler_params=pltpu.CompilerParams(
            dimension_semantics=("arbitrary", "arbitrary"), vmem_limit_bytes=VMEM_LIMIT),
        name="pool_mix",
    )(proj, proj, proj, state_t, w_grp, scale.reshape(-1, 1, POOL_WIDTH))
    return outs


def _att_prompt_kernel(q0, k0, v0, q1, k1, v1, q2, k2, v2, z_ref, b0, b1, b2,
                       a_ref, qd, kd, vd, od, ld, on, ln, tmp):
    q_refs, k_refs, v_refs = (q0, q1, q2), (k0, k1, k2), (v0, v1, v2)
    bias_refs = (b0, b1, b2)
    nu = ATT_UNITS

    for g, dil in enumerate(DILATIONS):
        n = SEQ // dil
        nb = n // BAND
        zero_blk = jnp.zeros((BAND, HEAD_DIM), BF16)
        kd[g, 0:BAND, :] = zero_blk
        vd[g, 0:BAND, :] = zero_blk
        for src, dst, off in ((q_refs[g], qd, 0), (k_refs[g], kd, BAND), (v_refs[g], vd, BAND)):
            if dil == 1:
                def cp(i, c, src=src, dst=dst, g=g, off=off):
                    r = pl.multiple_of(i * 256, 256)
                    dst[g, pl.ds(off + r, 256), :] = src[pl.ds(r, 256), :].astype(BF16)
                    return c
                lax.fori_loop(0, SEQ // 256, cp, 0)
            else:
                for r in range(dil):
                    dst[g, off + r * n:off + (r + 1) * n, :] = src[pl.ds(r, n, stride=dil), :].astype(BF16)

    def unit_batch(g, u0, seq_blocks):
        with_prev = seq_blocks > 1
        rows = nu * BAND
        row = pl.multiple_of(u0 * BAND, rows)
        shape3 = (nu, BAND, HEAD_DIM)
        q = qd[g, pl.ds(row, rows), :].reshape(shape3)
        kc = kd[g, pl.ds(row + BAND, rows), :].reshape(shape3)
        vc = vd[g, pl.ds(row + BAND, rows), :].reshape(shape3)
        bias = bias_refs[g]
        s_c = jnp.einsum('uqe,uke->uqk', q, kc, preferred_element_type=F32) * ATT_SCALE \
            + bias[:, BAND:2 * BAND][None]
        if with_prev:
            kp = kd[g, pl.ds(row, rows), :].reshape(shape3)
            vp = vd[g, pl.ds(row, rows), :].reshape(shape3)
            bias_p = jnp.broadcast_to(bias[:, 0:BAND][None], (nu, BAND, BAND))
            blk = u0 + lax.broadcasted_iota(jnp.int32, (nu, BAND, BAND), 0)
            bias_p = jnp.where((blk & (seq_blocks - 1)) == 0, NEG_INF, bias_p)
            s_p = jnp.einsum('uqe,uke->uqk', q, kp, preferred_element_type=F32) * ATT_SCALE + bias_p
            m = jnp.max(jnp.maximum(s_c, s_p), axis=-1, keepdims=True)
            p_c = jnp.exp(s_c - m)
            p_p = jnp.exp(s_p - m)
            l = jnp.sum(p_c + p_p, axis=-1, keepdims=True)
            acc = jnp.einsum('uqk,uke->uqe', p_c.astype(BF16), vc, preferred_element_type=F32) \
                + jnp.einsum('uqk,uke->uqe', p_p.astype(BF16), vp, preferred_element_type=F32)
        else:
            m = jnp.max(s_c, axis=-1, keepdims=True)
            p_c = jnp.exp(s_c - m)
            l = jnp.sum(p_c, axis=-1, keepdims=True)
            acc = jnp.einsum('uqk,uke->uqe', p_c.astype(BF16), vc, preferred_element_type=F32)
        od[g, pl.ds(row, rows), :] = (acc / l).reshape(rows, HEAD_DIM)
        ld[g, pl.ds(row, rows), :] = jnp.broadcast_to(m + jnp.log(l), shape3).reshape(rows, HEAD_DIM)

    for g, dil in enumerate(DILATIONS):
        nb = (SEQ // dil) // BAND

        def batch_body(i, c, g=g, nb=nb):
            unit_batch(g, i * nu, nb)
            return c
        lax.fori_loop(0, N_BLOCKS // nu, batch_body, 0)

    def interleave4(src, dst, n_rows):
        quarter = n_rows // 4
        for r in range(4):
            dst[pl.ds(r, quarter, stride=4), :] = src[r * quarter:(r + 1) * quarter, :]

    for src_all, dst_all in ((od, on), (ld, ln)):
        interleave4(src_all.at[1], dst_all.at[0], SEQ)
        for r_lo in range(4):
            for r_hi in range(4):
                r = r_lo + 4 * r_hi
                tmp[pl.ds(r_lo * 512 + r_hi, BAND, stride=4), :] = src_all[2, r * BAND:(r + 1) * BAND, :]
        interleave4(tmp, dst_all.at[1], SEQ)

    def comb(i, c):
        rs = pl.ds(pl.multiple_of(i * 256, 256), 256)
        l0, l1, l2 = ld[0, rs, :], ln[0, rs, :], ln[1, rs, :]
        mx = jnp.maximum(jnp.maximum(l0, l1), l2)
        w0, w1, w2 = jnp.exp(l0 - mx), jnp.exp(l1 - mx), jnp.exp(l2 - mx)
        o = (w0 * od[0, rs, :] + w1 * on[0, rs, :] + w2 * on[1, rs, :]) / (w0 + w1 + w2)
        a_ref[rs, :] = (o * _silu(z_ref[rs, :])).astype(BF16)
        return c
    lax.fori_loop(0, SEQ // 256, comb, 0)


def _att_prompt(proj, bias_mats):
    hb = N_HEADS
    in_specs = []
    for g in range(N_GROUPS):
        for part in range(3):
            off = part * N_GROUPS * hb + g * hb
            in_specs.append(pl.BlockSpec((None, SEQ, HEAD_DIM), lambda b, h, off=off: (off + h, b, 0)))
    in_specs.append(pl.BlockSpec((None, SEQ, HEAD_DIM), lambda b, h: (9 * hb + h, b, 0)))
    for g in range(N_GROUPS):
        in_specs.append(pl.BlockSpec((None, BAND, 2 * BAND), lambda b, h, g=g: (g * hb + h, 0, 0)))
    scratch = [pltpu.VMEM((N_GROUPS, SEQ, HEAD_DIM), BF16)] + \
              [pltpu.VMEM((N_GROUPS, SEQ + BAND, HEAD_DIM), BF16)] * 2 + \
              [pltpu.VMEM((N_GROUPS, SEQ, HEAD_DIM), F32)] * 2 + \
              [pltpu.VMEM((N_GROUPS - 1, SEQ, HEAD_DIM), F32)] * 2 + \
              [pltpu.VMEM((SEQ, HEAD_DIM), F32)]
    return pl.pallas_call(
        _att_prompt_kernel,
        grid=(BATCH, N_HEADS),
        in_specs=in_specs,
        out_specs=pl.BlockSpec((SEQ, HEAD_DIM), lambda b, h: (b, h)),
        out_shape=jax.ShapeDtypeStruct((M_ALL, D_MODEL), BF16),
        scratch_shapes=scratch,
        compiler_params=pltpu.CompilerParams(
            dimension_semantics=("arbitrary", "arbitrary"), vmem_limit_bytes=VMEM_LIMIT),
        name="att_prompt",
    )(*([proj] * 10), bias_mats, bias_mats, bias_mats)


def _kv_rows_kernel(src_a, src_b, o_ref, pad_ref, *, tr):
    pitch = tr + 8

    def emit(src):
        for h in range(N_HEADS):
            pad_ref[h * pitch:h * pitch + tr, :] = src[h]

        def body(i, c):
            t0 = i * 8
            for k in range(8):
                o_ref[t0 + k] = pad_ref[pl.ds(t0 + k, N_HEADS, stride=pitch), :]
            return c
        lax.fori_loop(0, tr // 8, body, 0)

    @pl.when(pl.program_id(0) == 0)
    def _():
        emit(src_a)

    @pl.when(pl.program_id(0) == 1)
    def _():
        emit(src_b)


def _kv_rows(proj_a, proj_b, g):
    keep = WINDOWS[g]
    tr = 256 if keep >= 256 else keep
    nt = keep // tr
    first_blk = (SEQ - keep) // tr
    blks_per_batch = SEQ // tr

    def row_blk(b, t):
        return b * blks_per_batch + first_blk + t

    def col_blk(part):
        return (1 + part) * N_GROUPS + g

    def spec(layer):
        parked = (row_blk(BATCH - 1, nt - 1), col_blk(1)) if layer == 0 else (row_blk(0, 0), col_blk(0))
        return pl.BlockSpec((N_HEADS, tr, HEAD_DIM),
                            lambda l, b, t, p: (jnp.where(l == layer, col_blk(p), parked[1]),
                                                jnp.where(l == layer, row_blk(b, t), parked[0]), 0))

    return pl.pallas_call(
        functools.partial(_kv_rows_kernel, tr=tr),
        grid=(2, BATCH, nt, 2),
        in_specs=[spec(0), spec(1)],
        out_specs=pl.BlockSpec((None, None, tr, None, N_HEADS, HEAD_DIM), lambda l, b, t, p: (l, b, t, p, 0, 0)),
        out_shape=jax.ShapeDtypeStruct((2, BATCH, keep, 2, N_HEADS, HEAD_DIM), F32),
        scratch_shapes=[pltpu.VMEM((N_HEADS * (tr + 8), HEAD_DIM), F32)],
        compiler_params=pltpu.CompilerParams(
            dimension_semantics=("arbitrary",) * 4, vmem_limit_bytes=VMEM_LIMIT),
        name="kv_rows",
    )(proj_a, proj_b)


def _att_sample_kernel(x_ref, c0, c1, c2, bias_ref, a_ref, kv0, kv1, kv2):
    caches = (c0, c1, c2)
    kv_outs = (kv0, kv1, kv2)
    hb = N_HEADS
    outs, lses = [], []
    for g in range(N_GROUPS):
        q = x_ref[g * hb:(g + 1) * hb, :]
        kn = x_ref[(3 + g) * hb:(4 + g) * hb, :]
        vn = x_ref[(6 + g) * hb:(7 + g) * hb, :]
        kv_outs[g][0] = kn
        kv_outs[g][1] = vn
        kc = caches[g][:, 0]
        vc = caches[g][:, 1]
        s_c = jnp.sum(kc * q[None], axis=-1, keepdims=True) * ATT_SCALE + bias_ref[g, 0:BAND]
        s_n = jnp.sum(kn * q, axis=-1, keepdims=True) * ATT_SCALE + bias_ref[g, BAND]
        m = jnp.maximum(jnp.max(s_c, axis=0), s_n)
        p_c = jnp.exp(s_c - m[None])
        p_n = jnp.exp(s_n - m)
        l = jnp.sum(p_c, axis=0) + p_n
        o = (jnp.sum(p_c * vc, axis=0) + p_n * vn) / l
        outs.append(o)
        lses.append(m + jnp.log(l))
    mx = jnp.maximum(jnp.maximum(lses[0], lses[1]), lses[2])
    ws = [jnp.exp(ls - mx) for ls in lses]
    o = (ws[0] * outs[0] + ws[1] * outs[1] + ws[2] * outs[2]) / (ws[0] + ws[1] + ws[2])
    z = x_ref[9 * hb:10 * hb, :]
    a_ref[...] = o * _silu(z)


def _att_sample(proj_s, caches, bias_s, layer):
    hb = N_HEADS
    in_specs = [pl.BlockSpec((None, ATT_IN_COLS // HEAD_DIM, HEAD_DIM), lambda b: (b, 0, 0))]
    for g in range(N_GROUPS):
        in_specs.append(pl.BlockSpec((None, None, BAND, None, 2, hb, HEAD_DIM),
                                     lambda b: (layer, b, 0, 0, 0, 0, 0)))
    in_specs.append(pl.BlockSpec((N_GROUPS, BAND + 1, hb, 1), lambda b: (0, 0, 0, 0)))
    out_specs = [pl.BlockSpec((None, hb, HEAD_DIM), lambda b: (b, 0, 0))]
    out_shapes = [jax.ShapeDtypeStruct((DEC_BATCH, hb, HEAD_DIM), F32)]
    for g in range(N_GROUPS):
        out_specs.append(pl.BlockSpec((None, 2, hb, HEAD_DIM), lambda b: (b, 0, 0, 0)))
        out_shapes.append(jax.ShapeDtypeStruct((DEC_BATCH, 2, hb, HEAD_DIM), F32))
    return pl.pallas_call(
        _att_sample_kernel,
        grid=(DEC_BATCH,),
        in_specs=in_specs,
        out_specs=out_specs,
        out_shape=out_shapes,
        compiler_params=pltpu.CompilerParams(
            dimension_semantics=("arbitrary",), vmem_limit_bytes=VMEM_LIMIT),
        name="att_sample",
    )(proj_s, *caches, bias_s)


def kernel(x_prompt, x_sample, c_prompt, c_sample, cache_kv0, cache_kv1, cache_kv2, state_pool,
           norm_pre, norm_post, ada_w, ada_b, t5_bias, pool_w_in, pool_w_grp, pool_scale,
           pool_w_out, att_w_in, att_w_out):
    n_att = DEPTH // 2
    xp0 = x_prompt.reshape(M_PROMPT, D_MODEL)
    xs0 = jnp.zeros((SAMPLE_ROWS, D_MODEL), F32).at[0:DEC_BATCH].set(x_sample.reshape(DEC_BATCH, D_MODEL))
    c_all = jnp.zeros((C_ROWS, D_MODEL), F32)
    c_all = c_all.at[0:BATCH].set(c_prompt).at[C_SAMPLE_ROW0:C_SAMPLE_ROW0 + DEC_BATCH].set(c_sample)

    mod = _ada_all(c_all, ada_w, ada_b)
    mod4 = mod.reshape(DEPTH, C_ROWS, 1, 3 * D_MODEL)
    gains_pre = norm_pre.reshape(DEPTH, 1, D_MODEL)
    gains_post = norm_post.reshape(DEPTH, 1, D_MODEL)

    bias_mats = _bias_matrices(t5_bias)
    bias_s = bias_mats[:, 0, 0:BAND + 1].reshape(N_GROUPS, N_HEADS, BAND + 1)
    bias_s = jnp.transpose(bias_s, (0, 2, 1))[..., None]
    caches = [c.reshape(n_att, DEC_BATCH, BAND, dil, 2, N_HEADS, HEAD_DIM)
              for c, dil in zip((cache_kv0, cache_kv1, cache_kv2), DILATIONS)]
    state_t = jnp.transpose(state_pool, (0, 2, 1, 3))

    kv_s = [[] for _ in range(N_GROUPS)]
    pool_p, pool_s, att_projs = [], [], []

    (h,) = _norm_step(xp0, None, mod, mod4, gains_post, gains_pre, None, 0, x_sample=xs0)
    x = None
    for i in range(DEPTH):
        li = i // 2
        if i % 2 == 0:
            proj = _matmul(h, pool_w_in, li, 2 * POOL_WIDTH, 1024, F32, "pool_in_proj")
            a, pp, ps = _pool_mix(proj, state_t, pool_w_grp, pool_scale, li)
            pool_p.append(pp)
            pool_s.append(jnp.transpose(ps, (1, 0, 2)))
            y = _matmul(a, pool_w_out, li, D_MODEL, 512, F32, "pool_out_proj")
        else:
            proj = _matmul(h, att_w_in, li, ATT_IN_COLS, 1024, F32, "att_in_proj", slabs=True)
            att_projs.append(proj)
            a = _att_prompt(proj, bias_mats)
            proj_s = jnp.transpose(proj[:, M_PROMPT:M_PROMPT + DEC_BATCH, :], (1, 0, 2))
            souts = _att_sample(proj_s, caches, bias_s, li)
            a_s = jnp.zeros((SAMPLE_ROWS, D_MODEL), F32).at[0:DEC_BATCH].set(souts[0].reshape(DEC_BATCH, D_MODEL))
            a = lax.dynamic_update_slice(a, a_s.astype(BF16), (M_PROMPT, 0))
            for g in range(N_GROUPS):
                kv_s[g].append(souts[1 + g].reshape(DEC_BATCH, 1, 2, N_HEADS, HEAD_DIM))
            y = _matmul(a, att_w_out, li, D_MODEL, 512, F32, "att_out_proj")
        last = i + 1 == DEPTH
        if i == 0:
            x, h = _norm_step(xp0, y, mod, mod4, gains_post, gains_pre, i, i + 1, x_sample=xs0)
        elif not last:
            x, h = _norm_step(x, y, mod, mod4, gains_post, gains_pre, i, i + 1)
        else:
            y_p, y_s = _norm_step(x, y, mod, mod4, gains_post, gains_pre, i, None, split_out=True)

    kv_p = [_kv_rows(att_projs[0], att_projs[1], g) for g in range(N_GROUPS)]
    y_prompt = y_p.reshape(BATCH, SEQ, D_MODEL)
    y_sample = y_s[0:DEC_BATCH].reshape(DEC_BATCH, 1, D_MODEL)
    return (y_prompt, y_sample, kv_p[0], kv_p[1], kv_p[2], jnp.stack(pool_p),
            jnp.stack(kv_s[0]), jnp.stack(kv_s[1]), jnp.stack(kv_s[2]), jnp.stack(pool_s))
```

```python
import functools

import numpy as np
import jax
import jax.numpy as jnp
from jax import lax
from jax.experimental import pallas as pl
from jax.experimental.pallas import tpu as pltpu

D_MODEL = 2048
BATCH = 4
SEQ = 2048
DEPTH = 4
DEC_BATCH = 8
HEAD_DIM = 128
N_HEADS = 16
DILATIONS = (1, 4, 16)
WINDOWS = (128, 512, 2048)
N_GROUPS = 3
QKV_WIDTH = N_GROUPS * D_MODEL
ATT_IN_COLS = 3 * QKV_WIDTH + D_MODEL
BAND = 128
ATT_SCALE = HEAD_DIM ** -0.5
POOL_WINDOWS = (2, 4, 8, 16)
POOL_WIDTH = 2 * D_MODEL
POOL_GROUP = POOL_WIDTH // 4
POOL_BUF = 15
N_BUCKETS = 32
T5_MAX_DIST = 2048
RMS_EPS = 1e-6
NEG_INF = -1e30

M_PROMPT = BATCH * SEQ
M_ALL = M_PROMPT + 16
SAMPLE_ROWS = 16
C_ROWS = 32
C_SAMPLE_ROW0 = 16

TM_MATMUL = 912
TE_NORM = 256
TP_POOL = 512
POOL_HALO = 24
ATT_UNITS = 16
N_BLOCKS = SEQ // BAND
VMEM_LIMIT = 56 * 1024 * 1024

F32 = jnp.float32
BF16 = jnp.bfloat16


def _silu(x):
    return x * (1.0 / (1.0 + jnp.exp(-x)))


def _cast_rows_to_bf16(src_ref, dst_ref, rows, chunk=256):
    def body(i, c):
        r = pl.multiple_of(i * chunk, chunk)
        dst_ref[pl.ds(r, chunk), :] = src_ref[pl.ds(r, chunk), :].astype(BF16)
        return c
    lax.fori_loop(0, rows // chunk, body, 0)


def _ada_kernel(c_ref, w_ref, b_ref, o_ref):
    a = _silu(c_ref[...]).astype(BF16)
    kc = 512
    acc = jnp.zeros(o_ref.shape, F32)
    for k0 in range(0, D_MODEL, kc):
        acc = acc + jnp.dot(a[:, k0:k0 + kc], w_ref[k0:k0 + kc, :].astype(BF16),
                            preferred_element_type=F32)
    o_ref[...] = acc + b_ref[...]


def _ada_all(c_all, ada_w, ada_b):
    tn = 1024
    n = 3 * D_MODEL
    return pl.pallas_call(
        _ada_kernel,
        grid=(DEPTH, n // tn),
        in_specs=[pl.BlockSpec((C_ROWS, D_MODEL), lambda l, j: (0, 0)),
                  pl.BlockSpec((None, D_MODEL, tn), lambda l, j: (l, 0, j)),
                  pl.BlockSpec((None, 1, tn), lambda l, j: (l, 0, j))],
        out_specs=pl.BlockSpec((None, C_ROWS, tn), lambda l, j: (l, 0, j)),
        out_shape=jax.ShapeDtypeStruct((DEPTH, C_ROWS, n), F32),
        compiler_params=pltpu.CompilerParams(
            dimension_semantics=("arbitrary", "arbitrary"), vmem_limit_bytes=VMEM_LIMIT),
        name="ada_mod",
    )(c_all, ada_w, ada_b.reshape(DEPTH, 1, n))


def _t5_bucket(dist):
    dist = np.asarray(dist, dtype=np.int64)
    max_exact = N_BUCKETS // 2
    ratio = np.log(np.maximum(dist, 1) / max_exact) / np.log(T5_MAX_DIST / max_exact)
    large = np.minimum(max_exact + (ratio * (N_BUCKETS - max_exact)).astype(np.int64), N_BUCKETS - 1)
    return np.where(dist < max_exact, dist, large).astype(np.int32)


def _bucket_index_table():
    rel = np.arange(BAND)[:, None] + BAND - np.arange(2 * BAND)[None, :]
    inband = (rel >= 0) & (rel <= BAND)
    out = []
    for dil in DILATIONS:
        bucket = _t5_bucket(np.clip(rel, 0, BAND) * dil)
        out.append(np.where(inband, bucket, -1))
    return np.stack(out).astype(np.int32)


def _bias_kernel(tab_ref, idx_ref, o_ref):
    g = pl.program_id(0)
    idx = idx_ref[...]

    def head(h, c):
        acc = jnp.full(idx.shape, NEG_INF, F32)
        for b in range(N_BUCKETS):
            acc = jnp.where(idx == b, tab_ref[b, g * N_HEADS + h], acc)
        o_ref[h] = acc
        return c
    lax.fori_loop(0, N_HEADS, head, 0)


def _bias_matrices(t5_bias):
    idx = jnp.asarray(_bucket_index_table())
    n_sub = N_GROUPS * N_HEADS
    return pl.pallas_call(
        _bias_kernel,
        grid=(N_GROUPS,),
        in_specs=[pl.BlockSpec(memory_space=pltpu.SMEM),
                  pl.BlockSpec((None, BAND, 2 * BAND), lambda g: (g, 0, 0))],
        out_specs=pl.BlockSpec((N_HEADS, BAND, 2 * BAND), lambda g: (g, 0, 0)),
        out_shape=jax.ShapeDtypeStruct((n_sub, BAND, 2 * BAND), F32),
        name="t5_bias_mats",
    )(t5_bias, idx)


def _norm_kernel(*refs, has_post, has_pre, split_in, split_out, n_prompt_tiles):
    refs = list(refs)
    xp_ref = refs.pop(0)
    xs_ref = refs.pop(0) if split_in else xp_ref
    if has_post:
        y_ref, gpost_ref, gate_p_ref, gate_s_ref = refs[:4]
        refs = refs[4:]
    if has_pre:
        gpre_ref, shift_p_ref, scale_p_ref, shift_s_ref, scale_s_ref = refs[:5]
        refs = refs[5:]
    if has_post:
        xop_ref = refs.pop(0)
        xos_ref = refs.pop(0) if split_out else xop_ref
    if has_pre:
        h_ref = refs.pop(0)

    def rms(v, g):
        return v * lax.rsqrt(jnp.mean(v * v, axis=-1, keepdims=True) + RMS_EPS) * g

    def body(x_ref, xo_ref, rows, gate, shift, scale):
        x = x_ref[rows, :]
        if has_post:
            x = x + gate * rms(y_ref[rows, :], gpost_ref[...])
            xo_ref[rows, :] = x
        if has_pre:
            h = rms(x, gpre_ref[...]) * (1.0 + scale) + shift
            h_ref[rows, :] = h.astype(BF16)

    t = pl.program_id(0)

    @pl.when(t < n_prompt_tiles)
    def _():
        body(xp_ref, xop_ref if has_post else None, slice(None),
             gate_p_ref[...] if has_post else None,
             shift_p_ref[...] if has_pre else None,
             scale_p_ref[...] if has_pre else None)

    @pl.when(t == n_prompt_tiles)
    def _():
        body(xs_ref, xos_ref if has_post else None, slice(0, SAMPLE_ROWS),
             gate_s_ref[...] if has_post else None,
             shift_s_ref[...] if has_pre else None,
             scale_s_ref[...] if has_pre else None)


def _norm_step(x, y, mod, mod4, norm_post, norm_pre, post_layer, pre_layer, x_sample=None, split_out=False):
    has_post = post_layer is not None
    has_pre = pre_layer is not None
    split_in = x_sample is not None
    te = TE_NORM
    npt = M_PROMPT // te
    tiles_per_batch = SEQ // te
    row_spec = pl.BlockSpec((te, D_MODEL), lambda t: (t, 0))
    prompt_row_spec = pl.BlockSpec((te, D_MODEL), lambda t: (jnp.minimum(t, npt - 1), 0))
    sample_row_spec = pl.BlockSpec((SAMPLE_ROWS, D_MODEL), lambda t: (0, 0))

    def mod_p_spec(layer, part):
        return pl.BlockSpec((None, None, 1, D_MODEL),
                            lambda t: (layer, jnp.minimum(t // tiles_per_batch, BATCH - 1), 0, part))

    def mod_s_spec(layer, part):
        return pl.BlockSpec((None, SAMPLE_ROWS, D_MODEL),
                            lambda t: (layer, C_SAMPLE_ROW0 // SAMPLE_ROWS, part))

    def gain_spec(layer):
        return pl.BlockSpec((None, 1, D_MODEL), lambda t: (layer, 0, 0))

    if split_in:
        args, in_specs = [x, x_sample], [prompt_row_spec, sample_row_spec]
    else:
        args, in_specs = [x], [row_spec]
    out_shapes, out_specs = [], []
    if has_post:
        args += [y, norm_post, mod4, mod]
        in_specs += [row_spec, gain_spec(post_layer), mod_p_spec(post_layer, 2), mod_s_spec(post_layer, 2)]
        if split_out:
            out_shapes += [jax.ShapeDtypeStruct((M_PROMPT, D_MODEL), F32),
                           jax.ShapeDtypeStruct((SAMPLE_ROWS, D_MODEL), F32)]
            out_specs += [prompt_row_spec, sample_row_spec]
        else:
            out_shapes.append(jax.ShapeDtypeStruct((M_ALL, D_MODEL), F32))
            out_specs.append(row_spec)
    if has_pre:
        args += [norm_pre, mod4, mod4, mod, mod]
        in_specs += [gain_spec(pre_layer), mod_p_spec(pre_layer, 0), mod_p_spec(pre_layer, 1),
                     mod_s_spec(pre_layer, 0), mod_s_spec(pre_layer, 1)]
        out_shapes.append(jax.ShapeDtypeStruct((M_ALL, D_MODEL), BF16))
        out_specs.append(row_spec)
    outs = pl.pallas_call(
        functools.partial(_norm_kernel, has_post=has_post, has_pre=has_pre, split_in=split_in,
                          split_out=split_out, n_prompt_tiles=npt),
        grid=(npt + 1,),
        in_specs=in_specs,
        out_specs=out_specs,
        out_shape=out_shapes,
        compiler_params=pltpu.CompilerParams(
            dimension_semantics=("arbitrary",), vmem_limit_bytes=VMEM_LIMIT),
        name="norm_step",
    )(*args)
    return outs


def _mm_kernel(a_ref, w_ref, o_ref, wb_ref, *, k_rows, slabs):
    @pl.when(pl.program_id(1) == 0)
    def _():
        _cast_rows_to_bf16(w_ref, wb_ref, k_rows)

    acc = jnp.dot(a_ref[...], wb_ref[...], preferred_element_type=F32).astype(o_ref.dtype)
    if slabs:
        for c in range(o_ref.shape[0]):
            o_ref[c] = acc[:, c * HEAD_DIM:(c + 1) * HEAD_DIM]
    else:
        o_ref[...] = acc


def _matmul(a, w, layer, n_out, tn, out_dtype, name, slabs=False):
    m, k = a.shape
    tm = TM_MATMUL
    if slabs:
        out_spec = pl.BlockSpec((tn // HEAD_DIM, tm, HEAD_DIM), lambda j, i: (j, i, 0))
        out_shape = jax.ShapeDtypeStruct((n_out // HEAD_DIM, m, HEAD_DIM), out_dtype)
    else:
        out_spec = pl.BlockSpec((tm, tn), lambda j, i: (i, j))
        out_shape = jax.ShapeDtypeStruct((m, n_out), out_dtype)
    return pl.pallas_call(
        functools.partial(_mm_kernel, k_rows=k, slabs=slabs),
        grid=(n_out // tn, m // tm),
        in_specs=[pl.BlockSpec((tm, k), lambda j, i: (i, 0)),
                  pl.BlockSpec((None, k, tn), lambda j, i: (layer, 0, j))],
        out_specs=out_spec,
        out_shape=out_shape,
        scratch_shapes=[pltpu.VMEM((k, tn), BF16)],
        compiler_params=pltpu.CompilerParams(
            dimension_semantics=("arbitrary", "arbitrary"), vmem_limit_bytes=VMEM_LIMIT),
        name=name,
    )(a, w)


def _pool_kernel(u_ref, halo_ref, z_ref, st_ref, w_ref, sc_ref,
                 a_ref, pp_ref, ps_ref, wb_ref, buf_a, buf_b, *, n_prompt_tiles, tiles_per_batch):
    g = pl.program_id(0)
    t = pl.program_id(1)
    tp = TP_POOL
    h0 = POOL_HALO

    @pl.when(t == 0)
    def _():
        _cast_rows_to_bf16(w_ref, wb_ref, POOL_GROUP)

    def finish(r, z, rows):
        y = jnp.dot(r.astype(BF16), wb_ref[...], preferred_element_type=F32) * sc_ref[...]
        a_ref[rows, :] = (y * _silu(z)).astype(BF16)

    def prompt_tile(n_steps):
        w = 2 ** n_steps
        first = (t % tiles_per_batch) == 0
        buf_a[0:8, :] = jnp.zeros((8, POOL_GROUP), F32)
        buf_b[0:8, :] = jnp.zeros((8, POOL_GROUP), F32)
        buf_a[8:h0, :] = jnp.where(first, 0.0, halo_ref[...])
        buf_a[h0:h0 + tp, :] = u_ref[...]
        src, dst = buf_a, buf_b
        n = tp + h0 - 8
        for s in range(n_steps):
            sh = 2 ** s
            dst[8:8 + n, :] = src[8:8 + n, :] + src[8 - sh:8 - sh + n, :]
            src, dst = dst, src
        pos = (t % tiles_per_batch) * tp + lax.broadcasted_iota(jnp.int32, (tp, 1), 0)
        inv_cnt = 1.0 / jnp.minimum(pos + 1, w).astype(F32)
        u = u_ref[...]
        r = src[h0:h0 + tp, :] * inv_cnt - u
        finish(r, z_ref[...], slice(None))

        @pl.when((t % tiles_per_batch) == tiles_per_batch - 1)
        def _():
            pp_ref[...] = u_ref[tp - POOL_BUF:tp, :]

    def sample_tile(n_steps):
        w = 2 ** n_steps
        u_new = u_ref[0:DEC_BATCH, :]
        acc = u_new
        for k in range(1, w):
            acc = acc + st_ref[POOL_BUF - k]
        r = acc / float(w) - u_new
        buf_a[0:DEC_BATCH, :] = r
        buf_a[DEC_BATCH:SAMPLE_ROWS, :] = jnp.zeros((SAMPLE_ROWS - DEC_BATCH, POOL_GROUP), F32)
        finish(buf_a[0:SAMPLE_ROWS, :], z_ref[0:SAMPLE_ROWS, :], slice(0, SAMPLE_ROWS))
        for k in range(POOL_BUF - 1):
            ps_ref[k] = st_ref[k + 1]
        ps_ref[POOL_BUF - 1] = u_new

    for gi in range(len(POOL_WINDOWS)):
        @pl.when((g == gi) & (t < n_prompt_tiles))
        def _(gi=gi):
            prompt_tile(gi + 1)

        @pl.when((g == gi) & (t == n_prompt_tiles))
        def _(gi=gi):
            sample_tile(gi + 1)


def _pool_mix(proj, state_t, w_grp, scale, layer):
    tp = TP_POOL
    npt = M_PROMPT // tp
    tpb = SEQ // tp
    ng = len(POOL_WINDOWS)
    halo_blocks = tp // 16
    outs = pl.pallas_call(
        functools.partial(_pool_kernel, n_prompt_tiles=npt, tiles_per_batch=tpb),
        grid=(ng, npt + 1),
        in_specs=[
            pl.BlockSpec((tp, POOL_GROUP), lambda g, t: (t, g)),
            pl.BlockSpec((16, POOL_GROUP), lambda g, t: (jnp.maximum(t * halo_blocks - 1, 0), g)),
            pl.BlockSpec((tp, POOL_GROUP), lambda g, t: (t, ng + g)),
            pl.BlockSpec((None, POOL_BUF, DEC_BATCH, POOL_GROUP), lambda g, t: (layer, 0, 0, g)),
            pl.BlockSpec((None, None, POOL_GROUP, POOL_GROUP), lambda g, t: (layer, g, 0, 0)),
            pl.BlockSpec((None, 1, POOL_GROUP), lambda g, t: (layer, 0, g)),
        ],
        out_specs=[
            pl.BlockSpec((tp, POOL_GROUP), lambda g, t: (t, g)),
            pl.BlockSpec((None, POOL_BUF, POOL_GROUP),
                         lambda g, t: (jnp.minimum(t // tpb, BATCH - 1), 0, g)),
            pl.BlockSpec((POOL_BUF, DEC_BATCH, POOL_GROUP), lambda g, t: (0, 0, g)),
        ],
        out_shape=[
            jax.ShapeDtypeStruct((M_ALL, POOL_WIDTH), BF16),
            jax.ShapeDtypeStruct((BATCH, POOL_BUF, POOL_WIDTH), F32),
            jax.ShapeDtypeStruct((POOL_BUF, DEC_BATCH, POOL_WIDTH), F32),
        ],
        scratch_shapes=[pltpu.VMEM((POOL_GROUP, POOL_GROUP), BF16),
                        pltpu.VMEM((tp + POOL_HALO, POOL_GROUP), F32),
                        pltpu.VMEM((tp + POOL_HALO, POOL_GROUP), F32)],
        compiler_params=pltpu.CompilerParams(
            dimension_semantics=("arbitrary", "arbitrary"), vmem_limit_bytes=VMEM_LIMIT),
        name="pool_mix",
    )(proj, proj, proj, state_t, w_grp, scale.reshape(-1, 1, POOL_WIDTH))
    return outs


def _att_prompt_kernel(*refs):
    n_in = 13
    as_ref, a_ref = refs[n_in], refs[n_in + 1]

    @pl.when(pl.program_id(0) < BATCH)
    def _():
        _att_prompt_tile(*refs[:n_in], *refs[n_in + 1:])

    @pl.when(pl.program_id(0) == BATCH)
    def _():
        a_ref[0:SAMPLE_ROWS, :] = as_ref[...]


def _att_prompt_tile(q0, k0, v0, q1, k1, v1, q2, k2, v2, z_ref, b0, b1, b2,
                     a_ref, qd, kd, vd, od, ld, on, ln, tmp):
    q_refs, k_refs, v_refs = (q0, q1, q2), (k0, k1, k2), (v0, v1, v2)
    bias_refs = (b0, b1, b2)
    nu = ATT_UNITS

    for g, dil in enumerate(DILATIONS):
        n = SEQ // dil
        nb = n // BAND
        zero_blk = jnp.zeros((BAND, HEAD_DIM), BF16)
        kd[g, 0:BAND, :] = zero_blk
        vd[g, 0:BAND, :] = zero_blk
        for src, dst, off in ((q_refs[g], qd, 0), (k_refs[g], kd, BAND), (v_refs[g], vd, BAND)):
            if dil == 1:
                def cp(i, c, src=src, dst=dst, g=g, off=off):
                    r = pl.multiple_of(i * 256, 256)
                    dst[g, pl.ds(off + r, 256), :] = src[pl.ds(r, 256), :].astype(BF16)
                    return c
                lax.fori_loop(0, SEQ // 256, cp, 0)
            else:
                for r in range(dil):
                    dst[g, off + r * n:off + (r + 1) * n, :] = src[pl.ds(r, n, stride=dil), :].astype(BF16)

    def unit_batch(g, u0, seq_blocks):
        with_prev = seq_blocks > 1
        rows = nu * BAND
        row = pl.multiple_of(u0 * BAND, rows)
        shape3 = (nu, BAND, HEAD_DIM)
        q = qd[g, pl.ds(row, rows), :].reshape(shape3)
        kc = kd[g, pl.ds(row + BAND, rows), :].reshape(shape3)
        vc = vd[g, pl.ds(row + BAND, rows), :].reshape(shape3)
        bias = bias_refs[g]
        s_c = jnp.einsum('uqe,uke->uqk', q, kc, preferred_element_type=F32) * ATT_SCALE \
            + bias[:, BAND:2 * BAND][None]
        if with_prev:
            kp = kd[g, pl.ds(row, rows), :].reshape(shape3)
            vp = vd[g, pl.ds(row, rows), :].reshape(shape3)
            bias_p = jnp.broadcast_to(bias[:, 0:BAND][None], (nu, BAND, BAND))
            blk = u0 + lax.broadcasted_iota(jnp.int32, (nu, BAND, BAND), 0)
            bias_p = jnp.where((blk & (seq_blocks - 1)) == 0, NEG_INF, bias_p)
            s_p = jnp.einsum('uqe,uke->uqk', q, kp, preferred_element_type=F32) * ATT_SCALE + bias_p
            m = jnp.max(jnp.maximum(s_c, s_p), axis=-1, keepdims=True)
            p_c = jnp.exp(s_c - m)
            p_p = jnp.exp(s_p - m)
            l = jnp.sum(p_c + p_p, axis=-1, keepdims=True)
            acc = jnp.einsum('uqk,uke->uqe', p_c.astype(BF16), vc, preferred_element_type=F32) \
                + jnp.einsum('uqk,uke->uqe', p_p.astype(BF16), vp, preferred_element_type=F32)
        else:
            m = jnp.max(s_c, axis=-1, keepdims=True)
            p_c = jnp.exp(s_c - m)
            l = jnp.sum(p_c, axis=-1, keepdims=True)
            acc = jnp.einsum('uqk,uke->uqe', p_c.astype(BF16), vc, preferred_element_type=F32)
        od[g, pl.ds(row, rows), :] = (acc / l).reshape(rows, HEAD_DIM)
        ld[g, pl.ds(row, rows), :] = jnp.broadcast_to(m + jnp.log(l), shape3).reshape(rows, HEAD_DIM)

    for g, dil in enumerate(DILATIONS):
        nb = (SEQ // dil) // BAND

        def batch_body(i, c, g=g, nb=nb):
            unit_batch(g, i * nu, nb)
            return c
        lax.fori_loop(0, N_BLOCKS // nu, batch_body, 0)

    def interleave4(src, dst, n_rows):
        quarter = n_rows // 4
        for r in range(4):
            dst[pl.ds(r, quarter, stride=4), :] = src[r * quarter:(r + 1) * quarter, :]

    for src_all, dst_all in ((od, on), (ld, ln)):
        interleave4(src_all.at[1], dst_all.at[0], SEQ)
        for r_lo in range(4):
            for r_hi in range(4):
                r = r_lo + 4 * r_hi
                tmp[pl.ds(r_lo * 512 + r_hi, BAND, stride=4), :] = src_all[2, r * BAND:(r + 1) * BAND, :]
        interleave4(tmp, dst_all.at[1], SEQ)

    def comb(i, c):
        rs = pl.ds(pl.multiple_of(i * 256, 256), 256)
        l0, l1, l2 = ld[0, rs, :], ln[0, rs, :], ln[1, rs, :]
        mx = jnp.maximum(jnp.maximum(l0, l1), l2)
        w0, w1, w2 = jnp.exp(l0 - mx), jnp.exp(l1 - mx), jnp.exp(l2 - mx)
        o = (w0 * od[0, rs, :] + w1 * on[0, rs, :] + w2 * on[1, rs, :]) / (w0 + w1 + w2)
        a_ref[rs, :] = (o * _silu(z_ref[rs, :])).astype(BF16)
        return c
    lax.fori_loop(0, SEQ // 256, comb, 0)


def _att_prompt(proj, bias_mats, a_sample):
    hb = N_HEADS

    def head_blk(b, h):
        return jnp.where(b < BATCH, h, hb - 1)

    in_specs = []
    for g in range(N_GROUPS):
        for part in range(3):
            off = part * N_GROUPS * hb + g * hb
            in_specs.append(pl.BlockSpec(
                (None, SEQ, HEAD_DIM),
                lambda b, h, off=off: (off + head_blk(b, h), jnp.minimum(b, BATCH - 1), 0)))
    in_specs.append(pl.BlockSpec((None, SEQ, HEAD_DIM),
                                 lambda b, h: (9 * hb + head_blk(b, h), jnp.minimum(b, BATCH - 1), 0)))
    for g in range(N_GROUPS):
        in_specs.append(pl.BlockSpec((None, BAND, 2 * BAND), lambda b, h, g=g: (g * hb + head_blk(b, h), 0, 0)))
    in_specs.append(pl.BlockSpec((SAMPLE_ROWS, HEAD_DIM), lambda b, h: (0, h)))
    scratch = [pltpu.VMEM((N_GROUPS, SEQ, HEAD_DIM), BF16)] + \
              [pltpu.VMEM((N_GROUPS, SEQ + BAND, HEAD_DIM), BF16)] * 2 + \
              [pltpu.VMEM((N_GROUPS, SEQ, HEAD_DIM), F32)] * 2 + \
              [pltpu.VMEM((N_GROUPS - 1, SEQ, HEAD_DIM), F32)] * 2 + \
              [pltpu.VMEM((SEQ, HEAD_DIM), F32)]
    return pl.pallas_call(
        _att_prompt_kernel,
        grid=(BATCH + 1, N_HEADS),
        in_specs=in_specs,
        out_specs=pl.BlockSpec((SEQ, HEAD_DIM), lambda b, h: (b, h)),
        out_shape=jax.ShapeDtypeStruct((M_ALL, D_MODEL), BF16),
        scratch_shapes=scratch,
        compiler_params=pltpu.CompilerParams(
            dimension_semantics=("arbitrary", "arbitrary"), vmem_limit_bytes=VMEM_LIMIT),
        name="att_prompt",
    )(*([proj] * 10), bias_mats, bias_mats, bias_mats, a_sample)


def _kv_rows_kernel(src_a, src_b, o_ref, pad_ref, *, tr):
    pitch = tr + 8

    def emit(src):
        for h in range(N_HEADS):
            pad_ref[h * pitch:h * pitch + tr, :] = src[h]

        def body(i, c):
            t0 = i * 8
            for k in range(8):
                o_ref[t0 + k] = pad_ref[pl.ds(t0 + k, N_HEADS, stride=pitch), :]
            return c
        lax.fori_loop(0, tr // 8, body, 0)

    @pl.when(pl.program_id(0) == 0)
    def _():
        emit(src_a)

    @pl.when(pl.program_id(0) == 1)
    def _():
        emit(src_b)


def _kv_rows(proj_a, proj_b, g):
    keep = WINDOWS[g]
    tr = min(keep, 512)
    nt = keep // tr
    first_blk = (SEQ - keep) // tr
    blks_per_batch = SEQ // tr

    def row_blk(b, t):
        return b * blks_per_batch + first_blk + t

    def col_blk(part):
        return (1 + part) * N_GROUPS + g

    def spec(layer):
        parked = (row_blk(BATCH - 1, nt - 1), col_blk(1)) if layer == 0 else (row_blk(0, 0), col_blk(0))
        return pl.BlockSpec((N_HEADS, tr, HEAD_DIM),
                            lambda l, b, t, p: (jnp.where(l == layer, col_blk(p), parked[1]),
                                                jnp.where(l == layer, row_blk(b, t), parked[0]), 0))

    return pl.pallas_call(
        functools.partial(_kv_rows_kernel, tr=tr),
        grid=(2, BATCH, nt, 2),
        in_specs=[spec(0), spec(1)],
        out_specs=pl.BlockSpec((None, None, tr, None, N_HEADS, HEAD_DIM), lambda l, b, t, p: (l, b, t, p, 0, 0)),
        out_shape=jax.ShapeDtypeStruct((2, BATCH, keep, 2, N_HEADS, HEAD_DIM), F32),
        scratch_shapes=[pltpu.VMEM((N_HEADS * (tr + 8), HEAD_DIM), F32)],
        compiler_params=pltpu.CompilerParams(
            dimension_semantics=("arbitrary",) * 4, vmem_limit_bytes=VMEM_LIMIT),
        name="kv_rows",
    )(proj_a, proj_b)


def _att_sample_kernel(x_ref, c0, c1, c2, bias_ref, a_ref, kv0, kv1, kv2):
    caches = (c0, c1, c2)
    kv_outs = (kv0, kv1, kv2)
    hb = N_HEADS
    outs, lses = [], []
    for g in range(N_GROUPS):
        q = x_ref[g * hb:(g + 1) * hb, :]
        kn = x_ref[(3 + g) * hb:(4 + g) * hb, :]
        vn = x_ref[(6 + g) * hb:(7 + g) * hb, :]
        kv_outs[g][0] = kn
        kv_outs[g][1] = vn
        kc = caches[g][:, 0]
        vc = caches[g][:, 1]
        s_c = jnp.sum(kc * q[None], axis=-1, keepdims=True) * ATT_SCALE + bias_ref[g, 0:BAND]
        s_n = jnp.sum(kn * q, axis=-1, keepdims=True) * ATT_SCALE + bias_ref[g, BAND]
        m = jnp.maximum(jnp.max(s_c, axis=0), s_n)
        p_c = jnp.exp(s_c - m[None])
        p_n = jnp.exp(s_n - m)
        l = jnp.sum(p_c, axis=0) + p_n
        o = (jnp.sum(p_c * vc, axis=0) + p_n * vn) / l
        outs.append(o)
        lses.append(m + jnp.log(l))
    mx = jnp.maximum(jnp.maximum(lses[0], lses[1]), lses[2])
    ws = [jnp.exp(ls - mx) for ls in lses]
    o = (ws[0] * outs[0] + ws[1] * outs[1] + ws[2] * outs[2]) / (ws[0] + ws[1] + ws[2])
    z = x_ref[9 * hb:10 * hb, :]
    a_ref[...] = o * _silu(z)


def _att_sample(proj_s, caches, bias_s, layer):
    hb = N_HEADS
    in_specs = [pl.BlockSpec((None, ATT_IN_COLS // HEAD_DIM, HEAD_DIM), lambda b: (b, 0, 0))]
    for g in range(N_GROUPS):
        in_specs.append(pl.BlockSpec((None, None, BAND, None, 2, hb, HEAD_DIM),
                                     lambda b: (layer, b, 0, 0, 0, 0, 0)))
    in_specs.append(pl.BlockSpec((N_GROUPS, BAND + 1, hb, 1), lambda b: (0, 0, 0, 0)))
    out_specs = [pl.BlockSpec((None, hb, HEAD_DIM), lambda b: (b, 0, 0))]
    out_shapes = [jax.ShapeDtypeStruct((DEC_BATCH, hb, HEAD_DIM), F32)]
    for g in range(N_GROUPS):
        out_specs.append(pl.BlockSpec((None, 2, hb, HEAD_DIM), lambda b: (b, 0, 0, 0)))
        out_shapes.append(jax.ShapeDtypeStruct((DEC_BATCH, 2, hb, HEAD_DIM), F32))
    return pl.pallas_call(
        _att_sample_kernel,
        grid=(DEC_BATCH,),
        in_specs=in_specs,
        out_specs=out_specs,
        out_shape=out_shapes,
        compiler_params=pltpu.CompilerParams(
            dimension_semantics=("arbitrary",), vmem_limit_bytes=VMEM_LIMIT),
        name="att_sample",
    )(proj_s, *caches, bias_s)


def kernel(x_prompt, x_sample, c_prompt, c_sample, cache_kv0, cache_kv1, cache_kv2, state_pool,
           norm_pre, norm_post, ada_w, ada_b, t5_bias, pool_w_in, pool_w_grp, pool_scale,
           pool_w_out, att_w_in, att_w_out):
    n_att = DEPTH // 2
    xp0 = x_prompt.reshape(M_PROMPT, D_MODEL)
    xs0 = jnp.zeros((SAMPLE_ROWS, D_MODEL), F32).at[0:DEC_BATCH].set(x_sample.reshape(DEC_BATCH, D_MODEL))
    c_all = jnp.zeros((C_ROWS, D_MODEL), F32)
    c_all = c_all.at[0:BATCH].set(c_prompt).at[C_SAMPLE_ROW0:C_SAMPLE_ROW0 + DEC_BATCH].set(c_sample)

    mod = _ada_all(c_all, ada_w, ada_b)
    mod4 = mod.reshape(DEPTH, C_ROWS, 1, 3 * D_MODEL)
    gains_pre = norm_pre.reshape(DEPTH, 1, D_MODEL)
    gains_post = norm_post.reshape(DEPTH, 1, D_MODEL)

    bias_mats = _bias_matrices(t5_bias)
    bias_s = bias_mats[:, 0, 0:BAND + 1].reshape(N_GROUPS, N_HEADS, BAND + 1)
    bias_s = jnp.transpose(bias_s, (0, 2, 1))[..., None]
    caches = [c.reshape(n_att, DEC_BATCH, BAND, dil, 2, N_HEADS, HEAD_DIM)
              for c, dil in zip((cache_kv0, cache_kv1, cache_kv2), DILATIONS)]
    state_t = jnp.transpose(state_pool, (0, 2, 1, 3))

    kv_s = [[] for _ in range(N_GROUPS)]
    pool_p, pool_s, att_projs = [], [], []

    (h,) = _norm_step(xp0, None, mod, mod4, gains_post, gains_pre, None, 0, x_sample=xs0)
    x = None
    for i in range(DEPTH):
        li = i // 2
        if i % 2 == 0:
            proj = _matmul(h, pool_w_in, li, 2 * POOL_WIDTH, 1024, F32, "pool_in_proj")
            a, pp, ps = _pool_mix(proj, state_t, pool_w_grp, pool_scale, li)
            pool_p.append(pp)
            pool_s.append(jnp.transpose(ps, (1, 0, 2)))
            y = _matmul(a, pool_w_out, li, D_MODEL, 512, F32, "pool_out_proj")
        else:
            proj = _matmul(h, att_w_in, li, ATT_IN_COLS, 1024, F32, "att_in_proj", slabs=True)
            att_projs.append(proj)
            proj_s = jnp.transpose(proj[:, M_PROMPT:M_PROMPT + DEC_BATCH, :], (1, 0, 2))
            souts = _att_sample(proj_s, caches, bias_s, li)
            a_s = jnp.zeros((SAMPLE_ROWS, D_MODEL), F32).at[0:DEC_BATCH].set(souts[0].reshape(DEC_BATCH, D_MODEL))
            a = _att_prompt(proj, bias_mats, a_s.astype(BF16))
            for g in range(N_GROUPS):
                kv_s[g].append(souts[1 + g].reshape(DEC_BATCH, 1, 2, N_HEADS, HEAD_DIM))
            y = _matmul(a, att_w_out, li, D_MODEL, 512, F32, "att_out_proj")
        last = i + 1 == DEPTH
        if i == 0:
            x, h = _norm_step(xp0, y, mod, mod4, gains_post, gains_pre, i, i + 1, x_sample=xs0)
        elif not last:
            x, h = _norm_step(x, y, mod, mod4, gains_post, gains_pre, i, i + 1)
        else:
            y_p, y_s = _norm_step(x, y, mod, mod4, gains_post, gains_pre, i, None, split_out=True)

    kv_p = [_kv_rows(att_projs[0], att_projs[1], g) for g in range(N_GROUPS)]
    y_prompt = y_p.reshape(BATCH, SEQ, D_MODEL)
    y_sample = y_s[0:DEC_BATCH].reshape(DEC_BATCH, 1, D_MODEL)
    return (y_prompt, y_sample, kv_p[0], kv_p[1], kv_p[2], jnp.stack(pool_p),
            jnp.stack(kv_s[0]), jnp.stack(kv_s[1]), jnp.stack(kv_s[2]), jnp.stack(pool_s))
```

```python
import functools

import numpy as np
import jax
import jax.numpy as jnp
from jax import lax
from jax.experimental import pallas as pl
from jax.experimental.pallas import tpu as pltpu

D_MODEL = 2048
BATCH = 4
SEQ = 2048
DEPTH = 4
DEC_BATCH = 8
HEAD_DIM = 128
N_HEADS = 16
DILATIONS = (1, 4, 16)
WINDOWS = (128, 512, 2048)
N_GROUPS = 3
QKV_WIDTH = N_GROUPS * D_MODEL
ATT_IN_COLS = 3 * QKV_WIDTH + D_MODEL
BAND = 128
ATT_SCALE = HEAD_DIM ** -0.5
LOG2E = 1.4426950408889634
POOL_WINDOWS = (2, 4, 8, 16)
POOL_WIDTH = 2 * D_MODEL
POOL_GROUP = POOL_WIDTH // 4
POOL_BUF = 15
N_BUCKETS = 32
T5_MAX_DIST = 2048
RMS_EPS = 1e-6
NEG_INF = -1e30

M_PROMPT = BATCH * SEQ
M_ALL = M_PROMPT + 16
SAMPLE_ROWS = 16
C_ROWS = 32
C_SAMPLE_ROW0 = 16

TM_MATMUL = 912
TE_NORM = 512
TP_POOL = 1024
POOL_HALO = 24
ATT_UNITS = 16
N_BLOCKS = SEQ // BAND
VMEM_LIMIT = 56 * 1024 * 1024

F32 = jnp.float32
BF16 = jnp.bfloat16


def _silu(x):
    return x * (1.0 / (1.0 + jnp.exp(-x)))


def _cast_rows_to_bf16(src_ref, dst_ref, rows, chunk=256):
    def body(i, c):
        r = pl.multiple_of(i * chunk, chunk)
        dst_ref[pl.ds(r, chunk), :] = src_ref[pl.ds(r, chunk), :].astype(BF16)
        return c
    lax.fori_loop(0, rows // chunk, body, 0)


def _ada_kernel(c_ref, w_ref, b_ref, o_ref):
    a = _silu(c_ref[...]).astype(BF16)
    kc = 512
    acc = jnp.zeros(o_ref.shape, F32)
    for k0 in range(0, D_MODEL, kc):
        acc = acc + jnp.dot(a[:, k0:k0 + kc], w_ref[k0:k0 + kc, :].astype(BF16),
                            preferred_element_type=F32)
    o_ref[...] = acc + b_ref[...]


def _ada_all(c_all, ada_w, ada_b):
    tn = 1024
    n = 3 * D_MODEL
    return pl.pallas_call(
        _ada_kernel,
        grid=(DEPTH, n // tn),
        in_specs=[pl.BlockSpec((C_ROWS, D_MODEL), lambda l, j: (0, 0)),
                  pl.BlockSpec((None, D_MODEL, tn), lambda l, j: (l, 0, j)),
                  pl.BlockSpec((None, 1, tn), lambda l, j: (l, 0, j))],
        out_specs=pl.BlockSpec((None, C_ROWS, tn), lambda l, j: (l, 0, j)),
        out_shape=jax.ShapeDtypeStruct((DEPTH, C_ROWS, n), F32),
        compiler_params=pltpu.CompilerParams(
            dimension_semantics=("arbitrary", "arbitrary"), vmem_limit_bytes=VMEM_LIMIT),
        name="ada_mod",
    )(c_all, ada_w, ada_b.reshape(DEPTH, 1, n))


def _t5_bucket(dist):
    dist = np.asarray(dist, dtype=np.int64)
    max_exact = N_BUCKETS // 2
    ratio = np.log(np.maximum(dist, 1) / max_exact) / np.log(T5_MAX_DIST / max_exact)
    large = np.minimum(max_exact + (ratio * (N_BUCKETS - max_exact)).astype(np.int64), N_BUCKETS - 1)
    return np.where(dist < max_exact, dist, large).astype(np.int32)


def _bucket_index_table():
    rel = np.arange(BAND)[:, None] + BAND - np.arange(2 * BAND)[None, :]
    inband = (rel >= 0) & (rel <= BAND)
    out = []
    for dil in DILATIONS:
        bucket = _t5_bucket(np.clip(rel, 0, BAND) * dil)
        out.append(np.where(inband, bucket, -1))
    return np.stack(out).astype(np.int32)


def _bias_kernel(tab_ref, idx_ref, o_ref):
    g = pl.program_id(0)
    idx = idx_ref[...]

    def head(h, c):
        acc = jnp.full(idx.shape, NEG_INF, F32)
        for b in range(N_BUCKETS):
            acc = jnp.where(idx == b, tab_ref[b, g * N_HEADS + h], acc)
        o_ref[h] = acc
        return c
    lax.fori_loop(0, N_HEADS, head, 0)


def _bias_matrices(t5_bias):
    idx = jnp.asarray(_bucket_index_table())
    n_sub = N_GROUPS * N_HEADS
    return pl.pallas_call(
        _bias_kernel,
        grid=(N_GROUPS,),
        in_specs=[pl.BlockSpec(memory_space=pltpu.SMEM),
                  pl.BlockSpec((None, BAND, 2 * BAND), lambda g: (g, 0, 0))],
        out_specs=pl.BlockSpec((N_HEADS, BAND, 2 * BAND), lambda g: (g, 0, 0)),
        out_shape=jax.ShapeDtypeStruct((n_sub, BAND, 2 * BAND), F32),
        name="t5_bias_mats",
    )(t5_bias, idx)


def _norm_kernel(*refs, has_post, has_pre, split_in, split_out, n_prompt_tiles):
    refs = list(refs)
    xp_ref = refs.pop(0)
    xs_ref = refs.pop(0) if split_in else xp_ref
    if has_post:
        y_ref, gpost_ref, gate_p_ref, gate_s_ref = refs[:4]
        refs = refs[4:]
    if has_pre:
        gpre_ref, shift_p_ref, scale_p_ref, shift_s_ref, scale_s_ref = refs[:5]
        refs = refs[5:]
    if has_post:
        xop_ref = refs.pop(0)
        xos_ref = refs.pop(0) if split_out else xop_ref
    if has_pre:
        h_ref = refs.pop(0)

    def rms(v, g):
        return v * lax.rsqrt(jnp.mean(v * v, axis=-1, keepdims=True) + RMS_EPS) * g

    def body(x_ref, xo_ref, rows, gate, shift, scale):
        x = x_ref[rows, :]
        if has_post:
            x = x + gate * rms(y_ref[rows, :].astype(F32), gpost_ref[...])
            xo_ref[rows, :] = x
        if has_pre:
            h = rms(x, gpre_ref[...]) * (1.0 + scale) + shift
            h_ref[rows, :] = h.astype(BF16)

    t = pl.program_id(0)

    @pl.when(t < n_prompt_tiles)
    def _():
        body(xp_ref, xop_ref if has_post else None, slice(None),
             gate_p_ref[...] if has_post else None,
             shift_p_ref[...] if has_pre else None,
             scale_p_ref[...] if has_pre else None)

    @pl.when(t == n_prompt_tiles)
    def _():
        body(xs_ref, xos_ref if has_post else None, slice(0, SAMPLE_ROWS),
             gate_s_ref[...] if has_post else None,
             shift_s_ref[...] if has_pre else None,
             scale_s_ref[...] if has_pre else None)


def _norm_step(x, y, mod, mod4, norm_post, norm_pre, post_layer, pre_layer, x_sample=None, split_out=False):
    has_post = post_layer is not None
    has_pre = pre_layer is not None
    split_in = x_sample is not None
    te = TE_NORM
    npt = M_PROMPT // te
    tiles_per_batch = SEQ // te
    row_spec = pl.BlockSpec((te, D_MODEL), lambda t: (t, 0))
    prompt_row_spec = pl.BlockSpec((te, D_MODEL), lambda t: (jnp.minimum(t, npt - 1), 0))
    sample_row_spec = pl.BlockSpec((SAMPLE_ROWS, D_MODEL), lambda t: (0, 0))

    def mod_p_spec(layer, part):
        return pl.BlockSpec((None, None, 1, D_MODEL),
                            lambda t: (layer, jnp.minimum(t // tiles_per_batch, BATCH - 1), 0, part))

    def mod_s_spec(layer, part):
        return pl.BlockSpec((None, SAMPLE_ROWS, D_MODEL),
                            lambda t: (layer, C_SAMPLE_ROW0 // SAMPLE_ROWS, part))

    def gain_spec(layer):
        return pl.BlockSpec((None, 1, D_MODEL), lambda t: (layer, 0, 0))

    if split_in:
        args, in_specs = [x, x_sample], [prompt_row_spec, sample_row_spec]
    else:
        args, in_specs = [x], [row_spec]
    out_shapes, out_specs = [], []
    if has_post:
        args += [y, norm_post, mod4, mod]
        in_specs += [row_spec, gain_spec(post_layer), mod_p_spec(post_layer, 2), mod_s_spec(post_layer, 2)]
        if split_out:
            out_shapes += [jax.ShapeDtypeStruct((M_PROMPT, D_MODEL), F32),
                           jax.ShapeDtypeStruct((SAMPLE_ROWS, D_MODEL), F32)]
            out_specs += [prompt_row_spec, sample_row_spec]
        else:
            out_shapes.append(jax.ShapeDtypeStruct((M_ALL, D_MODEL), F32))
            out_specs.append(row_spec)
    if has_pre:
        args += [norm_pre, mod4, mod4, mod, mod]
        in_specs += [gain_spec(pre_layer), mod_p_spec(pre_layer, 0), mod_p_spec(pre_layer, 1),
                     mod_s_spec(pre_layer, 0), mod_s_spec(pre_layer, 1)]
        out_shapes.append(jax.ShapeDtypeStruct((M_ALL, D_MODEL), BF16))
        out_specs.append(row_spec)
    outs = pl.pallas_call(
        functools.partial(_norm_kernel, has_post=has_post, has_pre=has_pre, split_in=split_in,
                          split_out=split_out, n_prompt_tiles=npt),
        grid=(npt + 1,),
        in_specs=in_specs,
        out_specs=out_specs,
        out_shape=out_shapes,
        compiler_params=pltpu.CompilerParams(
            dimension_semantics=("arbitrary",), vmem_limit_bytes=VMEM_LIMIT),
        name="norm_step",
    )(*args)
    return outs


def _mm_kernel(a_ref, w_ref, o_ref, wb_ref, *, k_rows, slabs):
    @pl.when(pl.program_id(1) == 0)
    def _():
        _cast_rows_to_bf16(w_ref, wb_ref, k_rows)

    acc = jnp.dot(a_ref[...], wb_ref[...], preferred_element_type=F32).astype(o_ref.dtype)
    if slabs:
        for c in range(o_ref.shape[0]):
            o_ref[c] = acc[:, c * HEAD_DIM:(c + 1) * HEAD_DIM]
    else:
        o_ref[...] = acc


def _matmul(a, w, layer, n_out, tn, out_dtype, name, slabs=False):
    m, k = a.shape
    tm = TM_MATMUL
    if slabs:
        out_spec = pl.BlockSpec((tn // HEAD_DIM, tm, HEAD_DIM), lambda j, i: (j, i, 0))
        out_shape = jax.ShapeDtypeStruct((n_out // HEAD_DIM, m, HEAD_DIM), out_dtype)
    else:
        out_spec = pl.BlockSpec((tm, tn), lambda j, i: (i, j))
        out_shape = jax.ShapeDtypeStruct((m, n_out), out_dtype)
    return pl.pallas_call(
        functools.partial(_mm_kernel, k_rows=k, slabs=slabs),
        grid=(n_out // tn, m // tm),
        in_specs=[pl.BlockSpec((tm, k), lambda j, i: (i, 0)),
                  pl.BlockSpec((None, k, tn), lambda j, i: (layer, 0, j))],
        out_specs=out_spec,
        out_shape=out_shape,
        scratch_shapes=[pltpu.VMEM((k, tn), BF16)],
        compiler_params=pltpu.CompilerParams(
            dimension_semantics=("arbitrary", "arbitrary"), vmem_limit_bytes=VMEM_LIMIT),
        name=name,
    )(a, w)


def _pool_kernel(u_ref, halo_ref, z_ref, st_ref, w_ref, sc_ref,
                 a_ref, pp_ref, ps_ref, wb_ref, buf_a, buf_b, *, n_prompt_tiles, tiles_per_batch):
    g = pl.program_id(0)
    t = pl.program_id(1)
    tp = TP_POOL
    h0 = POOL_HALO

    @pl.when(t == 0)
    def _():
        _cast_rows_to_bf16(w_ref, wb_ref, POOL_GROUP)

    def finish(r, z, rows):
        y = jnp.dot(r.astype(BF16), wb_ref[...], preferred_element_type=F32) * sc_ref[...]
        a_ref[rows, :] = (y * _silu(z)).astype(BF16)

    def prompt_tile(n_steps):
        w = 2 ** n_steps
        first = (t % tiles_per_batch) == 0
        buf_a[0:8, :] = jnp.zeros((8, POOL_GROUP), F32)
        buf_b[0:8, :] = jnp.zeros((8, POOL_GROUP), F32)
        buf_a[8:h0, :] = jnp.where(first, 0.0, halo_ref[...])
        buf_a[h0:h0 + tp, :] = u_ref[...]
        src, dst = buf_a, buf_b
        n = tp + h0 - 8
        for s in range(n_steps):
            sh = 2 ** s
            dst[8:8 + n, :] = src[8:8 + n, :] + src[8 - sh:8 - sh + n, :]
            src, dst = dst, src
        pos = (t % tiles_per_batch) * tp + lax.broadcasted_iota(jnp.int32, (tp, 1), 0)
        inv_cnt = 1.0 / jnp.minimum(pos + 1, w).astype(F32)
        u = u_ref[...]
        r = src[h0:h0 + tp, :] * inv_cnt - u
        finish(r, z_ref[...], slice(None))

        @pl.when((t % tiles_per_batch) == tiles_per_batch - 1)
        def _():
            pp_ref[...] = u_ref[tp - POOL_BUF:tp, :]

    def sample_tile(n_steps):
        w = 2 ** n_steps
        u_new = u_ref[0:DEC_BATCH, :]
        acc = u_new
        for k in range(1, w):
            acc = acc + st_ref[POOL_BUF - k]
        r = acc / float(w) - u_new
        buf_a[0:DEC_BATCH, :] = r
        buf_a[DEC_BATCH:SAMPLE_ROWS, :] = jnp.zeros((SAMPLE_ROWS - DEC_BATCH, POOL_GROUP), F32)
        finish(buf_a[0:SAMPLE_ROWS, :], z_ref[0:SAMPLE_ROWS, :], slice(0, SAMPLE_ROWS))
        for k in range(POOL_BUF - 1):
            ps_ref[k] = st_ref[k + 1]
        ps_ref[POOL_BUF - 1] = u_new

    for gi in range(len(POOL_WINDOWS)):
        @pl.when((g == gi) & (t < n_prompt_tiles))
        def _(gi=gi):
            prompt_tile(gi + 1)

        @pl.when((g == gi) & (t == n_prompt_tiles))
        def _(gi=gi):
            sample_tile(gi + 1)


def _pool_mix(proj, state_t, w_grp, scale, layer):
    tp = TP_POOL
    npt = M_PROMPT // tp
    tpb = SEQ // tp
    ng = len(POOL_WINDOWS)
    halo_blocks = tp // 16
    outs = pl.pallas_call(
        functools.partial(_pool_kernel, n_prompt_tiles=npt, tiles_per_batch=tpb),
        grid=(ng, npt + 1),
        in_specs=[
            pl.BlockSpec((tp, POOL_GROUP), lambda g, t: (t, g)),
            pl.BlockSpec((16, POOL_GROUP), lambda g, t: (jnp.maximum(t * halo_blocks - 1, 0), g)),
            pl.BlockSpec((tp, POOL_GROUP), lambda g, t: (t, ng + g)),
            pl.BlockSpec((None, POOL_BUF, DEC_BATCH, POOL_GROUP), lambda g, t: (layer, 0, 0, g)),
            pl.BlockSpec((None, None, POOL_GROUP, POOL_GROUP), lambda g, t: (layer, g, 0, 0)),
            pl.BlockSpec((None, 1, POOL_GROUP), lambda g, t: (layer, 0, g)),
        ],
        out_specs=[
            pl.BlockSpec((tp, POOL_GROUP), lambda g, t: (t, g)),
            pl.BlockSpec((None, POOL_BUF, POOL_GROUP),
                         lambda g, t: (jnp.minimum(t // tpb, BATCH - 1), 0, g)),
            pl.BlockSpec((POOL_BUF, DEC_BATCH, POOL_GROUP), lambda g, t: (0, 0, g)),
        ],
        out_shape=[
            jax.ShapeDtypeStruct((M_ALL, POOL_WIDTH), BF16),
            jax.ShapeDtypeStruct((BATCH, POOL_BUF, POOL_WIDTH), F32),
            jax.ShapeDtypeStruct((POOL_BUF, DEC_BATCH, POOL_WIDTH), F32),
        ],
        scratch_shapes=[pltpu.VMEM((POOL_GROUP, POOL_GROUP), BF16),
                        pltpu.VMEM((tp + POOL_HALO, POOL_GROUP), F32),
                        pltpu.VMEM((tp + POOL_HALO, POOL_GROUP), F32)],
        compiler_params=pltpu.CompilerParams(
            dimension_semantics=("arbitrary", "arbitrary"), vmem_limit_bytes=VMEM_LIMIT),
        name="pool_mix",
    )(proj, proj, proj, state_t, w_grp, scale.reshape(-1, 1, POOL_WIDTH))
    return outs


def _att_prompt_kernel(*refs):
    n_in = 13
    as_ref, a_ref = refs[n_in], refs[n_in + 1]

    @pl.when(pl.program_id(0) < BATCH)
    def _():
        _att_prompt_tile(*refs[:n_in], *refs[n_in + 1:])

    @pl.when(pl.program_id(0) == BATCH)
    def _():
        a_ref[0:SAMPLE_ROWS, :] = as_ref[...]


def _att_prompt_tile(q0, k0, v0, q1, k1, v1, q2, k2, v2, z_ref, b0, b1, b2,
                     a_ref, qd, kd, vd, od, ld, on, ln, tmp):
    q_refs, k_refs, v_refs = (q0, q1, q2), (k0, k1, k2), (v0, v1, v2)
    bias_refs = (b0, b1, b2)
    nu = ATT_UNITS

    for g, dil in enumerate(DILATIONS):
        n = SEQ // dil
        nb = n // BAND
        zero_blk = jnp.zeros((BAND, HEAD_DIM), BF16)
        kd[g, 0:BAND, :] = zero_blk
        vd[g, 0:BAND, :] = zero_blk
        for src, dst, off, mul in ((q_refs[g], qd, 0, ATT_SCALE * LOG2E), (k_refs[g], kd, BAND, None),
                                   (v_refs[g], vd, BAND, None)):
            def to_bf16(x, mul=mul):
                return (x if mul is None else x * mul).astype(BF16)

            if dil == 1:
                def cp(i, c, src=src, dst=dst, g=g, off=off, to_bf16=to_bf16):
                    r = pl.multiple_of(i * 256, 256)
                    dst[g, pl.ds(off + r, 256), :] = to_bf16(src[pl.ds(r, 256), :])
                    return c
                lax.fori_loop(0, SEQ // 256, cp, 0)
            else:
                for r in range(dil):
                    dst[g, off + r * n:off + (r + 1) * n, :] = to_bf16(src[pl.ds(r, n, stride=dil), :])

    def unit_batch(g, u0, seq_blocks):
        with_prev = seq_blocks > 1
        rows = nu * BAND
        row = pl.multiple_of(u0 * BAND, rows)
        shape3 = (nu, BAND, HEAD_DIM)
        q = qd[g, pl.ds(row, rows), :].reshape(shape3)
        kc = kd[g, pl.ds(row + BAND, rows), :].reshape(shape3)
        vc = vd[g, pl.ds(row + BAND, rows), :].reshape(shape3)
        bias = bias_refs[g]
        s_c = jnp.einsum('uqe,uke->uqk', q, kc, preferred_element_type=F32) \
            + (bias[:, BAND:2 * BAND] * LOG2E)[None]
        if with_prev:
            kp = kd[g, pl.ds(row, rows), :].reshape(shape3)
            vp = vd[g, pl.ds(row, rows), :].reshape(shape3)
            bias_p = jnp.broadcast_to((bias[:, 0:BAND] * LOG2E)[None], (nu, BAND, BAND))
            blk = u0 + lax.broadcasted_iota(jnp.int32, (nu, BAND, BAND), 0)
            bias_p = jnp.where((blk & (seq_blocks - 1)) == 0, NEG_INF, bias_p)
            s_p = jnp.einsum('uqe,uke->uqk', q, kp, preferred_element_type=F32) + bias_p
            m = jnp.max(jnp.maximum(s_c, s_p), axis=-1, keepdims=True)
            p_c = jnp.exp2(s_c - m)
            p_p = jnp.exp2(s_p - m)
            l = jnp.sum(p_c + p_p, axis=-1, keepdims=True)
            acc = jnp.einsum('uqk,uke->uqe', p_c.astype(BF16), vc, preferred_element_type=F32) \
                + jnp.einsum('uqk,uke->uqe', p_p.astype(BF16), vp, preferred_element_type=F32)
        else:
            m = jnp.max(s_c, axis=-1, keepdims=True)
            p_c = jnp.exp2(s_c - m)
            l = jnp.sum(p_c, axis=-1, keepdims=True)
            acc = jnp.einsum('uqk,uke->uqe', p_c.astype(BF16), vc, preferred_element_type=F32)
        od[g, pl.ds(row, rows), :] = (acc / l).reshape(rows, HEAD_DIM)
        ld[g, pl.ds(row, rows), :] = jnp.broadcast_to(m + jnp.log(l) * LOG2E, shape3).reshape(rows, HEAD_DIM)

    for g, dil in enumerate(DILATIONS):
        nb = (SEQ // dil) // BAND

        def batch_body(i, c, g=g, nb=nb):
            unit_batch(g, i * nu, nb)
            return c
        lax.fori_loop(0, N_BLOCKS // nu, batch_body, 0)

    def interleave4(src, dst, n_rows):
        quarter = n_rows // 4
        for r in range(4):
            dst[pl.ds(r, quarter, stride=4), :] = src[r * quarter:(r + 1) * quarter, :]

    for src_all, dst_all in ((od, on), (ld, ln)):
        interleave4(src_all.at[1], dst_all.at[0], SEQ)
        for r_lo in range(4):
            for r_hi in range(4):
                r = r_lo + 4 * r_hi
                tmp[pl.ds(r_lo * 512 + r_hi, BAND, stride=4), :] = src_all[2, r * BAND:(r + 1) * BAND, :]
        interleave4(tmp, dst_all.at[1], SEQ)

    def comb(i, c):
        rs = pl.ds(pl.multiple_of(i * 256, 256), 256)
        l0, l1, l2 = ld[0, rs, :], ln[0, rs, :], ln[1, rs, :]
        mx = jnp.maximum(jnp.maximum(l0, l1), l2)
        w0, w1, w2 = jnp.exp2(l0 - mx), jnp.exp2(l1 - mx), jnp.exp2(l2 - mx)
        o = (w0 * od[0, rs, :] + w1 * on[0, rs, :] + w2 * on[1, rs, :]) / (w0 + w1 + w2)
        a_ref[rs, :] = (o * _silu(z_ref[rs, :])).astype(BF16)
        return c
    lax.fori_loop(0, SEQ // 256, comb, 0)


def _att_prompt(proj, bias_mats, a_sample):
    hb = N_HEADS

    def head_blk(b, h):
        return jnp.where(b < BATCH, h, hb - 1)

    in_specs = []
    for g in range(N_GROUPS):
        for part in range(3):
            off = part * N_GROUPS * hb + g * hb
            in_specs.append(pl.BlockSpec(
                (None, SEQ, HEAD_DIM),
                lambda b, h, off=off: (off + head_blk(b, h), jnp.minimum(b, BATCH - 1), 0)))
    in_specs.append(pl.BlockSpec((None, SEQ, HEAD_DIM),
                                 lambda b, h: (9 * hb + head_blk(b, h), jnp.minimum(b, BATCH - 1), 0)))
    for g in range(N_GROUPS):
        in_specs.append(pl.BlockSpec((None, BAND, 2 * BAND), lambda b, h, g=g: (g * hb + head_blk(b, h), 0, 0)))
    in_specs.append(pl.BlockSpec((SAMPLE_ROWS, HEAD_DIM), lambda b, h: (0, h)))
    scratch = [pltpu.VMEM((N_GROUPS, SEQ, HEAD_DIM), BF16)] + \
              [pltpu.VMEM((N_GROUPS, SEQ + BAND, HEAD_DIM), BF16)] * 2 + \
              [pltpu.VMEM((N_GROUPS, SEQ, HEAD_DIM), F32)] * 2 + \
              [pltpu.VMEM((N_GROUPS - 1, SEQ, HEAD_DIM), F32)] * 2 + \
              [pltpu.VMEM((SEQ, HEAD_DIM), F32)]
    return pl.pallas_call(
        _att_prompt_kernel,
        grid=(BATCH + 1, N_HEADS),
        in_specs=in_specs,
        out_specs=pl.BlockSpec((SEQ, HEAD_DIM), lambda b, h: (b, h)),
        out_shape=jax.ShapeDtypeStruct((M_ALL, D_MODEL), BF16),
        scratch_shapes=scratch,
        compiler_params=pltpu.CompilerParams(
            dimension_semantics=("arbitrary", "arbitrary"), vmem_limit_bytes=VMEM_LIMIT),
        name="att_prompt",
    )(*([proj] * 10), bias_mats, bias_mats, bias_mats, a_sample)


def _kv_rows_kernel(src_a, src_b, o_ref, pad_ref, *, tr):
    pitch = tr + 8

    def emit(src):
        for h in range(N_HEADS):
            pad_ref[h * pitch:h * pitch + tr, :] = src[h]

        def body(i, c):
            t0 = i * 8
            for k in range(8):
                o_ref[t0 + k] = pad_ref[pl.ds(t0 + k, N_HEADS, stride=pitch), :]
            return c
        lax.fori_loop(0, tr // 8, body, 0)

    @pl.when(pl.program_id(0) == 0)
    def _():
        emit(src_a)

    @pl.when(pl.program_id(0) == 1)
    def _():
        emit(src_b)


def _kv_rows(proj_a, proj_b, g):
    keep = WINDOWS[g]
    tr = min(keep, 512)
    nt = keep // tr
    first_blk = (SEQ - keep) // tr
    blks_per_batch = SEQ // tr

    def row_blk(b, t):
        return b * blks_per_batch + first_blk + t

    def col_blk(part):
        return (1 + part) * N_GROUPS + g

    def spec(layer):
        parked = (row_blk(BATCH - 1, nt - 1), col_blk(1)) if layer == 0 else (row_blk(0, 0), col_blk(0))
        return pl.BlockSpec((N_HEADS, tr, HEAD_DIM),
                            lambda l, b, t, p: (jnp.where(l == layer, col_blk(p), parked[1]),
                                                jnp.where(l == layer, row_blk(b, t), parked[0]), 0))

    return pl.pallas_call(
        functools.partial(_kv_rows_kernel, tr=tr),
        grid=(2, BATCH, nt, 2),
        in_specs=[spec(0), spec(1)],
        out_specs=pl.BlockSpec((None, None, tr, None, N_HEADS, HEAD_DIM), lambda l, b, t, p: (l, b, t, p, 0, 0)),
        out_shape=jax.ShapeDtypeStruct((2, BATCH, keep, 2, N_HEADS, HEAD_DIM), F32),
        scratch_shapes=[pltpu.VMEM((N_HEADS * (tr + 8), HEAD_DIM), F32)],
        compiler_params=pltpu.CompilerParams(
            dimension_semantics=("arbitrary",) * 4, vmem_limit_bytes=VMEM_LIMIT),
        name="kv_rows",
    )(proj_a, proj_b)


def _att_sample_kernel(x_ref, c0, c1, c2, bias_ref, a_ref, kv0, kv1, kv2):
    caches = (c0, c1, c2)
    kv_outs = (kv0, kv1, kv2)
    hb = N_HEADS
    outs, lses = [], []
    for g in range(N_GROUPS):
        q = x_ref[g * hb:(g + 1) * hb, :]
        kn = x_ref[(3 + g) * hb:(4 + g) * hb, :]
        vn = x_ref[(6 + g) * hb:(7 + g) * hb, :]
        kv_outs[g][0] = kn
        kv_outs[g][1] = vn
        kc = caches[g][:, 0]
        vc = caches[g][:, 1]
        s_c = jnp.sum(kc * q[None], axis=-1, keepdims=True) * ATT_SCALE + bias_ref[g, 0:BAND]
        s_n = jnp.sum(kn * q, axis=-1, keepdims=True) * ATT_SCALE + bias_ref[g, BAND]
        m = jnp.maximum(jnp.max(s_c, axis=0), s_n)
        p_c = jnp.exp(s_c - m[None])
        p_n = jnp.exp(s_n - m)
        l = jnp.sum(p_c, axis=0) + p_n
        o = (jnp.sum(p_c * vc, axis=0) + p_n * vn) / l
        outs.append(o)
        lses.append(m + jnp.log(l))
    mx = jnp.maximum(jnp.maximum(lses[0], lses[1]), lses[2])
    ws = [jnp.exp(ls - mx) for ls in lses]
    o = (ws[0] * outs[0] + ws[1] * outs[1] + ws[2] * outs[2]) / (ws[0] + ws[1] + ws[2])
    z = x_ref[9 * hb:10 * hb, :]
    a_ref[...] = o * _silu(z)


def _att_sample(proj_s, caches, bias_s, layer):
    hb = N_HEADS
    in_specs = [pl.BlockSpec((None, ATT_IN_COLS // HEAD_DIM, HEAD_DIM), lambda b: (b, 0, 0))]
    for g in range(N_GROUPS):
        in_specs.append(pl.BlockSpec((None, None, BAND, None, 2, hb, HEAD_DIM),
                                     lambda b: (layer, b, 0, 0, 0, 0, 0)))
    in_specs.append(pl.BlockSpec((N_GROUPS, BAND + 1, hb, 1), lambda b: (0, 0, 0, 0)))
    out_specs = [pl.BlockSpec((None, hb, HEAD_DIM), lambda b: (b, 0, 0))]
    out_shapes = [jax.ShapeDtypeStruct((DEC_BATCH, hb, HEAD_DIM), F32)]
    for g in range(N_GROUPS):
        out_specs.append(pl.BlockSpec((None, 2, hb, HEAD_DIM), lambda b: (b, 0, 0, 0)))
        out_shapes.append(jax.ShapeDtypeStruct((DEC_BATCH, 2, hb, HEAD_DIM), F32))
    return pl.pallas_call(
        _att_sample_kernel,
        grid=(DEC_BATCH,),
        in_specs=in_specs,
        out_specs=out_specs,
        out_shape=out_shapes,
        compiler_params=pltpu.CompilerParams(
            dimension_semantics=("arbitrary",), vmem_limit_bytes=VMEM_LIMIT),
        name="att_sample",
    )(proj_s, *caches, bias_s)


def kernel(x_prompt, x_sample, c_prompt, c_sample, cache_kv0, cache_kv1, cache_kv2, state_pool,
           norm_pre, norm_post, ada_w, ada_b, t5_bias, pool_w_in, pool_w_grp, pool_scale,
           pool_w_out, att_w_in, att_w_out):
    n_att = DEPTH // 2
    xp0 = x_prompt.reshape(M_PROMPT, D_MODEL)
    xs0 = jnp.zeros((SAMPLE_ROWS, D_MODEL), F32).at[0:DEC_BATCH].set(x_sample.reshape(DEC_BATCH, D_MODEL))
    c_all = jnp.zeros((C_ROWS, D_MODEL), F32)
    c_all = c_all.at[0:BATCH].set(c_prompt).at[C_SAMPLE_ROW0:C_SAMPLE_ROW0 + DEC_BATCH].set(c_sample)

    mod = _ada_all(c_all, ada_w, ada_b)
    mod4 = mod.reshape(DEPTH, C_ROWS, 1, 3 * D_MODEL)
    gains_pre = norm_pre.reshape(DEPTH, 1, D_MODEL)
    gains_post = norm_post.reshape(DEPTH, 1, D_MODEL)

    bias_mats = _bias_matrices(t5_bias)
    bias_s = bias_mats[:, 0, 0:BAND + 1].reshape(N_GROUPS, N_HEADS, BAND + 1)
    bias_s = jnp.transpose(bias_s, (0, 2, 1))[..., None]
    caches = [c.reshape(n_att, DEC_BATCH, BAND, dil, 2, N_HEADS, HEAD_DIM)
              for c, dil in zip((cache_kv0, cache_kv1, cache_kv2), DILATIONS)]
    state_t = jnp.transpose(state_pool, (0, 2, 1, 3))

    kv_s = [[] for _ in range(N_GROUPS)]
    pool_p, pool_s, att_projs = [], [], []

    (h,) = _norm_step(xp0, None, mod, mod4, gains_post, gains_pre, None, 0, x_sample=xs0)
    x = None
    for i in range(DEPTH):
        li = i // 2
        if i % 2 == 0:
            proj = _matmul(h, pool_w_in, li, 2 * POOL_WIDTH, 1024, F32, "pool_in_proj")
            a, pp, ps = _pool_mix(proj, state_t, pool_w_grp, pool_scale, li)
            pool_p.append(pp)
            pool_s.append(jnp.transpose(ps, (1, 0, 2)))
            y = _matmul(a, pool_w_out, li, D_MODEL, 512, BF16, "pool_out_proj")
        else:
            proj = _matmul(h, att_w_in, li, ATT_IN_COLS, 1024, F32, "att_in_proj", slabs=True)
            att_projs.append(proj)
            proj_s = jnp.transpose(proj[:, M_PROMPT:M_PROMPT + DEC_BATCH, :], (1, 0, 2))
            souts = _att_sample(proj_s, caches, bias_s, li)
            a_s = jnp.zeros((SAMPLE_ROWS, D_MODEL), F32).at[0:DEC_BATCH].set(souts[0].reshape(DEC_BATCH, D_MODEL))
            a = _att_prompt(proj, bias_mats, a_s.astype(BF16))
            for g in range(N_GROUPS):
                kv_s[g].append(souts[1 + g].reshape(DEC_BATCH, 1, 2, N_HEADS, HEAD_DIM))
            y = _matmul(a, att_w_out, li, D_MODEL, 512, BF16, "att_out_proj")
        last = i + 1 == DEPTH
        if i == 0:
            x, h = _norm_step(xp0, y, mod, mod4, gains_post, gains_pre, i, i + 1, x_sample=xs0)
        elif not last:
            x, h = _norm_step(x, y, mod, mod4, gains_post, gains_pre, i, i + 1)
        else:
            y_p, y_s = _norm_step(x, y, mod, mod4, gains_post, gains_pre, i, None, split_out=True)

    kv_p = [_kv_rows(att_projs[0], att_projs[1], g) for g in range(N_GROUPS)]
    y_prompt = y_p.reshape(BATCH, SEQ, D_MODEL)
    y_sample = y_s[0:DEC_BATCH].reshape(DEC_BATCH, 1, D_MODEL)
    return (y_prompt, y_sample, kv_p[0], kv_p[1], kv_p[2], jnp.stack(pool_p),
            jnp.stack(kv_s[0]), jnp.stack(kv_s[1]), jnp.stack(kv_s[2]), jnp.stack(pool_s))
```

```python
import functools

import numpy as np
import jax
import jax.numpy as jnp
from jax import lax
from jax.experimental import pallas as pl
from jax.experimental.pallas import tpu as pltpu

D_MODEL = 2048
BATCH = 4
SEQ = 2048
DEPTH = 4
DEC_BATCH = 8
HEAD_DIM = 128
N_HEADS = 16
DILATIONS = (1, 4, 16)
WINDOWS = (128, 512, 2048)
N_GROUPS = 3
QKV_WIDTH = N_GROUPS * D_MODEL
ATT_IN_COLS = 3 * QKV_WIDTH + D_MODEL
BAND = 128
ATT_SCALE = HEAD_DIM ** -0.5
LOG2E = 1.4426950408889634
POOL_WINDOWS = (2, 4, 8, 16)
POOL_WIDTH = 2 * D_MODEL
POOL_GROUP = POOL_WIDTH // 4
POOL_BUF = 15
N_BUCKETS = 32
T5_MAX_DIST = 2048
RMS_EPS = 1e-6
NEG_INF = -1e30

M_PROMPT = BATCH * SEQ
M_ALL = M_PROMPT + 16
SAMPLE_ROWS = 16
C_ROWS = 32
C_SAMPLE_ROW0 = 16

TM_MATMUL = 912
TE_NORM = 512
TP_POOL = 1024
POOL_HALO = 24
ATT_UNITS = 16
N_BLOCKS = SEQ // BAND
KV_PITCH_PAD = 8
VMEM_LIMIT = 56 * 1024 * 1024

F32 = jnp.float32
BF16 = jnp.bfloat16


def _silu(x):
    half = 0.5 * x
    return half + half * jnp.tanh(half)


def _cast_rows_to_bf16(src_ref, dst_ref, rows, chunk=256):
    def body(i, c):
        r = pl.multiple_of(i * chunk, chunk)
        dst_ref[pl.ds(r, chunk), :] = src_ref[pl.ds(r, chunk), :].astype(BF16)
        return c
    lax.fori_loop(0, rows // chunk, body, 0)


def _ada_kernel(c_ref, w_ref, b_ref, o_ref):
    a = _silu(c_ref[...]).astype(BF16)
    kc = 512
    acc = jnp.zeros(o_ref.shape, F32)
    for k0 in range(0, D_MODEL, kc):
        acc = acc + jnp.dot(a[:, k0:k0 + kc], w_ref[k0:k0 + kc, :].astype(BF16),
                            preferred_element_type=F32)
    o_ref[...] = acc + b_ref[...]


def _ada_all(c_all, ada_w, ada_b):
    tn = 1024
    n = 3 * D_MODEL
    return pl.pallas_call(
        _ada_kernel,
        grid=(DEPTH, n // tn),
        in_specs=[pl.BlockSpec((C_ROWS, D_MODEL), lambda l, j: (0, 0)),
                  pl.BlockSpec((None, D_MODEL, tn), lambda l, j: (l, 0, j)),
                  pl.BlockSpec((None, 1, tn), lambda l, j: (l, 0, j))],
        out_specs=pl.BlockSpec((None, C_ROWS, tn), lambda l, j: (l, 0, j)),
        out_shape=jax.ShapeDtypeStruct((DEPTH, C_ROWS, n), F32),
        compiler_params=pltpu.CompilerParams(
            dimension_semantics=("arbitrary", "arbitrary"), vmem_limit_bytes=VMEM_LIMIT),
        name="ada_mod",
    )(c_all, ada_w, ada_b.reshape(DEPTH, 1, n))


def _t5_bucket(dist):
    dist = np.asarray(dist, dtype=np.int64)
    max_exact = N_BUCKETS // 2
    ratio = np.log(np.maximum(dist, 1) / max_exact) / np.log(T5_MAX_DIST / max_exact)
    large = np.minimum(max_exact + (ratio * (N_BUCKETS - max_exact)).astype(np.int64), N_BUCKETS - 1)
    return np.where(dist < max_exact, dist, large).astype(np.int32)


def _bucket_index_table():
    rel = np.arange(BAND)[:, None] + BAND - np.arange(2 * BAND)[None, :]
    inband = (rel >= 0) & (rel <= BAND)
    out = []
    for dil in DILATIONS:
        bucket = _t5_bucket(np.clip(rel, 0, BAND) * dil)
        out.append(np.where(inband, bucket, -1))
    return np.stack(out).astype(np.int32)


def _bias_kernel(tab_ref, idx_ref, o_ref):
    g = pl.program_id(0)
    idx = idx_ref[...]

    def head(h, c):
        acc = jnp.full(idx.shape, NEG_INF, F32)
        for b in range(N_BUCKETS):
            acc = jnp.where(idx == b, tab_ref[b, g * N_HEADS + h], acc)
        o_ref[h] = acc
        return c
    lax.fori_loop(0, N_HEADS, head, 0)


def _bias_matrices(t5_bias):
    idx = jnp.asarray(_bucket_index_table())
    n_sub = N_GROUPS * N_HEADS
    return pl.pallas_call(
        _bias_kernel,
        grid=(N_GROUPS,),
        in_specs=[pl.BlockSpec(memory_space=pltpu.SMEM),
                  pl.BlockSpec((None, BAND, 2 * BAND), lambda g: (g, 0, 0))],
        out_specs=pl.BlockSpec((N_HEADS, BAND, 2 * BAND), lambda g: (g, 0, 0)),
        out_shape=jax.ShapeDtypeStruct((n_sub, BAND, 2 * BAND), F32),
        name="t5_bias_mats",
    )(t5_bias, idx)


def _norm_kernel(*refs, has_post, has_pre, split_in, split_out, n_prompt_tiles):
    refs = list(refs)
    xp_ref = refs.pop(0)
    xs_ref = refs.pop(0) if split_in else xp_ref
    if has_post:
        y_ref, gpost_ref, gate_p_ref, gate_s_ref = refs[:4]
        refs = refs[4:]
    if has_pre:
        gpre_ref, shift_p_ref, scale_p_ref, shift_s_ref, scale_s_ref = refs[:5]
        refs = refs[5:]
    if has_post:
        xop_ref = refs.pop(0)
        xos_ref = refs.pop(0) if split_out else xop_ref
    if has_pre:
        h_ref = refs.pop(0)

    def rms(v, g):
        return v * lax.rsqrt(jnp.mean(v * v, axis=-1, keepdims=True) + RMS_EPS) * g

    def body(x_ref, xo_ref, rows, gate, shift, scale):
        x = x_ref[rows, :]
        if has_post:
            x = x + gate * rms(y_ref[rows, :].astype(F32), gpost_ref[...])
            xo_ref[rows, :] = x
        if has_pre:
            h = rms(x, gpre_ref[...]) * (1.0 + scale) + shift
            h_ref[rows, :] = h.astype(BF16)

    t = pl.program_id(0)

    @pl.when(t < n_prompt_tiles)
    def _():
        body(xp_ref, xop_ref if has_post else None, slice(None),
             gate_p_ref[...] if has_post else None,
             shift_p_ref[...] if has_pre else None,
             scale_p_ref[...] if has_pre else None)

    @pl.when(t == n_prompt_tiles)
    def _():
        body(xs_ref, xos_ref if has_post else None, slice(0, SAMPLE_ROWS),
             gate_s_ref[...] if has_post else None,
             shift_s_ref[...] if has_pre else None,
             scale_s_ref[...] if has_pre else None)


def _norm_step(x, y, mod, mod4, norm_post, norm_pre, post_layer, pre_layer, x_sample=None, split_out=False):
    has_post = post_layer is not None
    has_pre = pre_layer is not None
    split_in = x_sample is not None
    te = TE_NORM
    npt = M_PROMPT // te
    tiles_per_batch = SEQ // te
    row_spec = pl.BlockSpec((te, D_MODEL), lambda t: (t, 0))
    prompt_row_spec = pl.BlockSpec((te, D_MODEL), lambda t: (jnp.minimum(t, npt - 1), 0))
    sample_row_spec = pl.BlockSpec((SAMPLE_ROWS, D_MODEL), lambda t: (0, 0))

    def mod_p_spec(layer, part):
        return pl.BlockSpec((None, None, 1, D_MODEL),
                            lambda t: (layer, jnp.minimum(t // tiles_per_batch, BATCH - 1), 0, part))

    def mod_s_spec(layer, part):
        return pl.BlockSpec((None, SAMPLE_ROWS, D_MODEL),
                            lambda t: (layer, C_SAMPLE_ROW0 // SAMPLE_ROWS, part))

    def gain_spec(layer):
        return pl.BlockSpec((None, 1, D_MODEL), lambda t: (layer, 0, 0))

    if split_in:
        args, in_specs = [x, x_sample], [prompt_row_spec, sample_row_spec]
    else:
        args, in_specs = [x], [row_spec]
    out_shapes, out_specs = [], []
    if has_post:
        args += [y, norm_post, mod4, mod]
        in_specs += [row_spec, gain_spec(post_layer), mod_p_spec(post_layer, 2), mod_s_spec(post_layer, 2)]
        if split_out:
            out_shapes += [jax.ShapeDtypeStruct((M_PROMPT, D_MODEL), F32),
                           jax.ShapeDtypeStruct((SAMPLE_ROWS, D_MODEL), F32)]
            out_specs += [prompt_row_spec, sample_row_spec]
        else:
            out_shapes.append(jax.ShapeDtypeStruct((M_ALL, D_MODEL), F32))
            out_specs.append(row_spec)
    if has_pre:
        args += [norm_pre, mod4, mod4, mod, mod]
        in_specs += [gain_spec(pre_layer), mod_p_spec(pre_layer, 0), mod_p_spec(pre_layer, 1),
                     mod_s_spec(pre_layer, 0), mod_s_spec(pre_layer, 1)]
        out_shapes.append(jax.ShapeDtypeStruct((M_ALL, D_MODEL), BF16))
        out_specs.append(row_spec)
    outs = pl.pallas_call(
        functools.partial(_norm_kernel, has_post=has_post, has_pre=has_pre, split_in=split_in,
                          split_out=split_out, n_prompt_tiles=npt),
        grid=(npt + 1,),
        in_specs=in_specs,
        out_specs=out_specs,
        out_shape=out_shapes,
        compiler_params=pltpu.CompilerParams(
            dimension_semantics=("arbitrary",), vmem_limit_bytes=VMEM_LIMIT),
        name="norm_step",
    )(*args)
    return outs


def _mm_kernel(a_ref, w_ref, o_ref, wb_ref, *, k_rows, slabs):
    @pl.when(pl.program_id(1) == 0)
    def _():
        _cast_rows_to_bf16(w_ref, wb_ref, k_rows)

    acc = jnp.dot(a_ref[...], wb_ref[...], preferred_element_type=F32).astype(o_ref.dtype)
    if slabs:
        for c in range(o_ref.shape[0]):
            o_ref[c] = acc[:, c * HEAD_DIM:(c + 1) * HEAD_DIM]
    else:
        o_ref[...] = acc


def _matmul(a, w, layer, n_out, tn, out_dtype, name, slabs=False):
    m, k = a.shape
    tm = TM_MATMUL
    if slabs:
        out_spec = pl.BlockSpec((tn // HEAD_DIM, tm, HEAD_DIM), lambda j, i: (j, i, 0))
        out_shape = jax.ShapeDtypeStruct((n_out // HEAD_DIM, m, HEAD_DIM), out_dtype)
    else:
        out_spec = pl.BlockSpec((tm, tn), lambda j, i: (i, j))
        out_shape = jax.ShapeDtypeStruct((m, n_out), out_dtype)
    return pl.pallas_call(
        functools.partial(_mm_kernel, k_rows=k, slabs=slabs),
        grid=(n_out // tn, m // tm),
        in_specs=[pl.BlockSpec((tm, k), lambda j, i: (i, 0)),
                  pl.BlockSpec((None, k, tn), lambda j, i: (layer, 0, j))],
        out_specs=out_spec,
        out_shape=out_shape,
        scratch_shapes=[pltpu.VMEM((k, tn), BF16)],
        compiler_params=pltpu.CompilerParams(
            dimension_semantics=("arbitrary", "arbitrary"), vmem_limit_bytes=VMEM_LIMIT),
        name=name,
    )(a, w)


def _pool_kernel(u_ref, halo_ref, z_ref, st_ref, w_ref, sc_ref,
                 a_ref, pp_ref, ps_ref, wb_ref, buf_a, buf_b, *, n_prompt_tiles, tiles_per_batch):
    g = pl.program_id(0)
    t = pl.program_id(1)
    tp = TP_POOL
    h0 = POOL_HALO

    @pl.when(t == 0)
    def _():
        _cast_rows_to_bf16(w_ref, wb_ref, POOL_GROUP)

    def finish(r, z, rows):
        y = jnp.dot(r.astype(BF16), wb_ref[...], preferred_element_type=F32) * sc_ref[...]
        a_ref[rows, :] = (y * _silu(z)).astype(BF16)

    def prompt_tile(n_steps):
        w = 2 ** n_steps
        first = (t % tiles_per_batch) == 0
        buf_a[0:8, :] = jnp.zeros((8, POOL_GROUP), F32)
        buf_b[0:8, :] = jnp.zeros((8, POOL_GROUP), F32)
        buf_a[8:h0, :] = jnp.where(first, 0.0, halo_ref[...])
        buf_a[h0:h0 + tp, :] = u_ref[...]
        src, dst = buf_a, buf_b
        n = tp + h0 - 8
        for s in range(n_steps):
            sh = 2 ** s
            dst[8:8 + n, :] = src[8:8 + n, :] + src[8 - sh:8 - sh + n, :]
            src, dst = dst, src
        pos = (t % tiles_per_batch) * tp + lax.broadcasted_iota(jnp.int32, (tp, 1), 0)
        inv_cnt = 1.0 / jnp.minimum(pos + 1, w).astype(F32)
        u = u_ref[...]
        r = src[h0:h0 + tp, :] * inv_cnt - u
        finish(r, z_ref[...], slice(None))

        @pl.when((t % tiles_per_batch) == tiles_per_batch - 1)
        def _():
            pp_ref[...] = u_ref[tp - POOL_BUF:tp, :]

    def sample_tile(n_steps):
        w = 2 ** n_steps
        u_new = u_ref[0:DEC_BATCH, :]
        acc = u_new
        for k in range(1, w):
            acc = acc + st_ref[POOL_BUF - k]
        r = acc / float(w) - u_new
        buf_a[0:DEC_BATCH, :] = r
        buf_a[DEC_BATCH:SAMPLE_ROWS, :] = jnp.zeros((SAMPLE_ROWS - DEC_BATCH, POOL_GROUP), F32)
        finish(buf_a[0:SAMPLE_ROWS, :], z_ref[0:SAMPLE_ROWS, :], slice(0, SAMPLE_ROWS))
        for k in range(POOL_BUF - 1):
            ps_ref[k] = st_ref[k + 1]
        ps_ref[POOL_BUF - 1] = u_new

    for gi in range(len(POOL_WINDOWS)):
        @pl.when((g == gi) & (t < n_prompt_tiles))
        def _(gi=gi):
            prompt_tile(gi + 1)

        @pl.when((g == gi) & (t == n_prompt_tiles))
        def _(gi=gi):
            sample_tile(gi + 1)


def _pool_mix(proj, state_t, w_grp, scale, layer):
    tp = TP_POOL
    npt = M_PROMPT // tp
    tpb = SEQ // tp
    ng = len(POOL_WINDOWS)
    halo_blocks = tp // 16
    outs = pl.pallas_call(
        functools.partial(_pool_kernel, n_prompt_tiles=npt, tiles_per_batch=tpb),
        grid=(ng, npt + 1),
        in_specs=[
            pl.BlockSpec((tp, POOL_GROUP), lambda g, t: (t, g)),
            pl.BlockSpec((16, POOL_GROUP), lambda g, t: (jnp.maximum(t * halo_blocks - 1, 0), g)),
            pl.BlockSpec((tp, POOL_GROUP), lambda g, t: (t, ng + g)),
            pl.BlockSpec((None, POOL_BUF, DEC_BATCH, POOL_GROUP), lambda g, t: (layer, 0, 0, g)),
            pl.BlockSpec((None, None, POOL_GROUP, POOL_GROUP), lambda g, t: (layer, g, 0, 0)),
            pl.BlockSpec((None, 1, POOL_GROUP), lambda g, t: (layer, 0, g)),
        ],
        out_specs=[
            pl.BlockSpec((tp, POOL_GROUP), lambda g, t: (t, g)),
            pl.BlockSpec((None, POOL_BUF, POOL_GROUP),
                         lambda g, t: (jnp.minimum(t // tpb, BATCH - 1), 0, g)),
            pl.BlockSpec((POOL_BUF, DEC_BATCH, POOL_GROUP), lambda g, t: (0, 0, g)),
        ],
        out_shape=[
            jax.ShapeDtypeStruct((M_ALL, POOL_WIDTH), BF16),
            jax.ShapeDtypeStruct((BATCH, POOL_BUF, POOL_WIDTH), F32),
            jax.ShapeDtypeStruct((POOL_BUF, DEC_BATCH, POOL_WIDTH), F32),
        ],
        scratch_shapes=[pltpu.VMEM((POOL_GROUP, POOL_GROUP), BF16),
                        pltpu.VMEM((tp + POOL_HALO, POOL_GROUP), F32),
                        pltpu.VMEM((tp + POOL_HALO, POOL_GROUP), F32)],
        compiler_params=pltpu.CompilerParams(
            dimension_semantics=("arbitrary", "arbitrary"), vmem_limit_bytes=VMEM_LIMIT),
        name="pool_mix",
    )(proj, proj, proj, state_t, w_grp, scale.reshape(-1, 1, POOL_WIDTH))
    return outs


def _att_prompt_kernel(*refs):
    n_in = 13
    as_ref, a_ref = refs[n_in], refs[n_in + 1]

    @pl.when(pl.program_id(0) < BATCH)
    def _():
        _att_prompt_tile(*refs[:n_in], *refs[n_in + 1:])

    @pl.when(pl.program_id(0) == BATCH)
    def _():
        a_ref[0:SAMPLE_ROWS, :] = as_ref[...]


def _att_prompt_tile(q0, k0, v0, q1, k1, v1, q2, k2, v2, z_ref, b0, b1, b2,
                     a_ref, qd, kd, vd, od, ld, on, ln, tmp3):
    tmp = tmp3.at[0]
    q_refs, k_refs, v_refs = (q0, q1, q2), (k0, k1, k2), (v0, v1, v2)
    bias_refs = (b0, b1, b2)
    nu = ATT_UNITS

    for g, dil in enumerate(DILATIONS):
        n = SEQ // dil
        nb = n // BAND
        zero_blk = jnp.zeros((BAND, HEAD_DIM), BF16)
        kd[g, 0:BAND, :] = zero_blk
        vd[g, 0:BAND, :] = zero_blk
        for src, dst, off, mul in ((q_refs[g], qd, 0, ATT_SCALE * LOG2E), (k_refs[g], kd, BAND, None),
                                   (v_refs[g], vd, BAND, None)):
            def to_bf16(x, mul=mul):
                return (x if mul is None else x * mul).astype(BF16)

            if dil == 1:
                def cp(i, c, src=src, dst=dst, g=g, off=off, to_bf16=to_bf16):
                    r = pl.multiple_of(i * 256, 256)
                    dst[g, pl.ds(off + r, 256), :] = to_bf16(src[pl.ds(r, 256), :])
                    return c
                lax.fori_loop(0, SEQ // 256, cp, 0)
            elif dil == 4:
                for r in range(dil):
                    dst[g, off + r * n:off + (r + 1) * n, :] = to_bf16(src[pl.ds(r, n, stride=dil), :])
            else:
                stage = tmp3.at[(off > 0) + (dst is vd)]
                quarter = SEQ // 4
                for r_lo in range(4):
                    stage[r_lo * quarter:(r_lo + 1) * quarter, :] = src[pl.ds(r_lo, quarter, stride=4), :]
                for r in range(dil):
                    r_lo, r_hi = r % 4, r // 4
                    dst[g, off + r * n:off + (r + 1) * n, :] = to_bf16(
                        stage[pl.ds(r_lo * quarter + r_hi, n, stride=4), :])

    def unit_batch(g, u0, seq_blocks):
        with_prev = seq_blocks > 1
        rows = nu * BAND
        row = pl.multiple_of(u0 * BAND, rows)
        shape3 = (nu, BAND, HEAD_DIM)
        q = qd[g, pl.ds(row, rows), :].reshape(shape3)
        kc = kd[g, pl.ds(row + BAND, rows), :].reshape(shape3)
        vc = vd[g, pl.ds(row + BAND, rows), :].reshape(shape3)
        bias = bias_refs[g]
        s_c = jnp.einsum('uqe,uke->uqk', q, kc, preferred_element_type=F32) \
            + (bias[:, BAND:2 * BAND] * LOG2E)[None]
        if with_prev:
            kp = kd[g, pl.ds(row, rows), :].reshape(shape3)
            vp = vd[g, pl.ds(row, rows), :].reshape(shape3)
            bias_p = jnp.broadcast_to((bias[:, 0:BAND] * LOG2E)[None], (nu, BAND, BAND))
            blk = u0 + lax.broadcasted_iota(jnp.int32, (nu, BAND, BAND), 0)
            bias_p = jnp.where((blk & (seq_blocks - 1)) == 0, NEG_INF, bias_p)
            s_p = jnp.einsum('uqe,uke->uqk', q, kp, preferred_element_type=F32) + bias_p
            m = jnp.max(jnp.maximum(s_c, s_p), axis=-1, keepdims=True)
            p_c = jnp.exp2(s_c - m)
            p_p = jnp.exp2(s_p - m)
            l = jnp.sum(p_c + p_p, axis=-1, keepdims=True)
            acc = jnp.einsum('uqk,uke->uqe', p_c.astype(BF16), vc, preferred_element_type=F32) \
                + jnp.einsum('uqk,uke->uqe', p_p.astype(BF16), vp, preferred_element_type=F32)
        else:
            m = jnp.max(s_c, axis=-1, keepdims=True)
            p_c = jnp.exp2(s_c - m)
            l = jnp.sum(p_c, axis=-1, keepdims=True)
            acc = jnp.einsum('uqk,uke->uqe', p_c.astype(BF16), vc, preferred_element_type=F32)
        od[g, pl.ds(row, rows), :] = (acc / l).reshape(rows, HEAD_DIM)
        ld[g, pl.ds(row, rows), :] = jnp.broadcast_to(m + jnp.log(l) * LOG2E, shape3).reshape(rows, HEAD_DIM)

    for g, dil in enumerate(DILATIONS):
        nb = (SEQ // dil) // BAND

        def batch_body(i, c, g=g, nb=nb):
            unit_batch(g, i * nu, nb)
            return c
        lax.fori_loop(0, N_BLOCKS // nu, batch_body, 0)

    def interleave4(src, dst, n_rows):
        quarter = n_rows // 4
        for r in range(4):
            dst[pl.ds(r, quarter, stride=4), :] = src[r * quarter:(r + 1) * quarter, :]

    for src_all, dst_all in ((od, on), (ld, ln)):
        interleave4(src_all.at[1], dst_all.at[0], SEQ)
        for r_lo in range(4):
            for r_hi in range(4):
                r = r_lo + 4 * r_hi
                tmp[pl.ds(r_lo * 512 + r_hi, BAND, stride=4), :] = src_all[2, r * BAND:(r + 1) * BAND, :]
        interleave4(tmp, dst_all.at[1], SEQ)

    def comb(i, c):
        rs = pl.ds(pl.multiple_of(i * 256, 256), 256)
        l0, l1, l2 = ld[0, rs, :], ln[0, rs, :], ln[1, rs, :]
        mx = jnp.maximum(jnp.maximum(l0, l1), l2)
        w0, w1, w2 = jnp.exp2(l0 - mx), jnp.exp2(l1 - mx), jnp.exp2(l2 - mx)
        o = (w0 * od[0, rs, :] + w1 * on[0, rs, :] + w2 * on[1, rs, :]) / (w0 + w1 + w2)
        a_ref[rs, :] = (o * _silu(z_ref[rs, :])).astype(BF16)
        return c
    lax.fori_loop(0, SEQ // 256, comb, 0)


def _att_prompt(proj, bias_mats, a_sample):
    hb = N_HEADS

    def head_blk(b, h):
        return jnp.where(b < BATCH, h, hb - 1)

    in_specs = []
    for g in range(N_GROUPS):
        for part in range(3):
            off = part * N_GROUPS * hb + g * hb
            in_specs.append(pl.BlockSpec(
                (None, SEQ, HEAD_DIM),
                lambda b, h, off=off: (off + head_blk(b, h), jnp.minimum(b, BATCH - 1), 0)))
    in_specs.append(pl.BlockSpec((None, SEQ, HEAD_DIM),
                                 lambda b, h: (9 * hb + head_blk(b, h), jnp.minimum(b, BATCH - 1), 0)))
    for g in range(N_GROUPS):
        in_specs.append(pl.BlockSpec((None, BAND, 2 * BAND), lambda b, h, g=g: (g * hb + head_blk(b, h), 0, 0)))
    in_specs.append(pl.BlockSpec((SAMPLE_ROWS, HEAD_DIM), lambda b, h: (0, h)))
    scratch = [pltpu.VMEM((N_GROUPS, SEQ, HEAD_DIM), BF16)] + \
              [pltpu.VMEM((N_GROUPS, SEQ + BAND, HEAD_DIM), BF16)] * 2 + \
              [pltpu.VMEM((N_GROUPS, SEQ, HEAD_DIM), F32)] * 2 + \
              [pltpu.VMEM((N_GROUPS - 1, SEQ, HEAD_DIM), F32)] * 2 + \
              [pltpu.VMEM((3, SEQ, HEAD_DIM), F32)]
    return pl.pallas_call(
        _att_prompt_kernel,
        grid=(BATCH + 1, N_HEADS),
        in_specs=in_specs,
        out_specs=pl.BlockSpec((SEQ, HEAD_DIM), lambda b, h: (b, h)),
        out_shape=jax.ShapeDtypeStruct((M_ALL, D_MODEL), BF16),
        scratch_shapes=scratch,
        compiler_params=pltpu.CompilerParams(
            dimension_semantics=("arbitrary", "arbitrary"), vmem_limit_bytes=VMEM_LIMIT),
        name="att_prompt",
    )(*([proj] * 10), bias_mats, bias_mats, bias_mats, a_sample)


def _kv_rows_kernel(src_a, src_b, o_ref, *, tr):
    pitch = tr + KV_PITCH_PAD

    def emit(src):
        flat = src.reshape(N_HEADS * pitch, HEAD_DIM)

        def body(i, c):
            t0 = i * 8
            for k in range(8):
                o_ref[t0 + k] = flat[pl.ds(t0 + k, N_HEADS, stride=pitch), :]
            return c
        lax.fori_loop(0, tr // 8, body, 0)

    @pl.when(pl.program_id(0) == 0)
    def _():
        emit(src_a)

    @pl.when(pl.program_id(0) == 1)
    def _():
        emit(src_b)


def _kv_rows(proj_a, proj_b, g):
    keep = WINDOWS[g]
    tr = min(keep, 512)
    nt = keep // tr
    first_blk = (SEQ - keep) // tr
    blks_per_batch = SEQ // tr

    def row_blk(b, t):
        return b * blks_per_batch + first_blk + t

    def col_blk(part):
        return (1 + part) * N_GROUPS + g

    def spec(layer):
        parked = (row_blk(BATCH - 1, nt - 1), col_blk(1)) if layer == 0 else (row_blk(0, 0), col_blk(0))
        return pl.BlockSpec((pl.Element(N_HEADS), pl.Element(tr + KV_PITCH_PAD), pl.Element(HEAD_DIM)),
                            lambda l, b, t, p: (jnp.where(l == layer, col_blk(p), parked[1]) * N_HEADS,
                                                jnp.where(l == layer, row_blk(b, t), parked[0]) * tr, 0))

    return pl.pallas_call(
        functools.partial(_kv_rows_kernel, tr=tr),
        grid=(2, BATCH, nt, 2),
        in_specs=[spec(0), spec(1)],
        out_specs=pl.BlockSpec((None, None, tr, None, N_HEADS, HEAD_DIM), lambda l, b, t, p: (l, b, t, p, 0, 0)),
        out_shape=jax.ShapeDtypeStruct((2, BATCH, keep, 2, N_HEADS, HEAD_DIM), F32),
        compiler_params=pltpu.CompilerParams(
            dimension_semantics=("arbitrary",) * 4, vmem_limit_bytes=VMEM_LIMIT),
        name="kv_rows",
    )(proj_a, proj_b)


def _att_sample_kernel(x_ref, c0, c1, c2, bias_ref, a_ref, kv0, kv1, kv2):
    caches = (c0, c1, c2)
    kv_outs = (kv0, kv1, kv2)
    hb = N_HEADS
    outs, lses = [], []
    for g in range(N_GROUPS):
        q = x_ref[g * hb:(g + 1) * hb, :]
        kn = x_ref[(3 + g) * hb:(4 + g) * hb, :]
        vn = x_ref[(6 + g) * hb:(7 + g) * hb, :]
        kv_outs[g][0] = kn
        kv_outs[g][1] = vn
        kc = caches[g][:, 0]
        vc = caches[g][:, 1]
        s_c = jnp.sum(kc * q[None], axis=-1, keepdims=True) * ATT_SCALE + bias_ref[g, 0:BAND]
        s_n = jnp.sum(kn * q, axis=-1, keepdims=True) * ATT_SCALE + bias_ref[g, BAND]
        m = jnp.maximum(jnp.max(s_c, axis=0), s_n)
        p_c = jnp.exp(s_c - m[None])
        p_n = jnp.exp(s_n - m)
        l = jnp.sum(p_c, axis=0) + p_n
        o = (jnp.sum(p_c * vc, axis=0) + p_n * vn) / l
        outs.append(o)
        lses.append(m + jnp.log(l))
    mx = jnp.maximum(jnp.maximum(lses[0], lses[1]), lses[2])
    ws = [jnp.exp(ls - mx) for ls in lses]
    o = (ws[0] * outs[0] + ws[1] * outs[1] + ws[2] * outs[2]) / (ws[0] + ws[1] + ws[2])
    z = x_ref[9 * hb:10 * hb, :]
    a_ref[...] = o * _silu(z)


def _att_sample(proj_s, caches, bias_s, layer):
    hb = N_HEADS
    in_specs = [pl.BlockSpec((None, ATT_IN_COLS // HEAD_DIM, HEAD_DIM), lambda b: (b, 0, 0))]
    for g in range(N_GROUPS):
        in_specs.append(pl.BlockSpec((None, None, BAND, None, 2, hb, HEAD_DIM),
                                     lambda b: (layer, b, 0, 0, 0, 0, 0)))
    in_specs.append(pl.BlockSpec((N_GROUPS, BAND + 1, hb, 1), lambda b: (0, 0, 0, 0)))
    out_specs = [pl.BlockSpec((None, hb, HEAD_DIM), lambda b: (b, 0, 0))]
    out_shapes = [jax.ShapeDtypeStruct((DEC_BATCH, hb, HEAD_DIM), F32)]
    for g in range(N_GROUPS):
        out_specs.append(pl.BlockSpec((None, 2, hb, HEAD_DIM), lambda b: (b, 0, 0, 0)))
        out_shapes.append(jax.ShapeDtypeStruct((DEC_BATCH, 2, hb, HEAD_DIM), F32))
    return pl.pallas_call(
        _att_sample_kernel,
        grid=(DEC_BATCH,),
        in_specs=in_specs,
        out_specs=out_specs,
        out_shape=out_shapes,
        compiler_params=pltpu.CompilerParams(
            dimension_semantics=("arbitrary",), vmem_limit_bytes=VMEM_LIMIT),
        name="att_sample",
    )(proj_s, *caches, bias_s)


def kernel(x_prompt, x_sample, c_prompt, c_sample, cache_kv0, cache_kv1, cache_kv2, state_pool,
           norm_pre, norm_post, ada_w, ada_b, t5_bias, pool_w_in, pool_w_grp, pool_scale,
           pool_w_out, att_w_in, att_w_out):
    n_att = DEPTH // 2
    xp0 = x_prompt.reshape(M_PROMPT, D_MODEL)
    xs0 = jnp.zeros((SAMPLE_ROWS, D_MODEL), F32).at[0:DEC_BATCH].set(x_sample.reshape(DEC_BATCH, D_MODEL))
    c_all = jnp.zeros((C_ROWS, D_MODEL), F32)
    c_all = c_all.at[0:BATCH].set(c_prompt).at[C_SAMPLE_ROW0:C_SAMPLE_ROW0 + DEC_BATCH].set(c_sample)

    mod = _ada_all(c_all, ada_w, ada_b)
    mod4 = mod.reshape(DEPTH, C_ROWS, 1, 3 * D_MODEL)
    gains_pre = norm_pre.reshape(DEPTH, 1, D_MODEL)
    gains_post = norm_post.reshape(DEPTH, 1, D_MODEL)

    bias_mats = _bias_matrices(t5_bias)
    bias_s = bias_mats[:, 0, 0:BAND + 1].reshape(N_GROUPS, N_HEADS, BAND + 1)
    bias_s = jnp.transpose(bias_s, (0, 2, 1))[..., None]
    caches = [c.reshape(n_att, DEC_BATCH, BAND, dil, 2, N_HEADS, HEAD_DIM)
              for c, dil in zip((cache_kv0, cache_kv1, cache_kv2), DILATIONS)]
    state_t = jnp.transpose(state_pool, (0, 2, 1, 3))

    kv_s = [[] for _ in range(N_GROUPS)]
    pool_p, pool_s, att_projs = [], [], []

    (h,) = _norm_step(xp0, None, mod, mod4, gains_post, gains_pre, None, 0, x_sample=xs0)
    x = None
    for i in range(DEPTH):
        li = i // 2
        if i % 2 == 0:
            proj = _matmul(h, pool_w_in, li, 2 * POOL_WIDTH, 1024, F32, "pool_in_proj")
            a, pp, ps = _pool_mix(proj, state_t, pool_w_grp, pool_scale, li)
            pool_p.append(pp)
            pool_s.append(jnp.transpose(ps, (1, 0, 2)))
            y = _matmul(a, pool_w_out, li, D_MODEL, 512, BF16, "pool_out_proj")
        else:
            proj = _matmul(h, att_w_in, li, ATT_IN_COLS, 1024, F32, "att_in_proj", slabs=True)
            att_projs.append(proj)
            proj_s = jnp.transpose(proj[:, M_PROMPT:M_PROMPT + DEC_BATCH, :], (1, 0, 2))
            souts = _att_sample(proj_s, caches, bias_s, li)
            a_s = jnp.zeros((SAMPLE_ROWS, D_MODEL), F32).at[0:DEC_BATCH].set(souts[0].reshape(DEC_BATCH, D_MODEL))
            a = _att_prompt(proj, bias_mats, a_s.astype(BF16))
            for g in range(N_GROUPS):
                kv_s[g].append(souts[1 + g].reshape(DEC_BATCH, 1, 2, N_HEADS, HEAD_DIM))
            y = _matmul(a, att_w_out, li, D_MODEL, 512, BF16, "att_out_proj")
        last = i + 1 == DEPTH
        if i == 0:
            x, h = _norm_step(xp0, y, mod, mod4, gains_post, gains_pre, i, i + 1, x_sample=xs0)
        elif not last:
            x, h = _norm_step(x, y, mod, mod4, gains_post, gains_pre, i, i + 1)
        else:
            y_p, y_s = _norm_step(x, y, mod, mod4, gains_post, gains_pre, i, None, split_out=True)

    kv_p = [_kv_rows(att_projs[0], att_projs[1], g) for g in range(N_GROUPS)]
    y_prompt = y_p.reshape(BATCH, SEQ, D_MODEL)
    y_sample = y_s[0:DEC_BATCH].reshape(DEC_BATCH, 1, D_MODEL)
    return (y_prompt, y_sample, kv_p[0], kv_p[1], kv_p[2], jnp.stack(pool_p),
            jnp.stack(kv_s[0]), jnp.stack(kv_s[1]), jnp.stack(kv_s[2]), jnp.stack(pool_s))
```

```python
import functools

import numpy as np
import jax
import jax.numpy as jnp
from jax import lax
from jax.experimental import pallas as pl
from jax.experimental.pallas import tpu as pltpu

D_MODEL = 2048
BATCH = 4
SEQ = 2048
DEPTH = 4
DEC_BATCH = 8
HEAD_DIM = 128
N_HEADS = 16
DILATIONS = (1, 4, 16)
WINDOWS = (128, 512, 2048)
N_GROUPS = 3
QKV_WIDTH = N_GROUPS * D_MODEL
ATT_IN_COLS = 3 * QKV_WIDTH + D_MODEL
BAND = 128
ATT_SCALE = HEAD_DIM ** -0.5
LOG2E = 1.4426950408889634
POOL_WINDOWS = (2, 4, 8, 16)
POOL_WIDTH = 2 * D_MODEL
POOL_GROUP = POOL_WIDTH // 4
POOL_BUF = 15
N_BUCKETS = 32
T5_MAX_DIST = 2048
RMS_EPS = 1e-6
NEG_INF = -1e30

M_PROMPT = BATCH * SEQ
M_ALL = M_PROMPT + 16
SAMPLE_ROWS = 16
C_ROWS = 32
C_SAMPLE_ROW0 = 16

TM_MATMUL = 912
TE_NORM = 512
TP_POOL = 1024
POOL_CHUNK = 256
POOL_HALO = 24
ATT_UNITS = 16
N_BLOCKS = SEQ // BAND
KV_PITCH_PAD = 8
VMEM_LIMIT = 56 * 1024 * 1024

F32 = jnp.float32
BF16 = jnp.bfloat16


def _silu(x):
    half = 0.5 * x
    return half + half * jnp.tanh(half)


def _cast_rows_to_bf16(src_ref, dst_ref, rows, chunk=256):
    def body(i, c):
        r = pl.multiple_of(i * chunk, chunk)
        dst_ref[pl.ds(r, chunk), :] = src_ref[pl.ds(r, chunk), :].astype(BF16)
        return c
    lax.fori_loop(0, rows // chunk, body, 0)


def _ada_kernel(c_ref, w_ref, b_ref, o_ref):
    a = _silu(c_ref[...]).astype(BF16)
    kc = 512
    acc = jnp.zeros(o_ref.shape, F32)
    for k0 in range(0, D_MODEL, kc):
        acc = acc + jnp.dot(a[:, k0:k0 + kc], w_ref[k0:k0 + kc, :].astype(BF16),
                            preferred_element_type=F32)
    o_ref[...] = acc + b_ref[...]


def _ada_all(c_all, ada_w, ada_b):
    tn = 1024
    n = 3 * D_MODEL
    return pl.pallas_call(
        _ada_kernel,
        grid=(DEPTH, n // tn),
        in_specs=[pl.BlockSpec((C_ROWS, D_MODEL), lambda l, j: (0, 0)),
                  pl.BlockSpec((None, D_MODEL, tn), lambda l, j: (l, 0, j)),
                  pl.BlockSpec((None, 1, tn), lambda l, j: (l, 0, j))],
        out_specs=pl.BlockSpec((None, C_ROWS, tn), lambda l, j: (l, 0, j)),
        out_shape=jax.ShapeDtypeStruct((DEPTH, C_ROWS, n), F32),
        compiler_params=pltpu.CompilerParams(
            dimension_semantics=("arbitrary", "arbitrary"), vmem_limit_bytes=VMEM_LIMIT),
        name="ada_mod",
    )(c_all, ada_w, ada_b.reshape(DEPTH, 1, n))


def _t5_bucket(dist):
    dist = np.asarray(dist, dtype=np.int64)
    max_exact = N_BUCKETS // 2
    ratio = np.log(np.maximum(dist, 1) / max_exact) / np.log(T5_MAX_DIST / max_exact)
    large = np.minimum(max_exact + (ratio * (N_BUCKETS - max_exact)).astype(np.int64), N_BUCKETS - 1)
    return np.where(dist < max_exact, dist, large).astype(np.int32)


def _bucket_index_table():
    rel = np.arange(BAND)[:, None] + BAND - np.arange(2 * BAND)[None, :]
    inband = (rel >= 0) & (rel <= BAND)
    out = []
    for dil in DILATIONS:
        bucket = _t5_bucket(np.clip(rel, 0, BAND) * dil)
        out.append(np.where(inband, bucket, -1))
    return np.stack(out).astype(np.int32)


def _bias_kernel(tab_ref, idx_ref, o_ref):
    g = pl.program_id(0)
    idx = idx_ref[...]

    def head(h, c):
        acc = jnp.full(idx.shape, NEG_INF, F32)
        for b in range(N_BUCKETS):
            acc = jnp.where(idx == b, tab_ref[b, g * N_HEADS + h], acc)
        o_ref[h] = acc
        return c
    lax.fori_loop(0, N_HEADS, head, 0)


def _bias_matrices(t5_bias):
    idx = jnp.asarray(_bucket_index_table())
    n_sub = N_GROUPS * N_HEADS
    return pl.pallas_call(
        _bias_kernel,
        grid=(N_GROUPS,),
        in_specs=[pl.BlockSpec(memory_space=pltpu.SMEM),
                  pl.BlockSpec((None, BAND, 2 * BAND), lambda g: (g, 0, 0))],
        out_specs=pl.BlockSpec((N_HEADS, BAND, 2 * BAND), lambda g: (g, 0, 0)),
        out_shape=jax.ShapeDtypeStruct((n_sub, BAND, 2 * BAND), F32),
        name="t5_bias_mats",
    )(t5_bias, idx)


def _norm_kernel(*refs, has_post, has_pre, split_in, split_out, n_prompt_tiles):
    refs = list(refs)
    xp_ref = refs.pop(0)
    xs_ref = refs.pop(0) if split_in else xp_ref
    if has_post:
        y_ref, gpost_ref, gate_p_ref, gate_s_ref = refs[:4]
        refs = refs[4:]
    if has_pre:
        gpre_ref, shift_p_ref, scale_p_ref, shift_s_ref, scale_s_ref = refs[:5]
        refs = refs[5:]
    if has_post:
        xop_ref = refs.pop(0)
        xos_ref = refs.pop(0) if split_out else xop_ref
    if has_pre:
        h_ref = refs.pop(0)

    def rms(v, g):
        return v * lax.rsqrt(jnp.mean(v * v, axis=-1, keepdims=True) + RMS_EPS) * g

    def body(x_ref, xo_ref, rows, gate, shift, scale):
        x = x_ref[rows, :]
        if has_post:
            x = x + gate * rms(y_ref[rows, :].astype(F32), gpost_ref[...])
            xo_ref[rows, :] = x
        if has_pre:
            h = rms(x, gpre_ref[...]) * (1.0 + scale) + shift
            h_ref[rows, :] = h.astype(BF16)

    t = pl.program_id(0)

    @pl.when(t < n_prompt_tiles)
    def _():
        body(xp_ref, xop_ref if has_post else None, slice(None),
             gate_p_ref[...] if has_post else None,
             shift_p_ref[...] if has_pre else None,
             scale_p_ref[...] if has_pre else None)

    @pl.when(t == n_prompt_tiles)
    def _():
        body(xs_ref, xos_ref if has_post else None, slice(0, SAMPLE_ROWS),
             gate_s_ref[...] if has_post else None,
             shift_s_ref[...] if has_pre else None,
             scale_s_ref[...] if has_pre else None)


def _norm_step(x, y, mod, mod4, norm_post, norm_pre, post_layer, pre_layer, x_sample=None, split_out=False):
    has_post = post_layer is not None
    has_pre = pre_layer is not None
    split_in = x_sample is not None
    te = TE_NORM
    npt = M_PROMPT // te
    tiles_per_batch = SEQ // te
    row_spec = pl.BlockSpec((te, D_MODEL), lambda t: (t, 0))
    prompt_row_spec = pl.BlockSpec((te, D_MODEL), lambda t: (jnp.minimum(t, npt - 1), 0))
    sample_row_spec = pl.BlockSpec((SAMPLE_ROWS, D_MODEL), lambda t: (0, 0))

    def mod_p_spec(layer, part):
        return pl.BlockSpec((None, None, 1, D_MODEL),
                            lambda t: (layer, jnp.minimum(t // tiles_per_batch, BATCH - 1), 0, part))

    def mod_s_spec(layer, part):
        return pl.BlockSpec((None, SAMPLE_ROWS, D_MODEL),
                            lambda t: (layer, C_SAMPLE_ROW0 // SAMPLE_ROWS, part))

    def gain_spec(layer):
        return pl.BlockSpec((None, 1, D_MODEL), lambda t: (layer, 0, 0))

    if split_in:
        args, in_specs = [x, x_sample], [prompt_row_spec, sample_row_spec]
    else:
        args, in_specs = [x], [row_spec]
    out_shapes, out_specs = [], []
    if has_post:
        args += [y, norm_post, mod4, mod]
        in_specs += [row_spec, gain_spec(post_layer), mod_p_spec(post_layer, 2), mod_s_spec(post_layer, 2)]
        if split_out:
            out_shapes += [jax.ShapeDtypeStruct((M_PROMPT, D_MODEL), F32),
                           jax.ShapeDtypeStruct((SAMPLE_ROWS, D_MODEL), F32)]
            out_specs += [prompt_row_spec, sample_row_spec]
        else:
            out_shapes.append(jax.ShapeDtypeStruct((M_ALL, D_MODEL), F32))
            out_specs.append(row_spec)
    if has_pre:
        args += [norm_pre, mod4, mod4, mod, mod]
        in_specs += [gain_spec(pre_layer), mod_p_spec(pre_layer, 0), mod_p_spec(pre_layer, 1),
                     mod_s_spec(pre_layer, 0), mod_s_spec(pre_layer, 1)]
        out_shapes.append(jax.ShapeDtypeStruct((M_ALL, D_MODEL), BF16))
        out_specs.append(row_spec)
    outs = pl.pallas_call(
        functools.partial(_norm_kernel, has_post=has_post, has_pre=has_pre, split_in=split_in,
                          split_out=split_out, n_prompt_tiles=npt),
        grid=(npt + 1,),
        in_specs=in_specs,
        out_specs=out_specs,
        out_shape=out_shapes,
        compiler_params=pltpu.CompilerParams(
            dimension_semantics=("arbitrary",), vmem_limit_bytes=VMEM_LIMIT),
        name="norm_step",
    )(*args)
    return outs


def _mm_kernel(a_ref, w_ref, o_ref, wb_ref, *, k_rows, slabs):
    @pl.when(pl.program_id(1) == 0)
    def _():
        _cast_rows_to_bf16(w_ref, wb_ref, k_rows)

    acc = jnp.dot(a_ref[...], wb_ref[...], preferred_element_type=F32).astype(o_ref.dtype)
    if slabs:
        for c in range(o_ref.shape[0]):
            o_ref[c] = acc[:, c * HEAD_DIM:(c + 1) * HEAD_DIM]
    else:
        o_ref[...] = acc


def _matmul(a, w, layer, n_out, tn, out_dtype, name, slabs=False):
    m, k = a.shape
    tm = TM_MATMUL
    if slabs:
        out_spec = pl.BlockSpec((tn // HEAD_DIM, tm, HEAD_DIM), lambda j, i: (j, i, 0))
        out_shape = jax.ShapeDtypeStruct((n_out // HEAD_DIM, m, HEAD_DIM), out_dtype)
    else:
        out_spec = pl.BlockSpec((tm, tn), lambda j, i: (i, j))
        out_shape = jax.ShapeDtypeStruct((m, n_out), out_dtype)
    return pl.pallas_call(
        functools.partial(_mm_kernel, k_rows=k, slabs=slabs),
        grid=(n_out // tn, m // tm),
        in_specs=[pl.BlockSpec((tm, k), lambda j, i: (i, 0)),
                  pl.BlockSpec((None, k, tn), lambda j, i: (layer, 0, j))],
        out_specs=out_spec,
        out_shape=out_shape,
        scratch_shapes=[pltpu.VMEM((k, tn), BF16)],
        compiler_params=pltpu.CompilerParams(
            dimension_semantics=("arbitrary", "arbitrary"), vmem_limit_bytes=VMEM_LIMIT),
        name=name,
    )(a, w)


def _pool_kernel(u_ref, halo_ref, z_ref, st_ref, w_ref, sc_ref,
                 a_ref, pp_ref, ps_ref, wb_ref, buf_a, buf_b, *, n_prompt_tiles, tiles_per_batch):
    g = pl.program_id(0)
    t = pl.program_id(1)
    tp = TP_POOL
    ch = POOL_CHUNK
    h0 = POOL_HALO

    @pl.when(t == 0)
    def _():
        _cast_rows_to_bf16(w_ref, wb_ref, POOL_GROUP)

    def finish(r, z, rows):
        y = jnp.dot(r.astype(BF16), wb_ref[...], preferred_element_type=F32) * sc_ref[...]
        a_ref[rows, :] = (y * _silu(z)).astype(BF16)

    def prompt_tile(n_steps):
        w = 2 ** n_steps
        first = (t % tiles_per_batch) == 0
        zero8 = jnp.zeros((8, POOL_GROUP), F32)
        n = ch + h0 - 8
        for c in range(tp // ch):
            r0 = c * ch
            src, dst = buf_a.at[c], buf_b.at[c]
            src[0:8, :] = zero8
            dst[0:8, :] = zero8
            if c == 0:
                src[8:h0, :] = jnp.where(first, 0.0, halo_ref[...])
            else:
                src[8:h0, :] = u_ref[r0 - 16:r0, :]
            u = u_ref[r0:r0 + ch, :]
            src[h0:h0 + ch, :] = u
            for s in range(n_steps):
                sh = 2 ** s
                dst[8:8 + n, :] = src[8:8 + n, :] + src[8 - sh:8 - sh + n, :]
                src, dst = dst, src
            pos = (t % tiles_per_batch) * tp + r0 + lax.broadcasted_iota(jnp.int32, (ch, 1), 0)
            inv_cnt = 1.0 / jnp.minimum(pos + 1, w).astype(F32)
            r = src[h0:h0 + ch, :] * inv_cnt - u
            finish(r, z_ref[r0:r0 + ch, :], slice(r0, r0 + ch))

        @pl.when((t % tiles_per_batch) == tiles_per_batch - 1)
        def _():
            pp_ref[...] = u_ref[tp - POOL_BUF:tp, :]

    def sample_tile(n_steps):
        w = 2 ** n_steps
        u_new = u_ref[0:DEC_BATCH, :]
        acc = u_new
        for k in range(1, w):
            acc = acc + st_ref[POOL_BUF - k]
        r = acc / float(w) - u_new
        rows16 = buf_a.at[0]
        rows16[0:DEC_BATCH, :] = r
        rows16[DEC_BATCH:SAMPLE_ROWS, :] = jnp.zeros((SAMPLE_ROWS - DEC_BATCH, POOL_GROUP), F32)
        finish(rows16[0:SAMPLE_ROWS, :], z_ref[0:SAMPLE_ROWS, :], slice(0, SAMPLE_ROWS))
        for k in range(POOL_BUF - 1):
            ps_ref[k] = st_ref[k + 1]
        ps_ref[POOL_BUF - 1] = u_new

    for gi in range(len(POOL_WINDOWS)):
        @pl.when((g == gi) & (t < n_prompt_tiles))
        def _(gi=gi):
            prompt_tile(gi + 1)

        @pl.when((g == gi) & (t == n_prompt_tiles))
        def _(gi=gi):
            sample_tile(gi + 1)


def _pool_mix(proj, state_t, w_grp, scale, layer):
    tp = TP_POOL
    npt = M_PROMPT // tp
    tpb = SEQ // tp
    ng = len(POOL_WINDOWS)
    halo_blocks = tp // 16
    outs = pl.pallas_call(
        functools.partial(_pool_kernel, n_prompt_tiles=npt, tiles_per_batch=tpb),
        grid=(ng, npt + 1),
        in_specs=[
            pl.BlockSpec((tp, POOL_GROUP), lambda g, t: (t, g)),
            pl.BlockSpec((16, POOL_GROUP), lambda g, t: (jnp.maximum(t * halo_blocks - 1, 0), g)),
            pl.BlockSpec((tp, POOL_GROUP), lambda g, t: (t, ng + g)),
            pl.BlockSpec((None, POOL_BUF, DEC_BATCH, POOL_GROUP), lambda g, t: (layer, 0, 0, g)),
            pl.BlockSpec((None, None, POOL_GROUP, POOL_GROUP), lambda g, t: (layer, g, 0, 0)),
            pl.BlockSpec((None, 1, POOL_GROUP), lambda g, t: (layer, 0, g)),
        ],
        out_specs=[
            pl.BlockSpec((tp, POOL_GROUP), lambda g, t: (t, g)),
            pl.BlockSpec((None, POOL_BUF, POOL_GROUP),
                         lambda g, t: (jnp.minimum(t // tpb, BATCH - 1), 0, g)),
            pl.BlockSpec((POOL_BUF, DEC_BATCH, POOL_GROUP), lambda g, t: (0, 0, g)),
        ],
        out_shape=[
            jax.ShapeDtypeStruct((M_ALL, POOL_WIDTH), BF16),
            jax.ShapeDtypeStruct((BATCH, POOL_BUF, POOL_WIDTH), F32),
            jax.ShapeDtypeStruct((POOL_BUF, DEC_BATCH, POOL_WIDTH), F32),
        ],
        scratch_shapes=[pltpu.VMEM((POOL_GROUP, POOL_GROUP), BF16),
                        pltpu.VMEM((tp // POOL_CHUNK, POOL_CHUNK + POOL_HALO, POOL_GROUP), F32),
                        pltpu.VMEM((tp // POOL_CHUNK, POOL_CHUNK + POOL_HALO, POOL_GROUP), F32)],
        compiler_params=pltpu.CompilerParams(
            dimension_semantics=("arbitrary", "arbitrary"), vmem_limit_bytes=VMEM_LIMIT),
        name="pool_mix",
    )(proj, proj, proj, state_t, w_grp, scale.reshape(-1, 1, POOL_WIDTH))
    return outs


def _att_prompt_kernel(*refs):
    n_in = 13
    as_ref, a_ref = refs[n_in], refs[n_in + 1]

    @pl.when(pl.program_id(0) < BATCH)
    def _():
        _att_prompt_tile(*refs[:n_in], *refs[n_in + 1:])

    @pl.when(pl.program_id(0) == BATCH)
    def _():
        a_ref[0:SAMPLE_ROWS, :] = as_ref[...]


def _att_prompt_tile(q0, k0, v0, q1, k1, v1, q2, k2, v2, z_ref, b0, b1, b2,
                     a_ref, qd, kd, vd, od, ld, on, ln, tmp3):
    tmp = tmp3.at[0]
    q_refs, k_refs, v_refs = (q0, q1, q2), (k0, k1, k2), (v0, v1, v2)
    bias_refs = (b0, b1, b2)
    nu = ATT_UNITS

    for g, dil in enumerate(DILATIONS):
        n = SEQ // dil
        nb = n // BAND
        zero_blk = jnp.zeros((BAND, HEAD_DIM), BF16)
        kd[g, 0:BAND, :] = zero_blk
        vd[g, 0:BAND, :] = zero_blk
        for src, dst, off, mul in ((q_refs[g], qd, 0, ATT_SCALE * LOG2E), (k_refs[g], kd, BAND, None),
                                   (v_refs[g], vd, BAND, None)):
            def to_bf16(x, mul=mul):
                return (x if mul is None else x * mul).astype(BF16)

            if dil == 1:
                def cp(i, c, src=src, dst=dst, g=g, off=off, to_bf16=to_bf16):
                    r = pl.multiple_of(i * 256, 256)
                    dst[g, pl.ds(off + r, 256), :] = to_bf16(src[pl.ds(r, 256), :])
                    return c
                lax.fori_loop(0, SEQ // 256, cp, 0)
            elif dil == 4:
                for r in range(dil):
                    dst[g, off + r * n:off + (r + 1) * n, :] = to_bf16(src[pl.ds(r, n, stride=dil), :])
            else:
                stage = tmp3.at[(off > 0) + (dst is vd)]
                quarter = SEQ // 4
                for r_lo in range(4):
                    stage[r_lo * quarter:(r_lo + 1) * quarter, :] = src[pl.ds(r_lo, quarter, stride=4), :]
                for r in range(dil):
                    r_lo, r_hi = r % 4, r // 4
                    dst[g, off + r * n:off + (r + 1) * n, :] = to_bf16(
                        stage[pl.ds(r_lo * quarter + r_hi, n, stride=4), :])

    def unit_batch(g, u0, seq_blocks):
        with_prev = seq_blocks > 1
        rows = nu * BAND
        row = pl.multiple_of(u0 * BAND, rows)
        shape3 = (nu, BAND, HEAD_DIM)
        q = qd[g, pl.ds(row, rows), :].reshape(shape3)
        kc = kd[g, pl.ds(row + BAND, rows), :].reshape(shape3)
        vc = vd[g, pl.ds(row + BAND, rows), :].reshape(shape3)
        bias = bias_refs[g]
        s_c = jnp.einsum('uqe,uke->uqk', q, kc, preferred_element_type=F32) \
            + (bias[:, BAND:2 * BAND] * LOG2E)[None]
        if with_prev:
            kp = kd[g, pl.ds(row, rows), :].reshape(shape3)
            vp = vd[g, pl.ds(row, rows), :].reshape(shape3)
            bias_p = jnp.broadcast_to((bias[:, 0:BAND] * LOG2E)[None], (nu, BAND, BAND))
            blk = u0 + lax.broadcasted_iota(jnp.int32, (nu, BAND, BAND), 0)
            bias_p = jnp.where((blk & (seq_blocks - 1)) == 0, NEG_INF, bias_p)
            s_p = jnp.einsum('uqe,uke->uqk', q, kp, preferred_element_type=F32) + bias_p
            m = jnp.max(jnp.maximum(s_c, s_p), axis=-1, keepdims=True)
            p_c = jnp.exp2(s_c - m)
            p_p = jnp.exp2(s_p - m)
            l = jnp.sum(p_c + p_p, axis=-1, keepdims=True)
            acc = jnp.einsum('uqk,uke->uqe', p_c.astype(BF16), vc, preferred_element_type=F32) \
                + jnp.einsum('uqk,uke->uqe', p_p.astype(BF16), vp, preferred_element_type=F32)
        else:
            m = jnp.max(s_c, axis=-1, keepdims=True)
            p_c = jnp.exp2(s_c - m)
            l = jnp.sum(p_c, axis=-1, keepdims=True)
            acc = jnp.einsum('uqk,uke->uqe', p_c.astype(BF16), vc, preferred_element_type=F32)
        od[g, pl.ds(row, rows), :] = (acc / l).reshape(rows, HEAD_DIM)
        ld[g, pl.ds(row, rows), :] = jnp.broadcast_to(m + jnp.log(l) * LOG2E, shape3).reshape(rows, HEAD_DIM)

    for g, dil in enumerate(DILATIONS):
        nb = (SEQ // dil) // BAND

        def batch_body(i, c, g=g, nb=nb):
            unit_batch(g, i * nu, nb)
            return c
        lax.fori_loop(0, N_BLOCKS // nu, batch_body, 0)

    def interleave4(src, dst, n_rows):
        quarter = n_rows // 4
        for r in range(4):
            dst[pl.ds(r, quarter, stride=4), :] = src[r * quarter:(r + 1) * quarter, :]

    for src_all, dst_all in ((od, on), (ld, ln)):
        interleave4(src_all.at[1], dst_all.at[0], SEQ)
        for r_lo in range(4):
            for r_hi in range(4):
                r = r_lo + 4 * r_hi
                tmp[pl.ds(r_lo * 512 + r_hi, BAND, stride=4), :] = src_all[2, r * BAND:(r + 1) * BAND, :]
        interleave4(tmp, dst_all.at[1], SEQ)

    def comb(i, c):
        rs = pl.ds(pl.multiple_of(i * 256, 256), 256)
        l0, l1, l2 = ld[0, rs, :], ln[0, rs, :], ln[1, rs, :]
        mx = jnp.maximum(jnp.maximum(l0, l1), l2)
        w0, w1, w2 = jnp.exp2(l0 - mx), jnp.exp2(l1 - mx), jnp.exp2(l2 - mx)
        o = (w0 * od[0, rs, :] + w1 * on[0, rs, :] + w2 * on[1, rs, :]) / (w0 + w1 + w2)
        a_ref[rs, :] = (o * _silu(z_ref[rs, :])).astype(BF16)
        return c
    lax.fori_loop(0, SEQ // 256, comb, 0)


def _att_prompt(proj, bias_mats, a_sample):
    hb = N_HEADS

    def head_blk(b, h):
        return jnp.where(b < BATCH, h, hb - 1)

    in_specs = []
    for g in range(N_GROUPS):
        for part in range(3):
            off = part * N_GROUPS * hb + g * hb
            in_specs.append(pl.BlockSpec(
                (None, SEQ, HEAD_DIM),
                lambda b, h, off=off: (off + head_blk(b, h), jnp.minimum(b, BATCH - 1), 0)))
    in_specs.append(pl.BlockSpec((None, SEQ, HEAD_DIM),
                                 lambda b, h: (9 * hb + head_blk(b, h), jnp.minimum(b, BATCH - 1), 0)))
    for g in range(N_GROUPS):
        in_specs.append(pl.BlockSpec((None, BAND, 2 * BAND), lambda b, h, g=g: (g * hb + head_blk(b, h), 0, 0)))
    in_specs.append(pl.BlockSpec((SAMPLE_ROWS, HEAD_DIM), lambda b, h: (0, h)))
    scratch = [pltpu.VMEM((N_GROUPS, SEQ, HEAD_DIM), BF16)] + \
              [pltpu.VMEM((N_GROUPS, SEQ + BAND, HEAD_DIM), BF16)] * 2 + \
              [pltpu.VMEM((N_GROUPS, SEQ, HEAD_DIM), F32)] * 2 + \
              [pltpu.VMEM((N_GROUPS - 1, SEQ, HEAD_DIM), F32)] * 2 + \
              [pltpu.VMEM((3, SEQ, HEAD_DIM), F32)]
    return pl.pallas_call(
        _att_prompt_kernel,
        grid=(BATCH + 1, N_HEADS),
        in_specs=in_specs,
        out_specs=pl.BlockSpec((SEQ, HEAD_DIM), lambda b, h: (b, h)),
        out_shape=jax.ShapeDtypeStruct((M_ALL, D_MODEL), BF16),
        scratch_shapes=scratch,
        compiler_params=pltpu.CompilerParams(
            dimension_semantics=("arbitrary", "arbitrary"), vmem_limit_bytes=VMEM_LIMIT),
        name="att_prompt",
    )(*([proj] * 10), bias_mats, bias_mats, bias_mats, a_sample)


def _kv_rows_kernel(k_a, v_a, k_b, v_b, o_ref, *, tr):
    pitch = tr + KV_PITCH_PAD

    def emit(srcs):
        flats = [src.reshape(N_HEADS * pitch, HEAD_DIM) for src in srcs]

        def body(i, c):
            t0 = i * 8
            for k in range(8):
                for part, flat in enumerate(flats):
                    o_ref[t0 + k, part] = flat[pl.ds(t0 + k, N_HEADS, stride=pitch), :]
            return c
        lax.fori_loop(0, tr // 8, body, 0)

    @pl.when(pl.program_id(0) == 0)
    def _():
        emit((k_a, v_a))

    @pl.when(pl.program_id(0) == 1)
    def _():
        emit((k_b, v_b))


def _kv_rows(proj_a, proj_b, g):
    keep = WINDOWS[g]
    tr = min(keep, 256)
    nt = keep // tr
    first_blk = (SEQ - keep) // tr
    blks_per_batch = SEQ // tr

    def row_blk(b, t):
        return b * blks_per_batch + first_blk + t

    def spec(layer, part):
        parked = row_blk(BATCH - 1, nt - 1) if layer == 0 else row_blk(0, 0)
        col0 = ((1 + part) * N_GROUPS + g) * N_HEADS
        return pl.BlockSpec((pl.Element(N_HEADS), pl.Element(tr + KV_PITCH_PAD), pl.Element(HEAD_DIM)),
                            lambda l, b, t: (col0, jnp.where(l == layer, row_blk(b, t), parked) * tr, 0))

    return pl.pallas_call(
        functools.partial(_kv_rows_kernel, tr=tr),
        grid=(2, BATCH, nt),
        in_specs=[spec(0, 0), spec(0, 1), spec(1, 0), spec(1, 1)],
        out_specs=pl.BlockSpec((None, None, tr, 2, N_HEADS, HEAD_DIM), lambda l, b, t: (l, b, t, 0, 0, 0)),
        out_shape=jax.ShapeDtypeStruct((2, BATCH, keep, 2, N_HEADS, HEAD_DIM), F32),
        compiler_params=pltpu.CompilerParams(
            dimension_semantics=("arbitrary",) * 3, vmem_limit_bytes=VMEM_LIMIT),
        name="kv_rows",
    )(proj_a, proj_a, proj_b, proj_b)


def _att_sample_kernel(x_ref, c0, c1, c2, bias_ref, a_ref, kv0, kv1, kv2):
    caches = (c0, c1, c2)
    kv_outs = (kv0, kv1, kv2)
    hb = N_HEADS
    outs, lses = [], []
    for g in range(N_GROUPS):
        q = x_ref[g * hb:(g + 1) * hb, :]
        kn = x_ref[(3 + g) * hb:(4 + g) * hb, :]
        vn = x_ref[(6 + g) * hb:(7 + g) * hb, :]
        kv_outs[g][0] = kn
        kv_outs[g][1] = vn
        kc = caches[g][:, 0]
        vc = caches[g][:, 1]
        s_c = jnp.sum(kc * q[None], axis=-1, keepdims=True) * ATT_SCALE + bias_ref[g, 0:BAND]
        s_n = jnp.sum(kn * q, axis=-1, keepdims=True) * ATT_SCALE + bias_ref[g, BAND]
        m = jnp.maximum(jnp.max(s_c, axis=0), s_n)
        p_c = jnp.exp(s_c - m[None])
        p_n = jnp.exp(s_n - m)
        l = jnp.sum(p_c, axis=0) + p_n
        o = (jnp.sum(p_c * vc, axis=0) + p_n * vn) / l
        outs.append(o)
        lses.append(m + jnp.log(l))
    mx = jnp.maximum(jnp.maximum(lses[0], lses[1]), lses[2])
    ws = [jnp.exp(ls - mx) for ls in lses]
    o = (ws[0] * outs[0] + ws[1] * outs[1] + ws[2] * outs[2]) / (ws[0] + ws[1] + ws[2])
    z = x_ref[9 * hb:10 * hb, :]
    a_ref[...] = o * _silu(z)


def _att_sample(proj_s, caches, bias_s, layer):
    hb = N_HEADS
    in_specs = [pl.BlockSpec((None, ATT_IN_COLS // HEAD_DIM, HEAD_DIM), lambda b: (b, 0, 0))]
    for g in range(N_GROUPS):
        in_specs.append(pl.BlockSpec((None, None, BAND, None, 2, hb, HEAD_DIM),
                                     lambda b: (layer, b, 0, 0, 0, 0, 0)))
    in_specs.append(pl.BlockSpec((N_GROUPS, BAND + 1, hb, 1), lambda b: (0, 0, 0, 0)))
    out_specs = [pl.BlockSpec((None, hb, HEAD_DIM), lambda b: (b, 0, 0))]
    out_shapes = [jax.ShapeDtypeStruct((DEC_BATCH, hb, HEAD_DIM), F32)]
    for g in range(N_GROUPS):
        out_specs.append(pl.BlockSpec((None, 2, hb, HEAD_DIM), lambda b: (b, 0, 0, 0)))
        out_shapes.append(jax.ShapeDtypeStruct((DEC_BATCH, 2, hb, HEAD_DIM), F32))
    return pl.pallas_call(
        _att_sample_kernel,
        grid=(DEC_BATCH,),
        in_specs=in_specs,
        out_specs=out_specs,
        out_shape=out_shapes,
        compiler_params=pltpu.CompilerParams(
            dimension_semantics=("arbitrary",), vmem_limit_bytes=VMEM_LIMIT),
        name="att_sample",
    )(proj_s, *caches, bias_s)


def kernel(x_prompt, x_sample, c_prompt, c_sample, cache_kv0, cache_kv1, cache_kv2, state_pool,
           norm_pre, norm_post, ada_w, ada_b, t5_bias, pool_w_in, pool_w_grp, pool_scale,
           pool_w_out, att_w_in, att_w_out):
    n_att = DEPTH // 2
    xp0 = x_prompt.reshape(M_PROMPT, D_MODEL)
    xs0 = jnp.zeros((SAMPLE_ROWS, D_MODEL), F32).at[0:DEC_BATCH].set(x_sample.reshape(DEC_BATCH, D_MODEL))
    c_all = jnp.zeros((C_ROWS, D_MODEL), F32)
    c_all = c_all.at[0:BATCH].set(c_prompt).at[C_SAMPLE_ROW0:C_SAMPLE_ROW0 + DEC_BATCH].set(c_sample)

    mod = _ada_all(c_all, ada_w, ada_b)
    mod4 = mod.reshape(DEPTH, C_ROWS, 1, 3 * D_MODEL)
    gains_pre = norm_pre.reshape(DEPTH, 1, D_MODEL)
    gains_post = norm_post.reshape(DEPTH, 1, D_MODEL)

    bias_mats = _bias_matrices(t5_bias)
    bias_s = bias_mats[:, 0, 0:BAND + 1].reshape(N_GROUPS, N_HEADS, BAND + 1)
    bias_s = jnp.transpose(bias_s, (0, 2, 1))[..., None]
    caches = [c.reshape(n_att, DEC_BATCH, BAND, dil, 2, N_HEADS, HEAD_DIM)
              for c, dil in zip((cache_kv0, cache_kv1, cache_kv2), DILATIONS)]
    state_t = jnp.transpose(state_pool, (0, 2, 1, 3))

    kv_s = [[] for _ in range(N_GROUPS)]
    pool_p, pool_s, att_projs = [], [], []

    (h,) = _norm_step(xp0, None, mod, mod4, gains_post, gains_pre, None, 0, x_sample=xs0)
    x = None
    for i in range(DEPTH):
        li = i // 2
        if i % 2 == 0:
            proj = _matmul(h, pool_w_in, li, 2 * POOL_WIDTH, 1024, F32, "pool_in_proj")
            a, pp, ps = _pool_mix(proj, state_t, pool_w_grp, pool_scale, li)
            pool_p.append(pp)
            pool_s.append(jnp.transpose(ps, (1, 0, 2)))
            y = _matmul(a, pool_w_out, li, D_MODEL, 512, BF16, "pool_out_proj")
        else:
            proj = _matmul(h, att_w_in, li, ATT_IN_COLS, 1024, F32, "att_in_proj", slabs=True)
            att_projs.append(proj)
            proj_s = jnp.transpose(proj[:, M_PROMPT:M_PROMPT + DEC_BATCH, :], (1, 0, 2))
            souts = _att_sample(proj_s, caches, bias_s, li)
            a_s = jnp.zeros((SAMPLE_ROWS, D_MODEL), F32).at[0:DEC_BATCH].set(souts[0].reshape(DEC_BATCH, D_MODEL))
            a = _att_prompt(proj, bias_mats, a_s.astype(BF16))
            for g in range(N_GROUPS):
                kv_s[g].append(souts[1 + g].reshape(DEC_BATCH, 1, 2, N_HEADS, HEAD_DIM))
            y = _matmul(a, att_w_out, li, D_MODEL, 512, BF16, "att_out_proj")
        last = i + 1 == DEPTH
        if i == 0:
            x, h = _norm_step(xp0, y, mod, mod4, gains_post, gains_pre, i, i + 1, x_sample=xs0)
        elif not last:
            x, h = _norm_step(x, y, mod, mod4, gains_post, gains_pre, i, i + 1)
        else:
            y_p, y_s = _norm_step(x, y, mod, mod4, gains_post, gains_pre, i, None, split_out=True)

    kv_p = [_kv_rows(att_projs[0], att_projs[1], g) for g in range(N_GROUPS)]
    y_prompt = y_p.reshape(BATCH, SEQ, D_MODEL)
    y_sample = y_s[0:DEC_BATCH].reshape(DEC_BATCH, 1, D_MODEL)
    return (y_prompt, y_sample, kv_p[0], kv_p[1], kv_p[2], jnp.stack(pool_p),
            jnp.stack(kv_s[0]), jnp.stack(kv_s[1]), jnp.stack(kv_s[2]), jnp.stack(pool_s))
```

```python
import functools

import numpy as np
import jax
import jax.numpy as jnp
from jax import lax
from jax.experimental import pallas as pl
from jax.experimental.pallas import tpu as pltpu

D_MODEL = 2048
BATCH = 4
SEQ = 2048
DEPTH = 4
DEC_BATCH = 8
HEAD_DIM = 128
N_HEADS = 16
DILATIONS = (1, 4, 16)
WINDOWS = (128, 512, 2048)
N_GROUPS = 3
QKV_WIDTH = N_GROUPS * D_MODEL
ATT_IN_COLS = 3 * QKV_WIDTH + D_MODEL
BAND = 128
ATT_SCALE = HEAD_DIM ** -0.5
LOG2E = 1.4426950408889634
POOL_WINDOWS = (2, 4, 8, 16)
POOL_WIDTH = 2 * D_MODEL
POOL_GROUP = POOL_WIDTH // 4
POOL_BUF = 15
N_BUCKETS = 32
T5_MAX_DIST = 2048
RMS_EPS = 1e-6
NEG_INF = -1e30

M_PROMPT = BATCH * SEQ
M_ALL = M_PROMPT + 16
SAMPLE_ROWS = 16
C_ROWS = 32
C_PROMPT_ROWS = 8
C_SAMPLE_ROW0 = 16

TM_MATMUL = 912
TE_NORM = 512
TP_POOL = 1024
POOL_HALO = 24
ATT_UNITS = 16
N_BLOCKS = SEQ // BAND
KV_PITCH_PAD = 8
VMEM_LIMIT = 56 * 1024 * 1024

F32 = jnp.float32
BF16 = jnp.bfloat16


def _silu(x):
    half = 0.5 * x
    return half + half * jnp.tanh(half)


def _cast_rows_to_bf16(src_ref, dst_ref, rows, chunk=256):
    def body(i, c):
        r = pl.multiple_of(i * chunk, chunk)
        dst_ref[pl.ds(r, chunk), :] = src_ref[pl.ds(r, chunk), :].astype(BF16)
        return c
    lax.fori_loop(0, rows // chunk, body, 0)


def _ada_kernel(c_ref, w_ref, b_ref, o_ref):
    a = _silu(c_ref[...]).astype(BF16)
    kc = 512
    acc = jnp.zeros(o_ref.shape, F32)
    for k0 in range(0, D_MODEL, kc):
        acc = acc + jnp.dot(a[:, k0:k0 + kc], w_ref[k0:k0 + kc, :].astype(BF16),
                            preferred_element_type=F32)
    o_ref[...] = acc + b_ref[...]


def _ada_all(c_all, ada_w, ada_b):
    tn = 1024
    n = 3 * D_MODEL
    return pl.pallas_call(
        _ada_kernel,
        grid=(DEPTH, n // tn),
        in_specs=[pl.BlockSpec((C_ROWS, D_MODEL), lambda l, j: (0, 0)),
                  pl.BlockSpec((None, D_MODEL, tn), lambda l, j: (l, 0, j)),
                  pl.BlockSpec((None, 1, tn), lambda l, j: (l, 0, j))],
        out_specs=pl.BlockSpec((None, C_ROWS, tn), lambda l, j: (l, 0, j)),
        out_shape=jax.ShapeDtypeStruct((DEPTH, C_ROWS, n), F32),
        compiler_params=pltpu.CompilerParams(
            dimension_semantics=("arbitrary", "arbitrary"), vmem_limit_bytes=VMEM_LIMIT),
        name="ada_mod",
    )(c_all, ada_w, ada_b.reshape(DEPTH, 1, n))


def _t5_bucket(dist):
    dist = np.asarray(dist, dtype=np.int64)
    max_exact = N_BUCKETS // 2
    ratio = np.log(np.maximum(dist, 1) / max_exact) / np.log(T5_MAX_DIST / max_exact)
    large = np.minimum(max_exact + (ratio * (N_BUCKETS - max_exact)).astype(np.int64), N_BUCKETS - 1)
    return np.where(dist < max_exact, dist, large).astype(np.int32)


def _bucket_index_table():
    rel = np.arange(BAND)[:, None] + BAND - np.arange(2 * BAND)[None, :]
    inband = (rel >= 0) & (rel <= BAND)
    out = []
    for dil in DILATIONS:
        bucket = _t5_bucket(np.clip(rel, 0, BAND) * dil)
        out.append(np.where(inband, bucket, -1))
    return np.stack(out).astype(np.int32)


def _bias_kernel(tab_ref, idx_ref, o_ref):
    g = pl.program_id(0)
    idx = idx_ref[...]

    def head(h, c):
        acc = jnp.full(idx.shape, NEG_INF, F32)
        for b in range(N_BUCKETS):
            acc = jnp.where(idx == b, tab_ref[b, g * N_HEADS + h], acc)
        o_ref[h] = acc
        return c
    lax.fori_loop(0, N_HEADS, head, 0)


def _bias_matrices(t5_bias):
    idx = jnp.asarray(_bucket_index_table())
    n_sub = N_GROUPS * N_HEADS
    return pl.pallas_call(
        _bias_kernel,
        grid=(N_GROUPS,),
        in_specs=[pl.BlockSpec(memory_space=pltpu.SMEM),
                  pl.BlockSpec((None, BAND, 2 * BAND), lambda g: (g, 0, 0))],
        out_specs=pl.BlockSpec((N_HEADS, BAND, 2 * BAND), lambda g: (g, 0, 0)),
        out_shape=jax.ShapeDtypeStruct((n_sub, BAND, 2 * BAND), F32),
        name="t5_bias_mats",
    )(t5_bias, idx)


def _norm_kernel(*refs, has_post, has_pre, split_in, split_out, n_prompt_tiles, tiles_per_batch):
    refs = list(refs)
    xp_ref = refs.pop(0)
    xs_ref = refs.pop(0) if split_in else xp_ref
    if has_post:
        y_ref, gpost_ref, gate_p_ref, gate_s_ref = refs[:4]
        refs = refs[4:]
    if has_pre:
        gpre_ref, shift_p_ref, scale_p_ref, shift_s_ref, scale_s_ref = refs[:5]
        refs = refs[5:]
    if has_post:
        xop_ref = refs.pop(0)
        xos_ref = refs.pop(0) if split_out else xop_ref
    if has_pre:
        h_ref = refs.pop(0)

    def rms(v, g):
        return v * lax.rsqrt(jnp.mean(v * v, axis=-1, keepdims=True) + RMS_EPS) * g

    def body(x_ref, xo_ref, rows, gate, shift, scale):
        x = x_ref[rows, :]
        if has_post:
            x = x + gate * rms(y_ref[rows, :].astype(F32), gpost_ref[...])
            xo_ref[rows, :] = x
        if has_pre:
            h = rms(x, gpre_ref[...]) * (1.0 + scale) + shift
            h_ref[rows, :] = h.astype(BF16)

    t = pl.program_id(0)

    @pl.when(t < n_prompt_tiles)
    def _():
        row = pl.ds(t // tiles_per_batch, 1)
        body(xp_ref, xop_ref if has_post else None, slice(None),
             gate_p_ref[row, :] if has_post else None,
             shift_p_ref[row, :] if has_pre else None,
             scale_p_ref[row, :] if has_pre else None)

    @pl.when(t == n_prompt_tiles)
    def _():
        body(xs_ref, xos_ref if has_post else None, slice(0, SAMPLE_ROWS),
             gate_s_ref[...] if has_post else None,
             shift_s_ref[...] if has_pre else None,
             scale_s_ref[...] if has_pre else None)


def _norm_step(x, y, mod, norm_post, norm_pre, post_layer, pre_layer, x_sample=None, split_out=False):
    has_post = post_layer is not None
    has_pre = pre_layer is not None
    split_in = x_sample is not None
    te = TE_NORM
    npt = M_PROMPT // te
    tiles_per_batch = SEQ // te
    row_spec = pl.BlockSpec((te, D_MODEL), lambda t: (t, 0))
    prompt_row_spec = pl.BlockSpec((te, D_MODEL), lambda t: (jnp.minimum(t, npt - 1), 0))
    sample_row_spec = pl.BlockSpec((SAMPLE_ROWS, D_MODEL), lambda t: (0, 0))

    def mod_p_spec(layer, part):
        return pl.BlockSpec((None, C_PROMPT_ROWS, D_MODEL), lambda t: (layer, 0, part))

    def mod_s_spec(layer, part):
        return pl.BlockSpec((None, SAMPLE_ROWS, D_MODEL),
                            lambda t: (layer, C_SAMPLE_ROW0 // SAMPLE_ROWS, part))

    def gain_spec(layer):
        return pl.BlockSpec((None, 1, D_MODEL), lambda t: (layer, 0, 0))

    if split_in:
        args, in_specs = [x, x_sample], [prompt_row_spec, sample_row_spec]
    else:
        args, in_specs = [x], [row_spec]
    out_shapes, out_specs = [], []
    if has_post:
        args += [y, norm_post, mod, mod]
        in_specs += [row_spec, gain_spec(post_layer), mod_p_spec(post_layer, 2), mod_s_spec(post_layer, 2)]
        if split_out:
            out_shapes += [jax.ShapeDtypeStruct((M_PROMPT, D_MODEL), F32),
                           jax.ShapeDtypeStruct((SAMPLE_ROWS, D_MODEL), F32)]
            out_specs += [prompt_row_spec, sample_row_spec]
        else:
            out_shapes.append(jax.ShapeDtypeStruct((M_ALL, D_MODEL), F32))
            out_specs.append(row_spec)
    if has_pre:
        args += [norm_pre, mod, mod, mod, mod]
        in_specs += [gain_spec(pre_layer), mod_p_spec(pre_layer, 0), mod_p_spec(pre_layer, 1),
                     mod_s_spec(pre_layer, 0), mod_s_spec(pre_layer, 1)]
        out_shapes.append(jax.ShapeDtypeStruct((M_ALL, D_MODEL), BF16))
        out_specs.append(row_spec)
    outs = pl.pallas_call(
        functools.partial(_norm_kernel, has_post=has_post, has_pre=has_pre, split_in=split_in,
                          split_out=split_out, n_prompt_tiles=npt, tiles_per_batch=tiles_per_batch),
        grid=(npt + 1,),
        in_specs=in_specs,
        out_specs=out_specs,
        out_shape=out_shapes,
        compiler_params=pltpu.CompilerParams(
            dimension_semantics=("arbitrary",), vmem_limit_bytes=VMEM_LIMIT),
        name="norm_step",
    )(*args)
    return outs


def _mm_kernel(a_ref, w_ref, o_ref, wb_ref, *, k_rows, slabs):
    @pl.when(pl.program_id(1) == 0)
    def _():
        _cast_rows_to_bf16(w_ref, wb_ref, k_rows)

    acc = jnp.dot(a_ref[...], wb_ref[...], preferred_element_type=F32).astype(o_ref.dtype)
    if slabs:
        for c in range(o_ref.shape[0]):
            o_ref[c] = acc[:, c * HEAD_DIM:(c + 1) * HEAD_DIM]
    else:
        o_ref[...] = acc


def _matmul(a, w, layer, n_out, tn, out_dtype, name, slabs=False):
    m, k = a.shape
    tm = TM_MATMUL
    if slabs:
        out_spec = pl.BlockSpec((tn // HEAD_DIM, tm, HEAD_DIM), lambda j, i: (j, i, 0))
        out_shape = jax.ShapeDtypeStruct((n_out // HEAD_DIM, m, HEAD_DIM), out_dtype)
    else:
        out_spec = pl.BlockSpec((tm, tn), lambda j, i: (i, j))
        out_shape = jax.ShapeDtypeStruct((m, n_out), out_dtype)
    return pl.pallas_call(
        functools.partial(_mm_kernel, k_rows=k, slabs=slabs),
        grid=(n_out // tn, m // tm),
        in_specs=[pl.BlockSpec((tm, k), lambda j, i: (i, 0)),
                  pl.BlockSpec((None, k, tn), lambda j, i: (layer, 0, j))],
        out_specs=out_spec,
        out_shape=out_shape,
        scratch_shapes=[pltpu.VMEM((k, tn), BF16)],
        compiler_params=pltpu.CompilerParams(
            dimension_semantics=("arbitrary", "arbitrary"), vmem_limit_bytes=VMEM_LIMIT),
        name=name,
    )(a, w)


def _pool_kernel(u_ref, halo_ref, z_ref, st_ref, w_ref, sc_ref,
                 a_ref, pp_ref, ps_ref, wb_ref, buf_a, buf_b, *, n_prompt_tiles, tiles_per_batch):
    g = pl.program_id(0)
    t = pl.program_id(1)
    tp = TP_POOL
    h0 = POOL_HALO

    @pl.when(t == 0)
    def _():
        _cast_rows_to_bf16(w_ref, wb_ref, POOL_GROUP)

    def finish(r, z, rows):
        y = jnp.dot(r.astype(BF16), wb_ref[...], preferred_element_type=F32) * sc_ref[...]
        a_ref[rows, :] = (y * _silu(z)).astype(BF16)

    def prompt_tile(n_steps):
        w = 2 ** n_steps
        first = (t % tiles_per_batch) == 0
        buf_a[0:8, :] = jnp.zeros((8, POOL_GROUP), F32)
        buf_b[0:8, :] = jnp.zeros((8, POOL_GROUP), F32)
        buf_a[8:h0, :] = jnp.where(first, 0.0, halo_ref[...])
        buf_a[h0:h0 + tp, :] = u_ref[...]
        src, dst = buf_a, buf_b
        n = tp + h0 - 8
        for s in range(n_steps):
            sh = 2 ** s
            dst[8:8 + n, :] = src[8:8 + n, :] + src[8 - sh:8 - sh + n, :]
            src, dst = dst, src
        pos = (t % tiles_per_batch) * tp + lax.broadcasted_iota(jnp.int32, (tp, 1), 0)
        inv_cnt = 1.0 / jnp.minimum(pos + 1, w).astype(F32)
        u = u_ref[...]
        r = src[h0:h0 + tp, :] * inv_cnt - u
        finish(r, z_ref[...], slice(None))

        @pl.when((t % tiles_per_batch) == tiles_per_batch - 1)
        def _():
            pp_ref[...] = u_ref[tp - POOL_BUF:tp, :]

    def sample_tile(n_steps):
        w = 2 ** n_steps
        u_new = u_ref[0:DEC_BATCH, :]
        acc = u_new
        for k in range(1, w):
            acc = acc + st_ref[POOL_BUF - k]
        r = acc / float(w) - u_new
        buf_a[0:DEC_BATCH, :] = r
        buf_a[DEC_BATCH:SAMPLE_ROWS, :] = jnp.zeros((SAMPLE_ROWS - DEC_BATCH, POOL_GROUP), F32)
        finish(buf_a[0:SAMPLE_ROWS, :], z_ref[0:SAMPLE_ROWS, :], slice(0, SAMPLE_ROWS))
        for k in range(POOL_BUF - 1):
            ps_ref[k] = st_ref[k + 1]
        ps_ref[POOL_BUF - 1] = u_new

    for gi in range(len(POOL_WINDOWS)):
        @pl.when((g == gi) & (t < n_prompt_tiles))
        def _(gi=gi):
            prompt_tile(gi + 1)

        @pl.when((g == gi) & (t == n_prompt_tiles))
        def _(gi=gi):
            sample_tile(gi + 1)


def _pool_mix(proj, state_t, w_grp, scale, layer):
    tp = TP_POOL
    npt = M_PROMPT // tp
    tpb = SEQ // tp
    ng = len(POOL_WINDOWS)
    halo_blocks = tp // 16
    outs = pl.pallas_call(
        functools.partial(_pool_kernel, n_prompt_tiles=npt, tiles_per_batch=tpb),
        grid=(ng, npt + 1),
        in_specs=[
            pl.BlockSpec((tp, POOL_GROUP), lambda g, t: (t, g)),
            pl.BlockSpec((16, POOL_GROUP), lambda g, t: (jnp.maximum(t * halo_blocks - 1, 0), g)),
            pl.BlockSpec((tp, POOL_GROUP), lambda g, t: (t, ng + g)),
            pl.BlockSpec((None, POOL_BUF, DEC_BATCH, POOL_GROUP), lambda g, t: (layer, 0, 0, g)),
            pl.BlockSpec((None, None, POOL_GROUP, POOL_GROUP), lambda g, t: (layer, g, 0, 0)),
            pl.BlockSpec((None, 1, POOL_GROUP), lambda g, t: (layer, 0, g)),
        ],
        out_specs=[
            pl.BlockSpec((tp, POOL_GROUP), lambda g, t: (t, g)),
            pl.BlockSpec((None, POOL_BUF, POOL_GROUP),
                         lambda g, t: (jnp.minimum(t // tpb, BATCH - 1), 0, g)),
            pl.BlockSpec((POOL_BUF, DEC_BATCH, POOL_GROUP), lambda g, t: (0, 0, g)),
        ],
        out_shape=[
            jax.ShapeDtypeStruct((M_ALL, POOL_WIDTH), BF16),
            jax.ShapeDtypeStruct((BATCH, POOL_BUF, POOL_WIDTH), F32),
            jax.ShapeDtypeStruct((POOL_BUF, DEC_BATCH, POOL_WIDTH), F32),
        ],
        scratch_shapes=[pltpu.VMEM((POOL_GROUP, POOL_GROUP), BF16),
                        pltpu.VMEM((tp + POOL_HALO, POOL_GROUP), F32),
                        pltpu.VMEM((tp + POOL_HALO, POOL_GROUP), F32)],
        compiler_params=pltpu.CompilerParams(
            dimension_semantics=("arbitrary", "arbitrary"), vmem_limit_bytes=VMEM_LIMIT),
        name="pool_mix",
    )(proj, proj, proj, state_t, w_grp, scale.reshape(-1, 1, POOL_WIDTH))
    return outs


def _att_prompt_kernel(*refs):
    n_in = 13
    as_ref, a_ref = refs[n_in], refs[n_in + 1]

    @pl.when(pl.program_id(0) < BATCH)
    def _():
        _att_prompt_tile(*refs[:n_in], *refs[n_in + 1:])

    @pl.when(pl.program_id(0) == BATCH)
    def _():
        a_ref[0:SAMPLE_ROWS, :] = as_ref[...]


def _att_prompt_tile(q0, k0, v0, q1, k1, v1, q2, k2, v2, z_ref, b0, b1, b2,
                     a_ref, qd, kd, vd, od, ld, on, ln, tmp3):
    tmp = tmp3.at[0]
    q_refs, k_refs, v_refs = (q0, q1, q2), (k0, k1, k2), (v0, v1, v2)
    bias_refs = (b0, b1, b2)
    nu = ATT_UNITS

    for g, dil in enumerate(DILATIONS):
        n = SEQ // dil
        nb = n // BAND
        zero_blk = jnp.zeros((BAND, HEAD_DIM), BF16)
        kd[g, 0:BAND, :] = zero_blk
        vd[g, 0:BAND, :] = zero_blk
        for src, dst, off, mul in ((q_refs[g], qd, 0, ATT_SCALE * LOG2E), (k_refs[g], kd, BAND, None),
                                   (v_refs[g], vd, BAND, None)):
            def to_bf16(x, mul=mul):
                return (x if mul is None else x * mul).astype(BF16)

            if dil == 1:
                def cp(i, c, src=src, dst=dst, g=g, off=off, to_bf16=to_bf16):
                    r = pl.multiple_of(i * 256, 256)
                    dst[g, pl.ds(off + r, 256), :] = to_bf16(src[pl.ds(r, 256), :])
                    return c
                lax.fori_loop(0, SEQ // 256, cp, 0)
            elif dil == 4:
                for r in range(dil):
                    dst[g, off + r * n:off + (r + 1) * n, :] = to_bf16(src[pl.ds(r, n, stride=dil), :])
            else:
                stage = tmp3.at[(off > 0) + (dst is vd)]
                quarter = SEQ // 4
                for r_lo in range(4):
                    stage[r_lo * quarter:(r_lo + 1) * quarter, :] = src[pl.ds(r_lo, quarter, stride=4), :]
                for r in range(dil):
                    r_lo, r_hi = r % 4, r // 4
                    dst[g, off + r * n:off + (r + 1) * n, :] = to_bf16(
                        stage[pl.ds(r_lo * quarter + r_hi, n, stride=4), :])

    def unit_batch(g, u0, seq_blocks):
        with_prev = seq_blocks > 1
        rows = nu * BAND
        row = pl.multiple_of(u0 * BAND, rows)
        shape3 = (nu, BAND, HEAD_DIM)
        q = qd[g, pl.ds(row, rows), :].reshape(shape3)
        kc = kd[g, pl.ds(row + BAND, rows), :].reshape(shape3)
        vc = vd[g, pl.ds(row + BAND, rows), :].reshape(shape3)
        bias = bias_refs[g]
        s_c = jnp.einsum('uqe,uke->uqk', q, kc, preferred_element_type=F32) \
            + (bias[:, BAND:2 * BAND] * LOG2E)[None]
        if with_prev:
            kp = kd[g, pl.ds(row, rows), :].reshape(shape3)
            vp = vd[g, pl.ds(row, rows), :].reshape(shape3)
            bias_p = jnp.broadcast_to((bias[:, 0:BAND] * LOG2E)[None], (nu, BAND, BAND))
            blk = u0 + lax.broadcasted_iota(jnp.int32, (nu, BAND, BAND), 0)
            bias_p = jnp.where((blk & (seq_blocks - 1)) == 0, NEG_INF, bias_p)
            s_p = jnp.einsum('uqe,uke->uqk', q, kp, preferred_element_type=F32) + bias_p
            m = jnp.max(jnp.maximum(s_c, s_p), axis=-1, keepdims=True)
            p_c = jnp.exp2(s_c - m)
            p_p = jnp.exp2(s_p - m)
            l = jnp.sum(p_c + p_p, axis=-1, keepdims=True)
            acc = jnp.einsum('uqk,uke->uqe', p_c.astype(BF16), vc, preferred_element_type=F32) \
                + jnp.einsum('uqk,uke->uqe', p_p.astype(BF16), vp, preferred_element_type=F32)
        else:
            m = jnp.max(s_c, axis=-1, keepdims=True)
            p_c = jnp.exp2(s_c - m)
            l = jnp.sum(p_c, axis=-1, keepdims=True)
            acc = jnp.einsum('uqk,uke->uqe', p_c.astype(BF16), vc, preferred_element_type=F32)
        od[g, pl.ds(row, rows), :] = (acc / l).reshape(rows, HEAD_DIM)
        ld[g, pl.ds(row, rows), :] = jnp.broadcast_to(m + jnp.log(l) * LOG2E, shape3).reshape(rows, HEAD_DIM)

    for g, dil in enumerate(DILATIONS):
        nb = (SEQ // dil) // BAND

        def batch_body(i, c, g=g, nb=nb):
            unit_batch(g, i * nu, nb)
            return c
        lax.fori_loop(0, N_BLOCKS // nu, batch_body, 0)

    def interleave4(src, dst, n_rows):
        quarter = n_rows // 4
        for r in range(4):
            dst[pl.ds(r, quarter, stride=4), :] = src[r * quarter:(r + 1) * quarter, :]

    for src_all, dst_all in ((od, on), (ld, ln)):
        interleave4(src_all.at[1], dst_all.at[0], SEQ)
        for r_lo in range(4):
            for r_hi in range(4):
                r = r_lo + 4 * r_hi
                tmp[pl.ds(r_lo * (SEQ // 4) + r_hi, BAND, stride=4), :] = src_all[2, r * BAND:(r + 1) * BAND, :]
        interleave4(tmp, dst_all.at[1], SEQ)

    def comb(i, c):
        rs = pl.ds(pl.multiple_of(i * 256, 256), 256)
        l0, l1, l2 = ld[0, rs, :], ln[0, rs, :], ln[1, rs, :]
        mx = jnp.maximum(jnp.maximum(l0, l1), l2)
        w0, w1, w2 = jnp.exp2(l0 - mx), jnp.exp2(l1 - mx), jnp.exp2(l2 - mx)
        o = (w0 * od[0, rs, :] + w1 * on[0, rs, :] + w2 * on[1, rs, :]) / (w0 + w1 + w2)
        a_ref[rs, :] = (o * _silu(z_ref[rs, :])).astype(BF16)
        return c
    lax.fori_loop(0, SEQ // 256, comb, 0)


def _att_prompt(proj, bias_mats, a_sample):
    hb = N_HEADS

    def head_blk(b, h):
        return jnp.where(b < BATCH, h, hb - 1)

    in_specs = []
    for g in range(N_GROUPS):
        for part in range(3):
            off = part * N_GROUPS * hb + g * hb
            in_specs.append(pl.BlockSpec(
                (None, SEQ, HEAD_DIM),
                lambda b, h, off=off: (off + head_blk(b, h), jnp.minimum(b, BATCH - 1), 0)))
    in_specs.append(pl.BlockSpec((None, SEQ, HEAD_DIM),
                                 lambda b, h: (9 * hb + head_blk(b, h), jnp.minimum(b, BATCH - 1), 0)))
    for g in range(N_GROUPS):
        in_specs.append(pl.BlockSpec((None, BAND, 2 * BAND), lambda b, h, g=g: (g * hb + head_blk(b, h), 0, 0)))
    in_specs.append(pl.BlockSpec((SAMPLE_ROWS, HEAD_DIM), lambda b, h: (0, h)))
    scratch = [pltpu.VMEM((N_GROUPS, SEQ, HEAD_DIM), BF16)] + \
              [pltpu.VMEM((N_GROUPS, SEQ + BAND, HEAD_DIM), BF16)] * 2 + \
              [pltpu.VMEM((N_GROUPS, SEQ, HEAD_DIM), F32)] * 2 + \
              [pltpu.VMEM((N_GROUPS - 1, SEQ, HEAD_DIM), F32)] * 2 + \
              [pltpu.VMEM((3, SEQ, HEAD_DIM), F32)]
    return pl.pallas_call(
        _att_prompt_kernel,
        grid=(BATCH + 1, N_HEADS),
        in_specs=in_specs,
        out_specs=pl.BlockSpec((SEQ, HEAD_DIM), lambda b, h: (b, h)),
        out_shape=jax.ShapeDtypeStruct((M_ALL, D_MODEL), BF16),
        scratch_shapes=scratch,
        compiler_params=pltpu.CompilerParams(
            dimension_semantics=("arbitrary", "arbitrary"), vmem_limit_bytes=VMEM_LIMIT),
        name="att_prompt",
    )(*([proj] * 10), bias_mats, bias_mats, bias_mats, a_sample)


def _kv_rows_kernel(k_a, v_a, k_b, v_b, o_ref, *, tr):
    pitch = tr + KV_PITCH_PAD

    def emit(srcs):
        flats = [src.reshape(N_HEADS * pitch, HEAD_DIM) for src in srcs]

        def body(i, c):
            t0 = i * 8
            for k in range(8):
                for part, flat in enumerate(flats):
                    o_ref[t0 + k, part] = flat[pl.ds(t0 + k, N_HEADS, stride=pitch), :]
            return c
        lax.fori_loop(0, tr // 8, body, 0)

    @pl.when(pl.program_id(0) == 0)
    def _():
        emit((k_a, v_a))

    @pl.when(pl.program_id(0) == 1)
    def _():
        emit((k_b, v_b))


def _kv_rows(proj_a, proj_b, g):
    keep = WINDOWS[g]
    tr = min(keep, 256)
    nt = keep // tr
    first_blk = (SEQ - keep) // tr
    blks_per_batch = SEQ // tr

    def row_blk(b, t):
        return b * blks_per_batch + first_blk + t

    def spec(layer, part):
        parked = row_blk(BATCH - 1, nt - 1) if layer == 0 else row_blk(0, 0)
        col0 = ((1 + part) * N_GROUPS + g) * N_HEADS
        return pl.BlockSpec((pl.Element(N_HEADS), pl.Element(tr + KV_PITCH_PAD), pl.Element(HEAD_DIM)),
                            lambda l, b, t: (col0, jnp.where(l == layer, row_blk(b, t), parked) * tr, 0))

    return pl.pallas_call(
        functools.partial(_kv_rows_kernel, tr=tr),
        grid=(2, BATCH, nt),
        in_specs=[spec(0, 0), spec(0, 1), spec(1, 0), spec(1, 1)],
        out_specs=pl.BlockSpec((None, None, tr, 2, N_HEADS, HEAD_DIM), lambda l, b, t: (l, b, t, 0, 0, 0)),
        out_shape=jax.ShapeDtypeStruct((2, BATCH, keep, 2, N_HEADS, HEAD_DIM), F32),
        compiler_params=pltpu.CompilerParams(
            dimension_semantics=("arbitrary",) * 3, vmem_limit_bytes=VMEM_LIMIT),
        name="kv_rows",
    )(proj_a, proj_a, proj_b, proj_b)


def _att_sample_kernel(x_ref, c0, c1, c2, bias_ref, a_ref, kv0, kv1, kv2):
    caches = (c0, c1, c2)
    kv_outs = (kv0, kv1, kv2)
    hb = N_HEADS
    outs, lses = [], []
    for g in range(N_GROUPS):
        q = x_ref[g * hb:(g + 1) * hb, :]
        kn = x_ref[(3 + g) * hb:(4 + g) * hb, :]
        vn = x_ref[(6 + g) * hb:(7 + g) * hb, :]
        kv_outs[g][0] = kn
        kv_outs[g][1] = vn
        kc = caches[g][:, 0]
        vc = caches[g][:, 1]
        s_c = jnp.sum(kc * q[None], axis=-1, keepdims=True) * ATT_SCALE + bias_ref[g, 0:BAND]
        s_n = jnp.sum(kn * q, axis=-1, keepdims=True) * ATT_SCALE + bias_ref[g, BAND]
        m = jnp.maximum(jnp.max(s_c, axis=0), s_n)
        p_c = jnp.exp(s_c - m[None])
        p_n = jnp.exp(s_n - m)
        l = jnp.sum(p_c, axis=0) + p_n
        o = (jnp.sum(p_c * vc, axis=0) + p_n * vn) / l
        outs.append(o)
        lses.append(m + jnp.log(l))
    mx = jnp.maximum(jnp.maximum(lses[0], lses[1]), lses[2])
    ws = [jnp.exp(ls - mx) for ls in lses]
    o = (ws[0] * outs[0] + ws[1] * outs[1] + ws[2] * outs[2]) / (ws[0] + ws[1] + ws[2])
    z = x_ref[9 * hb:10 * hb, :]
    a_ref[...] = o * _silu(z)


def _att_sample(proj_s, caches, bias_s, layer):
    hb = N_HEADS
    in_specs = [pl.BlockSpec((None, ATT_IN_COLS // HEAD_DIM, HEAD_DIM), lambda b: (b, 0, 0))]
    for g in range(N_GROUPS):
        in_specs.append(pl.BlockSpec((None, None, BAND, None, 2, hb, HEAD_DIM),
                                     lambda b: (layer, b, 0, 0, 0, 0, 0)))
    in_specs.append(pl.BlockSpec((N_GROUPS, BAND + 1, hb, 1), lambda b: (0, 0, 0, 0)))
    out_specs = [pl.BlockSpec((None, hb, HEAD_DIM), lambda b: (b, 0, 0))]
    out_shapes = [jax.ShapeDtypeStruct((DEC_BATCH, hb, HEAD_DIM), F32)]
    for g in range(N_GROUPS):
        out_specs.append(pl.BlockSpec((None, 2, hb, HEAD_DIM), lambda b: (b, 0, 0, 0)))
        out_shapes.append(jax.ShapeDtypeStruct((DEC_BATCH, 2, hb, HEAD_DIM), F32))
    return pl.pallas_call(
        _att_sample_kernel,
        grid=(DEC_BATCH,),
        in_specs=in_specs,
        out_specs=out_specs,
        out_shape=out_shapes,
        compiler_params=pltpu.CompilerParams(
            dimension_semantics=("arbitrary",), vmem_limit_bytes=VMEM_LIMIT),
        name="att_sample",
    )(proj_s, *caches, bias_s)


def kernel(x_prompt, x_sample, c_prompt, c_sample, cache_kv0, cache_kv1, cache_kv2, state_pool,
           norm_pre, norm_post, ada_w, ada_b, t5_bias, pool_w_in, pool_w_grp, pool_scale,
           pool_w_out, att_w_in, att_w_out):
    n_att = DEPTH // 2
    xp0 = x_prompt.reshape(M_PROMPT, D_MODEL)
    xs0 = jnp.zeros((SAMPLE_ROWS, D_MODEL), F32).at[0:DEC_BATCH].set(x_sample.reshape(DEC_BATCH, D_MODEL))
    c_all = jnp.zeros((C_ROWS, D_MODEL), F32)
    c_all = c_all.at[0:BATCH].set(c_prompt).at[C_SAMPLE_ROW0:C_SAMPLE_ROW0 + DEC_BATCH].set(c_sample)

    mod = _ada_all(c_all, ada_w, ada_b)
    gains_pre = norm_pre.reshape(DEPTH, 1, D_MODEL)
    gains_post = norm_post.reshape(DEPTH, 1, D_MODEL)

    bias_mats = _bias_matrices(t5_bias)
    bias_s = bias_mats[:, 0, 0:BAND + 1].reshape(N_GROUPS, N_HEADS, BAND + 1)
    bias_s = jnp.transpose(bias_s, (0, 2, 1))[..., None]
    caches = [c.reshape(n_att, DEC_BATCH, BAND, dil, 2, N_HEADS, HEAD_DIM)
              for c, dil in zip((cache_kv0, cache_kv1, cache_kv2), DILATIONS)]
    state_t = jnp.transpose(state_pool, (0, 2, 1, 3))

    kv_s = [[] for _ in range(N_GROUPS)]
    pool_p, pool_s, att_projs = [], [], []

    (h,) = _norm_step(xp0, None, mod, gains_post, gains_pre, None, 0, x_sample=xs0)
    x = None
    for i in range(DEPTH):
        li = i // 2
        if i % 2 == 0:
            proj = _matmul(h, pool_w_in, li, 2 * POOL_WIDTH, 1024, F32, "pool_in_proj")
            a, pp, ps = _pool_mix(proj, state_t, pool_w_grp, pool_scale, li)
            pool_p.append(pp)
            pool_s.append(jnp.transpose(ps, (1, 0, 2)))
            y = _matmul(a, pool_w_out, li, D_MODEL, 512, BF16, "pool_out_proj")
        else:
            proj = _matmul(h, att_w_in, li, ATT_IN_COLS, 1024, F32, "att_in_proj", slabs=True)
            att_projs.append(proj)
            proj_s = jnp.transpose(proj[:, M_PROMPT:M_PROMPT + DEC_BATCH, :], (1, 0, 2))
            souts = _att_sample(proj_s, caches, bias_s, li)
            a_s = jnp.zeros((SAMPLE_ROWS, D_MODEL), F32).at[0:DEC_BATCH].set(souts[0].reshape(DEC_BATCH, D_MODEL))
            a = _att_prompt(proj, bias_mats, a_s.astype(BF16))
            for g in range(N_GROUPS):
                kv_s[g].append(souts[1 + g].reshape(DEC_BATCH, 1, 2, N_HEADS, HEAD_DIM))
            y = _matmul(a, att_w_out, li, D_MODEL, 512, BF16, "att_out_proj")
        last = i + 1 == DEPTH
        if i == 0:
            x, h = _norm_step(xp0, y, mod, gains_post, gains_pre, i, i + 1, x_sample=xs0)
        elif not last:
            x, h = _norm_step(x, y, mod, gains_post, gains_pre, i, i + 1)
        else:
            y_p, y_s = _norm_step(x, y, mod, gains_post, gains_pre, i, None, split_out=True)

    kv_p = [_kv_rows(att_projs[0], att_projs[1], g) for g in range(N_GROUPS)]
    y_prompt = y_p.reshape(BATCH, SEQ, D_MODEL)
    y_sample = y_s[0:DEC_BATCH].reshape(DEC_BATCH, 1, D_MODEL)
    return (y_prompt, y_sample, kv_p[0], kv_p[1], kv_p[2], jnp.stack(pool_p),
            jnp.stack(kv_s[0]), jnp.stack(kv_s[1]), jnp.stack(kv_s[2]), jnp.stack(pool_s))
```

```python
import functools

import numpy as np
import jax
import jax.numpy as jnp
from jax import lax
from jax.experimental import pallas as pl
from jax.experimental.pallas import tpu as pltpu

D_MODEL = 2048
BATCH = 4
SEQ = 2048
DEPTH = 4
DEC_BATCH = 8
HEAD_DIM = 128
N_HEADS = 16
DILATIONS = (1, 4, 16)
WINDOWS = (128, 512, 2048)
N_GROUPS = 3
QKV_WIDTH = N_GROUPS * D_MODEL
ATT_IN_COLS = 3 * QKV_WIDTH + D_MODEL
BAND = 128
ATT_SCALE = HEAD_DIM ** -0.5
LOG2E = 1.4426950408889634
POOL_WINDOWS = (2, 4, 8, 16)
POOL_WIDTH = 2 * D_MODEL
POOL_GROUP = POOL_WIDTH // 4
POOL_BUF = 15
N_BUCKETS = 32
T5_MAX_DIST = 2048
RMS_EPS = 1e-6
NEG_INF = -1e30

M_PROMPT = BATCH * SEQ
M_ALL = M_PROMPT + 16
SAMPLE_ROWS = 16
C_ROWS = 32
C_PROMPT_ROWS = 8
C_SAMPLE_ROW0 = 16

TM_MATMUL = 912
TE_NORM = 512
TP_POOL = 1024
POOL_HALO = 24
ATT_UNITS = 16
N_BLOCKS = SEQ // BAND
TR_KV = 256
ROW_CHUNK = 256
KV_PITCH_PAD = 8
VMEM_LIMIT = 56 * 1024 * 1024

F32 = jnp.float32
BF16 = jnp.bfloat16


def _silu(x):
    half = 0.5 * x
    return half + half * jnp.tanh(half)


def _cast_rows_to_bf16(src_ref, dst_ref, rows, chunk=ROW_CHUNK):
    def body(i, c):
        r = pl.multiple_of(i * chunk, chunk)
        dst_ref[pl.ds(r, chunk), :] = src_ref[pl.ds(r, chunk), :].astype(BF16)
        return c
    lax.fori_loop(0, rows // chunk, body, 0)


def _ada_kernel(c_ref, w_ref, b_ref, o_ref):
    a = _silu(c_ref[...]).astype(BF16)
    kc = 512
    acc = jnp.zeros(o_ref.shape, F32)
    for k0 in range(0, D_MODEL, kc):
        acc = acc + jnp.dot(a[:, k0:k0 + kc], w_ref[k0:k0 + kc, :].astype(BF16),
                            preferred_element_type=F32)
    o_ref[...] = acc + b_ref[...]


def _ada_all(c_all, ada_w, ada_b):
    tn = 1024
    n = 3 * D_MODEL
    return pl.pallas_call(
        _ada_kernel,
        grid=(DEPTH, n // tn),
        in_specs=[pl.BlockSpec((C_ROWS, D_MODEL), lambda l, j: (0, 0)),
                  pl.BlockSpec((None, D_MODEL, tn), lambda l, j: (l, 0, j)),
                  pl.BlockSpec((None, 1, tn), lambda l, j: (l, 0, j))],
        out_specs=pl.BlockSpec((None, C_ROWS, tn), lambda l, j: (l, 0, j)),
        out_shape=jax.ShapeDtypeStruct((DEPTH, C_ROWS, n), F32),
        compiler_params=pltpu.CompilerParams(
            dimension_semantics=("arbitrary", "arbitrary"), vmem_limit_bytes=VMEM_LIMIT),
        name="ada_mod",
    )(c_all, ada_w, ada_b.reshape(DEPTH, 1, n))


def _t5_bucket(dist):
    dist = np.asarray(dist, dtype=np.int64)
    max_exact = N_BUCKETS // 2
    ratio = np.log(np.maximum(dist, 1) / max_exact) / np.log(T5_MAX_DIST / max_exact)
    large = np.minimum(max_exact + (ratio * (N_BUCKETS - max_exact)).astype(np.int64), N_BUCKETS - 1)
    return np.where(dist < max_exact, dist, large).astype(np.int32)


def _bucket_index_table():
    rel = np.arange(BAND)[:, None] + BAND - np.arange(2 * BAND)[None, :]
    inband = (rel >= 0) & (rel <= BAND)
    out = []
    for dil in DILATIONS:
        bucket = _t5_bucket(np.clip(rel, 0, BAND) * dil)
        out.append(np.where(inband, bucket, -1))
    return np.stack(out).astype(np.int32)


def _bias_kernel(tab_ref, idx_ref, o_ref):
    g = pl.program_id(0)
    idx = idx_ref[...]

    def head(h, c):
        acc = jnp.full(idx.shape, NEG_INF, F32)
        for b in range(N_BUCKETS):
            acc = jnp.where(idx == b, tab_ref[b, g * N_HEADS + h], acc)
        o_ref[h] = acc
        return c
    lax.fori_loop(0, N_HEADS, head, 0)


def _bias_matrices(t5_bias):
    idx = jnp.asarray(_bucket_index_table())
    n_sub = N_GROUPS * N_HEADS
    return pl.pallas_call(
        _bias_kernel,
        grid=(N_GROUPS,),
        in_specs=[pl.BlockSpec(memory_space=pltpu.SMEM),
                  pl.BlockSpec((None, BAND, 2 * BAND), lambda g: (g, 0, 0))],
        out_specs=pl.BlockSpec((N_HEADS, BAND, 2 * BAND), lambda g: (g, 0, 0)),
        out_shape=jax.ShapeDtypeStruct((n_sub, BAND, 2 * BAND), F32),
        name="t5_bias_mats",
    )(t5_bias, idx)


def _norm_kernel(*refs, has_post, has_pre, split_in, split_out, n_prompt_tiles, tiles_per_batch):
    refs = list(refs)
    xp_ref = refs.pop(0)
    xs_ref = refs.pop(0) if split_in else xp_ref
    if has_post:
        y_ref, gpost_ref, gate_p_ref, gate_s_ref = refs[:4]
        refs = refs[4:]
    if has_pre:
        gpre_ref, shift_p_ref, scale_p_ref, shift_s_ref, scale_s_ref = refs[:5]
        refs = refs[5:]
    if has_post:
        xop_ref = refs.pop(0)
        xos_ref = refs.pop(0) if split_out else xop_ref
    if has_pre:
        h_ref = refs.pop(0)

    def rms(v, g):
        return v * lax.rsqrt(jnp.mean(v * v, axis=-1, keepdims=True) + RMS_EPS) * g

    def body(x_ref, xo_ref, rows, gate, shift, scale):
        x = x_ref[rows, :]
        if has_post:
            x = x + gate * rms(y_ref[rows, :].astype(F32), gpost_ref[...])
            xo_ref[rows, :] = x
        if has_pre:
            h = rms(x, gpre_ref[...]) * (1.0 + scale) + shift
            h_ref[rows, :] = h.astype(BF16)

    t = pl.program_id(0)

    @pl.when(t < n_prompt_tiles)
    def _():
        row = pl.ds(t // tiles_per_batch, 1)
        body(xp_ref, xop_ref if has_post else None, slice(None),
             gate_p_ref[row, :] if has_post else None,
             shift_p_ref[row, :] if has_pre else None,
             scale_p_ref[row, :] if has_pre else None)

    @pl.when(t == n_prompt_tiles)
    def _():
        body(xs_ref, xos_ref if has_post else None, slice(0, SAMPLE_ROWS),
             gate_s_ref[...] if has_post else None,
             shift_s_ref[...] if has_pre else None,
             scale_s_ref[...] if has_pre else None)


def _norm_step(x, y, mod, norm_post, norm_pre, post_layer, pre_layer, x_sample=None, split_out=False):
    has_post = post_layer is not None
    has_pre = pre_layer is not None
    split_in = x_sample is not None
    te = TE_NORM
    npt = M_PROMPT // te
    tiles_per_batch = SEQ // te
    row_spec = pl.BlockSpec((te, D_MODEL), lambda t: (t, 0))
    prompt_row_spec = pl.BlockSpec((te, D_MODEL), lambda t: (jnp.minimum(t, npt - 1), 0))
    sample_row_spec = pl.BlockSpec((SAMPLE_ROWS, D_MODEL), lambda t: (0, 0))

    def mod_p_spec(layer, part):
        return pl.BlockSpec((None, C_PROMPT_ROWS, D_MODEL), lambda t: (layer, 0, part))

    def mod_s_spec(layer, part):
        return pl.BlockSpec((None, SAMPLE_ROWS, D_MODEL),
                            lambda t: (layer, C_SAMPLE_ROW0 // SAMPLE_ROWS, part))

    def gain_spec(layer):
        return pl.BlockSpec((None, 1, D_MODEL), lambda t: (layer, 0, 0))

    if split_in:
        args, in_specs = [x, x_sample], [prompt_row_spec, sample_row_spec]
    else:
        args, in_specs = [x], [row_spec]
    out_shapes, out_specs = [], []
    if has_post:
        args += [y, norm_post, mod, mod]
        in_specs += [row_spec, gain_spec(post_layer), mod_p_spec(post_layer, 2), mod_s_spec(post_layer, 2)]
        if split_out:
            out_shapes += [jax.ShapeDtypeStruct((M_PROMPT, D_MODEL), F32),
                           jax.ShapeDtypeStruct((SAMPLE_ROWS, D_MODEL), F32)]
            out_specs += [prompt_row_spec, sample_row_spec]
        else:
            out_shapes.append(jax.ShapeDtypeStruct((M_ALL, D_MODEL), F32))
            out_specs.append(row_spec)
    if has_pre:
        args += [norm_pre, mod, mod, mod, mod]
        in_specs += [gain_spec(pre_layer), mod_p_spec(pre_layer, 0), mod_p_spec(pre_layer, 1),
                     mod_s_spec(pre_layer, 0), mod_s_spec(pre_layer, 1)]
        out_shapes.append(jax.ShapeDtypeStruct((M_ALL, D_MODEL), BF16))
        out_specs.append(row_spec)
    outs = pl.pallas_call(
        functools.partial(_norm_kernel, has_post=has_post, has_pre=has_pre, split_in=split_in,
                          split_out=split_out, n_prompt_tiles=npt, tiles_per_batch=tiles_per_batch),
        grid=(npt + 1,),
        in_specs=in_specs,
        out_specs=out_specs,
        out_shape=out_shapes,
        compiler_params=pltpu.CompilerParams(
            dimension_semantics=("arbitrary",), vmem_limit_bytes=VMEM_LIMIT),
        name="norm_step",
    )(*args)
    return outs


def _mm_kernel(a_ref, w_ref, o_ref, wb_ref, *, k_rows, slabs):
    @pl.when(pl.program_id(1) == 0)
    def _():
        _cast_rows_to_bf16(w_ref, wb_ref, k_rows)

    acc = jnp.dot(a_ref[...], wb_ref[...], preferred_element_type=F32).astype(o_ref.dtype)
    if slabs:
        for c in range(o_ref.shape[0]):
            o_ref[c] = acc[:, c * HEAD_DIM:(c + 1) * HEAD_DIM]
    else:
        o_ref[...] = acc


def _matmul(a, w, layer, n_out, tn, out_dtype, name, slabs=False):
    m, k = a.shape
    tm = TM_MATMUL
    if slabs:
        out_spec = pl.BlockSpec((tn // HEAD_DIM, tm, HEAD_DIM), lambda j, i: (j, i, 0))
        out_shape = jax.ShapeDtypeStruct((n_out // HEAD_DIM, m, HEAD_DIM), out_dtype)
    else:
        out_spec = pl.BlockSpec((tm, tn), lambda j, i: (i, j))
        out_shape = jax.ShapeDtypeStruct((m, n_out), out_dtype)
    return pl.pallas_call(
        functools.partial(_mm_kernel, k_rows=k, slabs=slabs),
        grid=(n_out // tn, m // tm),
        in_specs=[pl.BlockSpec((tm, k), lambda j, i: (i, 0)),
                  pl.BlockSpec((None, k, tn), lambda j, i: (layer, 0, j))],
        out_specs=out_spec,
        out_shape=out_shape,
        scratch_shapes=[pltpu.VMEM((k, tn), BF16)],
        compiler_params=pltpu.CompilerParams(
            dimension_semantics=("arbitrary", "arbitrary"), vmem_limit_bytes=VMEM_LIMIT),
        name=name,
    )(a, w)


def _pool_kernel(u_ref, halo_ref, z_ref, st_ref, w_ref, sc_ref,
                 a_ref, pp_ref, ps_ref, wb_ref, buf_a, buf_b, *, n_prompt_tiles, tiles_per_batch):
    g = pl.program_id(0)
    t = pl.program_id(1)
    tp = TP_POOL
    h0 = POOL_HALO

    @pl.when(t == 0)
    def _():
        _cast_rows_to_bf16(w_ref, wb_ref, POOL_GROUP)

    def finish(r, z, rows):
        y = jnp.dot(r.astype(BF16), wb_ref[...], preferred_element_type=F32) * sc_ref[...]
        a_ref[rows, :] = (y * _silu(z)).astype(BF16)

    def prompt_tile(n_steps):
        w = 2 ** n_steps
        first = (t % tiles_per_batch) == 0
        buf_a[0:8, :] = jnp.zeros((8, POOL_GROUP), F32)
        buf_b[0:8, :] = jnp.zeros((8, POOL_GROUP), F32)
        buf_a[8:h0, :] = jnp.where(first, 0.0, halo_ref[...])
        buf_a[h0:h0 + tp, :] = u_ref[...]
        src, dst = buf_a, buf_b
        n = tp + h0 - 8
        for s in range(n_steps):
            sh = 2 ** s
            dst[8:8 + n, :] = src[8:8 + n, :] + src[8 - sh:8 - sh + n, :]
            src, dst = dst, src
        pos = (t % tiles_per_batch) * tp + lax.broadcasted_iota(jnp.int32, (tp, 1), 0)
        inv_cnt = 1.0 / jnp.minimum(pos + 1, w).astype(F32)
        u = u_ref[...]
        r = src[h0:h0 + tp, :] * inv_cnt - u
        finish(r, z_ref[...], slice(None))

        @pl.when((t % tiles_per_batch) == tiles_per_batch - 1)
        def _():
            pp_ref[...] = u_ref[tp - POOL_BUF:tp, :]

    def sample_tile(n_steps):
        w = 2 ** n_steps
        u_new = u_ref[0:DEC_BATCH, :]
        acc = u_new
        for k in range(1, w):
            acc = acc + st_ref[POOL_BUF - k]
        r = acc / float(w) - u_new
        buf_a[0:DEC_BATCH, :] = r
        buf_a[DEC_BATCH:SAMPLE_ROWS, :] = jnp.zeros((SAMPLE_ROWS - DEC_BATCH, POOL_GROUP), F32)
        finish(buf_a[0:SAMPLE_ROWS, :], z_ref[0:SAMPLE_ROWS, :], slice(0, SAMPLE_ROWS))
        for k in range(POOL_BUF - 1):
            ps_ref[k] = st_ref[k + 1]
        ps_ref[POOL_BUF - 1] = u_new

    for gi in range(len(POOL_WINDOWS)):
        @pl.when((g == gi) & (t < n_prompt_tiles))
        def _(gi=gi):
            prompt_tile(gi + 1)

        @pl.when((g == gi) & (t == n_prompt_tiles))
        def _(gi=gi):
            sample_tile(gi + 1)


def _pool_mix(proj, state_t, w_grp, scale, layer):
    tp = TP_POOL
    npt = M_PROMPT // tp
    tpb = SEQ // tp
    ng = len(POOL_WINDOWS)
    halo_blocks = tp // 16
    outs = pl.pallas_call(
        functools.partial(_pool_kernel, n_prompt_tiles=npt, tiles_per_batch=tpb),
        grid=(ng, npt + 1),
        in_specs=[
            pl.BlockSpec((tp, POOL_GROUP), lambda g, t: (t, g)),
            pl.BlockSpec((16, POOL_GROUP), lambda g, t: (jnp.maximum(t * halo_blocks - 1, 0), g)),
            pl.BlockSpec((tp, POOL_GROUP), lambda g, t: (t, ng + g)),
            pl.BlockSpec((None, POOL_BUF, DEC_BATCH, POOL_GROUP), lambda g, t: (layer, 0, 0, g)),
            pl.BlockSpec((None, None, POOL_GROUP, POOL_GROUP), lambda g, t: (layer, g, 0, 0)),
            pl.BlockSpec((None, 1, POOL_GROUP), lambda g, t: (layer, 0, g)),
        ],
        out_specs=[
            pl.BlockSpec((tp, POOL_GROUP), lambda g, t: (t, g)),
            pl.BlockSpec((None, POOL_BUF, POOL_GROUP),
                         lambda g, t: (jnp.minimum(t // tpb, BATCH - 1), 0, g)),
            pl.BlockSpec((POOL_BUF, DEC_BATCH, POOL_GROUP), lambda g, t: (0, 0, g)),
        ],
        out_shape=[
            jax.ShapeDtypeStruct((M_ALL, POOL_WIDTH), BF16),
            jax.ShapeDtypeStruct((BATCH, POOL_BUF, POOL_WIDTH), F32),
            jax.ShapeDtypeStruct((POOL_BUF, DEC_BATCH, POOL_WIDTH), F32),
        ],
        scratch_shapes=[pltpu.VMEM((POOL_GROUP, POOL_GROUP), BF16),
                        pltpu.VMEM((tp + POOL_HALO, POOL_GROUP), F32),
                        pltpu.VMEM((tp + POOL_HALO, POOL_GROUP), F32)],
        compiler_params=pltpu.CompilerParams(
            dimension_semantics=("arbitrary", "arbitrary"), vmem_limit_bytes=VMEM_LIMIT),
        name="pool_mix",
    )(proj, proj, proj, state_t, w_grp, scale.reshape(-1, 1, POOL_WIDTH))
    return outs


def _att_prompt_kernel(*refs):
    n_in = 13
    as_ref, a_ref = refs[n_in], refs[n_in + 1]

    @pl.when(pl.program_id(0) < BATCH)
    def _():
        _att_prompt_tile(*refs[:n_in], *refs[n_in + 1:])

    @pl.when(pl.program_id(0) == BATCH)
    def _():
        a_ref[0:SAMPLE_ROWS, :] = as_ref[...]


def _att_prompt_tile(q0, k0, v0, q1, k1, v1, q2, k2, v2, z_ref, b0, b1, b2,
                     a_ref, qd, kd, vd, od, ld, on, ln, tmp3):
    tmp = tmp3.at[0]
    q_refs, k_refs, v_refs = (q0, q1, q2), (k0, k1, k2), (v0, v1, v2)
    bias_refs = (b0, b1, b2)
    nu = ATT_UNITS

    for g, dil in enumerate(DILATIONS):
        n = SEQ // dil
        nb = n // BAND
        zero_blk = jnp.zeros((BAND, HEAD_DIM), BF16)
        kd[g, 0:BAND, :] = zero_blk
        vd[g, 0:BAND, :] = zero_blk
        for src, dst, off, mul in ((q_refs[g], qd, 0, ATT_SCALE * LOG2E), (k_refs[g], kd, BAND, None),
                                   (v_refs[g], vd, BAND, None)):
            def to_bf16(x, mul=mul):
                return (x if mul is None else x * mul).astype(BF16)

            if dil == 1:
                def cp(i, c, src=src, dst=dst, g=g, off=off, to_bf16=to_bf16):
                    r = pl.multiple_of(i * ROW_CHUNK, ROW_CHUNK)
                    dst[g, pl.ds(off + r, ROW_CHUNK), :] = to_bf16(src[pl.ds(r, ROW_CHUNK), :])
                    return c
                lax.fori_loop(0, SEQ // ROW_CHUNK, cp, 0)
            elif dil == 4:
                for r in range(dil):
                    dst[g, off + r * n:off + (r + 1) * n, :] = to_bf16(src[pl.ds(r, n, stride=dil), :])
            else:
                stage = tmp3.at[(off > 0) + (dst is vd)]
                quarter = SEQ // 4
                for r_lo in range(4):
                    stage[r_lo * quarter:(r_lo + 1) * quarter, :] = src[pl.ds(r_lo, quarter, stride=4), :]
                for r in range(dil):
                    r_lo, r_hi = r % 4, r // 4
                    dst[g, off + r * n:off + (r + 1) * n, :] = to_bf16(
                        stage[pl.ds(r_lo * quarter + r_hi, n, stride=4), :])

    def unit_batch(g, u0, seq_blocks):
        with_prev = seq_blocks > 1
        rows = nu * BAND
        row = pl.multiple_of(u0 * BAND, rows)
        shape3 = (nu, BAND, HEAD_DIM)
        q = qd[g, pl.ds(row, rows), :].reshape(shape3)
        kc = kd[g, pl.ds(row + BAND, rows), :].reshape(shape3)
        vc = vd[g, pl.ds(row + BAND, rows), :].reshape(shape3)
        bias = bias_refs[g]
        s_c = jnp.einsum('uqe,uke->uqk', q, kc, preferred_element_type=F32) \
            + (bias[:, BAND:2 * BAND] * LOG2E)[None]
        if with_prev:
            kp = kd[g, pl.ds(row, rows), :].reshape(shape3)
            vp = vd[g, pl.ds(row, rows), :].reshape(shape3)
            bias_p = jnp.broadcast_to((bias[:, 0:BAND] * LOG2E)[None], (nu, BAND, BAND))
            blk = u0 + lax.broadcasted_iota(jnp.int32, (nu, BAND, BAND), 0)
            bias_p = jnp.where((blk & (seq_blocks - 1)) == 0, NEG_INF, bias_p)
            s_p = jnp.einsum('uqe,uke->uqk', q, kp, preferred_element_type=F32) + bias_p
            m = jnp.max(jnp.maximum(s_c, s_p), axis=-1, keepdims=True)
            p_c = jnp.exp2(s_c - m)
            p_p = jnp.exp2(s_p - m)
            l = jnp.sum(p_c + p_p, axis=-1, keepdims=True)
            acc = jnp.einsum('uqk,uke->uqe', p_c.astype(BF16), vc, preferred_element_type=F32) \
                + jnp.einsum('uqk,uke->uqe', p_p.astype(BF16), vp, preferred_element_type=F32)
        else:
            m = jnp.max(s_c, axis=-1, keepdims=True)
            p_c = jnp.exp2(s_c - m)
            l = jnp.sum(p_c, axis=-1, keepdims=True)
            acc = jnp.einsum('uqk,uke->uqe', p_c.astype(BF16), vc, preferred_element_type=F32)
        od[g, pl.ds(row, rows), :] = (acc / l).reshape(rows, HEAD_DIM)
        ld[g, pl.ds(row, rows), :] = jnp.broadcast_to(m + jnp.log(l) * LOG2E, shape3).reshape(rows, HEAD_DIM)

    for g, dil in enumerate(DILATIONS):
        nb = (SEQ // dil) // BAND

        def batch_body(i, c, g=g, nb=nb):
            unit_batch(g, i * nu, nb)
            return c
        lax.fori_loop(0, N_BLOCKS // nu, batch_body, 0)

    def interleave4(src, dst, n_rows):
        quarter = n_rows // 4
        for r in range(4):
            dst[pl.ds(r, quarter, stride=4), :] = src[r * quarter:(r + 1) * quarter, :]

    for src_all, dst_all in ((od, on), (ld, ln)):
        interleave4(src_all.at[1], dst_all.at[0], SEQ)
        for r_lo in range(4):
            for r_hi in range(4):
                r = r_lo + 4 * r_hi
                tmp[pl.ds(r_lo * (SEQ // 4) + r_hi, BAND, stride=4), :] = src_all[2, r * BAND:(r + 1) * BAND, :]
        interleave4(tmp, dst_all.at[1], SEQ)

    def comb(i, c):
        rs = pl.ds(pl.multiple_of(i * ROW_CHUNK, ROW_CHUNK), ROW_CHUNK)
        l0, l1, l2 = ld[0, rs, :], ln[0, rs, :], ln[1, rs, :]
        mx = jnp.maximum(jnp.maximum(l0, l1), l2)
        w0, w1, w2 = jnp.exp2(l0 - mx), jnp.exp2(l1 - mx), jnp.exp2(l2 - mx)
        o = (w0 * od[0, rs, :] + w1 * on[0, rs, :] + w2 * on[1, rs, :]) / (w0 + w1 + w2)
        a_ref[rs, :] = (o * _silu(z_ref[rs, :])).astype(BF16)
        return c
    lax.fori_loop(0, SEQ // ROW_CHUNK, comb, 0)


def _att_prompt(proj, bias_mats, a_sample):
    hb = N_HEADS

    def head_blk(b, h):
        return jnp.where(b < BATCH, h, hb - 1)

    in_specs = []
    for g in range(N_GROUPS):
        for part in range(3):
            off = part * N_GROUPS * hb + g * hb
            in_specs.append(pl.BlockSpec(
                (None, SEQ, HEAD_DIM),
                lambda b, h, off=off: (off + head_blk(b, h), jnp.minimum(b, BATCH - 1), 0)))
    in_specs.append(pl.BlockSpec((None, SEQ, HEAD_DIM),
                                 lambda b, h: (9 * hb + head_blk(b, h), jnp.minimum(b, BATCH - 1), 0)))
    for g in range(N_GROUPS):
        in_specs.append(pl.BlockSpec((None, BAND, 2 * BAND), lambda b, h, g=g: (g * hb + head_blk(b, h), 0, 0)))
    in_specs.append(pl.BlockSpec((SAMPLE_ROWS, HEAD_DIM), lambda b, h: (0, h)))
    scratch = [pltpu.VMEM((N_GROUPS, SEQ, HEAD_DIM), BF16)] + \
              [pltpu.VMEM((N_GROUPS, SEQ + BAND, HEAD_DIM), BF16)] * 2 + \
              [pltpu.VMEM((N_GROUPS, SEQ, HEAD_DIM), F32)] * 2 + \
              [pltpu.VMEM((N_GROUPS - 1, SEQ, HEAD_DIM), F32)] * 2 + \
              [pltpu.VMEM((3, SEQ, HEAD_DIM), F32)]
    return pl.pallas_call(
        _att_prompt_kernel,
        grid=(BATCH + 1, N_HEADS),
        in_specs=in_specs,
        out_specs=pl.BlockSpec((SEQ, HEAD_DIM), lambda b, h: (b, h)),
        out_shape=jax.ShapeDtypeStruct((M_ALL, D_MODEL), BF16),
        scratch_shapes=scratch,
        compiler_params=pltpu.CompilerParams(
            dimension_semantics=("arbitrary", "arbitrary"), vmem_limit_bytes=VMEM_LIMIT),
        name="att_prompt",
    )(*([proj] * 10), bias_mats, bias_mats, bias_mats, a_sample)


def _kv_rows_kernel(k_a, v_a, k_b, v_b, o_ref, *, tr):
    pitch = tr + KV_PITCH_PAD

    def emit(srcs):
        flats = [src.reshape(N_HEADS * pitch, HEAD_DIM) for src in srcs]

        def body(i, c):
            t0 = i * 8
            for k in range(8):
                for part, flat in enumerate(flats):
                    o_ref[t0 + k, part] = flat[pl.ds(t0 + k, N_HEADS, stride=pitch), :]
            return c
        lax.fori_loop(0, tr // 8, body, 0)

    @pl.when(pl.program_id(0) == 0)
    def _():
        emit((k_a, v_a))

    @pl.when(pl.program_id(0) == 1)
    def _():
        emit((k_b, v_b))


def _kv_rows(proj_a, proj_b, g):
    keep = WINDOWS[g]
    tr = min(keep, TR_KV)
    nt = keep // tr
    first_blk = (SEQ - keep) // tr
    blks_per_batch = SEQ // tr

    def row_blk(b, t):
        return b * blks_per_batch + first_blk + t

    def spec(layer, part):
        parked = row_blk(BATCH - 1, nt - 1) if layer == 0 else row_blk(0, 0)
        col0 = ((1 + part) * N_GROUPS + g) * N_HEADS
        return pl.BlockSpec((pl.Element(N_HEADS), pl.Element(tr + KV_PITCH_PAD), pl.Element(HEAD_DIM)),
                            lambda l, b, t: (col0, jnp.where(l == layer, row_blk(b, t), parked) * tr, 0))

    return pl.pallas_call(
        functools.partial(_kv_rows_kernel, tr=tr),
        grid=(2, BATCH, nt),
        in_specs=[spec(0, 0), spec(0, 1), spec(1, 0), spec(1, 1)],
        out_specs=pl.BlockSpec((None, None, tr, 2, N_HEADS, HEAD_DIM), lambda l, b, t: (l, b, t, 0, 0, 0)),
        out_shape=jax.ShapeDtypeStruct((2, BATCH, keep, 2, N_HEADS, HEAD_DIM), F32),
        compiler_params=pltpu.CompilerParams(
            dimension_semantics=("arbitrary",) * 3, vmem_limit_bytes=VMEM_LIMIT),
        name="kv_rows",
    )(proj_a, proj_a, proj_b, proj_b)


def _att_sample_kernel(x_ref, c0, c1, c2, bias_ref, a_ref, kv0, kv1, kv2):
    caches = (c0, c1, c2)
    kv_outs = (kv0, kv1, kv2)
    hb = N_HEADS
    outs, lses = [], []
    for g in range(N_GROUPS):
        q = x_ref[g * hb:(g + 1) * hb, :]
        kn = x_ref[(3 + g) * hb:(4 + g) * hb, :]
        vn = x_ref[(6 + g) * hb:(7 + g) * hb, :]
        kv_outs[g][0] = kn
        kv_outs[g][1] = vn
        kc = caches[g][:, 0]
        vc = caches[g][:, 1]
        s_c = jnp.sum(kc * q[None], axis=-1, keepdims=True) * ATT_SCALE + bias_ref[g, 0:BAND]
        s_n = jnp.sum(kn * q, axis=-1, keepdims=True) * ATT_SCALE + bias_ref[g, BAND]
        m = jnp.maximum(jnp.max(s_c, axis=0), s_n)
        p_c = jnp.exp(s_c - m[None])
        p_n = jnp.exp(s_n - m)
        l = jnp.sum(p_c, axis=0) + p_n
        o = (jnp.sum(p_c * vc, axis=0) + p_n * vn) / l
        outs.append(o)
        lses.append(m + jnp.log(l))
    mx = jnp.maximum(jnp.maximum(lses[0], lses[1]), lses[2])
    ws = [jnp.exp(ls - mx) for ls in lses]
    o = (ws[0] * outs[0] + ws[1] * outs[1] + ws[2] * outs[2]) / (ws[0] + ws[1] + ws[2])
    z = x_ref[9 * hb:10 * hb, :]
    a_ref[...] = o * _silu(z)


def _att_sample(proj_s, caches, bias_s, layer):
    hb = N_HEADS
    in_specs = [pl.BlockSpec((None, ATT_IN_COLS // HEAD_DIM, HEAD_DIM), lambda b: (b, 0, 0))]
    for g in range(N_GROUPS):
        in_specs.append(pl.BlockSpec((None, None, BAND, None, 2, hb, HEAD_DIM),
                                     lambda b: (layer, b, 0, 0, 0, 0, 0)))
    in_specs.append(pl.BlockSpec((N_GROUPS, BAND + 1, hb, 1), lambda b: (0, 0, 0, 0)))
    out_specs = [pl.BlockSpec((None, hb, HEAD_DIM), lambda b: (b, 0, 0))]
    out_shapes = [jax.ShapeDtypeStruct((DEC_BATCH, hb, HEAD_DIM), F32)]
    for g in range(N_GROUPS):
        out_specs.append(pl.BlockSpec((None, 2, hb, HEAD_DIM), lambda b: (b, 0, 0, 0)))
        out_shapes.append(jax.ShapeDtypeStruct((DEC_BATCH, 2, hb, HEAD_DIM), F32))
    return pl.pallas_call(
        _att_sample_kernel,
        grid=(DEC_BATCH,),
        in_specs=in_specs,
        out_specs=out_specs,
        out_shape=out_shapes,
        compiler_params=pltpu.CompilerParams(
            dimension_semantics=("arbitrary",), vmem_limit_bytes=VMEM_LIMIT),
        name="att_sample",
    )(proj_s, *caches, bias_s)


def kernel(x_prompt, x_sample, c_prompt, c_sample, cache_kv0, cache_kv1, cache_kv2, state_pool,
           norm_pre, norm_post, ada_w, ada_b, t5_bias, pool_w_in, pool_w_grp, pool_scale,
           pool_w_out, att_w_in, att_w_out):
    n_att = DEPTH // 2
    xp0 = x_prompt.reshape(M_PROMPT, D_MODEL)
    xs0 = jnp.zeros((SAMPLE_ROWS, D_MODEL), F32).at[0:DEC_BATCH].set(x_sample.reshape(DEC_BATCH, D_MODEL))
    c_all = jnp.zeros((C_ROWS, D_MODEL), F32)
    c_all = c_all.at[0:BATCH].set(c_prompt).at[C_SAMPLE_ROW0:C_SAMPLE_ROW0 + DEC_BATCH].set(c_sample)

    mod = _ada_all(c_all, ada_w, ada_b)
    gains_pre = norm_pre.reshape(DEPTH, 1, D_MODEL)
    gains_post = norm_post.reshape(DEPTH, 1, D_MODEL)

    bias_mats = _bias_matrices(t5_bias)
    bias_s = bias_mats[:, 0, 0:BAND + 1].reshape(N_GROUPS, N_HEADS, BAND + 1)
    bias_s = jnp.transpose(bias_s, (0, 2, 1))[..., None]
    caches = [c.reshape(n_att, DEC_BATCH, BAND, dil, 2, N_HEADS, HEAD_DIM)
              for c, dil in zip((cache_kv0, cache_kv1, cache_kv2), DILATIONS)]
    state_t = jnp.transpose(state_pool, (0, 2, 1, 3))

    kv_s = [[] for _ in range(N_GROUPS)]
    pool_p, pool_s, att_projs = [], [], []

    (h,) = _norm_step(xp0, None, mod, gains_post, gains_pre, None, 0, x_sample=xs0)
    x = None
    for i in range(DEPTH):
        li = i // 2
        if i % 2 == 0:
            proj = _matmul(h, pool_w_in, li, 2 * POOL_WIDTH, 1024, F32, "pool_in_proj")
            a, pp, ps = _pool_mix(proj, state_t, pool_w_grp, pool_scale, li)
            pool_p.append(pp)
            pool_s.append(jnp.transpose(ps, (1, 0, 2)))
            y = _matmul(a, pool_w_out, li, D_MODEL, 512, BF16, "pool_out_proj")
        else:
            proj = _matmul(h, att_w_in, li, ATT_IN_COLS, 1024, F32, "att_in_proj", slabs=True)
            att_projs.append(proj)
            proj_s = jnp.transpose(proj[:, M_PROMPT:M_PROMPT + DEC_BATCH, :], (1, 0, 2))
            souts = _att_sample(proj_s, caches, bias_s, li)
            a_s = jnp.zeros((SAMPLE_ROWS, D_MODEL), F32).at[0:DEC_BATCH].set(souts[0].reshape(DEC_BATCH, D_MODEL))
            a = _att_prompt(proj, bias_mats, a_s.astype(BF16))
            for g in range(N_GROUPS):
                kv_s[g].append(souts[1 + g].reshape(DEC_BATCH, 1, 2, N_HEADS, HEAD_DIM))
            y = _matmul(a, att_w_out, li, D_MODEL, 1024, BF16, "att_out_proj")
        last = i + 1 == DEPTH
        if i == 0:
            x, h = _norm_step(xp0, y, mod, gains_post, gains_pre, i, i + 1, x_sample=xs0)
        elif not last:
            x, h = _norm_step(x, y, mod, gains_post, gains_pre, i, i + 1)
        else:
            y_p, y_s = _norm_step(x, y, mod, gains_post, gains_pre, i, None, split_out=True)

    kv_p = [_kv_rows(att_projs[0], att_projs[1], g) for g in range(N_GROUPS)]
    y_prompt = y_p.reshape(BATCH, SEQ, D_MODEL)
    y_sample = y_s[0:DEC_BATCH].reshape(DEC_BATCH, 1, D_MODEL)
    return (y_prompt, y_sample, kv_p[0], kv_p[1], kv_p[2], jnp.stack(pool_p),
            jnp.stack(kv_s[0]), jnp.stack(kv_s[1]), jnp.stack(kv_s[2]), jnp.stack(pool_s))
```

```python
import functools

import numpy as np
import jax
import jax.numpy as jnp
from jax import lax
from jax.experimental import pallas as pl
from jax.experimental.pallas import tpu as pltpu

D_MODEL = 2048
BATCH = 4
SEQ = 2048
DEPTH = 4
DEC_BATCH = 8
HEAD_DIM = 128
N_HEADS = 16
DILATIONS = (1, 4, 16)
WINDOWS = (128, 512, 2048)
N_GROUPS = 3
QKV_WIDTH = N_GROUPS * D_MODEL
ATT_IN_COLS = 3 * QKV_WIDTH + D_MODEL
BAND = 128
ATT_SCALE = HEAD_DIM ** -0.5
LOG2E = 1.4426950408889634
POOL_WINDOWS = (2, 4, 8, 16)
POOL_WIDTH = 2 * D_MODEL
POOL_GROUP = POOL_WIDTH // 4
POOL_BUF = 15
N_BUCKETS = 32
T5_MAX_DIST = 2048
RMS_EPS = 1e-6
NEG_INF = -1e30

M_PROMPT = BATCH * SEQ
SAMPLE_ROWS = 64
M_ALL = M_PROMPT + SAMPLE_ROWS
C_ROWS = 2 * SAMPLE_ROWS
C_PROMPT_ROWS = 8
C_SAMPLE_ROW0 = SAMPLE_ROWS

TM_MATMUL = 1376
TE_NORM = 512
TP_POOL = 1024
POOL_HALO = 24
ATT_UNITS = 16
N_BLOCKS = SEQ // BAND
TR_KV = 256
ROW_CHUNK = 256
KV_PITCH_PAD = 8
VMEM_LIMIT = 56 * 1024 * 1024

F32 = jnp.float32
BF16 = jnp.bfloat16


def _silu(x):
    half = 0.5 * x
    return half + half * jnp.tanh(half)


def _cast_rows_to_bf16(src_ref, dst_ref, rows, chunk=ROW_CHUNK):
    def body(i, c):
        r = pl.multiple_of(i * chunk, chunk)
        dst_ref[pl.ds(r, chunk), :] = src_ref[pl.ds(r, chunk), :].astype(BF16)
        return c
    lax.fori_loop(0, rows // chunk, body, 0)


def _ada_kernel(c_ref, w_ref, b_ref, o_ref):
    a = _silu(c_ref[...]).astype(BF16)
    kc = 512
    acc = jnp.zeros(o_ref.shape, F32)
    for k0 in range(0, D_MODEL, kc):
        acc = acc + jnp.dot(a[:, k0:k0 + kc], w_ref[k0:k0 + kc, :].astype(BF16),
                            preferred_element_type=F32)
    o_ref[...] = acc + b_ref[...]


def _ada_all(c_all, ada_w, ada_b):
    tn = 1024
    n = 3 * D_MODEL
    return pl.pallas_call(
        _ada_kernel,
        grid=(DEPTH, n // tn),
        in_specs=[pl.BlockSpec((C_ROWS, D_MODEL), lambda l, j: (0, 0)),
                  pl.BlockSpec((None, D_MODEL, tn), lambda l, j: (l, 0, j)),
                  pl.BlockSpec((None, 1, tn), lambda l, j: (l, 0, j))],
        out_specs=pl.BlockSpec((None, C_ROWS, tn), lambda l, j: (l, 0, j)),
        out_shape=jax.ShapeDtypeStruct((DEPTH, C_ROWS, n), F32),
        compiler_params=pltpu.CompilerParams(
            dimension_semantics=("arbitrary", "arbitrary"), vmem_limit_bytes=VMEM_LIMIT),
        name="ada_mod",
    )(c_all, ada_w, ada_b.reshape(DEPTH, 1, n))


def _t5_bucket(dist):
    dist = np.asarray(dist, dtype=np.int64)
    max_exact = N_BUCKETS // 2
    ratio = np.log(np.maximum(dist, 1) / max_exact) / np.log(T5_MAX_DIST / max_exact)
    large = np.minimum(max_exact + (ratio * (N_BUCKETS - max_exact)).astype(np.int64), N_BUCKETS - 1)
    return np.where(dist < max_exact, dist, large).astype(np.int32)


def _bucket_index_table():
    rel = np.arange(BAND)[:, None] + BAND - np.arange(2 * BAND)[None, :]
    inband = (rel >= 0) & (rel <= BAND)
    out = []
    for dil in DILATIONS:
        bucket = _t5_bucket(np.clip(rel, 0, BAND) * dil)
        out.append(np.where(inband, bucket, -1))
    return np.stack(out).astype(np.int32)


def _bias_kernel(tab_ref, idx_ref, o_ref):
    g = pl.program_id(0)
    idx = idx_ref[...]

    def head(h, c):
        acc = jnp.full(idx.shape, NEG_INF, F32)
        for b in range(N_BUCKETS):
            acc = jnp.where(idx == b, tab_ref[b, g * N_HEADS + h], acc)
        o_ref[h] = acc
        return c
    lax.fori_loop(0, N_HEADS, head, 0)


def _bias_matrices(t5_bias):
    idx = jnp.asarray(_bucket_index_table())
    n_sub = N_GROUPS * N_HEADS
    return pl.pallas_call(
        _bias_kernel,
        grid=(N_GROUPS,),
        in_specs=[pl.BlockSpec(memory_space=pltpu.SMEM),
                  pl.BlockSpec((None, BAND, 2 * BAND), lambda g: (g, 0, 0))],
        out_specs=pl.BlockSpec((N_HEADS, BAND, 2 * BAND), lambda g: (g, 0, 0)),
        out_shape=jax.ShapeDtypeStruct((n_sub, BAND, 2 * BAND), F32),
        name="t5_bias_mats",
    )(t5_bias, idx)


def _norm_kernel(*refs, has_post, has_pre, split_in, split_out, n_prompt_tiles, tiles_per_batch):
    refs = list(refs)
    xp_ref = refs.pop(0)
    xs_ref = refs.pop(0) if split_in else xp_ref
    if has_post:
        y_ref, gpost_ref, gate_p_ref, gate_s_ref = refs[:4]
        refs = refs[4:]
    if has_pre:
        gpre_ref, shift_p_ref, scale_p_ref, shift_s_ref, scale_s_ref = refs[:5]
        refs = refs[5:]
    if has_post:
        xop_ref = refs.pop(0)
        xos_ref = refs.pop(0) if split_out else xop_ref
    if has_pre:
        h_ref = refs.pop(0)

    def rms(v, g):
        return v * lax.rsqrt(jnp.mean(v * v, axis=-1, keepdims=True) + RMS_EPS) * g

    def body(x_ref, xo_ref, rows, gate, shift, scale):
        x = x_ref[rows, :]
        if has_post:
            x = x + gate * rms(y_ref[rows, :].astype(F32), gpost_ref[...])
            xo_ref[rows, :] = x
        if has_pre:
            h = rms(x, gpre_ref[...]) * (1.0 + scale) + shift
            h_ref[rows, :] = h.astype(BF16)

    t = pl.program_id(0)

    @pl.when(t < n_prompt_tiles)
    def _():
        row = pl.ds(t // tiles_per_batch, 1)
        body(xp_ref, xop_ref if has_post else None, slice(None),
             gate_p_ref[row, :] if has_post else None,
             shift_p_ref[row, :] if has_pre else None,
             scale_p_ref[row, :] if has_pre else None)

    @pl.when(t == n_prompt_tiles)
    def _():
        body(xs_ref, xos_ref if has_post else None, slice(0, SAMPLE_ROWS),
             gate_s_ref[...] if has_post else None,
             shift_s_ref[...] if has_pre else None,
             scale_s_ref[...] if has_pre else None)


def _norm_step(x, y, mod, norm_post, norm_pre, post_layer, pre_layer, x_sample=None, split_out=False):
    has_post = post_layer is not None
    has_pre = pre_layer is not None
    split_in = x_sample is not None
    te = TE_NORM
    npt = M_PROMPT // te
    tiles_per_batch = SEQ // te
    row_spec = pl.BlockSpec((te, D_MODEL), lambda t: (t, 0))
    prompt_row_spec = pl.BlockSpec((te, D_MODEL), lambda t: (jnp.minimum(t, npt - 1), 0))
    sample_row_spec = pl.BlockSpec((SAMPLE_ROWS, D_MODEL), lambda t: (0, 0))

    def mod_p_spec(layer, part):
        return pl.BlockSpec((None, C_PROMPT_ROWS, D_MODEL), lambda t: (layer, 0, part))

    def mod_s_spec(layer, part):
        return pl.BlockSpec((None, SAMPLE_ROWS, D_MODEL),
                            lambda t: (layer, C_SAMPLE_ROW0 // SAMPLE_ROWS, part))

    def gain_spec(layer):
        return pl.BlockSpec((None, 1, D_MODEL), lambda t: (layer, 0, 0))

    if split_in:
        args, in_specs = [x, x_sample], [prompt_row_spec, sample_row_spec]
    else:
        args, in_specs = [x], [row_spec]
    out_shapes, out_specs = [], []
    if has_post:
        args += [y, norm_post, mod, mod]
        in_specs += [row_spec, gain_spec(post_layer), mod_p_spec(post_layer, 2), mod_s_spec(post_layer, 2)]
        if split_out:
            out_shapes += [jax.ShapeDtypeStruct((M_PROMPT, D_MODEL), F32),
                           jax.ShapeDtypeStruct((SAMPLE_ROWS, D_MODEL), F32)]
            out_specs += [prompt_row_spec, sample_row_spec]
        else:
            out_shapes.append(jax.ShapeDtypeStruct((M_ALL, D_MODEL), F32))
            out_specs.append(row_spec)
    if has_pre:
        args += [norm_pre, mod, mod, mod, mod]
        in_specs += [gain_spec(pre_layer), mod_p_spec(pre_layer, 0), mod_p_spec(pre_layer, 1),
                     mod_s_spec(pre_layer, 0), mod_s_spec(pre_layer, 1)]
        out_shapes.append(jax.ShapeDtypeStruct((M_ALL, D_MODEL), BF16))
        out_specs.append(row_spec)
    outs = pl.pallas_call(
        functools.partial(_norm_kernel, has_post=has_post, has_pre=has_pre, split_in=split_in,
                          split_out=split_out, n_prompt_tiles=npt, tiles_per_batch=tiles_per_batch),
        grid=(npt + 1,),
        in_specs=in_specs,
        out_specs=out_specs,
        out_shape=out_shapes,
        compiler_params=pltpu.CompilerParams(
            dimension_semantics=("arbitrary",), vmem_limit_bytes=VMEM_LIMIT),
        name="norm_step",
    )(*args)
    return outs


def _mm_kernel(a_ref, w_ref, o_ref, wb_ref, *, k_rows, slabs):
    @pl.when(pl.program_id(1) == 0)
    def _():
        _cast_rows_to_bf16(w_ref, wb_ref, k_rows)

    acc = jnp.dot(a_ref[...], wb_ref[...], preferred_element_type=F32).astype(o_ref.dtype)
    if slabs:
        for c in range(o_ref.shape[0]):
            o_ref[c] = acc[:, c * HEAD_DIM:(c + 1) * HEAD_DIM]
    else:
        o_ref[...] = acc


def _matmul(a, w, layer, n_out, tn, tm, out_dtype, name, slabs=False):
    m, k = a.shape
    if slabs:
        out_spec = pl.BlockSpec((tn // HEAD_DIM, tm, HEAD_DIM), lambda j, i: (j, i, 0))
        out_shape = jax.ShapeDtypeStruct((n_out // HEAD_DIM, m, HEAD_DIM), out_dtype)
    else:
        out_spec = pl.BlockSpec((tm, tn), lambda j, i: (i, j))
        out_shape = jax.ShapeDtypeStruct((m, n_out), out_dtype)
    return pl.pallas_call(
        functools.partial(_mm_kernel, k_rows=k, slabs=slabs),
        grid=(n_out // tn, m // tm),
        in_specs=[pl.BlockSpec((tm, k), lambda j, i: (i, 0)),
                  pl.BlockSpec((None, k, tn), lambda j, i: (layer, 0, j))],
        out_specs=out_spec,
        out_shape=out_shape,
        scratch_shapes=[pltpu.VMEM((k, tn), BF16)],
        compiler_params=pltpu.CompilerParams(
            dimension_semantics=("arbitrary", "arbitrary"), vmem_limit_bytes=VMEM_LIMIT),
        name=name,
    )(a, w)


def _pool_kernel(u_ref, halo_ref, z_ref, st_ref, w_ref, sc_ref,
                 a_ref, pp_ref, ps_ref, wb_ref, buf_a, buf_b, *, n_prompt_tiles, tiles_per_batch):
    g = pl.program_id(0)
    t = pl.program_id(1)
    tp = TP_POOL
    h0 = POOL_HALO

    @pl.when(t == 0)
    def _():
        _cast_rows_to_bf16(w_ref, wb_ref, POOL_GROUP)

    def finish(r, z, rows):
        y = jnp.dot(r.astype(BF16), wb_ref[...], preferred_element_type=F32) * sc_ref[...]
        a_ref[rows, :] = (y * _silu(z)).astype(BF16)

    def prompt_tile(n_steps):
        w = 2 ** n_steps
        first = (t % tiles_per_batch) == 0
        buf_a[0:8, :] = jnp.zeros((8, POOL_GROUP), F32)
        buf_b[0:8, :] = jnp.zeros((8, POOL_GROUP), F32)
        buf_a[8:h0, :] = jnp.where(first, 0.0, halo_ref[...])
        buf_a[h0:h0 + tp, :] = u_ref[...]
        src, dst = buf_a, buf_b
        n = tp + h0 - 8
        for s in range(n_steps):
            sh = 2 ** s
            dst[8:8 + n, :] = src[8:8 + n, :] + src[8 - sh:8 - sh + n, :]
            src, dst = dst, src
        pos = (t % tiles_per_batch) * tp + lax.broadcasted_iota(jnp.int32, (tp, 1), 0)
        inv_cnt = 1.0 / jnp.minimum(pos + 1, w).astype(F32)
        u = u_ref[...]
        r = src[h0:h0 + tp, :] * inv_cnt - u
        finish(r, z_ref[...], slice(None))

        @pl.when((t % tiles_per_batch) == tiles_per_batch - 1)
        def _():
            pp_ref[...] = u_ref[tp - POOL_BUF:tp, :]

    def sample_tile(n_steps):
        w = 2 ** n_steps
        u_new = u_ref[0:DEC_BATCH, :]
        acc = u_new
        for k in range(1, w):
            acc = acc + st_ref[POOL_BUF - k]
        r = acc / float(w) - u_new
        buf_a[0:DEC_BATCH, :] = r
        buf_a[DEC_BATCH:SAMPLE_ROWS, :] = jnp.zeros((SAMPLE_ROWS - DEC_BATCH, POOL_GROUP), F32)
        finish(buf_a[0:SAMPLE_ROWS, :], z_ref[0:SAMPLE_ROWS, :], slice(0, SAMPLE_ROWS))
        for k in range(POOL_BUF - 1):
            ps_ref[k] = st_ref[k + 1]
        ps_ref[POOL_BUF - 1] = u_new

    for gi in range(len(POOL_WINDOWS)):
        @pl.when((g == gi) & (t < n_prompt_tiles))
        def _(gi=gi):
            prompt_tile(gi + 1)

        @pl.when((g == gi) & (t == n_prompt_tiles))
        def _(gi=gi):
            sample_tile(gi + 1)


def _pool_mix(proj, state_t, w_grp, scale, layer):
    tp = TP_POOL
    npt = M_PROMPT // tp
    tpb = SEQ // tp
    ng = len(POOL_WINDOWS)
    halo_blocks = tp // 16
    outs = pl.pallas_call(
        functools.partial(_pool_kernel, n_prompt_tiles=npt, tiles_per_batch=tpb),
        grid=(ng, npt + 1),
        in_specs=[
            pl.BlockSpec((tp, POOL_GROUP), lambda g, t: (t, g)),
            pl.BlockSpec((16, POOL_GROUP), lambda g, t: (jnp.maximum(t * halo_blocks - 1, 0), g)),
            pl.BlockSpec((tp, POOL_GROUP), lambda g, t: (t, ng + g)),
            pl.BlockSpec((None, POOL_BUF, DEC_BATCH, POOL_GROUP), lambda g, t: (layer, 0, 0, g)),
            pl.BlockSpec((None, None, POOL_GROUP, POOL_GROUP), lambda g, t: (layer, g, 0, 0)),
            pl.BlockSpec((None, 1, POOL_GROUP), lambda g, t: (layer, 0, g)),
        ],
        out_specs=[
            pl.BlockSpec((tp, POOL_GROUP), lambda g, t: (t, g)),
            pl.BlockSpec((None, POOL_BUF, POOL_GROUP),
                         lambda g, t: (jnp.minimum(t // tpb, BATCH - 1), 0, g)),
            pl.BlockSpec((POOL_BUF, DEC_BATCH, POOL_GROUP), lambda g, t: (0, 0, g)),
        ],
        out_shape=[
            jax.ShapeDtypeStruct((M_ALL, POOL_WIDTH), BF16),
            jax.ShapeDtypeStruct((BATCH, POOL_BUF, POOL_WIDTH), F32),
            jax.ShapeDtypeStruct((POOL_BUF, DEC_BATCH, POOL_WIDTH), F32),
        ],
        scratch_shapes=[pltpu.VMEM((POOL_GROUP, POOL_GROUP), BF16),
                        pltpu.VMEM((tp + POOL_HALO, POOL_GROUP), F32),
                        pltpu.VMEM((tp + POOL_HALO, POOL_GROUP), F32)],
        compiler_params=pltpu.CompilerParams(
            dimension_semantics=("arbitrary", "arbitrary"), vmem_limit_bytes=VMEM_LIMIT),
        name="pool_mix",
    )(proj, proj, proj, state_t, w_grp, scale.reshape(-1, 1, POOL_WIDTH))
    return outs


def _att_prompt_kernel(*refs):
    n_in = 13
    as_ref, a_ref = refs[n_in], refs[n_in + 1]

    @pl.when(pl.program_id(0) < BATCH)
    def _():
        _att_prompt_tile(*refs[:n_in], *refs[n_in + 1:])

    @pl.when(pl.program_id(0) == BATCH)
    def _():
        a_ref[0:SAMPLE_ROWS, :] = as_ref[...]


def _att_prompt_tile(q0, k0, v0, q1, k1, v1, q2, k2, v2, z_ref, b0, b1, b2,
                     a_ref, qd, kd, vd, od, ld, on, ln, tmp3):
    tmp = tmp3.at[0]
    q_refs, k_refs, v_refs = (q0, q1, q2), (k0, k1, k2), (v0, v1, v2)
    bias_refs = (b0, b1, b2)
    nu = ATT_UNITS

    for g, dil in enumerate(DILATIONS):
        n = SEQ // dil
        nb = n // BAND
        zero_blk = jnp.zeros((BAND, HEAD_DIM), BF16)
        kd[g, 0:BAND, :] = zero_blk
        vd[g, 0:BAND, :] = zero_blk
        for src, dst, off, mul in ((q_refs[g], qd, 0, ATT_SCALE * LOG2E), (k_refs[g], kd, BAND, None),
                                   (v_refs[g], vd, BAND, None)):
            def to_bf16(x, mul=mul):
                return (x if mul is None else x * mul).astype(BF16)

            if dil == 1:
                def cp(i, c, src=src, dst=dst, g=g, off=off, to_bf16=to_bf16):
                    r = pl.multiple_of(i * ROW_CHUNK, ROW_CHUNK)
                    dst[g, pl.ds(off + r, ROW_CHUNK), :] = to_bf16(src[pl.ds(r, ROW_CHUNK), :])
                    return c
                lax.fori_loop(0, SEQ // ROW_CHUNK, cp, 0)
            elif dil == 4:
                for r in range(dil):
                    dst[g, off + r * n:off + (r + 1) * n, :] = to_bf16(src[pl.ds(r, n, stride=dil), :])
            else:
                stage = tmp3.at[(off > 0) + (dst is vd)]
                quarter = SEQ // 4
                for r_lo in range(4):
                    stage[r_lo * quarter:(r_lo + 1) * quarter, :] = src[pl.ds(r_lo, quarter, stride=4), :]
                for r in range(dil):
                    r_lo, r_hi = r % 4, r // 4
                    dst[g, off + r * n:off + (r + 1) * n, :] = to_bf16(
                        stage[pl.ds(r_lo * quarter + r_hi, n, stride=4), :])

    def unit_batch(g, u0, seq_blocks):
        with_prev = seq_blocks > 1
        rows = nu * BAND
        row = pl.multiple_of(u0 * BAND, rows)
        shape3 = (nu, BAND, HEAD_DIM)
        q = qd[g, pl.ds(row, rows), :].reshape(shape3)
        kc = kd[g, pl.ds(row + BAND, rows), :].reshape(shape3)
        vc = vd[g, pl.ds(row + BAND, rows), :].reshape(shape3)
        bias = bias_refs[g]
        s_c = jnp.einsum('uqe,uke->uqk', q, kc, preferred_element_type=F32) \
            + (bias[:, BAND:2 * BAND] * LOG2E)[None]
        if with_prev:
            kp = kd[g, pl.ds(row, rows), :].reshape(shape3)
            vp = vd[g, pl.ds(row, rows), :].reshape(shape3)
            bias_p = jnp.broadcast_to((bias[:, 0:BAND] * LOG2E)[None], (nu, BAND, BAND))
            blk = u0 + lax.broadcasted_iota(jnp.int32, (nu, BAND, BAND), 0)
            bias_p = jnp.where((blk & (seq_blocks - 1)) == 0, NEG_INF, bias_p)
            s_p = jnp.einsum('uqe,uke->uqk', q, kp, preferred_element_type=F32) + bias_p
            m = jnp.max(jnp.maximum(s_c, s_p), axis=-1, keepdims=True)
            p_c = jnp.exp2(s_c - m)
            p_p = jnp.exp2(s_p - m)
            l = jnp.sum(p_c + p_p, axis=-1, keepdims=True)
            acc = jnp.einsum('uqk,uke->uqe', p_c.astype(BF16), vc, preferred_element_type=F32) \
                + jnp.einsum('uqk,uke->uqe', p_p.astype(BF16), vp, preferred_element_type=F32)
        else:
            m = jnp.max(s_c, axis=-1, keepdims=True)
            p_c = jnp.exp2(s_c - m)
            l = jnp.sum(p_c, axis=-1, keepdims=True)
            acc = jnp.einsum('uqk,uke->uqe', p_c.astype(BF16), vc, preferred_element_type=F32)
        od[g, pl.ds(row, rows), :] = (acc / l).reshape(rows, HEAD_DIM)
        ld[g, pl.ds(row, rows), :] = jnp.broadcast_to(m + jnp.log(l) * LOG2E, shape3).reshape(rows, HEAD_DIM)

    for g, dil in enumerate(DILATIONS):
        nb = (SEQ // dil) // BAND

        def batch_body(i, c, g=g, nb=nb):
            unit_batch(g, i * nu, nb)
            return c
        lax.fori_loop(0, N_BLOCKS // nu, batch_body, 0)

    def interleave4(src, dst, n_rows):
        quarter = n_rows // 4
        for r in range(4):
            dst[pl.ds(r, quarter, stride=4), :] = src[r * quarter:(r + 1) * quarter, :]

    for src_all, dst_all in ((od, on), (ld, ln)):
        interleave4(src_all.at[1], dst_all.at[0], SEQ)
        for r_lo in range(4):
            for r_hi in range(4):
                r = r_lo + 4 * r_hi
                tmp[pl.ds(r_lo * (SEQ // 4) + r_hi, BAND, stride=4), :] = src_all[2, r * BAND:(r + 1) * BAND, :]
        interleave4(tmp, dst_all.at[1], SEQ)

    def comb(i, c):
        rs = pl.ds(pl.multiple_of(i * ROW_CHUNK, ROW_CHUNK), ROW_CHUNK)
        l0, l1, l2 = ld[0, rs, :], ln[0, rs, :], ln[1, rs, :]
        mx = jnp.maximum(jnp.maximum(l0, l1), l2)
        w0, w1, w2 = jnp.exp2(l0 - mx), jnp.exp2(l1 - mx), jnp.exp2(l2 - mx)
        o = (w0 * od[0, rs, :] + w1 * on[0, rs, :] + w2 * on[1, rs, :]) / (w0 + w1 + w2)
        a_ref[rs, :] = (o * _silu(z_ref[rs, :])).astype(BF16)
        return c
    lax.fori_loop(0, SEQ // ROW_CHUNK, comb, 0)


def _att_prompt(proj, bias_mats, a_sample):
    hb = N_HEADS

    def head_blk(b, h):
        return jnp.where(b < BATCH, h, hb - 1)

    in_specs = []
    for g in range(N_GROUPS):
        for part in range(3):
            off = part * N_GROUPS * hb + g * hb
            in_specs.append(pl.BlockSpec(
                (None, SEQ, HEAD_DIM),
                lambda b, h, off=off: (off + head_blk(b, h), jnp.minimum(b, BATCH - 1), 0)))
    in_specs.append(pl.BlockSpec((None, SEQ, HEAD_DIM),
                                 lambda b, h: (9 * hb + head_blk(b, h), jnp.minimum(b, BATCH - 1), 0)))
    for g in range(N_GROUPS):
        in_specs.append(pl.BlockSpec((None, BAND, 2 * BAND), lambda b, h, g=g: (g * hb + head_blk(b, h), 0, 0)))
    in_specs.append(pl.BlockSpec((SAMPLE_ROWS, HEAD_DIM), lambda b, h: (0, h)))
    scratch = [pltpu.VMEM((N_GROUPS, SEQ, HEAD_DIM), BF16)] + \
              [pltpu.VMEM((N_GROUPS, SEQ + BAND, HEAD_DIM), BF16)] * 2 + \
              [pltpu.VMEM((N_GROUPS, SEQ, HEAD_DIM), F32)] * 2 + \
              [pltpu.VMEM((N_GROUPS - 1, SEQ, HEAD_DIM), F32)] * 2 + \
              [pltpu.VMEM((3, SEQ, HEAD_DIM), F32)]
    return pl.pallas_call(
        _att_prompt_kernel,
        grid=(BATCH + 1, N_HEADS),
        in_specs=in_specs,
        out_specs=pl.BlockSpec((SEQ, HEAD_DIM), lambda b, h: (b, h)),
        out_shape=jax.ShapeDtypeStruct((M_ALL, D_MODEL), BF16),
        scratch_shapes=scratch,
        compiler_params=pltpu.CompilerParams(
            dimension_semantics=("arbitrary", "arbitrary"), vmem_limit_bytes=VMEM_LIMIT),
        name="att_prompt",
    )(*([proj] * 10), bias_mats, bias_mats, bias_mats, a_sample)


def _kv_rows_kernel(k_a, v_a, k_b, v_b, o_ref, *, tr):
    pitch = tr + KV_PITCH_PAD

    def emit(srcs):
        flats = [src.reshape(N_HEADS * pitch, HEAD_DIM) for src in srcs]

        def body(i, c):
            t0 = i * 8
            for k in range(8):
                for part, flat in enumerate(flats):
                    o_ref[t0 + k, part] = flat[pl.ds(t0 + k, N_HEADS, stride=pitch), :]
            return c
        lax.fori_loop(0, tr // 8, body, 0)

    @pl.when(pl.program_id(0) == 0)
    def _():
        emit((k_a, v_a))

    @pl.when(pl.program_id(0) == 1)
    def _():
        emit((k_b, v_b))


def _kv_rows(proj_a, proj_b, g):
    keep = WINDOWS[g]
    tr = min(keep, TR_KV)
    nt = keep // tr
    first_blk = (SEQ - keep) // tr
    blks_per_batch = SEQ // tr

    def row_blk(b, t):
        return b * blks_per_batch + first_blk + t

    def spec(layer, part):
        parked = row_blk(BATCH - 1, nt - 1) if layer == 0 else row_blk(0, 0)
        col0 = ((1 + part) * N_GROUPS + g) * N_HEADS
        return pl.BlockSpec((pl.Element(N_HEADS), pl.Element(tr + KV_PITCH_PAD), pl.Element(HEAD_DIM)),
                            lambda l, b, t: (col0, jnp.where(l == layer, row_blk(b, t), parked) * tr, 0))

    return pl.pallas_call(
        functools.partial(_kv_rows_kernel, tr=tr),
        grid=(2, BATCH, nt),
        in_specs=[spec(0, 0), spec(0, 1), spec(1, 0), spec(1, 1)],
        out_specs=pl.BlockSpec((None, None, tr, 2, N_HEADS, HEAD_DIM), lambda l, b, t: (l, b, t, 0, 0, 0)),
        out_shape=jax.ShapeDtypeStruct((2, BATCH, keep, 2, N_HEADS, HEAD_DIM), F32),
        compiler_params=pltpu.CompilerParams(
            dimension_semantics=("arbitrary",) * 3, vmem_limit_bytes=VMEM_LIMIT),
        name="kv_rows",
    )(proj_a, proj_a, proj_b, proj_b)


def _att_sample_kernel(x_ref, c0, c1, c2, bias_ref, a_ref, kv0, kv1, kv2):
    caches = (c0, c1, c2)
    kv_outs = (kv0, kv1, kv2)
    hb = N_HEADS
    outs, lses = [], []
    for g in range(N_GROUPS):
        q = x_ref[g * hb:(g + 1) * hb, :]
        kn = x_ref[(3 + g) * hb:(4 + g) * hb, :]
        vn = x_ref[(6 + g) * hb:(7 + g) * hb, :]
        kv_outs[g][0] = kn
        kv_outs[g][1] = vn
        kc = caches[g][:, 0]
        vc = caches[g][:, 1]
        s_c = jnp.sum(kc * q[None], axis=-1, keepdims=True) * ATT_SCALE + bias_ref[g, 0:BAND]
        s_n = jnp.sum(kn * q, axis=-1, keepdims=True) * ATT_SCALE + bias_ref[g, BAND]
        m = jnp.maximum(jnp.max(s_c, axis=0), s_n)
        p_c = jnp.exp(s_c - m[None])
        p_n = jnp.exp(s_n - m)
        l = jnp.sum(p_c, axis=0) + p_n
        o = (jnp.sum(p_c * vc, axis=0) + p_n * vn) / l
        outs.append(o)
        lses.append(m + jnp.log(l))
    mx = jnp.maximum(jnp.maximum(lses[0], lses[1]), lses[2])
    ws = [jnp.exp(ls - mx) for ls in lses]
    o = (ws[0] * outs[0] + ws[1] * outs[1] + ws[2] * outs[2]) / (ws[0] + ws[1] + ws[2])
    z = x_ref[9 * hb:10 * hb, :]
    a_ref[...] = o * _silu(z)


def _att_sample(proj_s, caches, bias_s, layer):
    hb = N_HEADS
    in_specs = [pl.BlockSpec((None, ATT_IN_COLS // HEAD_DIM, HEAD_DIM), lambda b: (b, 0, 0))]
    for g in range(N_GROUPS):
        in_specs.append(pl.BlockSpec((None, None, BAND, None, 2, hb, HEAD_DIM),
                                     lambda b: (layer, b, 0, 0, 0, 0, 0)))
    in_specs.append(pl.BlockSpec((N_GROUPS, BAND + 1, hb, 1), lambda b: (0, 0, 0, 0)))
    out_specs = [pl.BlockSpec((None, hb, HEAD_DIM), lambda b: (b, 0, 0))]
    out_shapes = [jax.ShapeDtypeStruct((DEC_BATCH, hb, HEAD_DIM), F32)]
    for g in range(N_GROUPS):
        out_specs.append(pl.BlockSpec((None, 2, hb, HEAD_DIM), lambda b: (b, 0, 0, 0)))
        out_shapes.append(jax.ShapeDtypeStruct((DEC_BATCH, 2, hb, HEAD_DIM), F32))
    return pl.pallas_call(
        _att_sample_kernel,
        grid=(DEC_BATCH,),
        in_specs=in_specs,
        out_specs=out_specs,
        out_shape=out_shapes,
        compiler_params=pltpu.CompilerParams(
            dimension_semantics=("arbitrary",), vmem_limit_bytes=VMEM_LIMIT),
        name="att_sample",
    )(proj_s, *caches, bias_s)


def kernel(x_prompt, x_sample, c_prompt, c_sample, cache_kv0, cache_kv1, cache_kv2, state_pool,
           norm_pre, norm_post, ada_w, ada_b, t5_bias, pool_w_in, pool_w_grp, pool_scale,
           pool_w_out, att_w_in, att_w_out):
    n_att = DEPTH // 2
    xp0 = x_prompt.reshape(M_PROMPT, D_MODEL)
    xs0 = jnp.zeros((SAMPLE_ROWS, D_MODEL), F32).at[0:DEC_BATCH].set(x_sample.reshape(DEC_BATCH, D_MODEL))
    c_all = jnp.zeros((C_ROWS, D_MODEL), F32)
    c_all = c_all.at[0:BATCH].set(c_prompt).at[C_SAMPLE_ROW0:C_SAMPLE_ROW0 + DEC_BATCH].set(c_sample)

    mod = _ada_all(c_all, ada_w, ada_b)
    gains_pre = norm_pre.reshape(DEPTH, 1, D_MODEL)
    gains_post = norm_post.reshape(DEPTH, 1, D_MODEL)

    bias_mats = _bias_matrices(t5_bias)
    bias_s = bias_mats[:, 0, 0:BAND + 1].reshape(N_GROUPS, N_HEADS, BAND + 1)
    bias_s = jnp.transpose(bias_s, (0, 2, 1))[..., None]
    caches = [c.reshape(n_att, DEC_BATCH, BAND, dil, 2, N_HEADS, HEAD_DIM)
              for c, dil in zip((cache_kv0, cache_kv1, cache_kv2), DILATIONS)]
    state_t = jnp.transpose(state_pool, (0, 2, 1, 3))

    kv_s = [[] for _ in range(N_GROUPS)]
    pool_p, pool_s, att_projs = [], [], []

    (h,) = _norm_step(xp0, None, mod, gains_post, gains_pre, None, 0, x_sample=xs0)
    x = None
    for i in range(DEPTH):
        li = i // 2
        if i % 2 == 0:
            proj = _matmul(h, pool_w_in, li, 2 * POOL_WIDTH, 1024, TM_MATMUL, F32, "pool_in_proj")
            a, pp, ps = _pool_mix(proj, state_t, pool_w_grp, pool_scale, li)
            pool_p.append(pp)
            pool_s.append(jnp.transpose(ps, (1, 0, 2)))
            y = _matmul(a, pool_w_out, li, D_MODEL, 512, TM_MATMUL, BF16, "pool_out_proj")
        else:
            proj = _matmul(h, att_w_in, li, ATT_IN_COLS, 1024, TM_MATMUL, F32, "att_in_proj", slabs=True)
            att_projs.append(proj)
            proj_s = jnp.transpose(proj[:, M_PROMPT:M_PROMPT + DEC_BATCH, :], (1, 0, 2))
            souts = _att_sample(proj_s, caches, bias_s, li)
            a_s = jnp.zeros((SAMPLE_ROWS, D_MODEL), F32).at[0:DEC_BATCH].set(souts[0].reshape(DEC_BATCH, D_MODEL))
            a = _att_prompt(proj, bias_mats, a_s.astype(BF16))
            for g in range(N_GROUPS):
                kv_s[g].append(souts[1 + g].reshape(DEC_BATCH, 1, 2, N_HEADS, HEAD_DIM))
            y = _matmul(a, att_w_out, li, D_MODEL, 1024, TM_MATMUL, BF16, "att_out_proj")
        last = i + 1 == DEPTH
        if i == 0:
            x, h = _norm_step(xp0, y, mod, gains_post, gains_pre, i, i + 1, x_sample=xs0)
        elif not last:
            x, h = _norm_step(x, y, mod, gains_post, gains_pre, i, i + 1)
        else:
            y_p, y_s = _norm_step(x, y, mod, gains_post, gains_pre, i, None, split_out=True)

    kv_p = [_kv_rows(att_projs[0], att_projs[1], g) for g in range(N_GROUPS)]
    y_prompt = y_p.reshape(BATCH, SEQ, D_MODEL)
    y_sample = y_s[0:DEC_BATCH].reshape(DEC_BATCH, 1, D_MODEL)
    return (y_prompt, y_sample, kv_p[0], kv_p[1], kv_p[2], jnp.stack(pool_p),
            jnp.stack(kv_s[0]), jnp.stack(kv_s[1]), jnp.stack(kv_s[2]), jnp.stack(pool_s))
```

```python
import functools

import numpy as np
import jax
import jax.numpy as jnp
from jax import lax
from jax.experimental import pallas as pl
from jax.experimental.pallas import tpu as pltpu

D_MODEL = 2048
BATCH = 4
SEQ = 2048
DEPTH = 4
DEC_BATCH = 8
HEAD_DIM = 128
N_HEADS = 16
DILATIONS = (1, 4, 16)
WINDOWS = (128, 512, 2048)
N_GROUPS = 3
QKV_WIDTH = N_GROUPS * D_MODEL
ATT_IN_COLS = 3 * QKV_WIDTH + D_MODEL
BAND = 128
ATT_SCALE = HEAD_DIM ** -0.5
LOG2E = 1.4426950408889634
POOL_WINDOWS = (2, 4, 8, 16)
POOL_WIDTH = 2 * D_MODEL
POOL_GROUP = POOL_WIDTH // 4
POOL_BUF = 15
N_BUCKETS = 32
T5_MAX_DIST = 2048
RMS_EPS = 1e-6
NEG_INF = -1e30

M_PROMPT = BATCH * SEQ
SAMPLE_ROWS = 64
M_ALL = M_PROMPT + SAMPLE_ROWS
C_ROWS = 2 * SAMPLE_ROWS
C_PROMPT_ROWS = 8
C_SAMPLE_ROW0 = SAMPLE_ROWS

TM_MATMUL = 1376
TM_MATMUL_K2048 = 2064
MXU_COLS = 256
TE_NORM = 512
TP_POOL = 1024
POOL_HALO = 24
ATT_UNITS = 16
N_BLOCKS = SEQ // BAND
TR_KV = 256
ROW_CHUNK = 256
KV_PITCH_PAD = 8
VMEM_LIMIT = 58 * 1024 * 1024

F32 = jnp.float32
BF16 = jnp.bfloat16


def _silu(x):
    half = 0.5 * x
    return half + half * jnp.tanh(half)


def _cast_rows_to_bf16(src_ref, dst_ref, rows, chunk=ROW_CHUNK):
    def body(i, c):
        r = pl.multiple_of(i * chunk, chunk)
        dst_ref[pl.ds(r, chunk), :] = src_ref[pl.ds(r, chunk), :].astype(BF16)
        return c
    lax.fori_loop(0, rows // chunk, body, 0)


def _ada_kernel(c_ref, w_ref, b_ref, o_ref):
    a = _silu(c_ref[...]).astype(BF16)
    kc = 512
    acc = jnp.zeros(o_ref.shape, F32)
    for k0 in range(0, D_MODEL, kc):
        acc = acc + jnp.dot(a[:, k0:k0 + kc], w_ref[k0:k0 + kc, :].astype(BF16),
                            preferred_element_type=F32)
    o_ref[...] = acc + b_ref[...]


def _ada_all(c_all, ada_w, ada_b):
    tn = 1024
    n = 3 * D_MODEL
    return pl.pallas_call(
        _ada_kernel,
        grid=(DEPTH, n // tn),
        in_specs=[pl.BlockSpec((C_ROWS, D_MODEL), lambda l, j: (0, 0)),
                  pl.BlockSpec((None, D_MODEL, tn), lambda l, j: (l, 0, j)),
                  pl.BlockSpec((None, 1, tn), lambda l, j: (l, 0, j))],
        out_specs=pl.BlockSpec((None, C_ROWS, tn), lambda l, j: (l, 0, j)),
        out_shape=jax.ShapeDtypeStruct((DEPTH, C_ROWS, n), F32),
        compiler_params=pltpu.CompilerParams(
            dimension_semantics=("arbitrary", "arbitrary"), vmem_limit_bytes=VMEM_LIMIT),
        name="ada_mod",
    )(c_all, ada_w, ada_b.reshape(DEPTH, 1, n))


def _t5_bucket(dist):
    dist = np.asarray(dist, dtype=np.int64)
    max_exact = N_BUCKETS // 2
    ratio = np.log(np.maximum(dist, 1) / max_exact) / np.log(T5_MAX_DIST / max_exact)
    large = np.minimum(max_exact + (ratio * (N_BUCKETS - max_exact)).astype(np.int64), N_BUCKETS - 1)
    return np.where(dist < max_exact, dist, large).astype(np.int32)


def _bucket_index_table():
    rel = np.arange(BAND)[:, None] + BAND - np.arange(2 * BAND)[None, :]
    inband = (rel >= 0) & (rel <= BAND)
    out = []
    for dil in DILATIONS:
        bucket = _t5_bucket(np.clip(rel, 0, BAND) * dil)
        out.append(np.where(inband, bucket, -1))
    return np.stack(out).astype(np.int32)


def _bias_kernel(tab_ref, idx_ref, o_ref):
    g = pl.program_id(0)
    idx = idx_ref[...]

    def head(h, c):
        acc = jnp.full(idx.shape, NEG_INF, F32)
        for b in range(N_BUCKETS):
            acc = jnp.where(idx == b, tab_ref[b, g * N_HEADS + h], acc)
        o_ref[h] = acc
        return c
    lax.fori_loop(0, N_HEADS, head, 0)


def _bias_matrices(t5_bias):
    idx = jnp.asarray(_bucket_index_table())
    n_sub = N_GROUPS * N_HEADS
    return pl.pallas_call(
        _bias_kernel,
        grid=(N_GROUPS,),
        in_specs=[pl.BlockSpec(memory_space=pltpu.SMEM),
                  pl.BlockSpec((None, BAND, 2 * BAND), lambda g: (g, 0, 0))],
        out_specs=pl.BlockSpec((N_HEADS, BAND, 2 * BAND), lambda g: (g, 0, 0)),
        out_shape=jax.ShapeDtypeStruct((n_sub, BAND, 2 * BAND), F32),
        name="t5_bias_mats",
    )(t5_bias, idx)


def _norm_kernel(*refs, has_post, has_pre, split_in, split_out, n_prompt_tiles, tiles_per_batch):
    refs = list(refs)
    xp_ref = refs.pop(0)
    xs_ref = refs.pop(0) if split_in else xp_ref
    if has_post:
        y_ref, gpost_ref, gate_p_ref, gate_s_ref = refs[:4]
        refs = refs[4:]
    if has_pre:
        gpre_ref, shift_p_ref, scale_p_ref, shift_s_ref, scale_s_ref = refs[:5]
        refs = refs[5:]
    if has_post:
        xop_ref = refs.pop(0)
        xos_ref = refs.pop(0) if split_out else xop_ref
    if has_pre:
        h_ref = refs.pop(0)

    def rms(v, g):
        return v * lax.rsqrt(jnp.mean(v * v, axis=-1, keepdims=True) + RMS_EPS) * g

    def body(x_ref, xo_ref, rows, gate, shift, scale):
        x = x_ref[rows, :]
        if has_post:
            x = x + gate * rms(y_ref[rows, :].astype(F32), gpost_ref[...])
            xo_ref[rows, :] = x
        if has_pre:
            h = rms(x, gpre_ref[...]) * (1.0 + scale) + shift
            h_ref[rows, :] = h.astype(BF16)

    t = pl.program_id(0)

    @pl.when(t < n_prompt_tiles)
    def _():
        row = pl.ds(t // tiles_per_batch, 1)
        body(xp_ref, xop_ref if has_post else None, slice(None),
             gate_p_ref[row, :] if has_post else None,
             shift_p_ref[row, :] if has_pre else None,
             scale_p_ref[row, :] if has_pre else None)

    @pl.when(t == n_prompt_tiles)
    def _():
        body(xs_ref, xos_ref if has_post else None, slice(0, SAMPLE_ROWS),
             gate_s_ref[...] if has_post else None,
             shift_s_ref[...] if has_pre else None,
             scale_s_ref[...] if has_pre else None)


def _norm_step(x, y, mod, norm_post, norm_pre, post_layer, pre_layer, x_sample=None, split_out=False):
    has_post = post_layer is not None
    has_pre = pre_layer is not None
    split_in = x_sample is not None
    te = TE_NORM
    npt = M_PROMPT // te
    tiles_per_batch = SEQ // te
    row_spec = pl.BlockSpec((te, D_MODEL), lambda t: (t, 0))
    prompt_row_spec = pl.BlockSpec((te, D_MODEL), lambda t: (jnp.minimum(t, npt - 1), 0))
    sample_row_spec = pl.BlockSpec((SAMPLE_ROWS, D_MODEL), lambda t: (0, 0))

    def mod_p_spec(layer, part):
        return pl.BlockSpec((None, C_PROMPT_ROWS, D_MODEL), lambda t: (layer, 0, part))

    def mod_s_spec(layer, part):
        return pl.BlockSpec((None, SAMPLE_ROWS, D_MODEL),
                            lambda t: (layer, C_SAMPLE_ROW0 // SAMPLE_ROWS, part))

    def gain_spec(layer):
        return pl.BlockSpec((None, 1, D_MODEL), lambda t: (layer, 0, 0))

    if split_in:
        args, in_specs = [x, x_sample], [prompt_row_spec, sample_row_spec]
    else:
        args, in_specs = [x], [row_spec]
    out_shapes, out_specs = [], []
    if has_post:
        args += [y, norm_post, mod, mod]
        in_specs += [row_spec, gain_spec(post_layer), mod_p_spec(post_layer, 2), mod_s_spec(post_layer, 2)]
        if split_out:
            out_shapes += [jax.ShapeDtypeStruct((M_PROMPT, D_MODEL), F32),
                           jax.ShapeDtypeStruct((SAMPLE_ROWS, D_MODEL), F32)]
            out_specs += [prompt_row_spec, sample_row_spec]
        else:
            out_shapes.append(jax.ShapeDtypeStruct((M_ALL, D_MODEL), F32))
            out_specs.append(row_spec)
    if has_pre:
        args += [norm_pre, mod, mod, mod, mod]
        in_specs += [gain_spec(pre_layer), mod_p_spec(pre_layer, 0), mod_p_spec(pre_layer, 1),
                     mod_s_spec(pre_layer, 0), mod_s_spec(pre_layer, 1)]
        out_shapes.append(jax.ShapeDtypeStruct((M_ALL, D_MODEL), BF16))
        out_specs.append(row_spec)
    outs = pl.pallas_call(
        functools.partial(_norm_kernel, has_post=has_post, has_pre=has_pre, split_in=split_in,
                          split_out=split_out, n_prompt_tiles=npt, tiles_per_batch=tiles_per_batch),
        grid=(npt + 1,),
        in_specs=in_specs,
        out_specs=out_specs,
        out_shape=out_shapes,
        compiler_params=pltpu.CompilerParams(
            dimension_semantics=("arbitrary",), vmem_limit_bytes=VMEM_LIMIT),
        name="norm_step",
    )(*args)
    return outs


def _mm_kernel(a_ref, w_ref, o_ref, wb_ref, *, k_rows, slabs):
    @pl.when(pl.program_id(1) == 0)
    def _():
        _cast_rows_to_bf16(w_ref, wb_ref, k_rows)

    tn = wb_ref.shape[1]
    for c0 in range(0, tn, MXU_COLS):
        acc = jnp.dot(a_ref[...], wb_ref[:, c0:c0 + MXU_COLS], preferred_element_type=F32).astype(o_ref.dtype)
        if slabs:
            for c in range(MXU_COLS // HEAD_DIM):
                o_ref[c0 // HEAD_DIM + c] = acc[:, c * HEAD_DIM:(c + 1) * HEAD_DIM]
        else:
            o_ref[:, c0:c0 + MXU_COLS] = acc


def _matmul(a, w, layer, n_out, tn, tm, out_dtype, name, slabs=False):
    m, k = a.shape
    if slabs:
        out_spec = pl.BlockSpec((tn // HEAD_DIM, tm, HEAD_DIM), lambda j, i: (j, i, 0))
        out_shape = jax.ShapeDtypeStruct((n_out // HEAD_DIM, m, HEAD_DIM), out_dtype)
    else:
        out_spec = pl.BlockSpec((tm, tn), lambda j, i: (i, j))
        out_shape = jax.ShapeDtypeStruct((m, n_out), out_dtype)
    return pl.pallas_call(
        functools.partial(_mm_kernel, k_rows=k, slabs=slabs),
        grid=(n_out // tn, m // tm),
        in_specs=[pl.BlockSpec((tm, k), lambda j, i: (i, 0)),
                  pl.BlockSpec((None, k, tn), lambda j, i: (layer, 0, j))],
        out_specs=out_spec,
        out_shape=out_shape,
        scratch_shapes=[pltpu.VMEM((k, tn), BF16)],
        compiler_params=pltpu.CompilerParams(
            dimension_semantics=("arbitrary", "arbitrary"), vmem_limit_bytes=VMEM_LIMIT),
        name=name,
    )(a, w)


def _pool_kernel(u_ref, halo_ref, z_ref, st_ref, w_ref, sc_ref,
                 a_ref, pp_ref, ps_ref, wb_ref, buf_a, buf_b, *, n_prompt_tiles, tiles_per_batch):
    g = pl.program_id(0)
    t = pl.program_id(1)
    tp = TP_POOL
    h0 = POOL_HALO

    @pl.when(t == 0)
    def _():
        _cast_rows_to_bf16(w_ref, wb_ref, POOL_GROUP)

    def finish(r, z, rows):
        y = jnp.dot(r.astype(BF16), wb_ref[...], preferred_element_type=F32) * sc_ref[...]
        a_ref[rows, :] = (y * _silu(z)).astype(BF16)

    def prompt_tile(n_steps):
        w = 2 ** n_steps
        first = (t % tiles_per_batch) == 0
        buf_a[0:8, :] = jnp.zeros((8, POOL_GROUP), F32)
        buf_b[0:8, :] = jnp.zeros((8, POOL_GROUP), F32)
        buf_a[8:h0, :] = jnp.where(first, 0.0, halo_ref[...])
        buf_a[h0:h0 + tp, :] = u_ref[...]
        src, dst = buf_a, buf_b
        n = tp + h0 - 8
        for s in range(n_steps):
            sh = 2 ** s
            dst[8:8 + n, :] = src[8:8 + n, :] + src[8 - sh:8 - sh + n, :]
            src, dst = dst, src
        pos = (t % tiles_per_batch) * tp + lax.broadcasted_iota(jnp.int32, (tp, 1), 0)
        inv_cnt = 1.0 / jnp.minimum(pos + 1, w).astype(F32)
        u = u_ref[...]
        r = src[h0:h0 + tp, :] * inv_cnt - u
        finish(r, z_ref[...], slice(None))

        @pl.when((t % tiles_per_batch) == tiles_per_batch - 1)
        def _():
            pp_ref[...] = u_ref[tp - POOL_BUF:tp, :]

    def sample_tile(n_steps):
        w = 2 ** n_steps
        u_new = u_ref[0:DEC_BATCH, :]
        acc = u_new
        for k in range(1, w):
            acc = acc + st_ref[POOL_BUF - k]
        r = acc / float(w) - u_new
        buf_a[0:DEC_BATCH, :] = r
        buf_a[DEC_BATCH:SAMPLE_ROWS, :] = jnp.zeros((SAMPLE_ROWS - DEC_BATCH, POOL_GROUP), F32)
        finish(buf_a[0:SAMPLE_ROWS, :], z_ref[0:SAMPLE_ROWS, :], slice(0, SAMPLE_ROWS))
        for k in range(POOL_BUF - 1):
            ps_ref[k] = st_ref[k + 1]
        ps_ref[POOL_BUF - 1] = u_new

    for gi in range(len(POOL_WINDOWS)):
        @pl.when((g == gi) & (t < n_prompt_tiles))
        def _(gi=gi):
            prompt_tile(gi + 1)

        @pl.when((g == gi) & (t == n_prompt_tiles))
        def _(gi=gi):
            sample_tile(gi + 1)


def _pool_mix(proj, state_t, w_grp, scale, layer):
    tp = TP_POOL
    npt = M_PROMPT // tp
    tpb = SEQ // tp
    ng = len(POOL_WINDOWS)
    halo_blocks = tp // 16
    outs = pl.pallas_call(
        functools.partial(_pool_kernel, n_prompt_tiles=npt, tiles_per_batch=tpb),
        grid=(ng, npt + 1),
        in_specs=[
            pl.BlockSpec((tp, POOL_GROUP), lambda g, t: (t, g)),
            pl.BlockSpec((16, POOL_GROUP), lambda g, t: (jnp.maximum(t * halo_blocks - 1, 0), g)),
            pl.BlockSpec((tp, POOL_GROUP), lambda g, t: (t, ng + g)),
            pl.BlockSpec((None, POOL_BUF, DEC_BATCH, POOL_GROUP), lambda g, t: (layer, 0, 0, g)),
            pl.BlockSpec((None, None, POOL_GROUP, POOL_GROUP), lambda g, t: (layer, g, 0, 0)),
            pl.BlockSpec((None, 1, POOL_GROUP), lambda g, t: (layer, 0, g)),
        ],
        out_specs=[
            pl.BlockSpec((tp, POOL_GROUP), lambda g, t: (t, g)),
            pl.BlockSpec((None, POOL_BUF, POOL_GROUP),
                         lambda g, t: (jnp.minimum(t // tpb, BATCH - 1), 0, g)),
            pl.BlockSpec((POOL_BUF, DEC_BATCH, POOL_GROUP), lambda g, t: (0, 0, g)),
        ],
        out_shape=[
            jax.ShapeDtypeStruct((M_ALL, POOL_WIDTH), BF16),
            jax.ShapeDtypeStruct((BATCH, POOL_BUF, POOL_WIDTH), F32),
            jax.ShapeDtypeStruct((POOL_BUF, DEC_BATCH, POOL_WIDTH), F32),
        ],
        scratch_shapes=[pltpu.VMEM((POOL_GROUP, POOL_GROUP), BF16),
                        pltpu.VMEM((tp + POOL_HALO, POOL_GROUP), F32),
                        pltpu.VMEM((tp + POOL_HALO, POOL_GROUP), F32)],
        compiler_params=pltpu.CompilerParams(
            dimension_semantics=("arbitrary", "arbitrary"), vmem_limit_bytes=VMEM_LIMIT),
        name="pool_mix",
    )(proj, proj, proj, state_t, w_grp, scale.reshape(-1, 1, POOL_WIDTH))
    return outs


def _att_prompt_kernel(*refs):
    n_in = 13
    as_ref, a_ref = refs[n_in], refs[n_in + 1]

    @pl.when(pl.program_id(0) < BATCH)
    def _():
        _att_prompt_tile(*refs[:n_in], *refs[n_in + 1:])

    @pl.when(pl.program_id(0) == BATCH)
    def _():
        a_ref[0:SAMPLE_ROWS, :] = as_ref[...]


def _att_prompt_tile(q0, k0, v0, q1, k1, v1, q2, k2, v2, z_ref, b0, b1, b2,
                     a_ref, qd, kd, vd, od, ld, on, ln, tmp3):
    tmp = tmp3.at[0]
    q_refs, k_refs, v_refs = (q0, q1, q2), (k0, k1, k2), (v0, v1, v2)
    bias_refs = (b0, b1, b2)
    nu = ATT_UNITS

    for g, dil in enumerate(DILATIONS):
        n = SEQ // dil
        nb = n // BAND
        zero_blk = jnp.zeros((BAND, HEAD_DIM), BF16)
        kd[g, 0:BAND, :] = zero_blk
        vd[g, 0:BAND, :] = zero_blk
        for src, dst, off, mul in ((q_refs[g], qd, 0, ATT_SCALE * LOG2E), (k_refs[g], kd, BAND, None),
                                   (v_refs[g], vd, BAND, None)):
            def to_bf16(x, mul=mul):
                return (x if mul is None else x * mul).astype(BF16)

            if dil == 1:
                def cp(i, c, src=src, dst=dst, g=g, off=off, to_bf16=to_bf16):
                    r = pl.multiple_of(i * ROW_CHUNK, ROW_CHUNK)
                    dst[g, pl.ds(off + r, ROW_CHUNK), :] = to_bf16(src[pl.ds(r, ROW_CHUNK), :])
                    return c
                lax.fori_loop(0, SEQ // ROW_CHUNK, cp, 0)
            elif dil == 4:
                for r in range(dil):
                    dst[g, off + r * n:off + (r + 1) * n, :] = to_bf16(src[pl.ds(r, n, stride=dil), :])
            else:
                stage = tmp3.at[(off > 0) + (dst is vd)]
                quarter = SEQ // 4
                for r_lo in range(4):
                    stage[r_lo * quarter:(r_lo + 1) * quarter, :] = src[pl.ds(r_lo, quarter, stride=4), :]
                for r in range(dil):
                    r_lo, r_hi = r % 4, r // 4
                    dst[g, off + r * n:off + (r + 1) * n, :] = to_bf16(
                        stage[pl.ds(r_lo * quarter + r_hi, n, stride=4), :])

    def unit_batch(g, u0, seq_blocks):
        with_prev = seq_blocks > 1
        rows = nu * BAND
        row = pl.multiple_of(u0 * BAND, rows)
        shape3 = (nu, BAND, HEAD_DIM)
        q = qd[g, pl.ds(row, rows), :].reshape(shape3)
        kc = kd[g, pl.ds(row + BAND, rows), :].reshape(shape3)
        vc = vd[g, pl.ds(row + BAND, rows), :].reshape(shape3)
        bias = bias_refs[g]
        s_c = jnp.einsum('uqe,uke->uqk', q, kc, preferred_element_type=F32) \
            + (bias[:, BAND:2 * BAND] * LOG2E)[None]
        if with_prev:
            kp = kd[g, pl.ds(row, rows), :].reshape(shape3)
            vp = vd[g, pl.ds(row, rows), :].reshape(shape3)
            bias_p = jnp.broadcast_to((bias[:, 0:BAND] * LOG2E)[None], (nu, BAND, BAND))
            blk = u0 + lax.broadcasted_iota(jnp.int32, (nu, BAND, BAND), 0)
            bias_p = jnp.where((blk & (seq_blocks - 1)) == 0, NEG_INF, bias_p)
            s_p = jnp.einsum('uqe,uke->uqk', q, kp, preferred_element_type=F32) + bias_p
            m = jnp.max(jnp.maximum(s_c, s_p), axis=-1, keepdims=True)
            p_c = jnp.exp2(s_c - m)
            p_p = jnp.exp2(s_p - m)
            l = jnp.sum(p_c + p_p, axis=-1, keepdims=True)
            acc = jnp.einsum('uqk,uke->uqe', p_c.astype(BF16), vc, preferred_element_type=F32) \
                + jnp.einsum('uqk,uke->uqe', p_p.astype(BF16), vp, preferred_element_type=F32)
        else:
            m = jnp.max(s_c, axis=-1, keepdims=True)
            p_c = jnp.exp2(s_c - m)
            l = jnp.sum(p_c, axis=-1, keepdims=True)
            acc = jnp.einsum('uqk,uke->uqe', p_c.astype(BF16), vc, preferred_element_type=F32)
        od[g, pl.ds(row, rows), :] = (acc / l).reshape(rows, HEAD_DIM)
        ld[g, pl.ds(row, rows), :] = jnp.broadcast_to(m + jnp.log(l) * LOG2E, shape3).reshape(rows, HEAD_DIM)

    for g, dil in enumerate(DILATIONS):
        nb = (SEQ // dil) // BAND

        def batch_body(i, c, g=g, nb=nb):
            unit_batch(g, i * nu, nb)
            return c
        lax.fori_loop(0, N_BLOCKS // nu, batch_body, 0)

    def interleave4(src, dst, n_rows):
        quarter = n_rows // 4
        for r in range(4):
            dst[pl.ds(r, quarter, stride=4), :] = src[r * quarter:(r + 1) * quarter, :]

    for src_all, dst_all in ((od, on), (ld, ln)):
        interleave4(src_all.at[1], dst_all.at[0], SEQ)
        for r_lo in range(4):
            for r_hi in range(4):
                r = r_lo + 4 * r_hi
                tmp[pl.ds(r_lo * (SEQ // 4) + r_hi, BAND, stride=4), :] = src_all[2, r * BAND:(r + 1) * BAND, :]
        interleave4(tmp, dst_all.at[1], SEQ)

    def comb(i, c):
        rs = pl.ds(pl.multiple_of(i * ROW_CHUNK, ROW_CHUNK), ROW_CHUNK)
        l0, l1, l2 = ld[0, rs, :], ln[0, rs, :], ln[1, rs, :]
        mx = jnp.maximum(jnp.maximum(l0, l1), l2)
        w0, w1, w2 = jnp.exp2(l0 - mx), jnp.exp2(l1 - mx), jnp.exp2(l2 - mx)
        o = (w0 * od[0, rs, :] + w1 * on[0, rs, :] + w2 * on[1, rs, :]) / (w0 + w1 + w2)
        a_ref[rs, :] = (o * _silu(z_ref[rs, :])).astype(BF16)
        return c
    lax.fori_loop(0, SEQ // ROW_CHUNK, comb, 0)


def _att_prompt(proj, bias_mats, a_sample):
    hb = N_HEADS

    def head_blk(b, h):
        return jnp.where(b < BATCH, h, hb - 1)

    in_specs = []
    for g in range(N_GROUPS):
        for part in range(3):
            off = part * N_GROUPS * hb + g * hb
            in_specs.append(pl.BlockSpec(
                (None, SEQ, HEAD_DIM),
                lambda b, h, off=off: (off + head_blk(b, h), jnp.minimum(b, BATCH - 1), 0)))
    in_specs.append(pl.BlockSpec((None, SEQ, HEAD_DIM),
                                 lambda b, h: (9 * hb + head_blk(b, h), jnp.minimum(b, BATCH - 1), 0)))
    for g in range(N_GROUPS):
        in_specs.append(pl.BlockSpec((None, BAND, 2 * BAND), lambda b, h, g=g: (g * hb + head_blk(b, h), 0, 0)))
    in_specs.append(pl.BlockSpec((SAMPLE_ROWS, HEAD_DIM), lambda b, h: (0, h)))
    scratch = [pltpu.VMEM((N_GROUPS, SEQ, HEAD_DIM), BF16)] + \
              [pltpu.VMEM((N_GROUPS, SEQ + BAND, HEAD_DIM), BF16)] * 2 + \
              [pltpu.VMEM((N_GROUPS, SEQ, HEAD_DIM), F32)] * 2 + \
              [pltpu.VMEM((N_GROUPS - 1, SEQ, HEAD_DIM), F32)] * 2 + \
              [pltpu.VMEM((3, SEQ, HEAD_DIM), F32)]
    return pl.pallas_call(
        _att_prompt_kernel,
        grid=(BATCH + 1, N_HEADS),
        in_specs=in_specs,
        out_specs=pl.BlockSpec((SEQ, HEAD_DIM), lambda b, h: (b, h)),
        out_shape=jax.ShapeDtypeStruct((M_ALL, D_MODEL), BF16),
        scratch_shapes=scratch,
        compiler_params=pltpu.CompilerParams(
            dimension_semantics=("arbitrary", "arbitrary"), vmem_limit_bytes=VMEM_LIMIT),
        name="att_prompt",
    )(*([proj] * 10), bias_mats, bias_mats, bias_mats, a_sample)


def _kv_rows_kernel(k_a, v_a, k_b, v_b, o_ref, *, tr):
    pitch = tr + KV_PITCH_PAD

    def emit(srcs):
        flats = [src.reshape(N_HEADS * pitch, HEAD_DIM) for src in srcs]

        def body(i, c):
            t0 = i * 8
            for k in range(8):
                for part, flat in enumerate(flats):
                    o_ref[t0 + k, part] = flat[pl.ds(t0 + k, N_HEADS, stride=pitch), :]
            return c
        lax.fori_loop(0, tr // 8, body, 0)

    @pl.when(pl.program_id(0) == 0)
    def _():
        emit((k_a, v_a))

    @pl.when(pl.program_id(0) == 1)
    def _():
        emit((k_b, v_b))


def _kv_rows(proj_a, proj_b, g):
    keep = WINDOWS[g]
    tr = min(keep, TR_KV)
    nt = keep // tr
    first_blk = (SEQ - keep) // tr
    blks_per_batch = SEQ // tr

    def row_blk(b, t):
        return b * blks_per_batch + first_blk + t

    def spec(layer, part):
        parked = row_blk(BATCH - 1, nt - 1) if layer == 0 else row_blk(0, 0)
        col0 = ((1 + part) * N_GROUPS + g) * N_HEADS
        return pl.BlockSpec((pl.Element(N_HEADS), pl.Element(tr + KV_PITCH_PAD), pl.Element(HEAD_DIM)),
                            lambda l, b, t: (col0, jnp.where(l == layer, row_blk(b, t), parked) * tr, 0))

    return pl.pallas_call(
        functools.partial(_kv_rows_kernel, tr=tr),
        grid=(2, BATCH, nt),
        in_specs=[spec(0, 0), spec(0, 1), spec(1, 0), spec(1, 1)],
        out_specs=pl.BlockSpec((None, None, tr, 2, N_HEADS, HEAD_DIM), lambda l, b, t: (l, b, t, 0, 0, 0)),
        out_shape=jax.ShapeDtypeStruct((2, BATCH, keep, 2, N_HEADS, HEAD_DIM), F32),
        compiler_params=pltpu.CompilerParams(
            dimension_semantics=("arbitrary",) * 3, vmem_limit_bytes=VMEM_LIMIT),
        name="kv_rows",
    )(proj_a, proj_a, proj_b, proj_b)


def _att_sample_kernel(x_ref, c0, c1, c2, bias_ref, a_ref, kv0, kv1, kv2):
    caches = (c0, c1, c2)
    kv_outs = (kv0, kv1, kv2)
    hb = N_HEADS
    outs, lses = [], []
    for g in range(N_GROUPS):
        q = x_ref[g * hb:(g + 1) * hb, :]
        kn = x_ref[(3 + g) * hb:(4 + g) * hb, :]
        vn = x_ref[(6 + g) * hb:(7 + g) * hb, :]
        kv_outs[g][0] = kn
        kv_outs[g][1] = vn
        kc = caches[g][:, 0]
        vc = caches[g][:, 1]
        s_c = jnp.sum(kc * q[None], axis=-1, keepdims=True) * ATT_SCALE + bias_ref[g, 0:BAND]
        s_n = jnp.sum(kn * q, axis=-1, keepdims=True) * ATT_SCALE + bias_ref[g, BAND]
        m = jnp.maximum(jnp.max(s_c, axis=0), s_n)
        p_c = jnp.exp(s_c - m[None])
        p_n = jnp.exp(s_n - m)
        l = jnp.sum(p_c, axis=0) + p_n
        o = (jnp.sum(p_c * vc, axis=0) + p_n * vn) / l
        outs.append(o)
        lses.append(m + jnp.log(l))
    mx = jnp.maximum(jnp.maximum(lses[0], lses[1]), lses[2])
    ws = [jnp.exp(ls - mx) for ls in lses]
    o = (ws[0] * outs[0] + ws[1] * outs[1] + ws[2] * outs[2]) / (ws[0] + ws[1] + ws[2])
    z = x_ref[9 * hb:10 * hb, :]
    a_ref[...] = o * _silu(z)


def _att_sample(proj_s, caches, bias_s, layer):
    hb = N_HEADS
    in_specs = [pl.BlockSpec((None, ATT_IN_COLS // HEAD_DIM, HEAD_DIM), lambda b: (b, 0, 0))]
    for g in range(N_GROUPS):
        in_specs.append(pl.BlockSpec((None, None, BAND, None, 2, hb, HEAD_DIM),
                                     lambda b: (layer, b, 0, 0, 0, 0, 0)))
    in_specs.append(pl.BlockSpec((N_GROUPS, BAND + 1, hb, 1), lambda b: (0, 0, 0, 0)))
    out_specs = [pl.BlockSpec((None, hb, HEAD_DIM), lambda b: (b, 0, 0))]
    out_shapes = [jax.ShapeDtypeStruct((DEC_BATCH, hb, HEAD_DIM), F32)]
    for g in range(N_GROUPS):
        out_specs.append(pl.BlockSpec((None, 2, hb, HEAD_DIM), lambda b: (b, 0, 0, 0)))
        out_shapes.append(jax.ShapeDtypeStruct((DEC_BATCH, 2, hb, HEAD_DIM), F32))
    return pl.pallas_call(
        _att_sample_kernel,
        grid=(DEC_BATCH,),
        in_specs=in_specs,
        out_specs=out_specs,
        out_shape=out_shapes,
        compiler_params=pltpu.CompilerParams(
            dimension_semantics=("arbitrary",), vmem_limit_bytes=VMEM_LIMIT),
        name="att_sample",
    )(proj_s, *caches, bias_s)


def kernel(x_prompt, x_sample, c_prompt, c_sample, cache_kv0, cache_kv1, cache_kv2, state_pool,
           norm_pre, norm_post, ada_w, ada_b, t5_bias, pool_w_in, pool_w_grp, pool_scale,
           pool_w_out, att_w_in, att_w_out):
    n_att = DEPTH // 2
    xp0 = x_prompt.reshape(M_PROMPT, D_MODEL)
    xs0 = jnp.zeros((SAMPLE_ROWS, D_MODEL), F32).at[0:DEC_BATCH].set(x_sample.reshape(DEC_BATCH, D_MODEL))
    c_all = jnp.zeros((C_ROWS, D_MODEL), F32)
    c_all = c_all.at[0:BATCH].set(c_prompt).at[C_SAMPLE_ROW0:C_SAMPLE_ROW0 + DEC_BATCH].set(c_sample)

    mod = _ada_all(c_all, ada_w, ada_b)
    gains_pre = norm_pre.reshape(DEPTH, 1, D_MODEL)
    gains_post = norm_post.reshape(DEPTH, 1, D_MODEL)

    bias_mats = _bias_matrices(t5_bias)
    bias_s = bias_mats[:, 0, 0:BAND + 1].reshape(N_GROUPS, N_HEADS, BAND + 1)
    bias_s = jnp.transpose(bias_s, (0, 2, 1))[..., None]
    caches = [c.reshape(n_att, DEC_BATCH, BAND, dil, 2, N_HEADS, HEAD_DIM)
              for c, dil in zip((cache_kv0, cache_kv1, cache_kv2), DILATIONS)]
    state_t = jnp.transpose(state_pool, (0, 2, 1, 3))

    kv_s = [[] for _ in range(N_GROUPS)]
    pool_p, pool_s, att_projs = [], [], []

    (h,) = _norm_step(xp0, None, mod, gains_post, gains_pre, None, 0, x_sample=xs0)
    x = None
    for i in range(DEPTH):
        li = i // 2
        if i % 2 == 0:
            proj = _matmul(h, pool_w_in, li, 2 * POOL_WIDTH, 1024, TM_MATMUL_K2048, F32, "pool_in_proj")
            a, pp, ps = _pool_mix(proj, state_t, pool_w_grp, pool_scale, li)
            pool_p.append(pp)
            pool_s.append(jnp.transpose(ps, (1, 0, 2)))
            y = _matmul(a, pool_w_out, li, D_MODEL, 512, TM_MATMUL, BF16, "pool_out_proj")
        else:
            proj = _matmul(h, att_w_in, li, ATT_IN_COLS, 1024, TM_MATMUL_K2048, F32, "att_in_proj", slabs=True)
            att_projs.append(proj)
            proj_s = jnp.transpose(proj[:, M_PROMPT:M_PROMPT + DEC_BATCH, :], (1, 0, 2))
            souts = _att_sample(proj_s, caches, bias_s, li)
            a_s = jnp.zeros((SAMPLE_ROWS, D_MODEL), F32).at[0:DEC_BATCH].set(souts[0].reshape(DEC_BATCH, D_MODEL))
            a = _att_prompt(proj, bias_mats, a_s.astype(BF16))
            for g in range(N_GROUPS):
                kv_s[g].append(souts[1 + g].reshape(DEC_BATCH, 1, 2, N_HEADS, HEAD_DIM))
            y = _matmul(a, att_w_out, li, D_MODEL, 1024, TM_MATMUL_K2048, BF16, "att_out_proj")
        last = i + 1 == DEPTH
        if i == 0:
            x, h = _norm_step(xp0, y, mod, gains_post, gains_pre, i, i + 1, x_sample=xs0)
        elif not last:
            x, h = _norm_step(x, y, mod, gains_post, gains_pre, i, i + 1)
        else:
            y_p, y_s = _norm_step(x, y, mod, gains_post, gains_pre, i, None, split_out=True)

    kv_p = [_kv_rows(att_projs[0], att_projs[1], g) for g in range(N_GROUPS)]
    y_prompt = y_p.reshape(BATCH, SEQ, D_MODEL)
    y_sample = y_s[0:DEC_BATCH].reshape(DEC_BATCH, 1, D_MODEL)
    return (y_prompt, y_sample, kv_p[0], kv_p[1], kv_p[2], jnp.stack(pool_p),
            jnp.stack(kv_s[0]), jnp.stack(kv_s[1]), jnp.stack(kv_s[2]), jnp.stack(pool_s))
```

```python
import functools

import numpy as np
import jax
import jax.numpy as jnp
from jax import lax
from jax.experimental import pallas as pl
from jax.experimental.pallas import tpu as pltpu

D_MODEL = 2048
BATCH = 4
SEQ = 2048
DEPTH = 4
DEC_BATCH = 8
HEAD_DIM = 128
N_HEADS = 16
DILATIONS = (1, 4, 16)
WINDOWS = (128, 512, 2048)
N_GROUPS = 3
QKV_WIDTH = N_GROUPS * D_MODEL
ATT_IN_COLS = 3 * QKV_WIDTH + D_MODEL
BAND = 128
ATT_SCALE = HEAD_DIM ** -0.5
LOG2E = 1.4426950408889634
POOL_WINDOWS = (2, 4, 8, 16)
POOL_WIDTH = 2 * D_MODEL
POOL_GROUP = POOL_WIDTH // 4
POOL_BUF = 15
N_BUCKETS = 32
T5_MAX_DIST = 2048
RMS_EPS = 1e-6
NEG_INF = -1e30

M_PROMPT = BATCH * SEQ
SAMPLE_ROWS = 64
M_ALL = M_PROMPT + SAMPLE_ROWS
C_ROWS = 2 * SAMPLE_ROWS
C_PROMPT_ROWS = 8
C_SAMPLE_ROW0 = SAMPLE_ROWS

TM_MATMUL = 1376
TM_MATMUL_K2048 = 2064
MXU_COLS = 256
TE_NORM = 512
TP_POOL = 1024
POOL_HALO = 24
ATT_UNITS = 16
N_BLOCKS = SEQ // BAND
TR_KV = 256
ROW_CHUNK = 256
KV_PITCH_PAD = 8
VMEM_LIMIT = 58 * 1024 * 1024

F32 = jnp.float32
BF16 = jnp.bfloat16


def _silu(x):
    half = 0.5 * x
    return half + half * jnp.tanh(half)


def _cast_rows_to_bf16(src_ref, dst_ref, rows, chunk=ROW_CHUNK):
    def body(i, c):
        r = pl.multiple_of(i * chunk, chunk)
        dst_ref[pl.ds(r, chunk), :] = src_ref[pl.ds(r, chunk), :].astype(BF16)
        return c
    lax.fori_loop(0, rows // chunk, body, 0)


def _ada_kernel(c_ref, w_ref, b_ref, o_ref):
    a = _silu(c_ref[...]).astype(BF16)
    kc = 512
    acc = jnp.zeros(o_ref.shape, F32)
    for k0 in range(0, D_MODEL, kc):
        acc = acc + jnp.dot(a[:, k0:k0 + kc], w_ref[k0:k0 + kc, :].astype(BF16),
                            preferred_element_type=F32)
    o_ref[...] = acc + b_ref[...]


def _ada_all(c_all, ada_w, ada_b):
    tn = 1024
    n = 3 * D_MODEL
    return pl.pallas_call(
        _ada_kernel,
        grid=(DEPTH, n // tn),
        in_specs=[pl.BlockSpec((C_ROWS, D_MODEL), lambda l, j: (0, 0)),
                  pl.BlockSpec((None, D_MODEL, tn), lambda l, j: (l, 0, j)),
                  pl.BlockSpec((None, 1, tn), lambda l, j: (l, 0, j))],
        out_specs=pl.BlockSpec((None, C_ROWS, tn), lambda l, j: (l, 0, j)),
        out_shape=jax.ShapeDtypeStruct((DEPTH, C_ROWS, n), F32),
        compiler_params=pltpu.CompilerParams(
            dimension_semantics=("arbitrary", "arbitrary"), vmem_limit_bytes=VMEM_LIMIT),
        name="ada_mod",
    )(c_all, ada_w, ada_b.reshape(DEPTH, 1, n))


def _t5_bucket(dist):
    dist = np.asarray(dist, dtype=np.int64)
    max_exact = N_BUCKETS // 2
    ratio = np.log(np.maximum(dist, 1) / max_exact) / np.log(T5_MAX_DIST / max_exact)
    large = np.minimum(max_exact + (ratio * (N_BUCKETS - max_exact)).astype(np.int64), N_BUCKETS - 1)
    return np.where(dist < max_exact, dist, large).astype(np.int32)


def _bucket_index_table():
    rel = np.arange(BAND)[:, None] + BAND - np.arange(2 * BAND)[None, :]
    inband = (rel >= 0) & (rel <= BAND)
    out = []
    for dil in DILATIONS:
        bucket = _t5_bucket(np.clip(rel, 0, BAND) * dil)
        out.append(np.where(inband, bucket, -1))
    return np.stack(out).astype(np.int32)


def _bias_kernel(tab_ref, idx_ref, o_ref):
    g = pl.program_id(0)
    idx = idx_ref[...]

    def head(h, c):
        acc = jnp.full(idx.shape, NEG_INF, F32)
        for b in range(N_BUCKETS):
            acc = jnp.where(idx == b, tab_ref[b, g * N_HEADS + h], acc)
        o_ref[h] = acc
        return c
    lax.fori_loop(0, N_HEADS, head, 0)


def _bias_matrices(t5_bias):
    idx = jnp.asarray(_bucket_index_table())
    n_sub = N_GROUPS * N_HEADS
    return pl.pallas_call(
        _bias_kernel,
        grid=(N_GROUPS,),
        in_specs=[pl.BlockSpec(memory_space=pltpu.SMEM),
                  pl.BlockSpec((None, BAND, 2 * BAND), lambda g: (g, 0, 0))],
        out_specs=pl.BlockSpec((N_HEADS, BAND, 2 * BAND), lambda g: (g, 0, 0)),
        out_shape=jax.ShapeDtypeStruct((n_sub, BAND, 2 * BAND), F32),
        name="t5_bias_mats",
    )(t5_bias, idx)


def _norm_kernel(*refs, has_post, has_pre, split_in, split_out, n_prompt_tiles, tiles_per_batch):
    refs = list(refs)
    xp_ref = refs.pop(0)
    xs_ref = refs.pop(0) if split_in else xp_ref
    if has_post:
        y_ref, gpost_ref, gate_p_ref, gate_s_ref = refs[:4]
        refs = refs[4:]
    if has_pre:
        gpre_ref, shift_p_ref, scale_p_ref, shift_s_ref, scale_s_ref = refs[:5]
        refs = refs[5:]
    if has_post:
        xop_ref = refs.pop(0)
        xos_ref = refs.pop(0) if split_out else xop_ref
    if has_pre:
        h_ref = refs.pop(0)

    def rms(v, g):
        return v * lax.rsqrt(jnp.mean(v * v, axis=-1, keepdims=True) + RMS_EPS) * g

    def body(x_ref, xo_ref, rows, gate, shift, scale):
        x = x_ref[rows, :]
        if has_post:
            x = x + gate * rms(y_ref[rows, :].astype(F32), gpost_ref[...])
            xo_ref[rows, :] = x
        if has_pre:
            h = rms(x, gpre_ref[...]) * (1.0 + scale) + shift
            h_ref[rows, :] = h.astype(BF16)

    t = pl.program_id(0)

    @pl.when(t < n_prompt_tiles)
    def _():
        row = pl.ds(t // tiles_per_batch, 1)
        body(xp_ref, xop_ref if has_post else None, slice(None),
             gate_p_ref[row, :] if has_post else None,
             shift_p_ref[row, :] if has_pre else None,
             scale_p_ref[row, :] if has_pre else None)

    @pl.when(t == n_prompt_tiles)
    def _():
        body(xs_ref, xos_ref if has_post else None, slice(0, SAMPLE_ROWS),
             gate_s_ref[...] if has_post else None,
             shift_s_ref[...] if has_pre else None,
             scale_s_ref[...] if has_pre else None)


def _norm_step(x, y, mod, norm_post, norm_pre, post_layer, pre_layer, x_sample=None, split_out=False):
    has_post = post_layer is not None
    has_pre = pre_layer is not None
    split_in = x_sample is not None
    te = TE_NORM
    npt = M_PROMPT // te
    tiles_per_batch = SEQ // te
    row_spec = pl.BlockSpec((te, D_MODEL), lambda t: (t, 0))
    prompt_row_spec = pl.BlockSpec((te, D_MODEL), lambda t: (jnp.minimum(t, npt - 1), 0))
    sample_row_spec = pl.BlockSpec((SAMPLE_ROWS, D_MODEL), lambda t: (0, 0))

    def mod_p_spec(layer, part):
        return pl.BlockSpec((None, C_PROMPT_ROWS, D_MODEL), lambda t: (layer, 0, part))

    def mod_s_spec(layer, part):
        return pl.BlockSpec((None, SAMPLE_ROWS, D_MODEL),
                            lambda t: (layer, C_SAMPLE_ROW0 // SAMPLE_ROWS, part))

    def gain_spec(layer):
        return pl.BlockSpec((None, 1, D_MODEL), lambda t: (layer, 0, 0))

    if split_in:
        args, in_specs = [x, x_sample], [prompt_row_spec, sample_row_spec]
    else:
        args, in_specs = [x], [row_spec]
    out_shapes, out_specs = [], []
    if has_post:
        args += [y, norm_post, mod, mod]
        in_specs += [row_spec, gain_spec(post_layer), mod_p_spec(post_layer, 2), mod_s_spec(post_layer, 2)]
        if split_out:
            out_shapes += [jax.ShapeDtypeStruct((M_PROMPT, D_MODEL), F32),
                           jax.ShapeDtypeStruct((SAMPLE_ROWS, D_MODEL), F32)]
            out_specs += [prompt_row_spec, sample_row_spec]
        else:
            out_shapes.append(jax.ShapeDtypeStruct((M_ALL, D_MODEL), F32))
            out_specs.append(row_spec)
    if has_pre:
        args += [norm_pre, mod, mod, mod, mod]
        in_specs += [gain_spec(pre_layer), mod_p_spec(pre_layer, 0), mod_p_spec(pre_layer, 1),
                     mod_s_spec(pre_layer, 0), mod_s_spec(pre_layer, 1)]
        out_shapes.append(jax.ShapeDtypeStruct((M_ALL, D_MODEL), BF16))
        out_specs.append(row_spec)
    outs = pl.pallas_call(
        functools.partial(_norm_kernel, has_post=has_post, has_pre=has_pre, split_in=split_in,
                          split_out=split_out, n_prompt_tiles=npt, tiles_per_batch=tiles_per_batch),
        grid=(npt + 1,),
        in_specs=in_specs,
        out_specs=out_specs,
        out_shape=out_shapes,
        compiler_params=pltpu.CompilerParams(
            dimension_semantics=("arbitrary",), vmem_limit_bytes=VMEM_LIMIT),
        name="norm_step",
    )(*args)
    return outs


def _mm_kernel(a_ref, w_ref, o_ref, wb_ref, *, k_rows, slabs):
    @pl.when(pl.program_id(1) == 0)
    def _():
        _cast_rows_to_bf16(w_ref, wb_ref, k_rows)

    tn = wb_ref.shape[1]
    for c0 in range(0, tn, MXU_COLS):
        acc = jnp.dot(a_ref[...], wb_ref[:, c0:c0 + MXU_COLS], preferred_element_type=F32).astype(o_ref.dtype)
        if slabs:
            for c in range(MXU_COLS // HEAD_DIM):
                o_ref[c0 // HEAD_DIM + c] = acc[:, c * HEAD_DIM:(c + 1) * HEAD_DIM]
        else:
            o_ref[:, c0:c0 + MXU_COLS] = acc


def _matmul(a, w, layer, n_out, tn, tm, out_dtype, name, slabs=False):
    m, k = a.shape
    if slabs:
        out_spec = pl.BlockSpec((tn // HEAD_DIM, tm, HEAD_DIM), lambda j, i: (j, i, 0))
        out_shape = jax.ShapeDtypeStruct((n_out // HEAD_DIM, m, HEAD_DIM), out_dtype)
    else:
        out_spec = pl.BlockSpec((tm, tn), lambda j, i: (i, j))
        out_shape = jax.ShapeDtypeStruct((m, n_out), out_dtype)
    return pl.pallas_call(
        functools.partial(_mm_kernel, k_rows=k, slabs=slabs),
        grid=(n_out // tn, m // tm),
        in_specs=[pl.BlockSpec((tm, k), lambda j, i: (i, 0)),
                  pl.BlockSpec((None, k, tn), lambda j, i: (layer, 0, j))],
        out_specs=out_spec,
        out_shape=out_shape,
        scratch_shapes=[pltpu.VMEM((k, tn), BF16)],
        compiler_params=pltpu.CompilerParams(
            dimension_semantics=("arbitrary", "arbitrary"), vmem_limit_bytes=VMEM_LIMIT),
        name=name,
    )(a, w)


def _pool_kernel(u_ref, halo_ref, z_ref, st_ref, w_ref, sc_ref,
                 a_ref, pp_ref, ps_ref, wb_ref, buf_a, buf_b, *, n_prompt_tiles, tiles_per_batch):
    g = pl.program_id(0)
    t = pl.program_id(1)
    tp = TP_POOL
    h0 = POOL_HALO

    @pl.when(t == 0)
    def _():
        _cast_rows_to_bf16(w_ref, wb_ref, POOL_GROUP)

    def finish(r, z, rows):
        y = jnp.dot(r.astype(BF16), wb_ref[...], preferred_element_type=F32) * sc_ref[...]
        a_ref[rows, :] = (y * _silu(z)).astype(BF16)

    def prompt_tile(n_steps):
        w = 2 ** n_steps
        first = (t % tiles_per_batch) == 0
        buf_a[0:8, :] = jnp.zeros((8, POOL_GROUP), F32)
        buf_b[0:8, :] = jnp.zeros((8, POOL_GROUP), F32)
        buf_a[8:h0, :] = jnp.where(first, 0.0, halo_ref[...])
        buf_a[h0:h0 + tp, :] = u_ref[...]
        src, dst = buf_a, buf_b
        n = tp + h0 - 8
        for s in range(n_steps):
            sh = 2 ** s
            dst[8:8 + n, :] = src[8:8 + n, :] + src[8 - sh:8 - sh + n, :]
            src, dst = dst, src
        pos = (t % tiles_per_batch) * tp + lax.broadcasted_iota(jnp.int32, (tp, 1), 0)
        inv_cnt = 1.0 / jnp.minimum(pos + 1, w).astype(F32)
        u = u_ref[...]
        r = src[h0:h0 + tp, :] * inv_cnt - u
        finish(r, z_ref[...], slice(None))

        @pl.when((t % tiles_per_batch) == tiles_per_batch - 1)
        def _():
            pp_ref[...] = u_ref[tp - POOL_BUF:tp, :]

    def sample_tile(n_steps):
        w = 2 ** n_steps
        u_new = u_ref[0:DEC_BATCH, :]
        acc = u_new
        for k in range(1, w):
            acc = acc + st_ref[POOL_BUF - k]
        r = acc / float(w) - u_new
        buf_a[0:DEC_BATCH, :] = r
        buf_a[DEC_BATCH:SAMPLE_ROWS, :] = jnp.zeros((SAMPLE_ROWS - DEC_BATCH, POOL_GROUP), F32)
        finish(buf_a[0:SAMPLE_ROWS, :], z_ref[0:SAMPLE_ROWS, :], slice(0, SAMPLE_ROWS))
        for k in range(POOL_BUF - 1):
            ps_ref[k] = st_ref[k + 1]
        ps_ref[POOL_BUF - 1] = u_new

    for gi in range(len(POOL_WINDOWS)):
        @pl.when((g == gi) & (t < n_prompt_tiles))
        def _(gi=gi):
            prompt_tile(gi + 1)

        @pl.when((g == gi) & (t == n_prompt_tiles))
        def _(gi=gi):
            sample_tile(gi + 1)


def _pool_mix(proj, state_t, w_grp, scale, layer):
    tp = TP_POOL
    npt = M_PROMPT // tp
    tpb = SEQ // tp
    ng = len(POOL_WINDOWS)
    halo_blocks = tp // 16
    outs = pl.pallas_call(
        functools.partial(_pool_kernel, n_prompt_tiles=npt, tiles_per_batch=tpb),
        grid=(ng, npt + 1),
        in_specs=[
            pl.BlockSpec((tp, POOL_GROUP), lambda g, t: (t, g)),
            pl.BlockSpec((16, POOL_GROUP), lambda g, t: (jnp.maximum(t * halo_blocks - 1, 0), g)),
            pl.BlockSpec((tp, POOL_GROUP), lambda g, t: (t, ng + g)),
            pl.BlockSpec((None, POOL_BUF, DEC_BATCH, POOL_GROUP), lambda g, t: (layer, 0, 0, g)),
            pl.BlockSpec((None, None, POOL_GROUP, POOL_GROUP), lambda g, t: (layer, g, 0, 0)),
            pl.BlockSpec((None, 1, POOL_GROUP), lambda g, t: (layer, 0, g)),
        ],
        out_specs=[
            pl.BlockSpec((tp, POOL_GROUP), lambda g, t: (t, g)),
            pl.BlockSpec((None, POOL_BUF, POOL_GROUP),
                         lambda g, t: (jnp.minimum(t // tpb, BATCH - 1), 0, g)),
            pl.BlockSpec((POOL_BUF, DEC_BATCH, POOL_GROUP), lambda g, t: (0, 0, g)),
        ],
        out_shape=[
            jax.ShapeDtypeStruct((M_ALL, POOL_WIDTH), BF16),
            jax.ShapeDtypeStruct((BATCH, POOL_BUF, POOL_WIDTH), F32),
            jax.ShapeDtypeStruct((POOL_BUF, DEC_BATCH, POOL_WIDTH), F32),
        ],
        scratch_shapes=[pltpu.VMEM((POOL_GROUP, POOL_GROUP), BF16),
                        pltpu.VMEM((tp + POOL_HALO, POOL_GROUP), F32),
                        pltpu.VMEM((tp + POOL_HALO, POOL_GROUP), F32)],
        compiler_params=pltpu.CompilerParams(
            dimension_semantics=("arbitrary", "arbitrary"), vmem_limit_bytes=VMEM_LIMIT),
        name="pool_mix",
    )(proj, proj, proj, state_t, w_grp, scale.reshape(-1, 1, POOL_WIDTH))
    return outs


def _att_prompt_kernel(x_ref, bias_ref, as_ref, a_ref, *scratch):
    @pl.when(pl.program_id(0) < BATCH)
    def _():
        _att_prompt_tile(x_ref, bias_ref, a_ref, *scratch)

    @pl.when(pl.program_id(0) == BATCH)
    def _():
        a_ref[0:SAMPLE_ROWS, :] = as_ref[...]


def _att_prompt_tile(x_ref, bias_ref, a_ref, qd, kd, vd, od, ld, on, ln, tmp3):
    tmp = tmp3.at[0]
    q_refs = tuple(x_ref.at[g] for g in range(N_GROUPS))
    k_refs = tuple(x_ref.at[N_GROUPS + g] for g in range(N_GROUPS))
    v_refs = tuple(x_ref.at[2 * N_GROUPS + g] for g in range(N_GROUPS))
    z_ref = x_ref.at[3 * N_GROUPS]
    bias_refs = tuple(bias_ref.at[g] for g in range(N_GROUPS))
    nu = ATT_UNITS

    for g, dil in enumerate(DILATIONS):
        n = SEQ // dil
        nb = n // BAND
        zero_blk = jnp.zeros((BAND, HEAD_DIM), BF16)
        kd[g, 0:BAND, :] = zero_blk
        vd[g, 0:BAND, :] = zero_blk
        for src, dst, off, mul in ((q_refs[g], qd, 0, ATT_SCALE * LOG2E), (k_refs[g], kd, BAND, None),
                                   (v_refs[g], vd, BAND, None)):
            def to_bf16(x, mul=mul):
                return (x if mul is None else x * mul).astype(BF16)

            if dil == 1:
                def cp(i, c, src=src, dst=dst, g=g, off=off, to_bf16=to_bf16):
                    r = pl.multiple_of(i * ROW_CHUNK, ROW_CHUNK)
                    dst[g, pl.ds(off + r, ROW_CHUNK), :] = to_bf16(src[pl.ds(r, ROW_CHUNK), :])
                    return c
                lax.fori_loop(0, SEQ // ROW_CHUNK, cp, 0)
            elif dil == 4:
                for r in range(dil):
                    dst[g, off + r * n:off + (r + 1) * n, :] = to_bf16(src[pl.ds(r, n, stride=dil), :])
            else:
                stage = tmp3.at[(off > 0) + (dst is vd)]
                quarter = SEQ // 4
                for r_lo in range(4):
                    stage[r_lo * quarter:(r_lo + 1) * quarter, :] = src[pl.ds(r_lo, quarter, stride=4), :]
                for r in range(dil):
                    r_lo, r_hi = r % 4, r // 4
                    dst[g, off + r * n:off + (r + 1) * n, :] = to_bf16(
                        stage[pl.ds(r_lo * quarter + r_hi, n, stride=4), :])

    def unit_batch(g, u0, seq_blocks):
        with_prev = seq_blocks > 1
        rows = nu * BAND
        row = pl.multiple_of(u0 * BAND, rows)
        shape3 = (nu, BAND, HEAD_DIM)
        q = qd[g, pl.ds(row, rows), :].reshape(shape3)
        kc = kd[g, pl.ds(row + BAND, rows), :].reshape(shape3)
        vc = vd[g, pl.ds(row + BAND, rows), :].reshape(shape3)
        bias = bias_refs[g]
        s_c = jnp.einsum('uqe,uke->uqk', q, kc, preferred_element_type=F32) \
            + (bias[:, BAND:2 * BAND] * LOG2E)[None]
        if with_prev:
            kp = kd[g, pl.ds(row, rows), :].reshape(shape3)
            vp = vd[g, pl.ds(row, rows), :].reshape(shape3)
            bias_p = jnp.broadcast_to((bias[:, 0:BAND] * LOG2E)[None], (nu, BAND, BAND))
            blk = u0 + lax.broadcasted_iota(jnp.int32, (nu, BAND, BAND), 0)
            bias_p = jnp.where((blk & (seq_blocks - 1)) == 0, NEG_INF, bias_p)
            s_p = jnp.einsum('uqe,uke->uqk', q, kp, preferred_element_type=F32) + bias_p
            m = jnp.max(jnp.maximum(s_c, s_p), axis=-1, keepdims=True)
            p_c = jnp.exp2(s_c - m)
            p_p = jnp.exp2(s_p - m)
            l = jnp.sum(p_c + p_p, axis=-1, keepdims=True)
            acc = jnp.einsum('uqk,uke->uqe', p_c.astype(BF16), vc, preferred_element_type=F32) \
                + jnp.einsum('uqk,uke->uqe', p_p.astype(BF16), vp, preferred_element_type=F32)
        else:
            m = jnp.max(s_c, axis=-1, keepdims=True)
            p_c = jnp.exp2(s_c - m)
            l = jnp.sum(p_c, axis=-1, keepdims=True)
            acc = jnp.einsum('uqk,uke->uqe', p_c.astype(BF16), vc, preferred_element_type=F32)
        od[g, pl.ds(row, rows), :] = (acc / l).reshape(rows, HEAD_DIM)
        ld[g, pl.ds(row, rows), :] = jnp.broadcast_to(m + jnp.log(l) * LOG2E, shape3).reshape(rows, HEAD_DIM)

    for g, dil in enumerate(DILATIONS):
        nb = (SEQ // dil) // BAND

        def batch_body(i, c, g=g, nb=nb):
            unit_batch(g, i * nu, nb)
            return c
        lax.fori_loop(0, N_BLOCKS // nu, batch_body, 0)

    def interleave4(src, dst, n_rows):
        quarter = n_rows // 4
        for r in range(4):
            dst[pl.ds(r, quarter, stride=4), :] = src[r * quarter:(r + 1) * quarter, :]

    for src_all, dst_all in ((od, on), (ld, ln)):
        interleave4(src_all.at[1], dst_all.at[0], SEQ)
        for r_lo in range(4):
            for r_hi in range(4):
                r = r_lo + 4 * r_hi
                tmp[pl.ds(r_lo * (SEQ // 4) + r_hi, BAND, stride=4), :] = src_all[2, r * BAND:(r + 1) * BAND, :]
        interleave4(tmp, dst_all.at[1], SEQ)

    def comb(i, c):
        rs = pl.ds(pl.multiple_of(i * ROW_CHUNK, ROW_CHUNK), ROW_CHUNK)
        l0, l1, l2 = ld[0, rs, :], ln[0, rs, :], ln[1, rs, :]
        mx = jnp.maximum(jnp.maximum(l0, l1), l2)
        w0, w1, w2 = jnp.exp2(l0 - mx), jnp.exp2(l1 - mx), jnp.exp2(l2 - mx)
        o = (w0 * od[0, rs, :] + w1 * on[0, rs, :] + w2 * on[1, rs, :]) / (w0 + w1 + w2)
        a_ref[rs, :] = (o * _silu(z_ref[rs, :])).astype(BF16)
        return c
    lax.fori_loop(0, SEQ // ROW_CHUNK, comb, 0)


def _att_prompt(proj, bias_mats, a_sample):
    hb = N_HEADS

    def head_blk(b, h):
        return jnp.where(b < BATCH, h, hb - 1)

    n_pg = ATT_IN_COLS // D_MODEL
    in_specs = [
        pl.BlockSpec((n_pg, None, SEQ, HEAD_DIM),
                     lambda b, h: (0, head_blk(b, h), jnp.minimum(b, BATCH - 1), 0)),
        pl.BlockSpec((N_GROUPS, None, BAND, 2 * BAND), lambda b, h: (0, head_blk(b, h), 0, 0)),
        pl.BlockSpec((SAMPLE_ROWS, HEAD_DIM), lambda b, h: (0, h)),
    ]
    scratch = [pltpu.VMEM((N_GROUPS, SEQ, HEAD_DIM), BF16)] + \
              [pltpu.VMEM((N_GROUPS, SEQ + BAND, HEAD_DIM), BF16)] * 2 + \
              [pltpu.VMEM((N_GROUPS, SEQ, HEAD_DIM), F32)] * 2 + \
              [pltpu.VMEM((N_GROUPS - 1, SEQ, HEAD_DIM), F32)] * 2 + \
              [pltpu.VMEM((3, SEQ, HEAD_DIM), F32)]
    return pl.pallas_call(
        _att_prompt_kernel,
        grid=(BATCH + 1, N_HEADS),
        in_specs=in_specs,
        out_specs=pl.BlockSpec((SEQ, HEAD_DIM), lambda b, h: (b, h)),
        out_shape=jax.ShapeDtypeStruct((M_ALL, D_MODEL), BF16),
        scratch_shapes=scratch,
        compiler_params=pltpu.CompilerParams(
            dimension_semantics=("arbitrary", "arbitrary"), vmem_limit_bytes=VMEM_LIMIT),
        name="att_prompt",
    )(proj.reshape(n_pg, hb, M_ALL, HEAD_DIM), bias_mats.reshape(N_GROUPS, hb, BAND, 2 * BAND), a_sample)


def _kv_rows_kernel(k_a, v_a, k_b, v_b, o_ref, *, tr):
    pitch = tr + KV_PITCH_PAD

    def emit(srcs):
        flats = [src.reshape(N_HEADS * pitch, HEAD_DIM) for src in srcs]

        def body(i, c):
            t0 = i * 8
            for k in range(8):
                for part, flat in enumerate(flats):
                    o_ref[t0 + k, part] = flat[pl.ds(t0 + k, N_HEADS, stride=pitch), :]
            return c
        lax.fori_loop(0, tr // 8, body, 0)

    @pl.when(pl.program_id(0) == 0)
    def _():
        emit((k_a, v_a))

    @pl.when(pl.program_id(0) == 1)
    def _():
        emit((k_b, v_b))


def _kv_rows(proj_a, proj_b, g):
    keep = WINDOWS[g]
    tr = min(keep, TR_KV)
    nt = keep // tr
    first_blk = (SEQ - keep) // tr
    blks_per_batch = SEQ // tr

    def row_blk(b, t):
        return b * blks_per_batch + first_blk + t

    def spec(layer, part):
        parked = row_blk(BATCH - 1, nt - 1) if layer == 0 else row_blk(0, 0)
        col0 = ((1 + part) * N_GROUPS + g) * N_HEADS
        return pl.BlockSpec((pl.Element(N_HEADS), pl.Element(tr + KV_PITCH_PAD), pl.Element(HEAD_DIM)),
                            lambda l, b, t: (col0, jnp.where(l == layer, row_blk(b, t), parked) * tr, 0))

    return pl.pallas_call(
        functools.partial(_kv_rows_kernel, tr=tr),
        grid=(2, BATCH, nt),
        in_specs=[spec(0, 0), spec(0, 1), spec(1, 0), spec(1, 1)],
        out_specs=pl.BlockSpec((None, None, tr, 2, N_HEADS, HEAD_DIM), lambda l, b, t: (l, b, t, 0, 0, 0)),
        out_shape=jax.ShapeDtypeStruct((2, BATCH, keep, 2, N_HEADS, HEAD_DIM), F32),
        compiler_params=pltpu.CompilerParams(
            dimension_semantics=("arbitrary",) * 3, vmem_limit_bytes=VMEM_LIMIT),
        name="kv_rows",
    )(proj_a, proj_a, proj_b, proj_b)


def _att_sample_kernel(x_ref, c0, c1, c2, bias_ref, a_ref, kv0, kv1, kv2):
    caches = (c0, c1, c2)
    kv_outs = (kv0, kv1, kv2)
    hb = N_HEADS
    outs, lses = [], []
    for g in range(N_GROUPS):
        q = x_ref[g * hb:(g + 1) * hb, :]
        kn = x_ref[(3 + g) * hb:(4 + g) * hb, :]
        vn = x_ref[(6 + g) * hb:(7 + g) * hb, :]
        kv_outs[g][0] = kn
        kv_outs[g][1] = vn
        kc = caches[g][:, 0]
        vc = caches[g][:, 1]
        s_c = jnp.sum(kc * q[None], axis=-1, keepdims=True) * ATT_SCALE + bias_ref[g, 0:BAND]
        s_n = jnp.sum(kn * q, axis=-1, keepdims=True) * ATT_SCALE + bias_ref[g, BAND]
        m = jnp.maximum(jnp.max(s_c, axis=0), s_n)
        p_c = jnp.exp(s_c - m[None])
        p_n = jnp.exp(s_n - m)
        l = jnp.sum(p_c, axis=0) + p_n
        o = (jnp.sum(p_c * vc, axis=0) + p_n * vn) / l
        outs.append(o)
        lses.append(m + jnp.log(l))
    mx = jnp.maximum(jnp.maximum(lses[0], lses[1]), lses[2])
    ws = [jnp.exp(ls - mx) for ls in lses]
    o = (ws[0] * outs[0] + ws[1] * outs[1] + ws[2] * outs[2]) / (ws[0] + ws[1] + ws[2])
    z = x_ref[9 * hb:10 * hb, :]
    a_ref[...] = o * _silu(z)


def _att_sample(proj_s, caches, bias_s, layer):
    hb = N_HEADS
    in_specs = [pl.BlockSpec((None, ATT_IN_COLS // HEAD_DIM, HEAD_DIM), lambda b: (b, 0, 0))]
    for g in range(N_GROUPS):
        in_specs.append(pl.BlockSpec((None, None, BAND, None, 2, hb, HEAD_DIM),
                                     lambda b: (layer, b, 0, 0, 0, 0, 0)))
    in_specs.append(pl.BlockSpec((N_GROUPS, BAND + 1, hb, 1), lambda b: (0, 0, 0, 0)))
    out_specs = [pl.BlockSpec((None, hb, HEAD_DIM), lambda b: (b, 0, 0))]
    out_shapes = [jax.ShapeDtypeStruct((DEC_BATCH, hb, HEAD_DIM), F32)]
    for g in range(N_GROUPS):
        out_specs.append(pl.BlockSpec((None, 2, hb, HEAD_DIM), lambda b: (b, 0, 0, 0)))
        out_shapes.append(jax.ShapeDtypeStruct((DEC_BATCH, 2, hb, HEAD_DIM), F32))
    return pl.pallas_call(
        _att_sample_kernel,
        grid=(DEC_BATCH,),
        in_specs=in_specs,
        out_specs=out_specs,
        out_shape=out_shapes,
        compiler_params=pltpu.CompilerParams(
            dimension_semantics=("arbitrary",), vmem_limit_bytes=VMEM_LIMIT),
        name="att_sample",
    )(proj_s, *caches, bias_s)


def kernel(x_prompt, x_sample, c_prompt, c_sample, cache_kv0, cache_kv1, cache_kv2, state_pool,
           norm_pre, norm_post, ada_w, ada_b, t5_bias, pool_w_in, pool_w_grp, pool_scale,
           pool_w_out, att_w_in, att_w_out):
    n_att = DEPTH // 2
    xp0 = x_prompt.reshape(M_PROMPT, D_MODEL)
    xs0 = jnp.zeros((SAMPLE_ROWS, D_MODEL), F32).at[0:DEC_BATCH].set(x_sample.reshape(DEC_BATCH, D_MODEL))
    c_all = jnp.zeros((C_ROWS, D_MODEL), F32)
    c_all = c_all.at[0:BATCH].set(c_prompt).at[C_SAMPLE_ROW0:C_SAMPLE_ROW0 + DEC_BATCH].set(c_sample)

    mod = _ada_all(c_all, ada_w, ada_b)
    gains_pre = norm_pre.reshape(DEPTH, 1, D_MODEL)
    gains_post = norm_post.reshape(DEPTH, 1, D_MODEL)

    bias_mats = _bias_matrices(t5_bias)
    bias_s = bias_mats[:, 0, 0:BAND + 1].reshape(N_GROUPS, N_HEADS, BAND + 1)
    bias_s = jnp.transpose(bias_s, (0, 2, 1))[..., None]
    caches = [c.reshape(n_att, DEC_BATCH, BAND, dil, 2, N_HEADS, HEAD_DIM)
              for c, dil in zip((cache_kv0, cache_kv1, cache_kv2), DILATIONS)]
    state_t = jnp.transpose(state_pool, (0, 2, 1, 3))

    kv_s = [[] for _ in range(N_GROUPS)]
    pool_p, pool_s, att_projs = [], [], []

    (h,) = _norm_step(xp0, None, mod, gains_post, gains_pre, None, 0, x_sample=xs0)
    x = None
    for i in range(DEPTH):
        li = i // 2
        if i % 2 == 0:
            proj = _matmul(h, pool_w_in, li, 2 * POOL_WIDTH, 1024, TM_MATMUL_K2048, F32, "pool_in_proj")
            a, pp, ps = _pool_mix(proj, state_t, pool_w_grp, pool_scale, li)
            pool_p.append(pp)
            pool_s.append(jnp.transpose(ps, (1, 0, 2)))
            y = _matmul(a, pool_w_out, li, D_MODEL, 512, TM_MATMUL, BF16, "pool_out_proj")
        else:
            proj = _matmul(h, att_w_in, li, ATT_IN_COLS, 1024, TM_MATMUL_K2048, F32, "att_in_proj", slabs=True)
            att_projs.append(proj)
            proj_s = jnp.transpose(proj[:, M_PROMPT:M_PROMPT + DEC_BATCH, :], (1, 0, 2))
            souts = _att_sample(proj_s, caches, bias_s, li)
            a_s = jnp.zeros((SAMPLE_ROWS, D_MODEL), F32).at[0:DEC_BATCH].set(souts[0].reshape(DEC_BATCH, D_MODEL))
            a = _att_prompt(proj, bias_mats, a_s.astype(BF16))
            for g in range(N_GROUPS):
                kv_s[g].append(souts[1 + g].reshape(DEC_BATCH, 1, 2, N_HEADS, HEAD_DIM))
            y = _matmul(a, att_w_out, li, D_MODEL, 1024, TM_MATMUL_K2048, BF16, "att_out_proj")
        last = i + 1 == DEPTH
        if i == 0:
            x, h = _norm_step(xp0, y, mod, gains_post, gains_pre, i, i + 1, x_sample=xs0)
        elif not last:
            x, h = _norm_step(x, y, mod, gains_post, gains_pre, i, i + 1)
        else:
            y_p, y_s = _norm_step(x, y, mod, gains_post, gains_pre, i, None, split_out=True)

    kv_p = [_kv_rows(att_projs[0], att_projs[1], g) for g in range(N_GROUPS)]
    y_prompt = y_p.reshape(BATCH, SEQ, D_MODEL)
    y_sample = y_s[0:DEC_BATCH].reshape(DEC_BATCH, 1, D_MODEL)
    return (y_prompt, y_sample, kv_p[0], kv_p[1], kv_p[2], jnp.stack(pool_p),
            jnp.stack(kv_s[0]), jnp.stack(kv_s[1]), jnp.stack(kv_s[2]), jnp.stack(pool_s))
```

```python
import functools

import numpy as np
import jax
import jax.numpy as jnp
from jax import lax
from jax.experimental import pallas as pl
from jax.experimental.pallas import tpu as pltpu

D_MODEL = 2048
BATCH = 4
SEQ = 2048
DEPTH = 4
DEC_BATCH = 8
HEAD_DIM = 128
N_HEADS = 16
DILATIONS = (1, 4, 16)
WINDOWS = (128, 512, 2048)
N_GROUPS = 3
QKV_WIDTH = N_GROUPS * D_MODEL
ATT_IN_COLS = 3 * QKV_WIDTH + D_MODEL
BAND = 128
ATT_SCALE = HEAD_DIM ** -0.5
LOG2E = 1.4426950408889634
POOL_WINDOWS = (2, 4, 8, 16)
POOL_WIDTH = 2 * D_MODEL
POOL_GROUP = POOL_WIDTH // 4
POOL_BUF = 15
N_BUCKETS = 32
T5_MAX_DIST = 2048
RMS_EPS = 1e-6
NEG_INF = -1e30

M_PROMPT = BATCH * SEQ
SAMPLE_ROWS = 64
M_ALL = M_PROMPT + SAMPLE_ROWS
C_ROWS = 2 * SAMPLE_ROWS
C_PROMPT_ROWS = 8
C_SAMPLE_ROW0 = SAMPLE_ROWS

TM_MATMUL = 1376
TM_MATMUL_K2048 = 2064
MXU_COLS = 256
TE_NORM = 512
TP_POOL = 1024
POOL_HALO = 24
ATT_UNITS = 16
N_BLOCKS = SEQ // BAND
TR_KV = 256
ROW_CHUNK = 256
KV_PITCH_PAD = 8
VMEM_LIMIT = 58 * 1024 * 1024

F32 = jnp.float32
BF16 = jnp.bfloat16


def _silu(x):
    half = 0.5 * x
    return half + half * jnp.tanh(half)


def _cast_rows_to_bf16(src_ref, dst_ref, rows, chunk=ROW_CHUNK):
    def body(i, c):
        r = pl.multiple_of(i * chunk, chunk)
        dst_ref[pl.ds(r, chunk), :] = src_ref[pl.ds(r, chunk), :].astype(BF16)
        return c
    lax.fori_loop(0, rows // chunk, body, 0)


def _ada_kernel(c_ref, w_ref, b_ref, o_ref):
    a = _silu(c_ref[...]).astype(BF16)
    kc = 512
    acc = jnp.zeros(o_ref.shape, F32)
    for k0 in range(0, D_MODEL, kc):
        acc = acc + jnp.dot(a[:, k0:k0 + kc], w_ref[k0:k0 + kc, :].astype(BF16),
                            preferred_element_type=F32)
    o_ref[...] = acc + b_ref[...]


def _ada_all(c_all, ada_w, ada_b):
    tn = 1024
    n = 3 * D_MODEL
    return pl.pallas_call(
        _ada_kernel,
        grid=(DEPTH, n // tn),
        in_specs=[pl.BlockSpec((C_ROWS, D_MODEL), lambda l, j: (0, 0)),
                  pl.BlockSpec((None, D_MODEL, tn), lambda l, j: (l, 0, j)),
                  pl.BlockSpec((None, 1, tn), lambda l, j: (l, 0, j))],
        out_specs=pl.BlockSpec((None, C_ROWS, tn), lambda l, j: (l, 0, j)),
        out_shape=jax.ShapeDtypeStruct((DEPTH, C_ROWS, n), F32),
        compiler_params=pltpu.CompilerParams(
            dimension_semantics=("arbitrary", "arbitrary"), vmem_limit_bytes=VMEM_LIMIT),
        name="ada_mod",
    )(c_all, ada_w, ada_b.reshape(DEPTH, 1, n))


def _t5_bucket(dist):
    dist = np.asarray(dist, dtype=np.int64)
    max_exact = N_BUCKETS // 2
    ratio = np.log(np.maximum(dist, 1) / max_exact) / np.log(T5_MAX_DIST / max_exact)
    large = np.minimum(max_exact + (ratio * (N_BUCKETS - max_exact)).astype(np.int64), N_BUCKETS - 1)
    return np.where(dist < max_exact, dist, large).astype(np.int32)


def _bucket_index_table():
    rel = np.arange(BAND)[:, None] + BAND - np.arange(2 * BAND)[None, :]
    inband = (rel >= 0) & (rel <= BAND)
    out = []
    for dil in DILATIONS:
        bucket = _t5_bucket(np.clip(rel, 0, BAND) * dil)
        out.append(np.where(inband, bucket, -1))
    return np.stack(out).astype(np.int32)


def _bias_kernel(tab_ref, idx_ref, o_ref):
    g = pl.program_id(0)
    idx = idx_ref[...]

    def head(h, c):
        acc = jnp.full(idx.shape, NEG_INF, F32)
        for b in range(N_BUCKETS):
            acc = jnp.where(idx == b, tab_ref[b, g * N_HEADS + h], acc)
        o_ref[h] = acc
        return c
    lax.fori_loop(0, N_HEADS, head, 0)


def _bias_matrices(t5_bias):
    idx = jnp.asarray(_bucket_index_table())
    n_sub = N_GROUPS * N_HEADS
    return pl.pallas_call(
        _bias_kernel,
        grid=(N_GROUPS,),
        in_specs=[pl.BlockSpec(memory_space=pltpu.SMEM),
                  pl.BlockSpec((None, BAND, 2 * BAND), lambda g: (g, 0, 0))],
        out_specs=pl.BlockSpec((N_HEADS, BAND, 2 * BAND), lambda g: (g, 0, 0)),
        out_shape=jax.ShapeDtypeStruct((n_sub, BAND, 2 * BAND), F32),
        name="t5_bias_mats",
    )(t5_bias, idx)


def _norm_kernel(*refs, has_post, has_pre, split_in, split_out, n_prompt_tiles, tiles_per_batch):
    refs = list(refs)
    xp_ref = refs.pop(0)
    xs_ref = refs.pop(0) if split_in else xp_ref
    if has_post:
        y_ref, gpost_ref, gate_p_ref, gate_s_ref = refs[:4]
        refs = refs[4:]
    if has_pre:
        gpre_ref, shift_p_ref, scale_p_ref, shift_s_ref, scale_s_ref = refs[:5]
        refs = refs[5:]
    if has_post:
        xop_ref = refs.pop(0)
        xos_ref = refs.pop(0) if split_out else xop_ref
    if has_pre:
        h_ref = refs.pop(0)

    def rms(v, g):
        return v * lax.rsqrt(jnp.mean(v * v, axis=-1, keepdims=True) + RMS_EPS) * g

    def body(x_ref, xo_ref, rows, gate, shift, scale):
        x = x_ref[rows, :]
        if has_post:
            x = x + gate * rms(y_ref[rows, :].astype(F32), gpost_ref[...])
            xo_ref[rows, :] = x
        if has_pre:
            h = rms(x, gpre_ref[...]) * (1.0 + scale) + shift
            h_ref[rows, :] = h.astype(BF16)

    t = pl.program_id(0)

    @pl.when(t < n_prompt_tiles)
    def _():
        row = pl.ds(t // tiles_per_batch, 1)
        body(xp_ref, xop_ref if has_post else None, slice(None),
             gate_p_ref[row, :] if has_post else None,
             shift_p_ref[row, :] if has_pre else None,
             scale_p_ref[row, :] if has_pre else None)

    @pl.when(t == n_prompt_tiles)
    def _():
        body(xs_ref, xos_ref if has_post else None, slice(0, SAMPLE_ROWS),
             gate_s_ref[...] if has_post else None,
             shift_s_ref[...] if has_pre else None,
             scale_s_ref[...] if has_pre else None)


def _norm_step(x, y, mod, norm_post, norm_pre, post_layer, pre_layer, x_sample=None, split_out=False):
    has_post = post_layer is not None
    has_pre = pre_layer is not None
    split_in = x_sample is not None
    te = TE_NORM
    npt = M_PROMPT // te
    tiles_per_batch = SEQ // te
    row_spec = pl.BlockSpec((te, D_MODEL), lambda t: (t, 0))
    prompt_row_spec = pl.BlockSpec((te, D_MODEL), lambda t: (jnp.minimum(t, npt - 1), 0))
    sample_row_spec = pl.BlockSpec((SAMPLE_ROWS, D_MODEL), lambda t: (0, 0))

    def mod_p_spec(layer, part):
        return pl.BlockSpec((None, C_PROMPT_ROWS, D_MODEL), lambda t: (layer, 0, part))

    def mod_s_spec(layer, part):
        return pl.BlockSpec((None, SAMPLE_ROWS, D_MODEL),
                            lambda t: (layer, C_SAMPLE_ROW0 // SAMPLE_ROWS, part))

    def gain_spec(layer):
        return pl.BlockSpec((None, 1, D_MODEL), lambda t: (layer, 0, 0))

    if split_in:
        args, in_specs = [x, x_sample], [prompt_row_spec, sample_row_spec]
    else:
        args, in_specs = [x], [row_spec]
    out_shapes, out_specs = [], []
    if has_post:
        args += [y, norm_post, mod, mod]
        in_specs += [row_spec, gain_spec(post_layer), mod_p_spec(post_layer, 2), mod_s_spec(post_layer, 2)]
        if split_out:
            out_shapes += [jax.ShapeDtypeStruct((M_PROMPT, D_MODEL), F32),
                           jax.ShapeDtypeStruct((SAMPLE_ROWS, D_MODEL), F32)]
            out_specs += [prompt_row_spec, sample_row_spec]
        else:
            out_shapes.append(jax.ShapeDtypeStruct((M_ALL, D_MODEL), F32))
            out_specs.append(row_spec)
    if has_pre:
        args += [norm_pre, mod, mod, mod, mod]
        in_specs += [gain_spec(pre_layer), mod_p_spec(pre_layer, 0), mod_p_spec(pre_layer, 1),
                     mod_s_spec(pre_layer, 0), mod_s_spec(pre_layer, 1)]
        out_shapes.append(jax.ShapeDtypeStruct((M_ALL, D_MODEL), BF16))
        out_specs.append(row_spec)
    outs = pl.pallas_call(
        functools.partial(_norm_kernel, has_post=has_post, has_pre=has_pre, split_in=split_in,
                          split_out=split_out, n_prompt_tiles=npt, tiles_per_batch=tiles_per_batch),
        grid=(npt + 1,),
        in_specs=in_specs,
        out_specs=out_specs,
        out_shape=out_shapes,
        compiler_params=pltpu.CompilerParams(
            dimension_semantics=("arbitrary",), vmem_limit_bytes=VMEM_LIMIT),
        name="norm_step",
    )(*args)
    return outs


def _mm_kernel(a_ref, w_ref, o_ref, wb_ref, *, k_rows, slabs):
    @pl.when(pl.program_id(1) == 0)
    def _():
        _cast_rows_to_bf16(w_ref, wb_ref, k_rows)

    tn = wb_ref.shape[1]
    for c0 in range(0, tn, MXU_COLS):
        acc = jnp.dot(a_ref[...], wb_ref[:, c0:c0 + MXU_COLS], preferred_element_type=F32).astype(o_ref.dtype)
        if slabs:
            for c in range(MXU_COLS // HEAD_DIM):
                o_ref[c0 // HEAD_DIM + c] = acc[:, c * HEAD_DIM:(c + 1) * HEAD_DIM]
        else:
            o_ref[:, c0:c0 + MXU_COLS] = acc


def _matmul(a, w, layer, n_out, tn, tm, out_dtype, name, slabs=False):
    m, k = a.shape
    if slabs:
        out_spec = pl.BlockSpec((tn // HEAD_DIM, tm, HEAD_DIM), lambda j, i: (j, i, 0))
        out_shape = jax.ShapeDtypeStruct((n_out // HEAD_DIM, m, HEAD_DIM), out_dtype)
    else:
        out_spec = pl.BlockSpec((tm, tn), lambda j, i: (i, j))
        out_shape = jax.ShapeDtypeStruct((m, n_out), out_dtype)
    return pl.pallas_call(
        functools.partial(_mm_kernel, k_rows=k, slabs=slabs),
        grid=(n_out // tn, m // tm),
        in_specs=[pl.BlockSpec((tm, k), lambda j, i: (i, 0)),
                  pl.BlockSpec((None, k, tn), lambda j, i: (layer, 0, j))],
        out_specs=out_spec,
        out_shape=out_shape,
        scratch_shapes=[pltpu.VMEM((k, tn), BF16)],
        compiler_params=pltpu.CompilerParams(
            dimension_semantics=("arbitrary", "arbitrary"), vmem_limit_bytes=VMEM_LIMIT),
        name=name,
    )(a, w)


def _pool_kernel(u_ref, halo_ref, z_ref, st_ref, w_ref, sc_ref,
                 a_ref, pp_ref, ps_ref, wb_ref, buf_a, buf_b, *, n_prompt_tiles, tiles_per_batch):
    g = pl.program_id(0)
    t = pl.program_id(1)
    tp = TP_POOL
    h0 = POOL_HALO

    @pl.when(t == 0)
    def _():
        _cast_rows_to_bf16(w_ref, wb_ref, POOL_GROUP)

    def finish(r, z, rows):
        y = jnp.dot(r.astype(BF16), wb_ref[...], preferred_element_type=F32) * sc_ref[...]
        a_ref[rows, :] = (y * _silu(z)).astype(BF16)

    def prompt_tile(n_steps):
        w = 2 ** n_steps
        first = (t % tiles_per_batch) == 0
        buf_a[0:8, :] = jnp.zeros((8, POOL_GROUP), F32)
        buf_b[0:8, :] = jnp.zeros((8, POOL_GROUP), F32)
        buf_a[8:h0, :] = jnp.where(first, 0.0, halo_ref[...])
        buf_a[h0:h0 + tp, :] = u_ref[...]
        src, dst = buf_a, buf_b
        n = tp + h0 - 8
        for s in range(n_steps):
            sh = 2 ** s
            dst[8:8 + n, :] = src[8:8 + n, :] + src[8 - sh:8 - sh + n, :]
            src, dst = dst, src
        pos = (t % tiles_per_batch) * tp + lax.broadcasted_iota(jnp.int32, (tp, 1), 0)
        inv_cnt = 1.0 / jnp.minimum(pos + 1, w).astype(F32)
        u = u_ref[...]
        r = src[h0:h0 + tp, :] * inv_cnt - u
        finish(r, z_ref[...], slice(None))

        @pl.when((t % tiles_per_batch) == tiles_per_batch - 1)
        def _():
            pp_ref[...] = u_ref[tp - POOL_BUF:tp, :]

    def sample_tile(n_steps):
        w = 2 ** n_steps
        u_new = u_ref[0:DEC_BATCH, :]
        acc = u_new
        for k in range(1, w):
            acc = acc + st_ref[POOL_BUF - k]
        r = acc / float(w) - u_new
        buf_a[0:DEC_BATCH, :] = r
        buf_a[DEC_BATCH:SAMPLE_ROWS, :] = jnp.zeros((SAMPLE_ROWS - DEC_BATCH, POOL_GROUP), F32)
        finish(buf_a[0:SAMPLE_ROWS, :], z_ref[0:SAMPLE_ROWS, :], slice(0, SAMPLE_ROWS))
        for k in range(POOL_BUF - 1):
            ps_ref[k] = st_ref[k + 1]
        ps_ref[POOL_BUF - 1] = u_new

    for gi in range(len(POOL_WINDOWS)):
        @pl.when((g == gi) & (t < n_prompt_tiles))
        def _(gi=gi):
            prompt_tile(gi + 1)

        @pl.when((g == gi) & (t == n_prompt_tiles))
        def _(gi=gi):
            sample_tile(gi + 1)


def _pool_mix(proj, state_t, w_grp, scale, layer):
    tp = TP_POOL
    npt = M_PROMPT // tp
    tpb = SEQ // tp
    ng = len(POOL_WINDOWS)
    halo_blocks = tp // 16
    outs = pl.pallas_call(
        functools.partial(_pool_kernel, n_prompt_tiles=npt, tiles_per_batch=tpb),
        grid=(ng, npt + 1),
        in_specs=[
            pl.BlockSpec((tp, POOL_GROUP), lambda g, t: (t, g)),
            pl.BlockSpec((16, POOL_GROUP), lambda g, t: (jnp.maximum(t * halo_blocks - 1, 0), g)),
            pl.BlockSpec((tp, POOL_GROUP), lambda g, t: (t, ng + g)),
            pl.BlockSpec((None, POOL_BUF, DEC_BATCH, POOL_GROUP), lambda g, t: (layer, 0, 0, g)),
            pl.BlockSpec((None, None, POOL_GROUP, POOL_GROUP), lambda g, t: (layer, g, 0, 0)),
            pl.BlockSpec((None, 1, POOL_GROUP), lambda g, t: (layer, 0, g)),
        ],
        out_specs=[
            pl.BlockSpec((tp, POOL_GROUP), lambda g, t: (t, g)),
            pl.BlockSpec((None, POOL_BUF, POOL_GROUP),
                         lambda g, t: (jnp.minimum(t // tpb, BATCH - 1), 0, g)),
            pl.BlockSpec((POOL_BUF, DEC_BATCH, POOL_GROUP), lambda g, t: (0, 0, g)),
        ],
        out_shape=[
            jax.ShapeDtypeStruct((M_ALL, POOL_WIDTH), BF16),
            jax.ShapeDtypeStruct((BATCH, POOL_BUF, POOL_WIDTH), F32),
            jax.ShapeDtypeStruct((POOL_BUF, DEC_BATCH, POOL_WIDTH), F32),
        ],
        scratch_shapes=[pltpu.VMEM((POOL_GROUP, POOL_GROUP), BF16),
                        pltpu.VMEM((tp + POOL_HALO, POOL_GROUP), F32),
                        pltpu.VMEM((tp + POOL_HALO, POOL_GROUP), F32)],
        compiler_params=pltpu.CompilerParams(
            dimension_semantics=("arbitrary", "arbitrary"), vmem_limit_bytes=VMEM_LIMIT),
        name="pool_mix",
    )(proj, proj, proj, state_t, w_grp, scale.reshape(-1, 1, POOL_WIDTH))
    return outs


def _att_prompt_kernel(x_ref, bias_ref, as_ref, a_ref, *scratch):
    @pl.when(pl.program_id(0) < BATCH)
    def _():
        _att_prompt_tile(x_ref, bias_ref, a_ref, *scratch)

    @pl.when(pl.program_id(0) == BATCH)
    def _():
        a_ref[0:SAMPLE_ROWS, :] = as_ref[...]


def _att_prompt_tile(x_ref, bias_ref, a_ref, qd, kd, vd, od, ld, on, ln, tmp3):
    tmp = tmp3.at[0]
    q_refs = tuple(x_ref.at[g] for g in range(N_GROUPS))
    k_refs = tuple(x_ref.at[N_GROUPS + g] for g in range(N_GROUPS))
    v_refs = tuple(x_ref.at[2 * N_GROUPS + g] for g in range(N_GROUPS))
    z_ref = x_ref.at[3 * N_GROUPS]
    bias_refs = tuple(bias_ref.at[g] for g in range(N_GROUPS))
    nu = ATT_UNITS

    for g, dil in enumerate(DILATIONS):
        n = SEQ // dil
        nb = n // BAND
        zero_blk = jnp.zeros((BAND, HEAD_DIM), BF16)
        kd[g, 0:BAND, :] = zero_blk
        vd[g, 0:BAND, :] = zero_blk
        for src, dst, off, mul in ((q_refs[g], qd, 0, ATT_SCALE * LOG2E), (k_refs[g], kd, BAND, None),
                                   (v_refs[g], vd, BAND, None)):
            def to_bf16(x, mul=mul):
                return (x if mul is None else x * mul).astype(BF16)

            if dil == 1:
                dst[g, off:off + SEQ, :] = to_bf16(src[...])
            elif dil == 4:
                for r in range(dil):
                    dst[g, off + r * n:off + (r + 1) * n, :] = to_bf16(src[pl.ds(r, n, stride=dil), :])
            else:
                stage = tmp3.at[(off > 0) + (dst is vd)]
                quarter = SEQ // 4
                for r_lo in range(4):
                    stage[r_lo * quarter:(r_lo + 1) * quarter, :] = src[pl.ds(r_lo, quarter, stride=4), :]
                for r in range(dil):
                    r_lo, r_hi = r % 4, r // 4
                    dst[g, off + r * n:off + (r + 1) * n, :] = to_bf16(
                        stage[pl.ds(r_lo * quarter + r_hi, n, stride=4), :])

    def unit_batch(g, u0, seq_blocks):
        with_prev = seq_blocks > 1
        rows = nu * BAND
        row = pl.multiple_of(u0 * BAND, rows)
        shape3 = (nu, BAND, HEAD_DIM)
        q = qd[g, pl.ds(row, rows), :].reshape(shape3)
        kc = kd[g, pl.ds(row + BAND, rows), :].reshape(shape3)
        vc = vd[g, pl.ds(row + BAND, rows), :].reshape(shape3)
        bias = bias_refs[g]
        s_c = jnp.einsum('uqe,uke->uqk', q, kc, preferred_element_type=F32) \
            + (bias[:, BAND:2 * BAND] * LOG2E)[None]
        if with_prev:
            kp = kd[g, pl.ds(row, rows), :].reshape(shape3)
            vp = vd[g, pl.ds(row, rows), :].reshape(shape3)
            bias_p = jnp.broadcast_to((bias[:, 0:BAND] * LOG2E)[None], (nu, BAND, BAND))
            blk = u0 + lax.broadcasted_iota(jnp.int32, (nu, BAND, BAND), 0)
            bias_p = jnp.where((blk & (seq_blocks - 1)) == 0, NEG_INF, bias_p)
            s_p = jnp.einsum('uqe,uke->uqk', q, kp, preferred_element_type=F32) + bias_p
            m = jnp.max(jnp.maximum(s_c, s_p), axis=-1, keepdims=True)
            p_c = jnp.exp2(s_c - m)
            p_p = jnp.exp2(s_p - m)
            l = jnp.sum(p_c + p_p, axis=-1, keepdims=True)
            acc = jnp.einsum('uqk,uke->uqe', p_c.astype(BF16), vc, preferred_element_type=F32) \
                + jnp.einsum('uqk,uke->uqe', p_p.astype(BF16), vp, preferred_element_type=F32)
        else:
            m = jnp.max(s_c, axis=-1, keepdims=True)
            p_c = jnp.exp2(s_c - m)
            l = jnp.sum(p_c, axis=-1, keepdims=True)
            acc = jnp.einsum('uqk,uke->uqe', p_c.astype(BF16), vc, preferred_element_type=F32)
        od[g, pl.ds(row, rows), :] = (acc / l).reshape(rows, HEAD_DIM)
        ld[g, pl.ds(row, rows), :] = jnp.broadcast_to(m + jnp.log(l) * LOG2E, shape3).reshape(rows, HEAD_DIM)

    for g, dil in enumerate(DILATIONS):
        nb = (SEQ // dil) // BAND

        def batch_body(i, c, g=g, nb=nb):
            unit_batch(g, i * nu, nb)
            return c
        lax.fori_loop(0, N_BLOCKS // nu, batch_body, 0)

    def interleave4(src, dst, n_rows):
        quarter = n_rows // 4
        for r in range(4):
            dst[pl.ds(r, quarter, stride=4), :] = src[r * quarter:(r + 1) * quarter, :]

    for src_all, dst_all in ((od, on), (ld, ln)):
        interleave4(src_all.at[1], dst_all.at[0], SEQ)
        for r_lo in range(4):
            for r_hi in range(4):
                r = r_lo + 4 * r_hi
                tmp[pl.ds(r_lo * (SEQ // 4) + r_hi, BAND, stride=4), :] = src_all[2, r * BAND:(r + 1) * BAND, :]
        interleave4(tmp, dst_all.at[1], SEQ)

    def comb(i, c):
        rs = pl.ds(pl.multiple_of(i * ROW_CHUNK, ROW_CHUNK), ROW_CHUNK)
        l0, l1, l2 = ld[0, rs, :], ln[0, rs, :], ln[1, rs, :]
        mx = jnp.maximum(jnp.maximum(l0, l1), l2)
        w0, w1, w2 = jnp.exp2(l0 - mx), jnp.exp2(l1 - mx), jnp.exp2(l2 - mx)
        o = (w0 * od[0, rs, :] + w1 * on[0, rs, :] + w2 * on[1, rs, :]) / (w0 + w1 + w2)
        a_ref[rs, :] = (o * _silu(z_ref[rs, :])).astype(BF16)
        return c
    lax.fori_loop(0, SEQ // ROW_CHUNK, comb, 0)


def _att_prompt(proj, bias_mats, a_sample):
    hb = N_HEADS

    def head_blk(b, h):
        return jnp.where(b < BATCH, h, hb - 1)

    n_pg = ATT_IN_COLS // D_MODEL
    in_specs = [
        pl.BlockSpec((n_pg, None, SEQ, HEAD_DIM),
                     lambda b, h: (0, head_blk(b, h), jnp.minimum(b, BATCH - 1), 0)),
        pl.BlockSpec((N_GROUPS, None, BAND, 2 * BAND), lambda b, h: (0, head_blk(b, h), 0, 0)),
        pl.BlockSpec((SAMPLE_ROWS, HEAD_DIM), lambda b, h: (0, h)),
    ]
    scratch = [pltpu.VMEM((N_GROUPS, SEQ, HEAD_DIM), BF16)] + \
              [pltpu.VMEM((N_GROUPS, SEQ + BAND, HEAD_DIM), BF16)] * 2 + \
              [pltpu.VMEM((N_GROUPS, SEQ, HEAD_DIM), F32)] * 2 + \
              [pltpu.VMEM((N_GROUPS - 1, SEQ, HEAD_DIM), F32)] * 2 + \
              [pltpu.VMEM((3, SEQ, HEAD_DIM), F32)]
    return pl.pallas_call(
        _att_prompt_kernel,
        grid=(BATCH + 1, N_HEADS),
        in_specs=in_specs,
        out_specs=pl.BlockSpec((SEQ, HEAD_DIM), lambda b, h: (b, h)),
        out_shape=jax.ShapeDtypeStruct((M_ALL, D_MODEL), BF16),
        scratch_shapes=scratch,
        compiler_params=pltpu.CompilerParams(
            dimension_semantics=("arbitrary", "arbitrary"), vmem_limit_bytes=VMEM_LIMIT),
        name="att_prompt",
    )(proj.reshape(n_pg, hb, M_ALL, HEAD_DIM), bias_mats.reshape(N_GROUPS, hb, BAND, 2 * BAND), a_sample)


def _kv_rows_kernel(k_a, v_a, k_b, v_b, o_ref, *, tr):
    pitch = tr + KV_PITCH_PAD

    def emit(srcs):
        flats = [src.reshape(N_HEADS * pitch, HEAD_DIM) for src in srcs]

        def body(i, c):
            t0 = i * 8
            for k in range(8):
                for part, flat in enumerate(flats):
                    o_ref[t0 + k, part] = flat[pl.ds(t0 + k, N_HEADS, stride=pitch), :]
            return c
        lax.fori_loop(0, tr // 8, body, 0)

    @pl.when(pl.program_id(0) == 0)
    def _():
        emit((k_a, v_a))

    @pl.when(pl.program_id(0) == 1)
    def _():
        emit((k_b, v_b))


def _kv_rows(proj_a, proj_b, g):
    keep = WINDOWS[g]
    tr = min(keep, TR_KV)
    nt = keep // tr
    first_blk = (SEQ - keep) // tr
    blks_per_batch = SEQ // tr

    def row_blk(b, t):
        return b * blks_per_batch + first_blk + t

    def spec(layer, part):
        parked = row_blk(BATCH - 1, nt - 1) if layer == 0 else row_blk(0, 0)
        col0 = ((1 + part) * N_GROUPS + g) * N_HEADS
        return pl.BlockSpec((pl.Element(N_HEADS), pl.Element(tr + KV_PITCH_PAD), pl.Element(HEAD_DIM)),
                            lambda l, b, t: (col0, jnp.where(l == layer, row_blk(b, t), parked) * tr, 0))

    return pl.pallas_call(
        functools.partial(_kv_rows_kernel, tr=tr),
        grid=(2, BATCH, nt),
        in_specs=[spec(0, 0), spec(0, 1), spec(1, 0), spec(1, 1)],
        out_specs=pl.BlockSpec((None, None, tr, 2, N_HEADS, HEAD_DIM), lambda l, b, t: (l, b, t, 0, 0, 0)),
        out_shape=jax.ShapeDtypeStruct((2, BATCH, keep, 2, N_HEADS, HEAD_DIM), F32),
        compiler_params=pltpu.CompilerParams(
            dimension_semantics=("arbitrary",) * 3, vmem_limit_bytes=VMEM_LIMIT),
        name="kv_rows",
    )(proj_a, proj_a, proj_b, proj_b)


def _att_sample_kernel(x_ref, c0, c1, c2, bias_ref, a_ref, kv0, kv1, kv2):
    caches = (c0, c1, c2)
    kv_outs = (kv0, kv1, kv2)
    hb = N_HEADS
    outs, lses = [], []
    for g in range(N_GROUPS):
        q = x_ref[g * hb:(g + 1) * hb, :]
        kn = x_ref[(3 + g) * hb:(4 + g) * hb, :]
        vn = x_ref[(6 + g) * hb:(7 + g) * hb, :]
        kv_outs[g][0] = kn
        kv_outs[g][1] = vn
        kc = caches[g][:, 0]
        vc = caches[g][:, 1]
        s_c = jnp.sum(kc * q[None], axis=-1, keepdims=True) * ATT_SCALE + bias_ref[g, 0:BAND]
        s_n = jnp.sum(kn * q, axis=-1, keepdims=True) * ATT_SCALE + bias_ref[g, BAND]
        m = jnp.maximum(jnp.max(s_c, axis=0), s_n)
        p_c = jnp.exp(s_c - m[None])
        p_n = jnp.exp(s_n - m)
        l = jnp.sum(p_c, axis=0) + p_n
        o = (jnp.sum(p_c * vc, axis=0) + p_n * vn) / l
        outs.append(o)
        lses.append(m + jnp.log(l))
    mx = jnp.maximum(jnp.maximum(lses[0], lses[1]), lses[2])
    ws = [jnp.exp(ls - mx) for ls in lses]
    o = (ws[0] * outs[0] + ws[1] * outs[1] + ws[2] * outs[2]) / (ws[0] + ws[1] + ws[2])
    z = x_ref[9 * hb:10 * hb, :]
    a_ref[...] = o * _silu(z)


def _att_sample(proj_s, caches, bias_s, layer):
    hb = N_HEADS
    in_specs = [pl.BlockSpec((None, ATT_IN_COLS // HEAD_DIM, HEAD_DIM), lambda b: (b, 0, 0))]
    for g in range(N_GROUPS):
        in_specs.append(pl.BlockSpec((None, None, BAND, None, 2, hb, HEAD_DIM),
                                     lambda b: (layer, b, 0, 0, 0, 0, 0)))
    in_specs.append(pl.BlockSpec((N_GROUPS, BAND + 1, hb, 1), lambda b: (0, 0, 0, 0)))
    out_specs = [pl.BlockSpec((None, hb, HEAD_DIM), lambda b: (b, 0, 0))]
    out_shapes = [jax.ShapeDtypeStruct((DEC_BATCH, hb, HEAD_DIM), F32)]
    for g in range(N_GROUPS):
        out_specs.append(pl.BlockSpec((None, 2, hb, HEAD_DIM), lambda b: (b, 0, 0, 0)))
        out_shapes.append(jax.ShapeDtypeStruct((DEC_BATCH, 2, hb, HEAD_DIM), F32))
    return pl.pallas_call(
        _att_sample_kernel,
        grid=(DEC_BATCH,),
        in_specs=in_specs,
        out_specs=out_specs,
        out_shape=out_shapes,
        compiler_params=pltpu.CompilerParams(
            dimension_semantics=("arbitrary",), vmem_limit_bytes=VMEM_LIMIT),
        name="att_sample",
    )(proj_s, *caches, bias_s)


def kernel(x_prompt, x_sample, c_prompt, c_sample, cache_kv0, cache_kv1, cache_kv2, state_pool,
           norm_pre, norm_post, ada_w, ada_b, t5_bias, pool_w_in, pool_w_grp, pool_scale,
           pool_w_out, att_w_in, att_w_out):
    n_att = DEPTH // 2
    xp0 = x_prompt.reshape(M_PROMPT, D_MODEL)
    xs0 = jnp.zeros((SAMPLE_ROWS, D_MODEL), F32).at[0:DEC_BATCH].set(x_sample.reshape(DEC_BATCH, D_MODEL))
    c_all = jnp.zeros((C_ROWS, D_MODEL), F32)
    c_all = c_all.at[0:BATCH].set(c_prompt).at[C_SAMPLE_ROW0:C_SAMPLE_ROW0 + DEC_BATCH].set(c_sample)

    mod = _ada_all(c_all, ada_w, ada_b)
    gains_pre = norm_pre.reshape(DEPTH, 1, D_MODEL)
    gains_post = norm_post.reshape(DEPTH, 1, D_MODEL)

    bias_mats = _bias_matrices(t5_bias)
    bias_s = bias_mats[:, 0, 0:BAND + 1].reshape(N_GROUPS, N_HEADS, BAND + 1)
    bias_s = jnp.transpose(bias_s, (0, 2, 1))[..., None]
    caches = [c.reshape(n_att, DEC_BATCH, BAND, dil, 2, N_HEADS, HEAD_DIM)
              for c, dil in zip((cache_kv0, cache_kv1, cache_kv2), DILATIONS)]
    state_t = jnp.transpose(state_pool, (0, 2, 1, 3))

    kv_s = [[] for _ in range(N_GROUPS)]
    pool_p, pool_s, att_projs = [], [], []

    (h,) = _norm_step(xp0, None, mod, gains_post, gains_pre, None, 0, x_sample=xs0)
    x = None
    for i in range(DEPTH):
        li = i // 2
        if i % 2 == 0:
            proj = _matmul(h, pool_w_in, li, 2 * POOL_WIDTH, 1024, TM_MATMUL_K2048, F32, "pool_in_proj")
            a, pp, ps = _pool_mix(proj, state_t, pool_w_grp, pool_scale, li)
            pool_p.append(pp)
            pool_s.append(jnp.transpose(ps, (1, 0, 2)))
            y = _matmul(a, pool_w_out, li, D_MODEL, 512, TM_MATMUL, BF16, "pool_out_proj")
        else:
            proj = _matmul(h, att_w_in, li, ATT_IN_COLS, 1024, TM_MATMUL_K2048, F32, "att_in_proj", slabs=True)
            att_projs.append(proj)
            proj_s = jnp.transpose(proj[:, M_PROMPT:M_PROMPT + DEC_BATCH, :], (1, 0, 2))
            souts = _att_sample(proj_s, caches, bias_s, li)
            a_s = jnp.zeros((SAMPLE_ROWS, D_MODEL), F32).at[0:DEC_BATCH].set(souts[0].reshape(DEC_BATCH, D_MODEL))
            a = _att_prompt(proj, bias_mats, a_s.astype(BF16))
            for g in range(N_GROUPS):
                kv_s[g].append(souts[1 + g].reshape(DEC_BATCH, 1, 2, N_HEADS, HEAD_DIM))
            y = _matmul(a, att_w_out, li, D_MODEL, 1024, TM_MATMUL_K2048, BF16, "att_out_proj")
        last = i + 1 == DEPTH
        if i == 0:
            x, h = _norm_step(xp0, y, mod, gains_post, gains_pre, i, i + 1, x_sample=xs0)
        elif not last:
            x, h = _norm_step(x, y, mod, gains_post, gains_pre, i, i + 1)
        else:
            y_p, y_s = _norm_step(x, y, mod, gains_post, gains_pre, i, None, split_out=True)

    kv_p = [_kv_rows(att_projs[0], att_projs[1], g) for g in range(N_GROUPS)]
    y_prompt = y_p.reshape(BATCH, SEQ, D_MODEL)
    y_sample = y_s[0:DEC_BATCH].reshape(DEC_BATCH, 1, D_MODEL)
    return (y_prompt, y_sample, kv_p[0], kv_p[1], kv_p[2], jnp.stack(pool_p),
            jnp.stack(kv_s[0]), jnp.stack(kv_s[1]), jnp.stack(kv_s[2]), jnp.stack(pool_s))
```

```python
import functools

import numpy as np
import jax
import jax.numpy as jnp
from jax import lax
from jax.experimental import pallas as pl
from jax.experimental.pallas import tpu as pltpu

D_MODEL = 2048
BATCH = 4
SEQ = 2048
DEPTH = 4
DEC_BATCH = 8
HEAD_DIM = 128
N_HEADS = 16
DILATIONS = (1, 4, 16)
WINDOWS = (128, 512, 2048)
N_GROUPS = 3
QKV_WIDTH = N_GROUPS * D_MODEL
ATT_IN_COLS = 3 * QKV_WIDTH + D_MODEL
BAND = 128
ATT_SCALE = HEAD_DIM ** -0.5
LOG2E = 1.4426950408889634
POOL_WINDOWS = (2, 4, 8, 16)
POOL_WIDTH = 2 * D_MODEL
POOL_GROUP = POOL_WIDTH // 4
POOL_BUF = 15
N_BUCKETS = 32
T5_MAX_DIST = 2048
RMS_EPS = 1e-6
NEG_INF = -1e30

M_PROMPT = BATCH * SEQ
SAMPLE_ROWS = 64
M_ALL = M_PROMPT + SAMPLE_ROWS
C_ROWS = 2 * SAMPLE_ROWS
C_PROMPT_ROWS = 8
C_SAMPLE_ROW0 = SAMPLE_ROWS

TM_MATMUL = 2064
MXU_COLS = 256
TE_NORM = 512
TP_POOL = 1024
POOL_HALO = 24
ATT_UNITS = 16
N_BLOCKS = SEQ // BAND
TR_KV = 256
ROW_CHUNK = 256
KV_PITCH_PAD = 8
VMEM_LIMIT = 58 * 1024 * 1024

F32 = jnp.float32
BF16 = jnp.bfloat16


def _silu(x):
    half = 0.5 * x
    return half + half * jnp.tanh(half)


def _cast_rows_to_bf16(src_ref, dst_ref, rows, chunk=ROW_CHUNK):
    def body(i, c):
        r = pl.multiple_of(i * chunk, chunk)
        dst_ref[pl.ds(r, chunk), :] = src_ref[pl.ds(r, chunk), :].astype(BF16)
        return c
    lax.fori_loop(0, rows // chunk, body, 0)


def _ada_kernel(c_ref, w_ref, b_ref, o_ref):
    a = _silu(c_ref[...]).astype(BF16)
    kc = 512
    acc = jnp.zeros(o_ref.shape, F32)
    for k0 in range(0, D_MODEL, kc):
        acc = acc + jnp.dot(a[:, k0:k0 + kc], w_ref[k0:k0 + kc, :].astype(BF16),
                            preferred_element_type=F32)
    o_ref[...] = acc + b_ref[...]


def _ada_all(c_all, ada_w, ada_b):
    tn = 1024
    n = 3 * D_MODEL
    return pl.pallas_call(
        _ada_kernel,
        grid=(DEPTH, n // tn),
        in_specs=[pl.BlockSpec((C_ROWS, D_MODEL), lambda l, j: (0, 0)),
                  pl.BlockSpec((None, D_MODEL, tn), lambda l, j: (l, 0, j)),
                  pl.BlockSpec((None, 1, tn), lambda l, j: (l, 0, j))],
        out_specs=pl.BlockSpec((None, C_ROWS, tn), lambda l, j: (l, 0, j)),
        out_shape=jax.ShapeDtypeStruct((DEPTH, C_ROWS, n), F32),
        compiler_params=pltpu.CompilerParams(
            dimension_semantics=("arbitrary", "arbitrary"), vmem_limit_bytes=VMEM_LIMIT),
        name="ada_mod",
    )(c_all, ada_w, ada_b.reshape(DEPTH, 1, n))


def _t5_bucket(dist):
    dist = np.asarray(dist, dtype=np.int64)
    max_exact = N_BUCKETS // 2
    ratio = np.log(np.maximum(dist, 1) / max_exact) / np.log(T5_MAX_DIST / max_exact)
    large = np.minimum(max_exact + (ratio * (N_BUCKETS - max_exact)).astype(np.int64), N_BUCKETS - 1)
    return np.where(dist < max_exact, dist, large).astype(np.int32)


def _bucket_index_table():
    rel = np.arange(BAND)[:, None] + BAND - np.arange(2 * BAND)[None, :]
    inband = (rel >= 0) & (rel <= BAND)
    out = []
    for dil in DILATIONS:
        bucket = _t5_bucket(np.clip(rel, 0, BAND) * dil)
        out.append(np.where(inband, bucket, -1))
    return np.stack(out).astype(np.int32)


def _bias_kernel(tab_ref, idx_ref, o_ref):
    g = pl.program_id(0)
    idx = idx_ref[...]

    def head(h, c):
        acc = jnp.full(idx.shape, NEG_INF, F32)
        for b in range(N_BUCKETS):
            acc = jnp.where(idx == b, tab_ref[b, g * N_HEADS + h], acc)
        o_ref[h] = acc
        return c
    lax.fori_loop(0, N_HEADS, head, 0)


def _bias_matrices(t5_bias):
    idx = jnp.asarray(_bucket_index_table())
    n_sub = N_GROUPS * N_HEADS
    return pl.pallas_call(
        _bias_kernel,
        grid=(N_GROUPS,),
        in_specs=[pl.BlockSpec(memory_space=pltpu.SMEM),
                  pl.BlockSpec((None, BAND, 2 * BAND), lambda g: (g, 0, 0))],
        out_specs=pl.BlockSpec((N_HEADS, BAND, 2 * BAND), lambda g: (g, 0, 0)),
        out_shape=jax.ShapeDtypeStruct((n_sub, BAND, 2 * BAND), F32),
        name="t5_bias_mats",
    )(t5_bias, idx)


def _norm_kernel(*refs, has_post, has_pre, split_in, split_out, n_prompt_tiles, tiles_per_batch):
    refs = list(refs)
    xp_ref = refs.pop(0)
    xs_ref = refs.pop(0) if split_in else xp_ref
    if has_post:
        y_ref, gpost_ref, gate_p_ref, gate_s_ref = refs[:4]
        refs = refs[4:]
    if has_pre:
        gpre_ref, shift_p_ref, scale_p_ref, shift_s_ref, scale_s_ref = refs[:5]
        refs = refs[5:]
    if has_post:
        xop_ref = refs.pop(0)
        xos_ref = refs.pop(0) if split_out else xop_ref
    if has_pre:
        h_ref = refs.pop(0)

    def rms(v, g):
        return v * lax.rsqrt(jnp.mean(v * v, axis=-1, keepdims=True) + RMS_EPS) * g

    def body(x_ref, xo_ref, rows, gate, shift, scale):
        x = x_ref[rows, :]
        if has_post:
            x = x + gate * rms(y_ref[rows, :].astype(F32), gpost_ref[...])
            xo_ref[rows, :] = x
        if has_pre:
            h = rms(x, gpre_ref[...]) * (1.0 + scale) + shift
            h_ref[rows, :] = h.astype(BF16)

    t = pl.program_id(0)

    @pl.when(t < n_prompt_tiles)
    def _():
        row = pl.ds(t // tiles_per_batch, 1)
        body(xp_ref, xop_ref if has_post else None, slice(None),
             gate_p_ref[row, :] if has_post else None,
             shift_p_ref[row, :] if has_pre else None,
             scale_p_ref[row, :] if has_pre else None)

    @pl.when(t == n_prompt_tiles)
    def _():
        body(xs_ref, xos_ref if has_post else None, slice(0, SAMPLE_ROWS),
             gate_s_ref[...] if has_post else None,
             shift_s_ref[...] if has_pre else None,
             scale_s_ref[...] if has_pre else None)


def _norm_step(x, y, mod, norm_post, norm_pre, post_layer, pre_layer, x_sample=None, split_out=False):
    has_post = post_layer is not None
    has_pre = pre_layer is not None
    split_in = x_sample is not None
    te = TE_NORM
    npt = M_PROMPT // te
    tiles_per_batch = SEQ // te
    row_spec = pl.BlockSpec((te, D_MODEL), lambda t: (t, 0))
    prompt_row_spec = pl.BlockSpec((te, D_MODEL), lambda t: (jnp.minimum(t, npt - 1), 0))
    sample_row_spec = pl.BlockSpec((SAMPLE_ROWS, D_MODEL), lambda t: (0, 0))

    def mod_p_spec(layer, part):
        return pl.BlockSpec((None, C_PROMPT_ROWS, D_MODEL), lambda t: (layer, 0, part))

    def mod_s_spec(layer, part):
        return pl.BlockSpec((None, SAMPLE_ROWS, D_MODEL),
                            lambda t: (layer, C_SAMPLE_ROW0 // SAMPLE_ROWS, part))

    def gain_spec(layer):
        return pl.BlockSpec((None, 1, D_MODEL), lambda t: (layer, 0, 0))

    if split_in:
        args, in_specs = [x, x_sample], [prompt_row_spec, sample_row_spec]
    else:
        args, in_specs = [x], [row_spec]
    out_shapes, out_specs = [], []
    if has_post:
        args += [y, norm_post, mod, mod]
        in_specs += [row_spec, gain_spec(post_layer), mod_p_spec(post_layer, 2), mod_s_spec(post_layer, 2)]
        if split_out:
            out_shapes += [jax.ShapeDtypeStruct((M_PROMPT, D_MODEL), F32),
                           jax.ShapeDtypeStruct((SAMPLE_ROWS, D_MODEL), F32)]
            out_specs += [prompt_row_spec, sample_row_spec]
        else:
            out_shapes.append(jax.ShapeDtypeStruct((M_ALL, D_MODEL), F32))
            out_specs.append(row_spec)
    if has_pre:
        args += [norm_pre, mod, mod, mod, mod]
        in_specs += [gain_spec(pre_layer), mod_p_spec(pre_layer, 0), mod_p_spec(pre_layer, 1),
                     mod_s_spec(pre_layer, 0), mod_s_spec(pre_layer, 1)]
        out_shapes.append(jax.ShapeDtypeStruct((M_ALL, D_MODEL), BF16))
        out_specs.append(row_spec)
    outs = pl.pallas_call(
        functools.partial(_norm_kernel, has_post=has_post, has_pre=has_pre, split_in=split_in,
                          split_out=split_out, n_prompt_tiles=npt, tiles_per_batch=tiles_per_batch),
        grid=(npt + 1,),
        in_specs=in_specs,
        out_specs=out_specs,
        out_shape=out_shapes,
        compiler_params=pltpu.CompilerParams(
            dimension_semantics=("arbitrary",), vmem_limit_bytes=VMEM_LIMIT),
        name="norm_step",
    )(*args)
    return outs


def _mm_kernel(a_ref, w_ref, o_ref, *, slabs):
    tn = w_ref.shape[1]
    for c0 in range(0, tn, MXU_COLS):
        acc = jnp.dot(a_ref[...], w_ref[:, c0:c0 + MXU_COLS].astype(BF16),
                      preferred_element_type=F32).astype(o_ref.dtype)
        if slabs:
            for c in range(MXU_COLS // HEAD_DIM):
                o_ref[c0 // HEAD_DIM + c] = acc[:, c * HEAD_DIM:(c + 1) * HEAD_DIM]
        else:
            o_ref[:, c0:c0 + MXU_COLS] = acc


def _matmul(a, w, layer, n_out, tn, tm, out_dtype, name, slabs=False):
    m, k = a.shape
    if slabs:
        out_spec = pl.BlockSpec((tn // HEAD_DIM, tm, HEAD_DIM), lambda j, i: (j, i, 0))
        out_shape = jax.ShapeDtypeStruct((n_out // HEAD_DIM, m, HEAD_DIM), out_dtype)
    else:
        out_spec = pl.BlockSpec((tm, tn), lambda j, i: (i, j))
        out_shape = jax.ShapeDtypeStruct((m, n_out), out_dtype)
    return pl.pallas_call(
        functools.partial(_mm_kernel, slabs=slabs),
        grid=(n_out // tn, m // tm),
        in_specs=[pl.BlockSpec((tm, k), lambda j, i: (i, 0)),
                  pl.BlockSpec((None, k, tn), lambda j, i: (layer, 0, j))],
        out_specs=out_spec,
        out_shape=out_shape,
        compiler_params=pltpu.CompilerParams(
            dimension_semantics=("arbitrary", "arbitrary"), vmem_limit_bytes=VMEM_LIMIT),
        name=name,
    )(a, w)


def _pool_kernel(u_ref, halo_ref, z_ref, st_ref, w_ref, sc_ref,
                 a_ref, pp_ref, ps_ref, wb_ref, buf_a, buf_b, *, n_prompt_tiles, tiles_per_batch):
    g = pl.program_id(0)
    t = pl.program_id(1)
    tp = TP_POOL
    h0 = POOL_HALO

    @pl.when(t == 0)
    def _():
        _cast_rows_to_bf16(w_ref, wb_ref, POOL_GROUP)

    def finish(r, z, rows):
        y = jnp.dot(r.astype(BF16), wb_ref[...], preferred_element_type=F32) * sc_ref[...]
        a_ref[rows, :] = (y * _silu(z)).astype(BF16)

    def prompt_tile(n_steps):
        w = 2 ** n_steps
        first = (t % tiles_per_batch) == 0
        buf_a[0:8, :] = jnp.zeros((8, POOL_GROUP), F32)
        buf_b[0:8, :] = jnp.zeros((8, POOL_GROUP), F32)
        buf_a[8:h0, :] = jnp.where(first, 0.0, halo_ref[...])
        buf_a[h0:h0 + tp, :] = u_ref[...]
        src, dst = buf_a, buf_b
        n = tp + h0 - 8
        for s in range(n_steps):
            sh = 2 ** s
            dst[8:8 + n, :] = src[8:8 + n, :] + src[8 - sh:8 - sh + n, :]
            src, dst = dst, src
        pos = (t % tiles_per_batch) * tp + lax.broadcasted_iota(jnp.int32, (tp, 1), 0)
        inv_cnt = 1.0 / jnp.minimum(pos + 1, w).astype(F32)
        u = u_ref[...]
        r = src[h0:h0 + tp, :] * inv_cnt - u
        finish(r, z_ref[...], slice(None))

        @pl.when((t % tiles_per_batch) == tiles_per_batch - 1)
        def _():
            pp_ref[...] = u_ref[tp - POOL_BUF:tp, :]

    def sample_tile(n_steps):
        w = 2 ** n_steps
        u_new = u_ref[0:DEC_BATCH, :]
        acc = u_new
        for k in range(1, w):
            acc = acc + st_ref[POOL_BUF - k]
        r = acc / float(w) - u_new
        buf_a[0:DEC_BATCH, :] = r
        buf_a[DEC_BATCH:SAMPLE_ROWS, :] = jnp.zeros((SAMPLE_ROWS - DEC_BATCH, POOL_GROUP), F32)
        finish(buf_a[0:SAMPLE_ROWS, :], z_ref[0:SAMPLE_ROWS, :], slice(0, SAMPLE_ROWS))
        for k in range(POOL_BUF - 1):
            ps_ref[k] = st_ref[k + 1]
        ps_ref[POOL_BUF - 1] = u_new

    for gi in range(len(POOL_WINDOWS)):
        @pl.when((g == gi) & (t < n_prompt_tiles))
        def _(gi=gi):
            prompt_tile(gi + 1)

        @pl.when((g == gi) & (t == n_prompt_tiles))
        def _(gi=gi):
            sample_tile(gi + 1)


def _pool_mix(proj, state_t, w_grp, scale, layer):
    tp = TP_POOL
    npt = M_PROMPT // tp
    tpb = SEQ // tp
    ng = len(POOL_WINDOWS)
    halo_blocks = tp // 16
    outs = pl.pallas_call(
        functools.partial(_pool_kernel, n_prompt_tiles=npt, tiles_per_batch=tpb),
        grid=(ng, npt + 1),
        in_specs=[
            pl.BlockSpec((tp, POOL_GROUP), lambda g, t: (t, g)),
            pl.BlockSpec((16, POOL_GROUP), lambda g, t: (jnp.maximum(t * halo_blocks - 1, 0), g)),
            pl.BlockSpec((tp, POOL_GROUP), lambda g, t: (t, ng + g)),
            pl.BlockSpec((None, POOL_BUF, DEC_BATCH, POOL_GROUP), lambda g, t: (layer, 0, 0, g)),
            pl.BlockSpec((None, None, POOL_GROUP, POOL_GROUP), lambda g, t: (layer, g, 0, 0)),
            pl.BlockSpec((None, 1, POOL_GROUP), lambda g, t: (layer, 0, g)),
        ],
        out_specs=[
            pl.BlockSpec((tp, POOL_GROUP), lambda g, t: (t, g)),
            pl.BlockSpec((None, POOL_BUF, POOL_GROUP),
                         lambda g, t: (jnp.minimum(t // tpb, BATCH - 1), 0, g)),
            pl.BlockSpec((POOL_BUF, DEC_BATCH, POOL_GROUP), lambda g, t: (0, 0, g)),
        ],
        out_shape=[
            jax.ShapeDtypeStruct((M_ALL, POOL_WIDTH), BF16),
            jax.ShapeDtypeStruct((BATCH, POOL_BUF, POOL_WIDTH), F32),
            jax.ShapeDtypeStruct((POOL_BUF, DEC_BATCH, POOL_WIDTH), F32),
        ],
        scratch_shapes=[pltpu.VMEM((POOL_GROUP, POOL_GROUP), BF16),
                        pltpu.VMEM((tp + POOL_HALO, POOL_GROUP), F32),
                        pltpu.VMEM((tp + POOL_HALO, POOL_GROUP), F32)],
        compiler_params=pltpu.CompilerParams(
            dimension_semantics=("arbitrary", "arbitrary"), vmem_limit_bytes=VMEM_LIMIT),
        name="pool_mix",
    )(proj, proj, proj, state_t, w_grp, scale.reshape(-1, 1, POOL_WIDTH))
    return outs


def _att_prompt_kernel(x_ref, bias_ref, as_ref, a_ref, *scratch):
    @pl.when(pl.program_id(0) < BATCH)
    def _():
        _att_prompt_tile(x_ref, bias_ref, a_ref, *scratch)

    @pl.when(pl.program_id(0) == BATCH)
    def _():
        a_ref[0:SAMPLE_ROWS, :] = as_ref[...]


def _att_prompt_tile(x_ref, bias_ref, a_ref, qd, kd, vd, od, ld, on, ln, tmp3):
    tmp = tmp3.at[0]
    q_refs = tuple(x_ref.at[g] for g in range(N_GROUPS))
    k_refs = tuple(x_ref.at[N_GROUPS + g] for g in range(N_GROUPS))
    v_refs = tuple(x_ref.at[2 * N_GROUPS + g] for g in range(N_GROUPS))
    z_ref = x_ref.at[3 * N_GROUPS]
    bias_refs = tuple(bias_ref.at[g] for g in range(N_GROUPS))
    nu = ATT_UNITS

    for g, dil in enumerate(DILATIONS):
        n = SEQ // dil
        nb = n // BAND
        zero_blk = jnp.zeros((BAND, HEAD_DIM), BF16)
        kd[g, 0:BAND, :] = zero_blk
        vd[g, 0:BAND, :] = zero_blk
        for src, dst, off, mul in ((q_refs[g], qd, 0, ATT_SCALE * LOG2E), (k_refs[g], kd, BAND, None),
                                   (v_refs[g], vd, BAND, None)):
            def to_bf16(x, mul=mul):
                return (x if mul is None else x * mul).astype(BF16)

            if dil == 1:
                dst[g, off:off + SEQ, :] = to_bf16(src[...])
            elif dil == 4:
                for r in range(dil):
                    dst[g, off + r * n:off + (r + 1) * n, :] = to_bf16(src[pl.ds(r, n, stride=dil), :])
            else:
                stage = tmp3.at[(off > 0) + (dst is vd)]
                quarter = SEQ // 4
                for r_lo in range(4):
                    stage[r_lo * quarter:(r_lo + 1) * quarter, :] = src[pl.ds(r_lo, quarter, stride=4), :]
                for r in range(dil):
                    r_lo, r_hi = r % 4, r // 4
                    dst[g, off + r * n:off + (r + 1) * n, :] = to_bf16(
                        stage[pl.ds(r_lo * quarter + r_hi, n, stride=4), :])

    def unit_batch(g, u0, seq_blocks):
        with_prev = seq_blocks > 1
        rows = nu * BAND
        row = pl.multiple_of(u0 * BAND, rows)
        shape3 = (nu, BAND, HEAD_DIM)
        q = qd[g, pl.ds(row, rows), :].reshape(shape3)
        kc = kd[g, pl.ds(row + BAND, rows), :].reshape(shape3)
        vc = vd[g, pl.ds(row + BAND, rows), :].reshape(shape3)
        bias = bias_refs[g]
        s_c = jnp.einsum('uqe,uke->uqk', q, kc, preferred_element_type=F32) \
            + (bias[:, BAND:2 * BAND] * LOG2E)[None]
        if with_prev:
            kp = kd[g, pl.ds(row, rows), :].reshape(shape3)
            vp = vd[g, pl.ds(row, rows), :].reshape(shape3)
            bias_p = jnp.broadcast_to((bias[:, 0:BAND] * LOG2E)[None], (nu, BAND, BAND))
            blk = u0 + lax.broadcasted_iota(jnp.int32, (nu, BAND, BAND), 0)
            bias_p = jnp.where((blk & (seq_blocks - 1)) == 0, NEG_INF, bias_p)
            s_p = jnp.einsum('uqe,uke->uqk', q, kp, preferred_element_type=F32) + bias_p
            m = jnp.max(jnp.maximum(s_c, s_p), axis=-1, keepdims=True)
            p_c = jnp.exp2(s_c - m)
            p_p = jnp.exp2(s_p - m)
            l = jnp.sum(p_c + p_p, axis=-1, keepdims=True)
            acc = jnp.einsum('uqk,uke->uqe', p_c.astype(BF16), vc, preferred_element_type=F32) \
                + jnp.einsum('uqk,uke->uqe', p_p.astype(BF16), vp, preferred_element_type=F32)
        else:
            m = jnp.max(s_c, axis=-1, keepdims=True)
            p_c = jnp.exp2(s_c - m)
            l = jnp.sum(p_c, axis=-1, keepdims=True)
            acc = jnp.einsum('uqk,uke->uqe', p_c.astype(BF16), vc, preferred_element_type=F32)
        od[g, pl.ds(row, rows), :] = (acc / l).reshape(rows, HEAD_DIM)
        ld[g, pl.ds(row, rows), :] = jnp.broadcast_to(m + jnp.log(l) * LOG2E, shape3).reshape(rows, HEAD_DIM)

    for g, dil in enumerate(DILATIONS):
        nb = (SEQ // dil) // BAND

        def batch_body(i, c, g=g, nb=nb):
            unit_batch(g, i * nu, nb)
            return c
        lax.fori_loop(0, N_BLOCKS // nu, batch_body, 0)

    def interleave4(src, dst, n_rows):
        quarter = n_rows // 4
        for r in range(4):
            dst[pl.ds(r, quarter, stride=4), :] = src[r * quarter:(r + 1) * quarter, :]

    for src_all, dst_all in ((od, on), (ld, ln)):
        interleave4(src_all.at[1], dst_all.at[0], SEQ)
        for r_lo in range(4):
            for r_hi in range(4):
                r = r_lo + 4 * r_hi
                tmp[pl.ds(r_lo * (SEQ // 4) + r_hi, BAND, stride=4), :] = src_all[2, r * BAND:(r + 1) * BAND, :]
        interleave4(tmp, dst_all.at[1], SEQ)

    def comb(i, c):
        rs = pl.ds(pl.multiple_of(i * ROW_CHUNK, ROW_CHUNK), ROW_CHUNK)
        l0, l1, l2 = ld[0, rs, :], ln[0, rs, :], ln[1, rs, :]
        mx = jnp.maximum(jnp.maximum(l0, l1), l2)
        w0, w1, w2 = jnp.exp2(l0 - mx), jnp.exp2(l1 - mx), jnp.exp2(l2 - mx)
        o = (w0 * od[0, rs, :] + w1 * on[0, rs, :] + w2 * on[1, rs, :]) / (w0 + w1 + w2)
        a_ref[rs, :] = (o * _silu(z_ref[rs, :])).astype(BF16)
        return c
    lax.fori_loop(0, SEQ // ROW_CHUNK, comb, 0)


def _att_prompt(proj, bias_mats, a_sample):
    hb = N_HEADS

    def head_blk(b, h):
        return jnp.where(b < BATCH, h, hb - 1)

    n_pg = ATT_IN_COLS // D_MODEL
    in_specs = [
        pl.BlockSpec((n_pg, None, SEQ, HEAD_DIM),
                     lambda b, h: (0, head_blk(b, h), jnp.minimum(b, BATCH - 1), 0)),
        pl.BlockSpec((N_GROUPS, None, BAND, 2 * BAND), lambda b, h: (0, head_blk(b, h), 0, 0)),
        pl.BlockSpec((SAMPLE_ROWS, HEAD_DIM), lambda b, h: (0, h)),
    ]
    scratch = [pltpu.VMEM((N_GROUPS, SEQ, HEAD_DIM), BF16)] + \
              [pltpu.VMEM((N_GROUPS, SEQ + BAND, HEAD_DIM), BF16)] * 2 + \
              [pltpu.VMEM((N_GROUPS, SEQ, HEAD_DIM), F32)] * 2 + \
              [pltpu.VMEM((N_GROUPS - 1, SEQ, HEAD_DIM), F32)] * 2 + \
              [pltpu.VMEM((3, SEQ, HEAD_DIM), F32)]
    return pl.pallas_call(
        _att_prompt_kernel,
        grid=(BATCH + 1, N_HEADS),
        in_specs=in_specs,
        out_specs=pl.BlockSpec((SEQ, HEAD_DIM), lambda b, h: (b, h)),
        out_shape=jax.ShapeDtypeStruct((M_ALL, D_MODEL), BF16),
        scratch_shapes=scratch,
        compiler_params=pltpu.CompilerParams(
            dimension_semantics=("arbitrary", "arbitrary"), vmem_limit_bytes=VMEM_LIMIT),
        name="att_prompt",
    )(proj.reshape(n_pg, hb, M_ALL, HEAD_DIM), bias_mats.reshape(N_GROUPS, hb, BAND, 2 * BAND), a_sample)


def _kv_rows_kernel(k_a, v_a, k_b, v_b, o_ref, *, tr):
    pitch = tr + KV_PITCH_PAD

    def emit(srcs):
        flats = [src.reshape(N_HEADS * pitch, HEAD_DIM) for src in srcs]

        def body(i, c):
            t0 = i * 8
            for k in range(8):
                for part, flat in enumerate(flats):
                    o_ref[t0 + k, part] = flat[pl.ds(t0 + k, N_HEADS, stride=pitch), :]
            return c
        lax.fori_loop(0, tr // 8, body, 0)

    @pl.when(pl.program_id(0) == 0)
    def _():
        emit((k_a, v_a))

    @pl.when(pl.program_id(0) == 1)
    def _():
        emit((k_b, v_b))


def _kv_rows(proj_a, proj_b, g):
    keep = WINDOWS[g]
    tr = min(keep, TR_KV)
    nt = keep // tr
    first_blk = (SEQ - keep) // tr
    blks_per_batch = SEQ // tr

    def row_blk(b, t):
        return b * blks_per_batch + first_blk + t

    def spec(layer, part):
        parked = row_blk(BATCH - 1, nt - 1) if layer == 0 else row_blk(0, 0)
        col0 = ((1 + part) * N_GROUPS + g) * N_HEADS
        return pl.BlockSpec((pl.Element(N_HEADS), pl.Element(tr + KV_PITCH_PAD), pl.Element(HEAD_DIM)),
                            lambda l, b, t: (col0, jnp.where(l == layer, row_blk(b, t), parked) * tr, 0))

    return pl.pallas_call(
        functools.partial(_kv_rows_kernel, tr=tr),
        grid=(2, BATCH, nt),
        in_specs=[spec(0, 0), spec(0, 1), spec(1, 0), spec(1, 1)],
        out_specs=pl.BlockSpec((None, None, tr, 2, N_HEADS, HEAD_DIM), lambda l, b, t: (l, b, t, 0, 0, 0)),
        out_shape=jax.ShapeDtypeStruct((2, BATCH, keep, 2, N_HEADS, HEAD_DIM), F32),
        compiler_params=pltpu.CompilerParams(
            dimension_semantics=("arbitrary",) * 3, vmem_limit_bytes=VMEM_LIMIT),
        name="kv_rows",
    )(proj_a, proj_a, proj_b, proj_b)


def _att_sample_kernel(x_ref, c0, c1, c2, bias_ref, a_ref, kv0, kv1, kv2):
    caches = (c0, c1, c2)
    kv_outs = (kv0, kv1, kv2)
    hb = N_HEADS
    outs, lses = [], []
    for g in range(N_GROUPS):
        q = x_ref[g * hb:(g + 1) * hb, :]
        kn = x_ref[(3 + g) * hb:(4 + g) * hb, :]
        vn = x_ref[(6 + g) * hb:(7 + g) * hb, :]
        kv_outs[g][0] = kn
        kv_outs[g][1] = vn
        kc = caches[g][:, 0]
        vc = caches[g][:, 1]
        s_c = jnp.sum(kc * q[None], axis=-1, keepdims=True) * ATT_SCALE + bias_ref[g, 0:BAND]
        s_n = jnp.sum(kn * q, axis=-1, keepdims=True) * ATT_SCALE + bias_ref[g, BAND]
        m = jnp.maximum(jnp.max(s_c, axis=0), s_n)
        p_c = jnp.exp(s_c - m[None])
        p_n = jnp.exp(s_n - m)
        l = jnp.sum(p_c, axis=0) + p_n
        o = (jnp.sum(p_c * vc, axis=0) + p_n * vn) / l
        outs.append(o)
        lses.append(m + jnp.log(l))
    mx = jnp.maximum(jnp.maximum(lses[0], lses[1]), lses[2])
    ws = [jnp.exp(ls - mx) for ls in lses]
    o = (ws[0] * outs[0] + ws[1] * outs[1] + ws[2] * outs[2]) / (ws[0] + ws[1] + ws[2])
    z = x_ref[9 * hb:10 * hb, :]
    a_ref[...] = o * _silu(z)


def _att_sample(proj_s, caches, bias_s, layer):
    hb = N_HEADS
    in_specs = [pl.BlockSpec((None, ATT_IN_COLS // HEAD_DIM, HEAD_DIM), lambda b: (b, 0, 0))]
    for g in range(N_GROUPS):
        in_specs.append(pl.BlockSpec((None, None, BAND, None, 2, hb, HEAD_DIM),
                                     lambda b: (layer, b, 0, 0, 0, 0, 0)))
    in_specs.append(pl.BlockSpec((N_GROUPS, BAND + 1, hb, 1), lambda b: (0, 0, 0, 0)))
    out_specs = [pl.BlockSpec((None, hb, HEAD_DIM), lambda b: (b, 0, 0))]
    out_shapes = [jax.ShapeDtypeStruct((DEC_BATCH, hb, HEAD_DIM), F32)]
    for g in range(N_GROUPS):
        out_specs.append(pl.BlockSpec((None, 2, hb, HEAD_DIM), lambda b: (b, 0, 0, 0)))
        out_shapes.append(jax.ShapeDtypeStruct((DEC_BATCH, 2, hb, HEAD_DIM), F32))
    return pl.pallas_call(
        _att_sample_kernel,
        grid=(DEC_BATCH,),
        in_specs=in_specs,
        out_specs=out_specs,
        out_shape=out_shapes,
        compiler_params=pltpu.CompilerParams(
            dimension_semantics=("arbitrary",), vmem_limit_bytes=VMEM_LIMIT),
        name="att_sample",
    )(proj_s, *caches, bias_s)


def kernel(x_prompt, x_sample, c_prompt, c_sample, cache_kv0, cache_kv1, cache_kv2, state_pool,
           norm_pre, norm_post, ada_w, ada_b, t5_bias, pool_w_in, pool_w_grp, pool_scale,
           pool_w_out, att_w_in, att_w_out):
    n_att = DEPTH // 2
    xp0 = x_prompt.reshape(M_PROMPT, D_MODEL)
    xs0 = jnp.zeros((SAMPLE_ROWS, D_MODEL), F32).at[0:DEC_BATCH].set(x_sample.reshape(DEC_BATCH, D_MODEL))
    c_all = jnp.zeros((C_ROWS, D_MODEL), F32)
    c_all = c_all.at[0:BATCH].set(c_prompt).at[C_SAMPLE_ROW0:C_SAMPLE_ROW0 + DEC_BATCH].set(c_sample)

    mod = _ada_all(c_all, ada_w, ada_b)
    gains_pre = norm_pre.reshape(DEPTH, 1, D_MODEL)
    gains_post = norm_post.reshape(DEPTH, 1, D_MODEL)

    bias_mats = _bias_matrices(t5_bias)
    bias_s = bias_mats[:, 0, 0:BAND + 1].reshape(N_GROUPS, N_HEADS, BAND + 1)
    bias_s = jnp.transpose(bias_s, (0, 2, 1))[..., None]
    caches = [c.reshape(n_att, DEC_BATCH, BAND, dil, 2, N_HEADS, HEAD_DIM)
              for c, dil in zip((cache_kv0, cache_kv1, cache_kv2), DILATIONS)]
    state_t = jnp.transpose(state_pool, (0, 2, 1, 3))

    kv_s = [[] for _ in range(N_GROUPS)]
    pool_p, pool_s, att_projs = [], [], []

    (h,) = _norm_step(xp0, None, mod, gains_post, gains_pre, None, 0, x_sample=xs0)
    x = None
    for i in range(DEPTH):
        li = i // 2
        if i % 2 == 0:
            proj = _matmul(h, pool_w_in, li, 2 * POOL_WIDTH, 1024, TM_MATMUL, F32, "pool_in_proj")
            a, pp, ps = _pool_mix(proj, state_t, pool_w_grp, pool_scale, li)
            pool_p.append(pp)
            pool_s.append(jnp.transpose(ps, (1, 0, 2)))
            y = _matmul(a, pool_w_out, li, D_MODEL, 512, TM_MATMUL, BF16, "pool_out_proj")
        else:
            proj = _matmul(h, att_w_in, li, ATT_IN_COLS, 1024, TM_MATMUL, F32, "att_in_proj", slabs=True)
            att_projs.append(proj)
            proj_s = jnp.transpose(proj[:, M_PROMPT:M_PROMPT + DEC_BATCH, :], (1, 0, 2))
            souts = _att_sample(proj_s, caches, bias_s, li)
            a_s = jnp.zeros((SAMPLE_ROWS, D_MODEL), F32).at[0:DEC_BATCH].set(souts[0].reshape(DEC_BATCH, D_MODEL))
            a = _att_prompt(proj, bias_mats, a_s.astype(BF16))
            for g in range(N_GROUPS):
                kv_s[g].append(souts[1 + g].reshape(DEC_BATCH, 1, 2, N_HEADS, HEAD_DIM))
            y = _matmul(a, att_w_out, li, D_MODEL, 1024, TM_MATMUL, BF16, "att_out_proj")
        last = i + 1 == DEPTH
        if i == 0:
            x, h = _norm_step(xp0, y, mod, gains_post, gains_pre, i, i + 1, x_sample=xs0)
        elif not last:
            x, h = _norm_step(x, y, mod, gains_post, gains_pre, i, i + 1)
        else:
            y_p, y_s = _norm_step(x, y, mod, gains_post, gains_pre, i, None, split_out=True)

    kv_p = [_kv_rows(att_projs[0], att_projs[1], g) for g in range(N_GROUPS)]
    y_prompt = y_p.reshape(BATCH, SEQ, D_MODEL)
    y_sample = y_s[0:DEC_BATCH].reshape(DEC_BATCH, 1, D_MODEL)
    return (y_prompt, y_sample, kv_p[0], kv_p[1], kv_p[2], jnp.stack(pool_p),
            jnp.stack(kv_s[0]), jnp.stack(kv_s[1]), jnp.stack(kv_s[2]), jnp.stack(pool_s))
```

```python
import functools

import numpy as np
import jax
import jax.numpy as jnp
from jax import lax
from jax.experimental import pallas as pl
from jax.experimental.pallas import tpu as pltpu

D_MODEL = 2048
BATCH = 4
SEQ = 2048
DEPTH = 4
DEC_BATCH = 8
HEAD_DIM = 128
N_HEADS = 16
DILATIONS = (1, 4, 16)
WINDOWS = (128, 512, 2048)
N_GROUPS = 3
QKV_WIDTH = N_GROUPS * D_MODEL
ATT_IN_COLS = 3 * QKV_WIDTH + D_MODEL
BAND = 128
ATT_SCALE = HEAD_DIM ** -0.5
LOG2E = 1.4426950408889634
POOL_WINDOWS = (2, 4, 8, 16)
POOL_WIDTH = 2 * D_MODEL
POOL_GROUP = POOL_WIDTH // 4
POOL_BUF = 15
N_BUCKETS = 32
T5_MAX_DIST = 2048
RMS_EPS = 1e-6
NEG_INF = -1e30

M_PROMPT = BATCH * SEQ
SAMPLE_ROWS = 64
M_ALL = M_PROMPT + SAMPLE_ROWS
C_ROWS = 2 * SAMPLE_ROWS
C_PROMPT_ROWS = 8
C_SAMPLE_ROW0 = SAMPLE_ROWS

TM_MATMUL = 2064
MXU_COLS = 256
TE_NORM = 512
TP_POOL = 1024
POOL_HALO = 24
ATT_UNITS = 8
N_BLOCKS = SEQ // BAND
TR_KV = 256
ROW_CHUNK = 256
KV_PITCH_PAD = 8
VMEM_LIMIT = 58 * 1024 * 1024

F32 = jnp.float32
BF16 = jnp.bfloat16


def _silu(x):
    half = 0.5 * x
    return half + half * jnp.tanh(half)


def _cast_rows_to_bf16(src_ref, dst_ref, rows, chunk=ROW_CHUNK):
    def body(i, c):
        r = pl.multiple_of(i * chunk, chunk)
        dst_ref[pl.ds(r, chunk), :] = src_ref[pl.ds(r, chunk), :].astype(BF16)
        return c
    lax.fori_loop(0, rows // chunk, body, 0)


def _ada_kernel(c_ref, w_ref, b_ref, o_ref):
    a = _silu(c_ref[...]).astype(BF16)
    kc = 512
    acc = jnp.zeros(o_ref.shape, F32)
    for k0 in range(0, D_MODEL, kc):
        acc = acc + jnp.dot(a[:, k0:k0 + kc], w_ref[k0:k0 + kc, :].astype(BF16),
                            preferred_element_type=F32)
    o_ref[...] = acc + b_ref[...]


def _ada_all(c_all, ada_w, ada_b):
    tn = 1024
    n = 3 * D_MODEL
    return pl.pallas_call(
        _ada_kernel,
        grid=(DEPTH, n // tn),
        in_specs=[pl.BlockSpec((C_ROWS, D_MODEL), lambda l, j: (0, 0)),
                  pl.BlockSpec((None, D_MODEL, tn), lambda l, j: (l, 0, j)),
                  pl.BlockSpec((None, 1, tn), lambda l, j: (l, 0, j))],
        out_specs=pl.BlockSpec((None, C_ROWS, tn), lambda l, j: (l, 0, j)),
        out_shape=jax.ShapeDtypeStruct((DEPTH, C_ROWS, n), F32),
        compiler_params=pltpu.CompilerParams(
            dimension_semantics=("arbitrary", "arbitrary"), vmem_limit_bytes=VMEM_LIMIT),
        name="ada_mod",
    )(c_all, ada_w, ada_b.reshape(DEPTH, 1, n))


def _t5_bucket(dist):
    dist = np.asarray(dist, dtype=np.int64)
    max_exact = N_BUCKETS // 2
    ratio = np.log(np.maximum(dist, 1) / max_exact) / np.log(T5_MAX_DIST / max_exact)
    large = np.minimum(max_exact + (ratio * (N_BUCKETS - max_exact)).astype(np.int64), N_BUCKETS - 1)
    return np.where(dist < max_exact, dist, large).astype(np.int32)


def _bucket_index_table():
    rel = np.arange(BAND)[:, None] + BAND - np.arange(2 * BAND)[None, :]
    inband = (rel >= 0) & (rel <= BAND)
    out = []
    for dil in DILATIONS:
        bucket = _t5_bucket(np.clip(rel, 0, BAND) * dil)
        out.append(np.where(inband, bucket, -1))
    return np.stack(out).astype(np.int32)


def _bias_kernel(tab_ref, idx_ref, o_ref):
    g = pl.program_id(0)
    idx = idx_ref[...]

    def head(h, c):
        acc = jnp.full(idx.shape, NEG_INF, F32)
        for b in range(N_BUCKETS):
            acc = jnp.where(idx == b, tab_ref[b, g * N_HEADS + h], acc)
        o_ref[h] = acc
        return c
    lax.fori_loop(0, N_HEADS, head, 0)


def _bias_matrices(t5_bias):
    idx = jnp.asarray(_bucket_index_table())
    n_sub = N_GROUPS * N_HEADS
    return pl.pallas_call(
        _bias_kernel,
        grid=(N_GROUPS,),
        in_specs=[pl.BlockSpec(memory_space=pltpu.SMEM),
                  pl.BlockSpec((None, BAND, 2 * BAND), lambda g: (g, 0, 0))],
        out_specs=pl.BlockSpec((N_HEADS, BAND, 2 * BAND), lambda g: (g, 0, 0)),
        out_shape=jax.ShapeDtypeStruct((n_sub, BAND, 2 * BAND), F32),
        name="t5_bias_mats",
    )(t5_bias, idx)


def _norm_kernel(*refs, has_post, has_pre, split_in, split_out, n_prompt_tiles, tiles_per_batch):
    refs = list(refs)
    xp_ref = refs.pop(0)
    xs_ref = refs.pop(0) if split_in else xp_ref
    if has_post:
        y_ref, gpost_ref, gate_p_ref, gate_s_ref = refs[:4]
        refs = refs[4:]
    if has_pre:
        gpre_ref, shift_p_ref, scale_p_ref, shift_s_ref, scale_s_ref = refs[:5]
        refs = refs[5:]
    if has_post:
        xop_ref = refs.pop(0)
        xos_ref = refs.pop(0) if split_out else xop_ref
    if has_pre:
        h_ref = refs.pop(0)

    def rms(v, g):
        return v * lax.rsqrt(jnp.mean(v * v, axis=-1, keepdims=True) + RMS_EPS) * g

    def body(x_ref, xo_ref, rows, gate, shift, scale):
        x = x_ref[rows, :]
        if has_post:
            x = x + gate * rms(y_ref[rows, :].astype(F32), gpost_ref[...])
            xo_ref[rows, :] = x
        if has_pre:
            h = rms(x, gpre_ref[...]) * (1.0 + scale) + shift
            h_ref[rows, :] = h.astype(BF16)

    t = pl.program_id(0)

    @pl.when(t < n_prompt_tiles)
    def _():
        row = pl.ds(t // tiles_per_batch, 1)
        body(xp_ref, xop_ref if has_post else None, slice(None),
             gate_p_ref[row, :] if has_post else None,
             shift_p_ref[row, :] if has_pre else None,
             scale_p_ref[row, :] if has_pre else None)

    @pl.when(t == n_prompt_tiles)
    def _():
        body(xs_ref, xos_ref if has_post else None, slice(0, SAMPLE_ROWS),
             gate_s_ref[...] if has_post else None,
             shift_s_ref[...] if has_pre else None,
             scale_s_ref[...] if has_pre else None)


def _norm_step(x, y, mod, norm_post, norm_pre, post_layer, pre_layer, x_sample=None, split_out=False):
    has_post = post_layer is not None
    has_pre = pre_layer is not None
    split_in = x_sample is not None
    te = TE_NORM
    npt = M_PROMPT // te
    tiles_per_batch = SEQ // te
    row_spec = pl.BlockSpec((te, D_MODEL), lambda t: (t, 0))
    prompt_row_spec = pl.BlockSpec((te, D_MODEL), lambda t: (jnp.minimum(t, npt - 1), 0))
    sample_row_spec = pl.BlockSpec((SAMPLE_ROWS, D_MODEL), lambda t: (0, 0))

    def mod_p_spec(layer, part):
        return pl.BlockSpec((None, C_PROMPT_ROWS, D_MODEL), lambda t: (layer, 0, part))

    def mod_s_spec(layer, part):
        return pl.BlockSpec((None, SAMPLE_ROWS, D_MODEL),
                            lambda t: (layer, C_SAMPLE_ROW0 // SAMPLE_ROWS, part))

    def gain_spec(layer):
        return pl.BlockSpec((None, 1, D_MODEL), lambda t: (layer, 0, 0))

    if split_in:
        args, in_specs = [x, x_sample], [prompt_row_spec, sample_row_spec]
    else:
        args, in_specs = [x], [row_spec]
    out_shapes, out_specs = [], []
    if has_post:
        args += [y, norm_post, mod, mod]
        in_specs += [row_spec, gain_spec(post_layer), mod_p_spec(post_layer, 2), mod_s_spec(post_layer, 2)]
        if split_out:
            out_shapes += [jax.ShapeDtypeStruct((M_PROMPT, D_MODEL), F32),
                           jax.ShapeDtypeStruct((SAMPLE_ROWS, D_MODEL), F32)]
            out_specs += [prompt_row_spec, sample_row_spec]
        else:
            out_shapes.append(jax.ShapeDtypeStruct((M_ALL, D_MODEL), F32))
            out_specs.append(row_spec)
    if has_pre:
        args += [norm_pre, mod, mod, mod, mod]
        in_specs += [gain_spec(pre_layer), mod_p_spec(pre_layer, 0), mod_p_spec(pre_layer, 1),
                     mod_s_spec(pre_layer, 0), mod_s_spec(pre_layer, 1)]
        out_shapes.append(jax.ShapeDtypeStruct((M_ALL, D_MODEL), BF16))
        out_specs.append(row_spec)
    outs = pl.pallas_call(
        functools.partial(_norm_kernel, has_post=has_post, has_pre=has_pre, split_in=split_in,
                          split_out=split_out, n_prompt_tiles=npt, tiles_per_batch=tiles_per_batch),
        grid=(npt + 1,),
        in_specs=in_specs,
        out_specs=out_specs,
        out_shape=out_shapes,
        compiler_params=pltpu.CompilerParams(
            dimension_semantics=("arbitrary",), vmem_limit_bytes=VMEM_LIMIT),
        name="norm_step",
    )(*args)
    return outs


def _mm_kernel(a_ref, w_ref, o_ref, *, slabs):
    tn = w_ref.shape[1]
    for c0 in range(0, tn, MXU_COLS):
        acc = jnp.dot(a_ref[...], w_ref[:, c0:c0 + MXU_COLS].astype(BF16),
                      preferred_element_type=F32).astype(o_ref.dtype)
        if slabs:
            for c in range(MXU_COLS // HEAD_DIM):
                o_ref[c0 // HEAD_DIM + c] = acc[:, c * HEAD_DIM:(c + 1) * HEAD_DIM]
        else:
            o_ref[:, c0:c0 + MXU_COLS] = acc


def _matmul(a, w, layer, n_out, tn, tm, out_dtype, name, slabs=False):
    m, k = a.shape
    if slabs:
        out_spec = pl.BlockSpec((tn // HEAD_DIM, tm, HEAD_DIM), lambda j, i: (j, i, 0))
        out_shape = jax.ShapeDtypeStruct((n_out // HEAD_DIM, m, HEAD_DIM), out_dtype)
    else:
        out_spec = pl.BlockSpec((tm, tn), lambda j, i: (i, j))
        out_shape = jax.ShapeDtypeStruct((m, n_out), out_dtype)
    return pl.pallas_call(
        functools.partial(_mm_kernel, slabs=slabs),
        grid=(n_out // tn, m // tm),
        in_specs=[pl.BlockSpec((tm, k), lambda j, i: (i, 0)),
                  pl.BlockSpec((None, k, tn), lambda j, i: (layer, 0, j))],
        out_specs=out_spec,
        out_shape=out_shape,
        compiler_params=pltpu.CompilerParams(
            dimension_semantics=("arbitrary", "arbitrary"), vmem_limit_bytes=VMEM_LIMIT),
        name=name,
    )(a, w)


def _pool_kernel(u_ref, halo_ref, z_ref, st_ref, w_ref, sc_ref,
                 a_ref, pp_ref, ps_ref, wb_ref, buf_a, buf_b, *, n_prompt_tiles, tiles_per_batch):
    g = pl.program_id(0)
    t = pl.program_id(1)
    tp = TP_POOL
    h0 = POOL_HALO

    @pl.when(t == 0)
    def _():
        _cast_rows_to_bf16(w_ref, wb_ref, POOL_GROUP)

    def finish(r, z, rows):
        y = jnp.dot(r.astype(BF16), wb_ref[...], preferred_element_type=F32) * sc_ref[...]
        a_ref[rows, :] = (y * _silu(z)).astype(BF16)

    def prompt_tile(n_steps):
        w = 2 ** n_steps
        first = (t % tiles_per_batch) == 0
        buf_a[0:8, :] = jnp.zeros((8, POOL_GROUP), F32)
        buf_b[0:8, :] = jnp.zeros((8, POOL_GROUP), F32)
        buf_a[8:h0, :] = jnp.where(first, 0.0, halo_ref[...])
        buf_a[h0:h0 + tp, :] = u_ref[...]
        src, dst = buf_a, buf_b
        n = tp + h0 - 8
        for s in range(n_steps):
            sh = 2 ** s
            dst[8:8 + n, :] = src[8:8 + n, :] + src[8 - sh:8 - sh + n, :]
            src, dst = dst, src
        pos = (t % tiles_per_batch) * tp + lax.broadcasted_iota(jnp.int32, (tp, 1), 0)
        inv_cnt = 1.0 / jnp.minimum(pos + 1, w).astype(F32)
        u = u_ref[...]
        r = src[h0:h0 + tp, :] * inv_cnt - u
        finish(r, z_ref[...], slice(None))

        @pl.when((t % tiles_per_batch) == tiles_per_batch - 1)
        def _():
            pp_ref[...] = u_ref[tp - POOL_BUF:tp, :]

    def sample_tile(n_steps):
        w = 2 ** n_steps
        u_new = u_ref[0:DEC_BATCH, :]
        acc = u_new
        for k in range(1, w):
            acc = acc + st_ref[POOL_BUF - k]
        r = acc / float(w) - u_new
        buf_a[0:DEC_BATCH, :] = r
        buf_a[DEC_BATCH:SAMPLE_ROWS, :] = jnp.zeros((SAMPLE_ROWS - DEC_BATCH, POOL_GROUP), F32)
        finish(buf_a[0:SAMPLE_ROWS, :], z_ref[0:SAMPLE_ROWS, :], slice(0, SAMPLE_ROWS))
        for k in range(POOL_BUF - 1):
            ps_ref[k] = st_ref[k + 1]
        ps_ref[POOL_BUF - 1] = u_new

    for gi in range(len(POOL_WINDOWS)):
        @pl.when((g == gi) & (t < n_prompt_tiles))
        def _(gi=gi):
            prompt_tile(gi + 1)

        @pl.when((g == gi) & (t == n_prompt_tiles))
        def _(gi=gi):
            sample_tile(gi + 1)


def _pool_mix(proj, state_t, w_grp, scale, layer):
    tp = TP_POOL
    npt = M_PROMPT // tp
    tpb = SEQ // tp
    ng = len(POOL_WINDOWS)
    halo_blocks = tp // 16
    outs = pl.pallas_call(
        functools.partial(_pool_kernel, n_prompt_tiles=npt, tiles_per_batch=tpb),
        grid=(ng, npt + 1),
        in_specs=[
            pl.BlockSpec((tp, POOL_GROUP), lambda g, t: (t, g)),
            pl.BlockSpec((16, POOL_GROUP), lambda g, t: (jnp.maximum(t * halo_blocks - 1, 0), g)),
            pl.BlockSpec((tp, POOL_GROUP), lambda g, t: (t, ng + g)),
            pl.BlockSpec((None, POOL_BUF, DEC_BATCH, POOL_GROUP), lambda g, t: (layer, 0, 0, g)),
            pl.BlockSpec((None, None, POOL_GROUP, POOL_GROUP), lambda g, t: (layer, g, 0, 0)),
            pl.BlockSpec((None, 1, POOL_GROUP), lambda g, t: (layer, 0, g)),
        ],
        out_specs=[
            pl.BlockSpec((tp, POOL_GROUP), lambda g, t: (t, g)),
            pl.BlockSpec((None, POOL_BUF, POOL_GROUP),
                         lambda g, t: (jnp.minimum(t // tpb, BATCH - 1), 0, g)),
            pl.BlockSpec((POOL_BUF, DEC_BATCH, POOL_GROUP), lambda g, t: (0, 0, g)),
        ],
        out_shape=[
            jax.ShapeDtypeStruct((M_ALL, POOL_WIDTH), BF16),
            jax.ShapeDtypeStruct((BATCH, POOL_BUF, POOL_WIDTH), F32),
            jax.ShapeDtypeStruct((POOL_BUF, DEC_BATCH, POOL_WIDTH), F32),
        ],
        scratch_shapes=[pltpu.VMEM((POOL_GROUP, POOL_GROUP), BF16),
                        pltpu.VMEM((tp + POOL_HALO, POOL_GROUP), F32),
                        pltpu.VMEM((tp + POOL_HALO, POOL_GROUP), F32)],
        compiler_params=pltpu.CompilerParams(
            dimension_semantics=("arbitrary", "arbitrary"), vmem_limit_bytes=VMEM_LIMIT),
        name="pool_mix",
    )(proj, proj, proj, state_t, w_grp, scale.reshape(-1, 1, POOL_WIDTH))
    return outs


def _att_prompt_kernel(x_ref, bias_ref, as_ref, a_ref, *scratch):
    @pl.when(pl.program_id(0) < BATCH)
    def _():
        _att_prompt_tile(x_ref, bias_ref, a_ref, *scratch)

    @pl.when(pl.program_id(0) == BATCH)
    def _():
        a_ref[0:SAMPLE_ROWS, :] = as_ref[...]


def _att_prompt_tile(x_ref, bias_ref, a_ref, qd, kd, vd, od, ld, on, ln, tmp3):
    tmp = tmp3.at[0]
    q_refs = tuple(x_ref.at[g] for g in range(N_GROUPS))
    k_refs = tuple(x_ref.at[N_GROUPS + g] for g in range(N_GROUPS))
    v_refs = tuple(x_ref.at[2 * N_GROUPS + g] for g in range(N_GROUPS))
    z_ref = x_ref.at[3 * N_GROUPS]
    bias_refs = tuple(bias_ref.at[g] for g in range(N_GROUPS))
    nu = ATT_UNITS

    for g, dil in enumerate(DILATIONS):
        n = SEQ // dil
        nb = n // BAND
        zero_blk = jnp.zeros((BAND, HEAD_DIM), BF16)
        kd[g, 0:BAND, :] = zero_blk
        vd[g, 0:BAND, :] = zero_blk
        for src, dst, off, mul in ((q_refs[g], qd, 0, ATT_SCALE * LOG2E), (k_refs[g], kd, BAND, None),
                                   (v_refs[g], vd, BAND, None)):
            def to_bf16(x, mul=mul):
                return (x if mul is None else x * mul).astype(BF16)

            if dil == 1:
                dst[g, off:off + SEQ, :] = to_bf16(src[...])
            elif dil == 4:
                for r in range(dil):
                    dst[g, off + r * n:off + (r + 1) * n, :] = to_bf16(src[pl.ds(r, n, stride=dil), :])
            else:
                stage = tmp3.at[(off > 0) + (dst is vd)]
                quarter = SEQ // 4
                for r_lo in range(4):
                    stage[r_lo * quarter:(r_lo + 1) * quarter, :] = src[pl.ds(r_lo, quarter, stride=4), :]
                for r in range(dil):
                    r_lo, r_hi = r % 4, r // 4
                    dst[g, off + r * n:off + (r + 1) * n, :] = to_bf16(
                        stage[pl.ds(r_lo * quarter + r_hi, n, stride=4), :])

    def unit_batch(g, u0, seq_blocks):
        with_prev = seq_blocks > 1
        rows = nu * BAND
        row = u0 * BAND
        shape3 = (nu, BAND, HEAD_DIM)
        q = qd[g, pl.ds(row, rows), :].reshape(shape3)
        kc = kd[g, pl.ds(row + BAND, rows), :].reshape(shape3)
        vc = vd[g, pl.ds(row + BAND, rows), :].reshape(shape3)
        bias = bias_refs[g]
        s_c = jnp.einsum('uqe,uke->uqk', q, kc, preferred_element_type=F32) \
            + (bias[:, BAND:2 * BAND] * LOG2E)[None]
        if with_prev:
            kp = kd[g, pl.ds(row, rows), :].reshape(shape3)
            vp = vd[g, pl.ds(row, rows), :].reshape(shape3)
            bias_p = jnp.broadcast_to((bias[:, 0:BAND] * LOG2E)[None], (nu, BAND, BAND))
            blk = u0 + lax.broadcasted_iota(jnp.int32, (nu, BAND, BAND), 0)
            bias_p = jnp.where((blk & (seq_blocks - 1)) == 0, NEG_INF, bias_p)
            s_p = jnp.einsum('uqe,uke->uqk', q, kp, preferred_element_type=F32) + bias_p
            m = jnp.max(jnp.maximum(s_c, s_p), axis=-1, keepdims=True)
            p_c = jnp.exp2(s_c - m)
            p_p = jnp.exp2(s_p - m)
            l = jnp.sum(p_c + p_p, axis=-1, keepdims=True)
            acc = jnp.einsum('uqk,uke->uqe', p_c.astype(BF16), vc, preferred_element_type=F32) \
                + jnp.einsum('uqk,uke->uqe', p_p.astype(BF16), vp, preferred_element_type=F32)
        else:
            m = jnp.max(s_c, axis=-1, keepdims=True)
            p_c = jnp.exp2(s_c - m)
            l = jnp.sum(p_c, axis=-1, keepdims=True)
            acc = jnp.einsum('uqk,uke->uqe', p_c.astype(BF16), vc, preferred_element_type=F32)
        od[g, pl.ds(row, rows), :] = (acc / l).reshape(rows, HEAD_DIM)
        ld[g, pl.ds(row, rows), :] = jnp.broadcast_to(m + jnp.log(l) * LOG2E, shape3).reshape(rows, HEAD_DIM)

    for g, dil in enumerate(DILATIONS):
        nb = (SEQ // dil) // BAND

        for u0 in range(0, N_BLOCKS, nu):
            unit_batch(g, u0, nb)

    def interleave4(src, dst, n_rows):
        quarter = n_rows // 4
        for r in range(4):
            dst[pl.ds(r, quarter, stride=4), :] = src[r * quarter:(r + 1) * quarter, :]

    for src_all, dst_all in ((od, on), (ld, ln)):
        interleave4(src_all.at[1], dst_all.at[0], SEQ)
        for r_lo in range(4):
            for r_hi in range(4):
                r = r_lo + 4 * r_hi
                tmp[pl.ds(r_lo * (SEQ // 4) + r_hi, BAND, stride=4), :] = src_all[2, r * BAND:(r + 1) * BAND, :]
        interleave4(tmp, dst_all.at[1], SEQ)

    def comb(i, c):
        rs = pl.ds(pl.multiple_of(i * ROW_CHUNK, ROW_CHUNK), ROW_CHUNK)
        l0, l1, l2 = ld[0, rs, :], ln[0, rs, :], ln[1, rs, :]
        mx = jnp.maximum(jnp.maximum(l0, l1), l2)
        w0, w1, w2 = jnp.exp2(l0 - mx), jnp.exp2(l1 - mx), jnp.exp2(l2 - mx)
        o = (w0 * od[0, rs, :] + w1 * on[0, rs, :] + w2 * on[1, rs, :]) / (w0 + w1 + w2)
        a_ref[rs, :] = (o * _silu(z_ref[rs, :])).astype(BF16)
        return c
    lax.fori_loop(0, SEQ // ROW_CHUNK, comb, 0)


def _att_prompt(proj, bias_mats, a_sample):
    hb = N_HEADS

    def head_blk(b, h):
        return jnp.where(b < BATCH, h, hb - 1)

    n_pg = ATT_IN_COLS // D_MODEL
    in_specs = [
        pl.BlockSpec((n_pg, None, SEQ, HEAD_DIM),
                     lambda b, h: (0, head_blk(b, h), jnp.minimum(b, BATCH - 1), 0)),
        pl.BlockSpec((N_GROUPS, None, BAND, 2 * BAND), lambda b, h: (0, head_blk(b, h), 0, 0)),
        pl.BlockSpec((SAMPLE_ROWS, HEAD_DIM), lambda b, h: (0, h)),
    ]
    scratch = [pltpu.VMEM((N_GROUPS, SEQ, HEAD_DIM), BF16)] + \
              [pltpu.VMEM((N_GROUPS, SEQ + BAND, HEAD_DIM), BF16)] * 2 + \
              [pltpu.VMEM((N_GROUPS, SEQ, HEAD_DIM), F32)] * 2 + \
              [pltpu.VMEM((N_GROUPS - 1, SEQ, HEAD_DIM), F32)] * 2 + \
              [pltpu.VMEM((3, SEQ, HEAD_DIM), F32)]
    return pl.pallas_call(
        _att_prompt_kernel,
        grid=(BATCH + 1, N_HEADS),
        in_specs=in_specs,
        out_specs=pl.BlockSpec((SEQ, HEAD_DIM), lambda b, h: (b, h)),
        out_shape=jax.ShapeDtypeStruct((M_ALL, D_MODEL), BF16),
        scratch_shapes=scratch,
        compiler_params=pltpu.CompilerParams(
            dimension_semantics=("arbitrary", "arbitrary"), vmem_limit_bytes=VMEM_LIMIT),
        name="att_prompt",
    )(proj.reshape(n_pg, hb, M_ALL, HEAD_DIM), bias_mats.reshape(N_GROUPS, hb, BAND, 2 * BAND), a_sample)


def _kv_rows_kernel(k_a, v_a, k_b, v_b, o_ref, *, tr):
    pitch = tr + KV_PITCH_PAD

    def emit(srcs):
        flats = [src.reshape(N_HEADS * pitch, HEAD_DIM) for src in srcs]

        def body(i, c):
            t0 = i * 8
            for k in range(8):
                for part, flat in enumerate(flats):
                    o_ref[t0 + k, part] = flat[pl.ds(t0 + k, N_HEADS, stride=pitch), :]
            return c
        lax.fori_loop(0, tr // 8, body, 0)

    @pl.when(pl.program_id(0) == 0)
    def _():
        emit((k_a, v_a))

    @pl.when(pl.program_id(0) == 1)
    def _():
        emit((k_b, v_b))


def _kv_rows(proj_a, proj_b, g):
    keep = WINDOWS[g]
    tr = min(keep, TR_KV)
    nt = keep // tr
    first_blk = (SEQ - keep) // tr
    blks_per_batch = SEQ // tr

    def row_blk(b, t):
        return b * blks_per_batch + first_blk + t

    def spec(layer, part):
        parked = row_blk(BATCH - 1, nt - 1) if layer == 0 else row_blk(0, 0)
        col0 = ((1 + part) * N_GROUPS + g) * N_HEADS
        return pl.BlockSpec((pl.Element(N_HEADS), pl.Element(tr + KV_PITCH_PAD), pl.Element(HEAD_DIM)),
                            lambda l, b, t: (col0, jnp.where(l == layer, row_blk(b, t), parked) * tr, 0))

    return pl.pallas_call(
        functools.partial(_kv_rows_kernel, tr=tr),
        grid=(2, BATCH, nt),
        in_specs=[spec(0, 0), spec(0, 1), spec(1, 0), spec(1, 1)],
        out_specs=pl.BlockSpec((None, None, tr, 2, N_HEADS, HEAD_DIM), lambda l, b, t: (l, b, t, 0, 0, 0)),
        out_shape=jax.ShapeDtypeStruct((2, BATCH, keep, 2, N_HEADS, HEAD_DIM), F32),
        compiler_params=pltpu.CompilerParams(
            dimension_semantics=("arbitrary",) * 3, vmem_limit_bytes=VMEM_LIMIT),
        name="kv_rows",
    )(proj_a, proj_a, proj_b, proj_b)


def _att_sample_kernel(x_ref, c0, c1, c2, bias_ref, a_ref, kv0, kv1, kv2):
    caches = (c0, c1, c2)
    kv_outs = (kv0, kv1, kv2)
    hb = N_HEADS
    outs, lses = [], []
    for g in range(N_GROUPS):
        q = x_ref[g * hb:(g + 1) * hb, :]
        kn = x_ref[(3 + g) * hb:(4 + g) * hb, :]
        vn = x_ref[(6 + g) * hb:(7 + g) * hb, :]
        kv_outs[g][0] = kn
        kv_outs[g][1] = vn
        kc = caches[g][:, 0]
        vc = caches[g][:, 1]
        s_c = jnp.sum(kc * q[None], axis=-1, keepdims=True) * ATT_SCALE + bias_ref[g, 0:BAND]
        s_n = jnp.sum(kn * q, axis=-1, keepdims=True) * ATT_SCALE + bias_ref[g, BAND]
        m = jnp.maximum(jnp.max(s_c, axis=0), s_n)
        p_c = jnp.exp(s_c - m[None])
        p_n = jnp.exp(s_n - m)
        l = jnp.sum(p_c, axis=0) + p_n
        o = (jnp.sum(p_c * vc, axis=0) + p_n * vn) / l
        outs.append(o)
        lses.append(m + jnp.log(l))
    mx = jnp.maximum(jnp.maximum(lses[0], lses[1]), lses[2])
    ws = [jnp.exp(ls - mx) for ls in lses]
    o = (ws[0] * outs[0] + ws[1] * outs[1] + ws[2] * outs[2]) / (ws[0] + ws[1] + ws[2])
    z = x_ref[9 * hb:10 * hb, :]
    a_ref[...] = o * _silu(z)


def _att_sample(proj_s, caches, bias_s, layer):
    hb = N_HEADS
    in_specs = [pl.BlockSpec((None, ATT_IN_COLS // HEAD_DIM, HEAD_DIM), lambda b: (b, 0, 0))]
    for g in range(N_GROUPS):
        in_specs.append(pl.BlockSpec((None, None, BAND, None, 2, hb, HEAD_DIM),
                                     lambda b: (layer, b, 0, 0, 0, 0, 0)))
    in_specs.append(pl.BlockSpec((N_GROUPS, BAND + 1, hb, 1), lambda b: (0, 0, 0, 0)))
    out_specs = [pl.BlockSpec((None, hb, HEAD_DIM), lambda b: (b, 0, 0))]
    out_shapes = [jax.ShapeDtypeStruct((DEC_BATCH, hb, HEAD_DIM), F32)]
    for g in range(N_GROUPS):
        out_specs.append(pl.BlockSpec((None, 2, hb, HEAD_DIM), lambda b: (b, 0, 0, 0)))
        out_shapes.append(jax.ShapeDtypeStruct((DEC_BATCH, 2, hb, HEAD_DIM), F32))
    return pl.pallas_call(
        _att_sample_kernel,
        grid=(DEC_BATCH,),
        in_specs=in_specs,
        out_specs=out_specs,
        out_shape=out_shapes,
        compiler_params=pltpu.CompilerParams(
            dimension_semantics=("arbitrary",), vmem_limit_bytes=VMEM_LIMIT),
        name="att_sample",
    )(proj_s, *caches, bias_s)


def kernel(x_prompt, x_sample, c_prompt, c_sample, cache_kv0, cache_kv1, cache_kv2, state_pool,
           norm_pre, norm_post, ada_w, ada_b, t5_bias, pool_w_in, pool_w_grp, pool_scale,
           pool_w_out, att_w_in, att_w_out):
    n_att = DEPTH // 2
    xp0 = x_prompt.reshape(M_PROMPT, D_MODEL)
    xs0 = jnp.zeros((SAMPLE_ROWS, D_MODEL), F32).at[0:DEC_BATCH].set(x_sample.reshape(DEC_BATCH, D_MODEL))
    c_all = jnp.zeros((C_ROWS, D_MODEL), F32)
    c_all = c_all.at[0:BATCH].set(c_prompt).at[C_SAMPLE_ROW0:C_SAMPLE_ROW0 + DEC_BATCH].set(c_sample)

    mod = _ada_all(c_all, ada_w, ada_b)
    gains_pre = norm_pre.reshape(DEPTH, 1, D_MODEL)
    gains_post = norm_post.reshape(DEPTH, 1, D_MODEL)

    bias_mats = _bias_matrices(t5_bias)
    bias_s = bias_mats[:, 0, 0:BAND + 1].reshape(N_GROUPS, N_HEADS, BAND + 1)
    bias_s = jnp.transpose(bias_s, (0, 2, 1))[..., None]
    caches = [c.reshape(n_att, DEC_BATCH, BAND, dil, 2, N_HEADS, HEAD_DIM)
              for c, dil in zip((cache_kv0, cache_kv1, cache_kv2), DILATIONS)]
    state_t = jnp.transpose(state_pool, (0, 2, 1, 3))

    kv_s = [[] for _ in range(N_GROUPS)]
    pool_p, pool_s, att_projs = [], [], []

    (h,) = _norm_step(xp0, None, mod, gains_post, gains_pre, None, 0, x_sample=xs0)
    x = None
    for i in range(DEPTH):
        li = i // 2
        if i % 2 == 0:
            proj = _matmul(h, pool_w_in, li, 2 * POOL_WIDTH, 1024, TM_MATMUL, F32, "pool_in_proj")
            a, pp, ps = _pool_mix(proj, state_t, pool_w_grp, pool_scale, li)
            pool_p.append(pp)
            pool_s.append(jnp.transpose(ps, (1, 0, 2)))
            y = _matmul(a, pool_w_out, li, D_MODEL, 512, TM_MATMUL, BF16, "pool_out_proj")
        else:
            proj = _matmul(h, att_w_in, li, ATT_IN_COLS, 1024, TM_MATMUL, F32, "att_in_proj", slabs=True)
            att_projs.append(proj)
            proj_s = jnp.transpose(proj[:, M_PROMPT:M_PROMPT + DEC_BATCH, :], (1, 0, 2))
            souts = _att_sample(proj_s, caches, bias_s, li)
            a_s = jnp.zeros((SAMPLE_ROWS, D_MODEL), F32).at[0:DEC_BATCH].set(souts[0].reshape(DEC_BATCH, D_MODEL))
            a = _att_prompt(proj, bias_mats, a_s.astype(BF16))
            for g in range(N_GROUPS):
                kv_s[g].append(souts[1 + g].reshape(DEC_BATCH, 1, 2, N_HEADS, HEAD_DIM))
            y = _matmul(a, att_w_out, li, D_MODEL, 1024, TM_MATMUL, BF16, "att_out_proj")
        last = i + 1 == DEPTH
        if i == 0:
            x, h = _norm_step(xp0, y, mod, gains_post, gains_pre, i, i + 1, x_sample=xs0)
        elif not last:
            x, h = _norm_step(x, y, mod, gains_post, gains_pre, i, i + 1)
        else:
            y_p, y_s = _norm_step(x, y, mod, gains_post, gains_pre, i, None, split_out=True)

    kv_p = [_kv_rows(att_projs[0], att_projs[1], g) for g in range(N_GROUPS)]
    y_prompt = y_p.reshape(BATCH, SEQ, D_MODEL)
    y_sample = y_s[0:DEC_BATCH].reshape(DEC_BATCH, 1, D_MODEL)
    return (y_prompt, y_sample, kv_p[0], kv_p[1], kv_p[2], jnp.stack(pool_p),
            jnp.stack(kv_s[0]), jnp.stack(kv_s[1]), jnp.stack(kv_s[2]), jnp.stack(pool_s))
```

```python
import functools

import numpy as np
import jax
import jax.numpy as jnp
from jax import lax
from jax.experimental import pallas as pl
from jax.experimental.pallas import tpu as pltpu

D_MODEL = 2048
BATCH = 4
SEQ = 2048
DEPTH = 4
DEC_BATCH = 8
HEAD_DIM = 128
N_HEADS = 16
DILATIONS = (1, 4, 16)
WINDOWS = (128, 512, 2048)
N_GROUPS = 3
QKV_WIDTH = N_GROUPS * D_MODEL
ATT_IN_COLS = 3 * QKV_WIDTH + D_MODEL
BAND = 128
ATT_SCALE = HEAD_DIM ** -0.5
LOG2E = 1.4426950408889634
POOL_WINDOWS = (2, 4, 8, 16)
POOL_WIDTH = 2 * D_MODEL
POOL_GROUP = POOL_WIDTH // 4
POOL_BUF = 15
N_BUCKETS = 32
T5_MAX_DIST = 2048
RMS_EPS = 1e-6
NEG_INF = -1e30

M_PROMPT = BATCH * SEQ
SAMPLE_ROWS = 64
M_ALL = M_PROMPT + SAMPLE_ROWS
C_ROWS = 2 * SAMPLE_ROWS
C_PROMPT_ROWS = 8
C_SAMPLE_ROW0 = SAMPLE_ROWS

TM_MATMUL = 2064
MXU_COLS = 256
TE_NORM = 1024
TP_POOL = 1024
POOL_HALO = 24
ATT_UNITS = 16
N_BLOCKS = SEQ // BAND
TR_KV = 256
ROW_CHUNK = 256
KV_PITCH_PAD = 8
VMEM_LIMIT = 60 * 1024 * 1024

F32 = jnp.float32
BF16 = jnp.bfloat16


def _silu(x):
    half = 0.5 * x
    return half + half * jnp.tanh(half)


def _cast_rows_to_bf16(src_ref, dst_ref, rows, chunk=ROW_CHUNK):
    def body(i, c):
        r = pl.multiple_of(i * chunk, chunk)
        dst_ref[pl.ds(r, chunk), :] = src_ref[pl.ds(r, chunk), :].astype(BF16)
        return c
    lax.fori_loop(0, rows // chunk, body, 0)


def _ada_kernel(c_ref, w_ref, b_ref, o_ref):
    a = _silu(c_ref[...]).astype(BF16)
    kc = 512
    acc = jnp.zeros(o_ref.shape, F32)
    for k0 in range(0, D_MODEL, kc):
        acc = acc + jnp.dot(a[:, k0:k0 + kc], w_ref[k0:k0 + kc, :].astype(BF16),
                            preferred_element_type=F32)
    o_ref[...] = acc + b_ref[...]


def _ada_all(c_all, ada_w, ada_b):
    tn = 2048
    n = 3 * D_MODEL
    return pl.pallas_call(
        _ada_kernel,
        grid=(DEPTH, n // tn),
        in_specs=[pl.BlockSpec((C_ROWS, D_MODEL), lambda l, j: (0, 0)),
                  pl.BlockSpec((None, D_MODEL, tn), lambda l, j: (l, 0, j)),
                  pl.BlockSpec((None, 1, tn), lambda l, j: (l, 0, j))],
        out_specs=pl.BlockSpec((None, C_ROWS, tn), lambda l, j: (l, 0, j)),
        out_shape=jax.ShapeDtypeStruct((DEPTH, C_ROWS, n), F32),
        compiler_params=pltpu.CompilerParams(
            dimension_semantics=("arbitrary", "arbitrary"), vmem_limit_bytes=VMEM_LIMIT),
        name="ada_mod",
    )(c_all, ada_w, ada_b.reshape(DEPTH, 1, n))


def _t5_bucket(dist):
    dist = np.asarray(dist, dtype=np.int64)
    max_exact = N_BUCKETS // 2
    ratio = np.log(np.maximum(dist, 1) / max_exact) / np.log(T5_MAX_DIST / max_exact)
    large = np.minimum(max_exact + (ratio * (N_BUCKETS - max_exact)).astype(np.int64), N_BUCKETS - 1)
    return np.where(dist < max_exact, dist, large).astype(np.int32)


def _bucket_index_table():
    rel = np.arange(BAND)[:, None] + BAND - np.arange(2 * BAND)[None, :]
    inband = (rel >= 0) & (rel <= BAND)
    out = []
    for dil in DILATIONS:
        bucket = _t5_bucket(np.clip(rel, 0, BAND) * dil)
        out.append(np.where(inband, bucket, -1))
    return np.stack(out).astype(np.int32)


def _bias_kernel(tab_ref, idx_ref, o_ref):
    g = pl.program_id(0)
    idx = idx_ref[...]

    def head(h, c):
        acc = jnp.full(idx.shape, NEG_INF, F32)
        for b in range(N_BUCKETS):
            acc = jnp.where(idx == b, tab_ref[b, g * N_HEADS + h], acc)
        o_ref[h] = acc
        return c
    lax.fori_loop(0, N_HEADS, head, 0)


def _bias_matrices(t5_bias):
    idx = jnp.asarray(_bucket_index_table())
    n_sub = N_GROUPS * N_HEADS
    return pl.pallas_call(
        _bias_kernel,
        grid=(N_GROUPS,),
        in_specs=[pl.BlockSpec(memory_space=pltpu.SMEM),
                  pl.BlockSpec((None, BAND, 2 * BAND), lambda g: (g, 0, 0))],
        out_specs=pl.BlockSpec((N_HEADS, BAND, 2 * BAND), lambda g: (g, 0, 0)),
        out_shape=jax.ShapeDtypeStruct((n_sub, BAND, 2 * BAND), F32),
        name="t5_bias_mats",
    )(t5_bias, idx)


def _norm_kernel(*refs, has_post, has_pre, split_in, split_out, n_prompt_tiles, tiles_per_batch):
    refs = list(refs)
    xp_ref = refs.pop(0)
    xs_ref = refs.pop(0) if split_in else xp_ref
    if has_post:
        y_ref, gpost_ref, gate_p_ref, gate_s_ref = refs[:4]
        refs = refs[4:]
    if has_pre:
        gpre_ref, shift_p_ref, scale_p_ref, shift_s_ref, scale_s_ref = refs[:5]
        refs = refs[5:]
    if has_post:
        xop_ref = refs.pop(0)
        xos_ref = refs.pop(0) if split_out else xop_ref
    if has_pre:
        h_ref = refs.pop(0)

    def rms(v, g):
        return v * lax.rsqrt(jnp.mean(v * v, axis=-1, keepdims=True) + RMS_EPS) * g

    def body(x_ref, xo_ref, rows, gate, shift, scale):
        x = x_ref[rows, :]
        if has_post:
            x = x + gate * rms(y_ref[rows, :].astype(F32), gpost_ref[...])
            xo_ref[rows, :] = x
        if has_pre:
            h = rms(x, gpre_ref[...]) * (1.0 + scale) + shift
            h_ref[rows, :] = h.astype(BF16)

    t = pl.program_id(0)

    @pl.when(t < n_prompt_tiles)
    def _():
        row = pl.ds(t // tiles_per_batch, 1)
        body(xp_ref, xop_ref if has_post else None, slice(None),
             gate_p_ref[row, :] if has_post else None,
             shift_p_ref[row, :] if has_pre else None,
             scale_p_ref[row, :] if has_pre else None)

    @pl.when(t == n_prompt_tiles)
    def _():
        body(xs_ref, xos_ref if has_post else None, slice(0, SAMPLE_ROWS),
             gate_s_ref[...] if has_post else None,
             shift_s_ref[...] if has_pre else None,
             scale_s_ref[...] if has_pre else None)


def _norm_step(x, y, mod, norm_post, norm_pre, post_layer, pre_layer, x_sample=None, split_out=False):
    has_post = post_layer is not None
    has_pre = pre_layer is not None
    split_in = x_sample is not None
    te = TE_NORM
    npt = M_PROMPT // te
    tiles_per_batch = SEQ // te
    row_spec = pl.BlockSpec((te, D_MODEL), lambda t: (t, 0))
    prompt_row_spec = pl.BlockSpec((te, D_MODEL), lambda t: (jnp.minimum(t, npt - 1), 0))
    sample_row_spec = pl.BlockSpec((SAMPLE_ROWS, D_MODEL), lambda t: (0, 0))

    def mod_p_spec(layer, part):
        return pl.BlockSpec((None, C_PROMPT_ROWS, D_MODEL), lambda t: (layer, 0, part))

    def mod_s_spec(layer, part):
        return pl.BlockSpec((None, SAMPLE_ROWS, D_MODEL),
                            lambda t: (layer, C_SAMPLE_ROW0 // SAMPLE_ROWS, part))

    def gain_spec(layer):
        return pl.BlockSpec((None, 1, D_MODEL), lambda t: (layer, 0, 0))

    if split_in:
        args, in_specs = [x, x_sample], [prompt_row_spec, sample_row_spec]
    else:
        args, in_specs = [x], [row_spec]
    out_shapes, out_specs = [], []
    if has_post:
        args += [y, norm_post, mod, mod]
        in_specs += [row_spec, gain_spec(post_layer), mod_p_spec(post_layer, 2), mod_s_spec(post_layer, 2)]
        if split_out:
            out_shapes += [jax.ShapeDtypeStruct((M_PROMPT, D_MODEL), F32),
                           jax.ShapeDtypeStruct((SAMPLE_ROWS, D_MODEL), F32)]
            out_specs += [prompt_row_spec, sample_row_spec]
        else:
            out_shapes.append(jax.ShapeDtypeStruct((M_ALL, D_MODEL), F32))
            out_specs.append(row_spec)
    if has_pre:
        args += [norm_pre, mod, mod, mod, mod]
        in_specs += [gain_spec(pre_layer), mod_p_spec(pre_layer, 0), mod_p_spec(pre_layer, 1),
                     mod_s_spec(pre_layer, 0), mod_s_spec(pre_layer, 1)]
        out_shapes.append(jax.ShapeDtypeStruct((M_ALL, D_MODEL), BF16))
        out_specs.append(row_spec)
    outs = pl.pallas_call(
        functools.partial(_norm_kernel, has_post=has_post, has_pre=has_pre, split_in=split_in,
                          split_out=split_out, n_prompt_tiles=npt, tiles_per_batch=tiles_per_batch),
        grid=(npt + 1,),
        in_specs=in_specs,
        out_specs=out_specs,
        out_shape=out_shapes,
        compiler_params=pltpu.CompilerParams(
            dimension_semantics=("arbitrary",), vmem_limit_bytes=VMEM_LIMIT),
        name="norm_step",
    )(*args)
    return outs


def _mm_kernel(a_ref, w_ref, o_ref, *, slabs):
    tn = w_ref.shape[1]
    for c0 in range(0, tn, MXU_COLS):
        acc = jnp.dot(a_ref[...], w_ref[:, c0:c0 + MXU_COLS].astype(BF16),
                      preferred_element_type=F32).astype(o_ref.dtype)
        if slabs:
            for c in range(MXU_COLS // HEAD_DIM):
                o_ref[c0 // HEAD_DIM + c] = acc[:, c * HEAD_DIM:(c + 1) * HEAD_DIM]
        else:
            o_ref[:, c0:c0 + MXU_COLS] = acc


def _matmul(a, w, layer, n_out, tn, tm, out_dtype, name, slabs=False):
    m, k = a.shape
    if slabs:
        out_spec = pl.BlockSpec((tn // HEAD_DIM, tm, HEAD_DIM), lambda j, i: (j, i, 0))
        out_shape = jax.ShapeDtypeStruct((n_out // HEAD_DIM, m, HEAD_DIM), out_dtype)
    else:
        out_spec = pl.BlockSpec((tm, tn), lambda j, i: (i, j))
        out_shape = jax.ShapeDtypeStruct((m, n_out), out_dtype)
    return pl.pallas_call(
        functools.partial(_mm_kernel, slabs=slabs),
        grid=(n_out // tn, m // tm),
        in_specs=[pl.BlockSpec((tm, k), lambda j, i: (i, 0)),
                  pl.BlockSpec((None, k, tn), lambda j, i: (layer, 0, j))],
        out_specs=out_spec,
        out_shape=out_shape,
        compiler_params=pltpu.CompilerParams(
            dimension_semantics=("arbitrary", "arbitrary"), vmem_limit_bytes=VMEM_LIMIT),
        name=name,
    )(a, w)


def _pool_kernel(u_ref, halo_ref, z_ref, st_ref, w_ref, sc_ref,
                 a_ref, pp_ref, ps_ref, wb_ref, buf_a, buf_b, *, n_prompt_tiles, tiles_per_batch):
    g = pl.program_id(0)
    t = pl.program_id(1)
    tp = TP_POOL
    h0 = POOL_HALO

    @pl.when(t == 0)
    def _():
        _cast_rows_to_bf16(w_ref, wb_ref, POOL_GROUP)

    def finish(r, z, rows):
        y = jnp.dot(r.astype(BF16), wb_ref[...], preferred_element_type=F32) * sc_ref[...]
        a_ref[rows, :] = (y * _silu(z)).astype(BF16)

    def prompt_tile(n_steps):
        w = 2 ** n_steps
        first = (t % tiles_per_batch) == 0
        buf_a[0:8, :] = jnp.zeros((8, POOL_GROUP), F32)
        buf_b[0:8, :] = jnp.zeros((8, POOL_GROUP), F32)
        buf_a[8:h0, :] = jnp.where(first, 0.0, halo_ref[...])
        buf_a[h0:h0 + tp, :] = u_ref[...]
        src, dst = buf_a, buf_b
        n = tp + h0 - 8
        for s in range(n_steps):
            sh = 2 ** s
            dst[8:8 + n, :] = src[8:8 + n, :] + src[8 - sh:8 - sh + n, :]
            src, dst = dst, src
        pos = (t % tiles_per_batch) * tp + lax.broadcasted_iota(jnp.int32, (tp, 1), 0)
        inv_cnt = 1.0 / jnp.minimum(pos + 1, w).astype(F32)
        u = u_ref[...]
        r = src[h0:h0 + tp, :] * inv_cnt - u
        finish(r, z_ref[...], slice(None))

        @pl.when((t % tiles_per_batch) == tiles_per_batch - 1)
        def _():
            pp_ref[...] = u_ref[tp - POOL_BUF:tp, :]

    def sample_tile(n_steps):
        w = 2 ** n_steps
        u_new = u_ref[0:DEC_BATCH, :]
        acc = u_new
        for k in range(1, w):
            acc = acc + st_ref[POOL_BUF - k]
        r = acc / float(w) - u_new
        buf_a[0:DEC_BATCH, :] = r
        buf_a[DEC_BATCH:SAMPLE_ROWS, :] = jnp.zeros((SAMPLE_ROWS - DEC_BATCH, POOL_GROUP), F32)
        finish(buf_a[0:SAMPLE_ROWS, :], z_ref[0:SAMPLE_ROWS, :], slice(0, SAMPLE_ROWS))
        for k in range(POOL_BUF - 1):
            ps_ref[k] = st_ref[k + 1]
        ps_ref[POOL_BUF - 1] = u_new

    for gi in range(len(POOL_WINDOWS)):
        @pl.when((g == gi) & (t < n_prompt_tiles))
        def _(gi=gi):
            prompt_tile(gi + 1)

        @pl.when((g == gi) & (t == n_prompt_tiles))
        def _(gi=gi):
            sample_tile(gi + 1)


def _pool_mix(proj, state_t, w_grp, scale, layer):
    tp = TP_POOL
    npt = M_PROMPT // tp
    tpb = SEQ // tp
    ng = len(POOL_WINDOWS)
    halo_blocks = tp // 16
    outs = pl.pallas_call(
        functools.partial(_pool_kernel, n_prompt_tiles=npt, tiles_per_batch=tpb),
        grid=(ng, npt + 1),
        in_specs=[
            pl.BlockSpec((tp, POOL_GROUP), lambda g, t: (t, g)),
            pl.BlockSpec((16, POOL_GROUP), lambda g, t: (jnp.maximum(t * halo_blocks - 1, 0), g)),
            pl.BlockSpec((tp, POOL_GROUP), lambda g, t: (t, ng + g)),
            pl.BlockSpec((None, POOL_BUF, DEC_BATCH, POOL_GROUP), lambda g, t: (layer, 0, 0, g)),
            pl.BlockSpec((None, None, POOL_GROUP, POOL_GROUP), lambda g, t: (layer, g, 0, 0)),
            pl.BlockSpec((None, 1, POOL_GROUP), lambda g, t: (layer, 0, g)),
        ],
        out_specs=[
            pl.BlockSpec((tp, POOL_GROUP), lambda g, t: (t, g)),
            pl.BlockSpec((None, POOL_BUF, POOL_GROUP),
                         lambda g, t: (jnp.minimum(t // tpb, BATCH - 1), 0, g)),
            pl.BlockSpec((POOL_BUF, DEC_BATCH, POOL_GROUP), lambda g, t: (0, 0, g)),
        ],
        out_shape=[
            jax.ShapeDtypeStruct((M_ALL, POOL_WIDTH), BF16),
            jax.ShapeDtypeStruct((BATCH, POOL_BUF, POOL_WIDTH), F32),
            jax.ShapeDtypeStruct((POOL_BUF, DEC_BATCH, POOL_WIDTH), F32),
        ],
        scratch_shapes=[pltpu.VMEM((POOL_GROUP, POOL_GROUP), BF16),
                        pltpu.VMEM((tp + POOL_HALO, POOL_GROUP), F32),
                        pltpu.VMEM((tp + POOL_HALO, POOL_GROUP), F32)],
        compiler_params=pltpu.CompilerParams(
            dimension_semantics=("arbitrary", "arbitrary"), vmem_limit_bytes=VMEM_LIMIT),
        name="pool_mix",
    )(proj, proj, proj, state_t, w_grp, scale.reshape(-1, 1, POOL_WIDTH))
    return outs


def _att_prompt_kernel(x_ref, bias_ref, as_ref, a_ref, *scratch):
    @pl.when(pl.program_id(0) < BATCH)
    def _():
        _att_prompt_tile(x_ref, bias_ref, a_ref, *scratch)

    @pl.when(pl.program_id(0) == BATCH)
    def _():
        a_ref[0:SAMPLE_ROWS, :] = as_ref[...]


def _att_prompt_tile(x_ref, bias_ref, a_ref, qd, kd, vd, od, ld, on, ln, tmp3):
    tmp = tmp3.at[0]
    q_refs = tuple(x_ref.at[g] for g in range(N_GROUPS))
    k_refs = tuple(x_ref.at[N_GROUPS + g] for g in range(N_GROUPS))
    v_refs = tuple(x_ref.at[2 * N_GROUPS + g] for g in range(N_GROUPS))
    z_ref = x_ref.at[3 * N_GROUPS]
    bias_refs = tuple(bias_ref.at[g] for g in range(N_GROUPS))
    nu = ATT_UNITS

    for g, dil in enumerate(DILATIONS):
        n = SEQ // dil
        nb = n // BAND
        zero_blk = jnp.zeros((BAND, HEAD_DIM), BF16)
        kd[g, 0:BAND, :] = zero_blk
        vd[g, 0:BAND, :] = zero_blk
        for src, dst, off, mul in ((q_refs[g], qd, 0, ATT_SCALE * LOG2E), (k_refs[g], kd, BAND, None),
                                   (v_refs[g], vd, BAND, None)):
            def to_bf16(x, mul=mul):
                return (x if mul is None else x * mul).astype(BF16)

            if dil == 1:
                dst[g, off:off + SEQ, :] = to_bf16(src[...])
            elif dil == 4:
                for r in range(dil):
                    dst[g, off + r * n:off + (r + 1) * n, :] = to_bf16(src[pl.ds(r, n, stride=dil), :])
            else:
                stage = tmp3.at[(off > 0) + (dst is vd)]
                quarter = SEQ // 4
                for r_lo in range(4):
                    stage[r_lo * quarter:(r_lo + 1) * quarter, :] = src[pl.ds(r_lo, quarter, stride=4), :]
                for r in range(dil):
                    r_lo, r_hi = r % 4, r // 4
                    dst[g, off + r * n:off + (r + 1) * n, :] = to_bf16(
                        stage[pl.ds(r_lo * quarter + r_hi, n, stride=4), :])

    def unit_batch(g, u0, seq_blocks):
        with_prev = seq_blocks > 1
        rows = nu * BAND
        row = pl.multiple_of(u0 * BAND, rows)
        shape3 = (nu, BAND, HEAD_DIM)
        q = qd[g, pl.ds(row, rows), :].reshape(shape3)
        kc = kd[g, pl.ds(row + BAND, rows), :].reshape(shape3)
        vc = vd[g, pl.ds(row + BAND, rows), :].reshape(shape3)
        bias = bias_refs[g]
        s_c = jnp.einsum('uqe,uke->uqk', q, kc, preferred_element_type=F32) \
            + (bias[:, BAND:2 * BAND] * LOG2E)[None]
        if with_prev:
            kp = kd[g, pl.ds(row, rows), :].reshape(shape3)
            vp = vd[g, pl.ds(row, rows), :].reshape(shape3)
            bias_p = jnp.broadcast_to((bias[:, 0:BAND] * LOG2E)[None], (nu, BAND, BAND))
            blk = u0 + lax.broadcasted_iota(jnp.int32, (nu, BAND, BAND), 0)
            bias_p = jnp.where((blk & (seq_blocks - 1)) == 0, NEG_INF, bias_p)
            s_p = jnp.einsum('uqe,uke->uqk', q, kp, preferred_element_type=F32) + bias_p
            m = jnp.max(jnp.maximum(s_c, s_p), axis=-1, keepdims=True)
            p_c = jnp.exp2(s_c - m)
            p_p = jnp.exp2(s_p - m)
            l = jnp.sum(p_c + p_p, axis=-1, keepdims=True)
            acc = jnp.einsum('uqk,uke->uqe', p_c.astype(BF16), vc, preferred_element_type=F32) \
                + jnp.einsum('uqk,uke->uqe', p_p.astype(BF16), vp, preferred_element_type=F32)
        else:
            m = jnp.max(s_c, axis=-1, keepdims=True)
            p_c = jnp.exp2(s_c - m)
            l = jnp.sum(p_c, axis=-1, keepdims=True)
            acc = jnp.einsum('uqk,uke->uqe', p_c.astype(BF16), vc, preferred_element_type=F32)
        od[g, pl.ds(row, rows), :] = (acc / l).reshape(rows, HEAD_DIM)
        ld[g, pl.ds(row, rows), :] = jnp.broadcast_to(m + jnp.log(l) * LOG2E, shape3).reshape(rows, HEAD_DIM)

    for g, dil in enumerate(DILATIONS):
        nb = (SEQ // dil) // BAND

        def batch_body(i, c, g=g, nb=nb):
            unit_batch(g, i * nu, nb)
            return c
        lax.fori_loop(0, N_BLOCKS // nu, batch_body, 0)

    def interleave4(src, dst, n_rows):
        quarter = n_rows // 4
        for r in range(4):
            dst[pl.ds(r, quarter, stride=4), :] = src[r * quarter:(r + 1) * quarter, :]

    for src_all, dst_all in ((od, on), (ld, ln)):
        interleave4(src_all.at[1], dst_all.at[0], SEQ)
        for r_lo in range(4):
            for r_hi in range(4):
                r = r_lo + 4 * r_hi
                tmp[pl.ds(r_lo * (SEQ // 4) + r_hi, BAND, stride=4), :] = src_all[2, r * BAND:(r + 1) * BAND, :]
        interleave4(tmp, dst_all.at[1], SEQ)

    def comb(i, c):
        rs = pl.ds(pl.multiple_of(i * ROW_CHUNK, ROW_CHUNK), ROW_CHUNK)
        l0, l1, l2 = ld[0, rs, :], ln[0, rs, :], ln[1, rs, :]
        mx = jnp.maximum(jnp.maximum(l0, l1), l2)
        w0, w1, w2 = jnp.exp2(l0 - mx), jnp.exp2(l1 - mx), jnp.exp2(l2 - mx)
        o = (w0 * od[0, rs, :] + w1 * on[0, rs, :] + w2 * on[1, rs, :]) / (w0 + w1 + w2)
        a_ref[rs, :] = (o * _silu(z_ref[rs, :])).astype(BF16)
        return c
    lax.fori_loop(0, SEQ // ROW_CHUNK, comb, 0)


def _att_prompt(proj, bias_mats, a_sample):
    hb = N_HEADS

    def head_blk(b, h):
        return jnp.where(b < BATCH, h, hb - 1)

    n_pg = ATT_IN_COLS // D_MODEL
    in_specs = [
        pl.BlockSpec((n_pg, None, SEQ, HEAD_DIM),
                     lambda b, h: (0, head_blk(b, h), jnp.minimum(b, BATCH - 1), 0)),
        pl.BlockSpec((N_GROUPS, None, BAND, 2 * BAND), lambda b, h: (0, head_blk(b, h), 0, 0)),
        pl.BlockSpec((SAMPLE_ROWS, HEAD_DIM), lambda b, h: (0, h)),
    ]
    scratch = [pltpu.VMEM((N_GROUPS, SEQ, HEAD_DIM), BF16)] + \
              [pltpu.VMEM((N_GROUPS, SEQ + BAND, HEAD_DIM), BF16)] * 2 + \
              [pltpu.VMEM((N_GROUPS, SEQ, HEAD_DIM), F32)] * 2 + \
              [pltpu.VMEM((N_GROUPS - 1, SEQ, HEAD_DIM), F32)] * 2 + \
              [pltpu.VMEM((3, SEQ, HEAD_DIM), F32)]
    return pl.pallas_call(
        _att_prompt_kernel,
        grid=(BATCH + 1, N_HEADS),
        in_specs=in_specs,
        out_specs=pl.BlockSpec((SEQ, HEAD_DIM), lambda b, h: (b, h)),
        out_shape=jax.ShapeDtypeStruct((M_ALL, D_MODEL), BF16),
        scratch_shapes=scratch,
        compiler_params=pltpu.CompilerParams(
            dimension_semantics=("arbitrary", "arbitrary"), vmem_limit_bytes=VMEM_LIMIT),
        name="att_prompt",
    )(proj.reshape(n_pg, hb, M_ALL, HEAD_DIM), bias_mats.reshape(N_GROUPS, hb, BAND, 2 * BAND), a_sample)


def _kv_rows_kernel(k_a, v_a, k_b, v_b, o_ref, *, tr):
    pitch = tr + KV_PITCH_PAD

    def emit(srcs):
        flats = [src.reshape(N_HEADS * pitch, HEAD_DIM) for src in srcs]

        def body(i, c):
            t0 = i * 8
            for k in range(8):
                for part, flat in enumerate(flats):
                    o_ref[t0 + k, part] = flat[pl.ds(t0 + k, N_HEADS, stride=pitch), :]
            return c
        lax.fori_loop(0, tr // 8, body, 0)

    @pl.when(pl.program_id(0) == 0)
    def _():
        emit((k_a, v_a))

    @pl.when(pl.program_id(0) == 1)
    def _():
        emit((k_b, v_b))


def _kv_rows(proj_a, proj_b, g):
    keep = WINDOWS[g]
    tr = min(keep, TR_KV)
    nt = keep // tr
    first_blk = (SEQ - keep) // tr
    blks_per_batch = SEQ // tr

    def row_blk(b, t):
        return b * blks_per_batch + first_blk + t

    def spec(layer, part):
        parked = row_blk(BATCH - 1, nt - 1) if layer == 0 else row_blk(0, 0)
        col0 = ((1 + part) * N_GROUPS + g) * N_HEADS
        return pl.BlockSpec((pl.Element(N_HEADS), pl.Element(tr + KV_PITCH_PAD), pl.Element(HEAD_DIM)),
                            lambda l, b, t: (col0, jnp.where(l == layer, row_blk(b, t), parked) * tr, 0))

    return pl.pallas_call(
        functools.partial(_kv_rows_kernel, tr=tr),
        grid=(2, BATCH, nt),
        in_specs=[spec(0, 0), spec(0, 1), spec(1, 0), spec(1, 1)],
        out_specs=pl.BlockSpec((None, None, tr, 2, N_HEADS, HEAD_DIM), lambda l, b, t: (l, b, t, 0, 0, 0)),
        out_shape=jax.ShapeDtypeStruct((2, BATCH, keep, 2, N_HEADS, HEAD_DIM), F32),
        compiler_params=pltpu.CompilerParams(
            dimension_semantics=("arbitrary",) * 3, vmem_limit_bytes=VMEM_LIMIT),
        name="kv_rows",
    )(proj_a, proj_a, proj_b, proj_b)


def _att_sample_kernel(x_ref, c0, c1, c2, bias_ref, a_ref, kv0, kv1, kv2):
    caches = (c0, c1, c2)
    kv_outs = (kv0, kv1, kv2)
    hb = N_HEADS
    outs, lses = [], []
    for g in range(N_GROUPS):
        q = x_ref[g * hb:(g + 1) * hb, :]
        kn = x_ref[(3 + g) * hb:(4 + g) * hb, :]
        vn = x_ref[(6 + g) * hb:(7 + g) * hb, :]
        kv_outs[g][0] = kn
        kv_outs[g][1] = vn
        kc = caches[g][:, 0]
        vc = caches[g][:, 1]
        s_c = jnp.sum(kc * q[None], axis=-1, keepdims=True) * ATT_SCALE + bias_ref[g, 0:BAND]
        s_n = jnp.sum(kn * q, axis=-1, keepdims=True) * ATT_SCALE + bias_ref[g, BAND]
        m = jnp.maximum(jnp.max(s_c, axis=0), s_n)
        p_c = jnp.exp(s_c - m[None])
        p_n = jnp.exp(s_n - m)
        l = jnp.sum(p_c, axis=0) + p_n
        o = (jnp.sum(p_c * vc, axis=0) + p_n * vn) / l
        outs.append(o)
        lses.append(m + jnp.log(l))
    mx = jnp.maximum(jnp.maximum(lses[0], lses[1]), lses[2])
    ws = [jnp.exp(ls - mx) for ls in lses]
    o = (ws[0] * outs[0] + ws[1] * outs[1] + ws[2] * outs[2]) / (ws[0] + ws[1] + ws[2])
    z = x_ref[9 * hb:10 * hb, :]
    a_ref[...] = o * _silu(z)


def _att_sample(proj_s, caches, bias_s, layer):
    hb = N_HEADS
    in_specs = [pl.BlockSpec((None, ATT_IN_COLS // HEAD_DIM, HEAD_DIM), lambda b: (b, 0, 0))]
    for g in range(N_GROUPS):
        in_specs.append(pl.BlockSpec((None, None, BAND, None, 2, hb, HEAD_DIM),
                                     lambda b: (layer, b, 0, 0, 0, 0, 0)))
    in_specs.append(pl.BlockSpec((N_GROUPS, BAND + 1, hb, 1), lambda b: (0, 0, 0, 0)))
    out_specs = [pl.BlockSpec((None, hb, HEAD_DIM), lambda b: (b, 0, 0))]
    out_shapes = [jax.ShapeDtypeStruct((DEC_BATCH, hb, HEAD_DIM), F32)]
    for g in range(N_GROUPS):
        out_specs.append(pl.BlockSpec((None, 2, hb, HEAD_DIM), lambda b: (b, 0, 0, 0)))
        out_shapes.append(jax.ShapeDtypeStruct((DEC_BATCH, 2, hb, HEAD_DIM), F32))
    return pl.pallas_call(
        _att_sample_kernel,
        grid=(DEC_BATCH,),
        in_specs=in_specs,
        out_specs=out_specs,
        out_shape=out_shapes,
        compiler_params=pltpu.CompilerParams(
            dimension_semantics=("arbitrary",), vmem_limit_bytes=VMEM_LIMIT),
        name="att_sample",
    )(proj_s, *caches, bias_s)


def kernel(x_prompt, x_sample, c_prompt, c_sample, cache_kv0, cache_kv1, cache_kv2, state_pool,
           norm_pre, norm_post, ada_w, ada_b, t5_bias, pool_w_in, pool_w_grp, pool_scale,
           pool_w_out, att_w_in, att_w_out):
    n_att = DEPTH // 2
    xp0 = x_prompt.reshape(M_PROMPT, D_MODEL)
    xs0 = jnp.zeros((SAMPLE_ROWS, D_MODEL), F32).at[0:DEC_BATCH].set(x_sample.reshape(DEC_BATCH, D_MODEL))
    c_all = jnp.zeros((C_ROWS, D_MODEL), F32)
    c_all = c_all.at[0:BATCH].set(c_prompt).at[C_SAMPLE_ROW0:C_SAMPLE_ROW0 + DEC_BATCH].set(c_sample)

    mod = _ada_all(c_all, ada_w, ada_b)
    gains_pre = norm_pre.reshape(DEPTH, 1, D_MODEL)
    gains_post = norm_post.reshape(DEPTH, 1, D_MODEL)

    bias_mats = _bias_matrices(t5_bias)
    bias_s = bias_mats[:, 0, 0:BAND + 1].reshape(N_GROUPS, N_HEADS, BAND + 1)
    bias_s = jnp.transpose(bias_s, (0, 2, 1))[..., None]
    caches = [c.reshape(n_att, DEC_BATCH, BAND, dil, 2, N_HEADS, HEAD_DIM)
              for c, dil in zip((cache_kv0, cache_kv1, cache_kv2), DILATIONS)]
    state_t = jnp.transpose(state_pool, (0, 2, 1, 3))

    kv_s = [[] for _ in range(N_GROUPS)]
    pool_p, pool_s, att_projs = [], [], []

    (h,) = _norm_step(xp0, None, mod, gains_post, gains_pre, None, 0, x_sample=xs0)
    x = None
    for i in range(DEPTH):
        li = i // 2
        if i % 2 == 0:
            proj = _matmul(h, pool_w_in, li, 2 * POOL_WIDTH, 1024, TM_MATMUL, F32, "pool_in_proj")
            a, pp, ps = _pool_mix(proj, state_t, pool_w_grp, pool_scale, li)
            pool_p.append(pp)
            pool_s.append(jnp.transpose(ps, (1, 0, 2)))
            y = _matmul(a, pool_w_out, li, D_MODEL, 512, TM_MATMUL, BF16, "pool_out_proj")
        else:
            proj = _matmul(h, att_w_in, li, ATT_IN_COLS, 1024, TM_MATMUL, F32, "att_in_proj", slabs=True)
            att_projs.append(proj)
            proj_s = jnp.transpose(proj[:, M_PROMPT:M_PROMPT + DEC_BATCH, :], (1, 0, 2))
            souts = _att_sample(proj_s, caches, bias_s, li)
            a_s = jnp.zeros((SAMPLE_ROWS, D_MODEL), F32).at[0:DEC_BATCH].set(souts[0].reshape(DEC_BATCH, D_MODEL))
            a = _att_prompt(proj, bias_mats, a_s.astype(BF16))
            for g in range(N_GROUPS):
                kv_s[g].append(souts[1 + g].reshape(DEC_BATCH, 1, 2, N_HEADS, HEAD_DIM))
            y = _matmul(a, att_w_out, li, D_MODEL, 1024, TM_MATMUL, BF16, "att_out_proj")
        last = i + 1 == DEPTH
        if i == 0:
            x, h = _norm_step(xp0, y, mod, gains_post, gains_pre, i, i + 1, x_sample=xs0)
        elif not last:
            x, h = _norm_step(x, y, mod, gains_post, gains_pre, i, i + 1)
        else:
            y_p, y_s = _norm_step(x, y, mod, gains_post, gains_pre, i, None, split_out=True)

    kv_p = [_kv_rows(att_projs[0], att_projs[1], g) for g in range(N_GROUPS)]
    y_prompt = y_p.reshape(BATCH, SEQ, D_MODEL)
    y_sample = y_s[0:DEC_BATCH].reshape(DEC_BATCH, 1, D_MODEL)
    return (y_prompt, y_sample, kv_p[0], kv_p[1], kv_p[2], jnp.stack(pool_p),
            jnp.stack(kv_s[0]), jnp.stack(kv_s[1]), jnp.stack(kv_s[2]), jnp.stack(pool_s))
```

```python
import functools

import numpy as np
import jax
import jax.numpy as jnp
from jax import lax
from jax.experimental import pallas as pl
from jax.experimental.pallas import tpu as pltpu

D_MODEL = 2048
BATCH = 4
SEQ = 2048
DEPTH = 4
DEC_BATCH = 8
HEAD_DIM = 128
N_HEADS = 16
DILATIONS = (1, 4, 16)
WINDOWS = (128, 512, 2048)
N_GROUPS = 3
QKV_WIDTH = N_GROUPS * D_MODEL
ATT_IN_COLS = 3 * QKV_WIDTH + D_MODEL
BAND = 128
ATT_SCALE = HEAD_DIM ** -0.5
LOG2E = 1.4426950408889634
POOL_WINDOWS = (2, 4, 8, 16)
POOL_WIDTH = 2 * D_MODEL
POOL_GROUP = POOL_WIDTH // 4
POOL_BUF = 15
N_BUCKETS = 32
T5_MAX_DIST = 2048
RMS_EPS = 1e-6
NEG_INF = -1e30

M_PROMPT = BATCH * SEQ
SAMPLE_ROWS = 64
M_ALL = M_PROMPT + SAMPLE_ROWS
C_ROWS = 2 * SAMPLE_ROWS
C_PROMPT_ROWS = 8
C_SAMPLE_ROW0 = SAMPLE_ROWS

TM_MATMUL = 2064
MXU_COLS = 256
TE_NORM = 512
TP_POOL = 1024
POOL_CHUNK = 512
POOL_HALO = 24
ATT_UNITS = 16
N_BLOCKS = SEQ // BAND
TR_KV = 256
ROW_CHUNK = 256
KV_PITCH_PAD = 8
VMEM_LIMIT = 58 * 1024 * 1024

F32 = jnp.float32
BF16 = jnp.bfloat16


def _silu(x):
    half = 0.5 * x
    return half + half * jnp.tanh(half)


def _cast_rows_to_bf16(src_ref, dst_ref, rows, chunk=ROW_CHUNK):
    def body(i, c):
        r = pl.multiple_of(i * chunk, chunk)
        dst_ref[pl.ds(r, chunk), :] = src_ref[pl.ds(r, chunk), :].astype(BF16)
        return c
    lax.fori_loop(0, rows // chunk, body, 0)


def _ada_kernel(c_ref, w_ref, b_ref, o_ref):
    a = _silu(c_ref[...]).astype(BF16)
    kc = 512
    acc = jnp.zeros(o_ref.shape, F32)
    for k0 in range(0, D_MODEL, kc):
        acc = acc + jnp.dot(a[:, k0:k0 + kc], w_ref[k0:k0 + kc, :].astype(BF16),
                            preferred_element_type=F32)
    o_ref[...] = acc + b_ref[...]


def _ada_all(c_all, ada_w, ada_b):
    tn = 1024
    n = 3 * D_MODEL
    return pl.pallas_call(
        _ada_kernel,
        grid=(DEPTH, n // tn),
        in_specs=[pl.BlockSpec((C_ROWS, D_MODEL), lambda l, j: (0, 0)),
                  pl.BlockSpec((None, D_MODEL, tn), lambda l, j: (l, 0, j)),
                  pl.BlockSpec((None, 1, tn), lambda l, j: (l, 0, j))],
        out_specs=pl.BlockSpec((None, C_ROWS, tn), lambda l, j: (l, 0, j)),
        out_shape=jax.ShapeDtypeStruct((DEPTH, C_ROWS, n), F32),
        compiler_params=pltpu.CompilerParams(
            dimension_semantics=("arbitrary", "arbitrary"), vmem_limit_bytes=VMEM_LIMIT),
        name="ada_mod",
    )(c_all, ada_w, ada_b.reshape(DEPTH, 1, n))


def _t5_bucket(dist):
    dist = np.asarray(dist, dtype=np.int64)
    max_exact = N_BUCKETS // 2
    ratio = np.log(np.maximum(dist, 1) / max_exact) / np.log(T5_MAX_DIST / max_exact)
    large = np.minimum(max_exact + (ratio * (N_BUCKETS - max_exact)).astype(np.int64), N_BUCKETS - 1)
    return np.where(dist < max_exact, dist, large).astype(np.int32)


def _bucket_index_table():
    rel = np.arange(BAND)[:, None] + BAND - np.arange(2 * BAND)[None, :]
    inband = (rel >= 0) & (rel <= BAND)
    out = []
    for dil in DILATIONS:
        bucket = _t5_bucket(np.clip(rel, 0, BAND) * dil)
        out.append(np.where(inband, bucket, -1))
    return np.stack(out).astype(np.int32)


def _bias_kernel(tab_ref, idx_ref, o_ref):
    g = pl.program_id(0)
    idx = idx_ref[...]

    def head(h, c):
        acc = jnp.full(idx.shape, NEG_INF, F32)
        for b in range(N_BUCKETS):
            acc = jnp.where(idx == b, tab_ref[b, g * N_HEADS + h], acc)
        o_ref[h] = acc
        return c
    lax.fori_loop(0, N_HEADS, head, 0)


def _bias_matrices(t5_bias):
    idx = jnp.asarray(_bucket_index_table())
    n_sub = N_GROUPS * N_HEADS
    return pl.pallas_call(
        _bias_kernel,
        grid=(N_GROUPS,),
        in_specs=[pl.BlockSpec(memory_space=pltpu.SMEM),
                  pl.BlockSpec((None, BAND, 2 * BAND), lambda g: (g, 0, 0))],
        out_specs=pl.BlockSpec((N_HEADS, BAND, 2 * BAND), lambda g: (g, 0, 0)),
        out_shape=jax.ShapeDtypeStruct((n_sub, BAND, 2 * BAND), F32),
        name="t5_bias_mats",
    )(t5_bias, idx)


def _norm_kernel(*refs, has_post, has_pre, split_in, split_out, n_prompt_tiles, tiles_per_batch):
    refs = list(refs)
    xp_ref = refs.pop(0)
    xs_ref = refs.pop(0) if split_in else xp_ref
    if has_post:
        y_ref, gpost_ref, gate_p_ref, gate_s_ref = refs[:4]
        refs = refs[4:]
    if has_pre:
        gpre_ref, shift_p_ref, scale_p_ref, shift_s_ref, scale_s_ref = refs[:5]
        refs = refs[5:]
    if has_post:
        xop_ref = refs.pop(0)
        xos_ref = refs.pop(0) if split_out else xop_ref
    if has_pre:
        h_ref = refs.pop(0)

    def rms(v, g):
        return v * lax.rsqrt(jnp.mean(v * v, axis=-1, keepdims=True) + RMS_EPS) * g

    def body(x_ref, xo_ref, rows, gate, shift, scale):
        x = x_ref[rows, :]
        if has_post:
            x = x + gate * rms(y_ref[rows, :].astype(F32), gpost_ref[...])
            xo_ref[rows, :] = x
        if has_pre:
            h = rms(x, gpre_ref[...]) * (1.0 + scale) + shift
            h_ref[rows, :] = h.astype(BF16)

    t = pl.program_id(0)

    @pl.when(t < n_prompt_tiles)
    def _():
        row = pl.ds(t // tiles_per_batch, 1)
        body(xp_ref, xop_ref if has_post else None, slice(None),
             gate_p_ref[row, :] if has_post else None,
             shift_p_ref[row, :] if has_pre else None,
             scale_p_ref[row, :] if has_pre else None)

    @pl.when(t == n_prompt_tiles)
    def _():
        body(xs_ref, xos_ref if has_post else None, slice(0, SAMPLE_ROWS),
             gate_s_ref[...] if has_post else None,
             shift_s_ref[...] if has_pre else None,
             scale_s_ref[...] if has_pre else None)


def _norm_step(x, y, mod, norm_post, norm_pre, post_layer, pre_layer, x_sample=None, split_out=False):
    has_post = post_layer is not None
    has_pre = pre_layer is not None
    split_in = x_sample is not None
    te = TE_NORM
    npt = M_PROMPT // te
    tiles_per_batch = SEQ // te
    row_spec = pl.BlockSpec((te, D_MODEL), lambda t: (t, 0))
    prompt_row_spec = pl.BlockSpec((te, D_MODEL), lambda t: (jnp.minimum(t, npt - 1), 0))
    sample_row_spec = pl.BlockSpec((SAMPLE_ROWS, D_MODEL), lambda t: (0, 0))

    def mod_p_spec(layer, part):
        return pl.BlockSpec((None, C_PROMPT_ROWS, D_MODEL), lambda t: (layer, 0, part))

    def mod_s_spec(layer, part):
        return pl.BlockSpec((None, SAMPLE_ROWS, D_MODEL),
                            lambda t: (layer, C_SAMPLE_ROW0 // SAMPLE_ROWS, part))

    def gain_spec(layer):
        return pl.BlockSpec((None, 1, D_MODEL), lambda t: (layer, 0, 0))

    if split_in:
        args, in_specs = [x, x_sample], [prompt_row_spec, sample_row_spec]
    else:
        args, in_specs = [x], [row_spec]
    out_shapes, out_specs = [], []
    if has_post:
        args += [y, norm_post, mod, mod]
        in_specs += [row_spec, gain_spec(post_layer), mod_p_spec(post_layer, 2), mod_s_spec(post_layer, 2)]
        if split_out:
            out_shapes += [jax.ShapeDtypeStruct((M_PROMPT, D_MODEL), F32),
                           jax.ShapeDtypeStruct((SAMPLE_ROWS, D_MODEL), F32)]
            out_specs += [prompt_row_spec, sample_row_spec]
        else:
            out_shapes.append(jax.ShapeDtypeStruct((M_ALL, D_MODEL), F32))
            out_specs.append(row_spec)
    if has_pre:
        args += [norm_pre, mod, mod, mod, mod]
        in_specs += [gain_spec(pre_layer), mod_p_spec(pre_layer, 0), mod_p_spec(pre_layer, 1),
                     mod_s_spec(pre_layer, 0), mod_s_spec(pre_layer, 1)]
        out_shapes.append(jax.ShapeDtypeStruct((M_ALL, D_MODEL), BF16))
        out_specs.append(row_spec)
    outs = pl.pallas_call(
        functools.partial(_norm_kernel, has_post=has_post, has_pre=has_pre, split_in=split_in,
                          split_out=split_out, n_prompt_tiles=npt, tiles_per_batch=tiles_per_batch),
        grid=(npt + 1,),
        in_specs=in_specs,
        out_specs=out_specs,
        out_shape=out_shapes,
        compiler_params=pltpu.CompilerParams(
            dimension_semantics=("arbitrary",), vmem_limit_bytes=VMEM_LIMIT),
        name="norm_step",
    )(*args)
    return outs


def _mm_kernel(a_ref, w_ref, o_ref, *, slabs):
    tn = w_ref.shape[1]
    for c0 in range(0, tn, MXU_COLS):
        acc = jnp.dot(a_ref[...], w_ref[:, c0:c0 + MXU_COLS].astype(BF16),
                      preferred_element_type=F32).astype(o_ref.dtype)
        if slabs:
            for c in range(MXU_COLS // HEAD_DIM):
                o_ref[c0 // HEAD_DIM + c] = acc[:, c * HEAD_DIM:(c + 1) * HEAD_DIM]
        else:
            o_ref[:, c0:c0 + MXU_COLS] = acc


def _matmul(a, w, layer, n_out, tn, tm, out_dtype, name, slabs=False):
    m, k = a.shape
    if slabs:
        out_spec = pl.BlockSpec((tn // HEAD_DIM, tm, HEAD_DIM), lambda j, i: (j, i, 0))
        out_shape = jax.ShapeDtypeStruct((n_out // HEAD_DIM, m, HEAD_DIM), out_dtype)
    else:
        out_spec = pl.BlockSpec((tm, tn), lambda j, i: (i, j))
        out_shape = jax.ShapeDtypeStruct((m, n_out), out_dtype)
    return pl.pallas_call(
        functools.partial(_mm_kernel, slabs=slabs),
        grid=(n_out // tn, m // tm),
        in_specs=[pl.BlockSpec((tm, k), lambda j, i: (i, 0)),
                  pl.BlockSpec((None, k, tn), lambda j, i: (layer, 0, j))],
        out_specs=out_spec,
        out_shape=out_shape,
        compiler_params=pltpu.CompilerParams(
            dimension_semantics=("arbitrary", "arbitrary"), vmem_limit_bytes=VMEM_LIMIT),
        name=name,
    )(a, w)


def _pool_kernel(u_ref, halo_ref, z_ref, st_ref, w_ref, sc_ref,
                 a_ref, pp_ref, ps_ref, wb_ref, buf_a, buf_b, *, n_prompt_tiles, tiles_per_batch):
    g = pl.program_id(0)
    t = pl.program_id(1)
    tp = TP_POOL
    ch = POOL_CHUNK
    h0 = POOL_HALO

    @pl.when(t == 0)
    def _():
        _cast_rows_to_bf16(w_ref, wb_ref, POOL_GROUP)

    def finish(r, z, rows):
        y = jnp.dot(r.astype(BF16), wb_ref[...], preferred_element_type=F32) * sc_ref[...]
        a_ref[rows, :] = (y * _silu(z)).astype(BF16)

    def prompt_tile(n_steps):
        w = 2 ** n_steps
        first = (t % tiles_per_batch) == 0
        zero8 = jnp.zeros((8, POOL_GROUP), F32)
        n = ch + h0 - 8
        for c in range(tp // ch):
            r0 = c * ch
            src, dst = buf_a.at[c], buf_b.at[c]
            src[0:8, :] = zero8
            dst[0:8, :] = zero8
            if c == 0:
                src[8:h0, :] = jnp.where(first, 0.0, halo_ref[...])
            else:
                src[8:h0, :] = u_ref[r0 - 16:r0, :]
            u = u_ref[r0:r0 + ch, :]
            src[h0:h0 + ch, :] = u
            for s in range(n_steps):
                sh = 2 ** s
                dst[8:8 + n, :] = src[8:8 + n, :] + src[8 - sh:8 - sh + n, :]
                src, dst = dst, src
            pos = (t % tiles_per_batch) * tp + r0 + lax.broadcasted_iota(jnp.int32, (ch, 1), 0)
            inv_cnt = 1.0 / jnp.minimum(pos + 1, w).astype(F32)
            r = src[h0:h0 + ch, :] * inv_cnt - u
            finish(r, z_ref[r0:r0 + ch, :], slice(r0, r0 + ch))

        @pl.when((t % tiles_per_batch) == tiles_per_batch - 1)
        def _():
            pp_ref[...] = u_ref[tp - POOL_BUF:tp, :]

    def sample_tile(n_steps):
        w = 2 ** n_steps
        u_new = u_ref[0:DEC_BATCH, :]
        acc = u_new
        for k in range(1, w):
            acc = acc + st_ref[POOL_BUF - k]
        r = acc / float(w) - u_new
        rows_s = buf_a.at[0]
        rows_s[0:DEC_BATCH, :] = r
        rows_s[DEC_BATCH:SAMPLE_ROWS, :] = jnp.zeros((SAMPLE_ROWS - DEC_BATCH, POOL_GROUP), F32)
        finish(rows_s[0:SAMPLE_ROWS, :], z_ref[0:SAMPLE_ROWS, :], slice(0, SAMPLE_ROWS))
        for k in range(POOL_BUF - 1):
            ps_ref[k] = st_ref[k + 1]
        ps_ref[POOL_BUF - 1] = u_new

    for gi in range(len(POOL_WINDOWS)):
        @pl.when((g == gi) & (t < n_prompt_tiles))
        def _(gi=gi):
            prompt_tile(gi + 1)

        @pl.when((g == gi) & (t == n_prompt_tiles))
        def _(gi=gi):
            sample_tile(gi + 1)


def _pool_mix(proj, state_t, w_grp, scale, layer):
    tp = TP_POOL
    npt = M_PROMPT // tp
    tpb = SEQ // tp
    ng = len(POOL_WINDOWS)
    halo_blocks = tp // 16
    outs = pl.pallas_call(
        functools.partial(_pool_kernel, n_prompt_tiles=npt, tiles_per_batch=tpb),
        grid=(ng, npt + 1),
        in_specs=[
            pl.BlockSpec((tp, POOL_GROUP), lambda g, t: (t, g)),
            pl.BlockSpec((16, POOL_GROUP), lambda g, t: (jnp.maximum(t * halo_blocks - 1, 0), g)),
            pl.BlockSpec((tp, POOL_GROUP), lambda g, t: (t, ng + g)),
            pl.BlockSpec((None, POOL_BUF, DEC_BATCH, POOL_GROUP), lambda g, t: (layer, 0, 0, g)),
            pl.BlockSpec((None, None, POOL_GROUP, POOL_GROUP), lambda g, t: (layer, g, 0, 0)),
            pl.BlockSpec((None, 1, POOL_GROUP), lambda g, t: (layer, 0, g)),
        ],
        out_specs=[
            pl.BlockSpec((tp, POOL_GROUP), lambda g, t: (t, g)),
            pl.BlockSpec((None, POOL_BUF, POOL_GROUP),
                         lambda g, t: (jnp.minimum(t // tpb, BATCH - 1), 0, g)),
            pl.BlockSpec((POOL_BUF, DEC_BATCH, POOL_GROUP), lambda g, t: (0, 0, g)),
        ],
        out_shape=[
            jax.ShapeDtypeStruct((M_ALL, POOL_WIDTH), BF16),
            jax.ShapeDtypeStruct((BATCH, POOL_BUF, POOL_WIDTH), F32),
            jax.ShapeDtypeStruct((POOL_BUF, DEC_BATCH, POOL_WIDTH), F32),
        ],
        scratch_shapes=[pltpu.VMEM((POOL_GROUP, POOL_GROUP), BF16),
                        pltpu.VMEM((tp // POOL_CHUNK, POOL_CHUNK + POOL_HALO, POOL_GROUP), F32),
                        pltpu.VMEM((tp // POOL_CHUNK, POOL_CHUNK + POOL_HALO, POOL_GROUP), F32)],
        compiler_params=pltpu.CompilerParams(
            dimension_semantics=("arbitrary", "arbitrary"), vmem_limit_bytes=VMEM_LIMIT),
        name="pool_mix",
    )(proj, proj, proj, state_t, w_grp, scale.reshape(-1, 1, POOL_WIDTH))
    return outs


def _att_prompt_kernel(x_ref, bias_ref, as_ref, a_ref, *scratch):
    @pl.when(pl.program_id(0) < BATCH)
    def _():
        _att_prompt_tile(x_ref, bias_ref, a_ref, *scratch)

    @pl.when(pl.program_id(0) == BATCH)
    def _():
        a_ref[0:SAMPLE_ROWS, :] = as_ref[...]


def _att_prompt_tile(x_ref, bias_ref, a_ref, qd, kd, vd, od, ld, on, ln, tmp3):
    tmp = tmp3.at[0]
    q_refs = tuple(x_ref.at[g] for g in range(N_GROUPS))
    k_refs = tuple(x_ref.at[N_GROUPS + g] for g in range(N_GROUPS))
    v_refs = tuple(x_ref.at[2 * N_GROUPS + g] for g in range(N_GROUPS))
    z_ref = x_ref.at[3 * N_GROUPS]
    bias_refs = tuple(bias_ref.at[g] for g in range(N_GROUPS))
    nu = ATT_UNITS

    for g, dil in enumerate(DILATIONS):
        n = SEQ // dil
        nb = n // BAND
        zero_blk = jnp.zeros((BAND, HEAD_DIM), BF16)
        kd[g, 0:BAND, :] = zero_blk
        vd[g, 0:BAND, :] = zero_blk
        for src, dst, off, mul in ((q_refs[g], qd, 0, ATT_SCALE * LOG2E), (k_refs[g], kd, BAND, None),
                                   (v_refs[g], vd, BAND, None)):
            def to_bf16(x, mul=mul):
                return (x if mul is None else x * mul).astype(BF16)

            if dil == 1:
                dst[g, off:off + SEQ, :] = to_bf16(src[...])
            elif dil == 4:
                for r in range(dil):
                    dst[g, off + r * n:off + (r + 1) * n, :] = to_bf16(src[pl.ds(r, n, stride=dil), :])
            else:
                stage = tmp3.at[(off > 0) + (dst is vd)]
                quarter = SEQ // 4
                for r_lo in range(4):
                    stage[r_lo * quarter:(r_lo + 1) * quarter, :] = src[pl.ds(r_lo, quarter, stride=4), :]
                for r in range(dil):
                    r_lo, r_hi = r % 4, r // 4
                    dst[g, off + r * n:off + (r + 1) * n, :] = to_bf16(
                        stage[pl.ds(r_lo * quarter + r_hi, n, stride=4), :])

    def unit_batch(g, u0, seq_blocks):
        with_prev = seq_blocks > 1
        rows = nu * BAND
        row = pl.multiple_of(u0 * BAND, rows)
        shape3 = (nu, BAND, HEAD_DIM)
        q = qd[g, pl.ds(row, rows), :].reshape(shape3)
        kc = kd[g, pl.ds(row + BAND, rows), :].reshape(shape3)
        vc = vd[g, pl.ds(row + BAND, rows), :].reshape(shape3)
        bias = bias_refs[g]
        s_c = jnp.einsum('uqe,uke->uqk', q, kc, preferred_element_type=F32) \
            + (bias[:, BAND:2 * BAND] * LOG2E)[None]
        if with_prev:
            kp = kd[g, pl.ds(row, rows), :].reshape(shape3)
            vp = vd[g, pl.ds(row, rows), :].reshape(shape3)
            bias_p = jnp.broadcast_to((bias[:, 0:BAND] * LOG2E)[None], (nu, BAND, BAND))
            blk = u0 + lax.broadcasted_iota(jnp.int32, (nu, BAND, BAND), 0)
            bias_p = jnp.where((blk & (seq_blocks - 1)) == 0, NEG_INF, bias_p)
            s_p = jnp.einsum('uqe,uke->uqk', q, kp, preferred_element_type=F32) + bias_p
            m = jnp.max(jnp.maximum(s_c, s_p), axis=-1, keepdims=True)
            p_c = jnp.exp2(s_c - m)
            p_p = jnp.exp2(s_p - m)
            l = jnp.sum(p_c + p_p, axis=-1, keepdims=True)
            acc = jnp.einsum('uqk,uke->uqe', p_c.astype(BF16), vc, preferred_element_type=F32) \
                + jnp.einsum('uqk,uke->uqe', p_p.astype(BF16), vp, preferred_element_type=F32)
        else:
            m = jnp.max(s_c, axis=-1, keepdims=True)
            p_c = jnp.exp2(s_c - m)
            l = jnp.sum(p_c, axis=-1, keepdims=True)
            acc = jnp.einsum('uqk,uke->uqe', p_c.astype(BF16), vc, preferred_element_type=F32)
        od[g, pl.ds(row, rows), :] = (acc / l).reshape(rows, HEAD_DIM)
        ld[g, pl.ds(row, rows), :] = jnp.broadcast_to(m + jnp.log(l) * LOG2E, shape3).reshape(rows, HEAD_DIM)

    for g, dil in enumerate(DILATIONS):
        nb = (SEQ // dil) // BAND

        def batch_body(i, c, g=g, nb=nb):
            unit_batch(g, i * nu, nb)
            return c
        lax.fori_loop(0, N_BLOCKS // nu, batch_body, 0)

    def interleave4(src, dst, n_rows):
        quarter = n_rows // 4
        for r in range(4):
            dst[pl.ds(r, quarter, stride=4), :] = src[r * quarter:(r + 1) * quarter, :]

    for src_all, dst_all in ((od, on), (ld, ln)):
        interleave4(src_all.at[1], dst_all.at[0], SEQ)
        for r_lo in range(4):
            for r_hi in range(4):
                r = r_lo + 4 * r_hi
                tmp[pl.ds(r_lo * (SEQ // 4) + r_hi, BAND, stride=4), :] = src_all[2, r * BAND:(r + 1) * BAND, :]
        interleave4(tmp, dst_all.at[1], SEQ)

    def comb(i, c):
        rs = pl.ds(pl.multiple_of(i * ROW_CHUNK, ROW_CHUNK), ROW_CHUNK)
        l0, l1, l2 = ld[0, rs, :], ln[0, rs, :], ln[1, rs, :]
        mx = jnp.maximum(jnp.maximum(l0, l1), l2)
        w0, w1, w2 = jnp.exp2(l0 - mx), jnp.exp2(l1 - mx), jnp.exp2(l2 - mx)
        o = (w0 * od[0, rs, :] + w1 * on[0, rs, :] + w2 * on[1, rs, :]) / (w0 + w1 + w2)
        a_ref[rs, :] = (o * _silu(z_ref[rs, :])).astype(BF16)
        return c
    lax.fori_loop(0, SEQ // ROW_CHUNK, comb, 0)


def _att_prompt(proj, bias_mats, a_sample):
    hb = N_HEADS

    def head_blk(b, h):
        return jnp.where(b < BATCH, h, hb - 1)

    n_pg = ATT_IN_COLS // D_MODEL
    in_specs = [
        pl.BlockSpec((n_pg, None, SEQ, HEAD_DIM),
                     lambda b, h: (0, head_blk(b, h), jnp.minimum(b, BATCH - 1), 0)),
        pl.BlockSpec((N_GROUPS, None, BAND, 2 * BAND), lambda b, h: (0, head_blk(b, h), 0, 0)),
        pl.BlockSpec((SAMPLE_ROWS, HEAD_DIM), lambda b, h: (0, h)),
    ]
    scratch = [pltpu.VMEM((N_GROUPS, SEQ, HEAD_DIM), BF16)] + \
              [pltpu.VMEM((N_GROUPS, SEQ + BAND, HEAD_DIM), BF16)] * 2 + \
              [pltpu.VMEM((N_GROUPS, SEQ, HEAD_DIM), F32)] * 2 + \
              [pltpu.VMEM((N_GROUPS - 1, SEQ, HEAD_DIM), F32)] * 2 + \
              [pltpu.VMEM((3, SEQ, HEAD_DIM), F32)]
    return pl.pallas_call(
        _att_prompt_kernel,
        grid=(BATCH + 1, N_HEADS),
        in_specs=in_specs,
        out_specs=pl.BlockSpec((SEQ, HEAD_DIM), lambda b, h: (b, h)),
        out_shape=jax.ShapeDtypeStruct((M_ALL, D_MODEL), BF16),
        scratch_shapes=scratch,
        compiler_params=pltpu.CompilerParams(
            dimension_semantics=("arbitrary", "arbitrary"), vmem_limit_bytes=VMEM_LIMIT),
        name="att_prompt",
    )(proj.reshape(n_pg, hb, M_ALL, HEAD_DIM), bias_mats.reshape(N_GROUPS, hb, BAND, 2 * BAND), a_sample)


def _kv_rows_kernel(k_a, v_a, k_b, v_b, o_ref, *, tr):
    pitch = tr + KV_PITCH_PAD

    def emit(srcs):
        flats = [src.reshape(N_HEADS * pitch, HEAD_DIM) for src in srcs]

        def body(i, c):
            t0 = i * 8
            for k in range(8):
                for part, flat in enumerate(flats):
                    o_ref[t0 + k, part] = flat[pl.ds(t0 + k, N_HEADS, stride=pitch), :]
            return c
        lax.fori_loop(0, tr // 8, body, 0)

    @pl.when(pl.program_id(0) == 0)
    def _():
        emit((k_a, v_a))

    @pl.when(pl.program_id(0) == 1)
    def _():
        emit((k_b, v_b))


def _kv_rows(proj_a, proj_b, g):
    keep = WINDOWS[g]
    tr = min(keep, TR_KV)
    nt = keep // tr
    first_blk = (SEQ - keep) // tr
    blks_per_batch = SEQ // tr

    def row_blk(b, t):
        return b * blks_per_batch + first_blk + t

    def spec(layer, part):
        parked = row_blk(BATCH - 1, nt - 1) if layer == 0 else row_blk(0, 0)
        col0 = ((1 + part) * N_GROUPS + g) * N_HEADS
        return pl.BlockSpec((pl.Element(N_HEADS), pl.Element(tr + KV_PITCH_PAD), pl.Element(HEAD_DIM)),
                            lambda l, b, t: (col0, jnp.where(l == layer, row_blk(b, t), parked) * tr, 0))

    return pl.pallas_call(
        functools.partial(_kv_rows_kernel, tr=tr),
        grid=(2, BATCH, nt),
        in_specs=[spec(0, 0), spec(0, 1), spec(1, 0), spec(1, 1)],
        out_specs=pl.BlockSpec((None, None, tr, 2, N_HEADS, HEAD_DIM), lambda l, b, t: (l, b, t, 0, 0, 0)),
        out_shape=jax.ShapeDtypeStruct((2, BATCH, keep, 2, N_HEADS, HEAD_DIM), F32),
        compiler_params=pltpu.CompilerParams(
            dimension_semantics=("arbitrary",) * 3, vmem_limit_bytes=VMEM_LIMIT),
        name="kv_rows",
    )(proj_a, proj_a, proj_b, proj_b)


def _att_sample_kernel(x_ref, c0, c1, c2, bias_ref, a_ref, kv0, kv1, kv2):
    caches = (c0, c1, c2)
    kv_outs = (kv0, kv1, kv2)
    hb = N_HEADS
    outs, lses = [], []
    for g in range(N_GROUPS):
        q = x_ref[g * hb:(g + 1) * hb, :]
        kn = x_ref[(3 + g) * hb:(4 + g) * hb, :]
        vn = x_ref[(6 + g) * hb:(7 + g) * hb, :]
        kv_outs[g][0] = kn
        kv_outs[g][1] = vn
        kc = caches[g][:, 0]
        vc = caches[g][:, 1]
        s_c = jnp.sum(kc * q[None], axis=-1, keepdims=True) * ATT_SCALE + bias_ref[g, 0:BAND]
        s_n = jnp.sum(kn * q, axis=-1, keepdims=True) * ATT_SCALE + bias_ref[g, BAND]
        m = jnp.maximum(jnp.max(s_c, axis=0), s_n)
        p_c = jnp.exp(s_c - m[None])
        p_n = jnp.exp(s_n - m)
        l = jnp.sum(p_c, axis=0) + p_n
        o = (jnp.sum(p_c * vc, axis=0) + p_n * vn) / l
        outs.append(o)
        lses.append(m + jnp.log(l))
    mx = jnp.maximum(jnp.maximum(lses[0], lses[1]), lses[2])
    ws = [jnp.exp(ls - mx) for ls in lses]
    o = (ws[0] * outs[0] + ws[1] * outs[1] + ws[2] * outs[2]) / (ws[0] + ws[1] + ws[2])
    z = x_ref[9 * hb:10 * hb, :]
    a_ref[...] = o * _silu(z)


def _att_sample(proj_s, caches, bias_s, layer):
    hb = N_HEADS
    in_specs = [pl.BlockSpec((None, ATT_IN_COLS // HEAD_DIM, HEAD_DIM), lambda b: (b, 0, 0))]
    for g in range(N_GROUPS):
        in_specs.append(pl.BlockSpec((None, None, BAND, None, 2, hb, HEAD_DIM),
                                     lambda b: (layer, b, 0, 0, 0, 0, 0)))
    in_specs.append(pl.BlockSpec((N_GROUPS, BAND + 1, hb, 1), lambda b: (0, 0, 0, 0)))
    out_specs = [pl.BlockSpec((None, hb, HEAD_DIM), lambda b: (b, 0, 0))]
    out_shapes = [jax.ShapeDtypeStruct((DEC_BATCH, hb, HEAD_DIM), F32)]
    for g in range(N_GROUPS):
        out_specs.append(pl.BlockSpec((None, 2, hb, HEAD_DIM), lambda b: (b, 0, 0, 0)))
        out_shapes.append(jax.ShapeDtypeStruct((DEC_BATCH, 2, hb, HEAD_DIM), F32))
    return pl.pallas_call(
        _att_sample_kernel,
        grid=(DEC_BATCH,),
        in_specs=in_specs,
        out_specs=out_specs,
        out_shape=out_shapes,
        compiler_params=pltpu.CompilerParams(
            dimension_semantics=("arbitrary",), vmem_limit_bytes=VMEM_LIMIT),
        name="att_sample",
    )(proj_s, *caches, bias_s)


def kernel(x_prompt, x_sample, c_prompt, c_sample, cache_kv0, cache_kv1, cache_kv2, state_pool,
           norm_pre, norm_post, ada_w, ada_b, t5_bias, pool_w_in, pool_w_grp, pool_scale,
           pool_w_out, att_w_in, att_w_out):
    n_att = DEPTH // 2
    xp0 = x_prompt.reshape(M_PROMPT, D_MODEL)
    xs0 = jnp.zeros((SAMPLE_ROWS, D_MODEL), F32).at[0:DEC_BATCH].set(x_sample.reshape(DEC_BATCH, D_MODEL))
    c_all = jnp.zeros((C_ROWS, D_MODEL), F32)
    c_all = c_all.at[0:BATCH].set(c_prompt).at[C_SAMPLE_ROW0:C_SAMPLE_ROW0 + DEC_BATCH].set(c_sample)

    mod = _ada_all(c_all, ada_w, ada_b)
    gains_pre = norm_pre.reshape(DEPTH, 1, D_MODEL)
    gains_post = norm_post.reshape(DEPTH, 1, D_MODEL)

    bias_mats = _bias_matrices(t5_bias)
    bias_s = bias_mats[:, 0, 0:BAND + 1].reshape(N_GROUPS, N_HEADS, BAND + 1)
    bias_s = jnp.transpose(bias_s, (0, 2, 1))[..., None]
    caches = [c.reshape(n_att, DEC_BATCH, BAND, dil, 2, N_HEADS, HEAD_DIM)
              for c, dil in zip((cache_kv0, cache_kv1, cache_kv2), DILATIONS)]
    state_t = jnp.transpose(state_pool, (0, 2, 1, 3))

    kv_s = [[] for _ in range(N_GROUPS)]
    pool_p, pool_s, att_projs = [], [], []

    (h,) = _norm_step(xp0, None, mod, gains_post, gains_pre, None, 0, x_sample=xs0)
    x = None
    for i in range(DEPTH):
        li = i // 2
        if i % 2 == 0:
            proj = _matmul(h, pool_w_in, li, 2 * POOL_WIDTH, 1024, TM_MATMUL, F32, "pool_in_proj")
            a, pp, ps = _pool_mix(proj, state_t, pool_w_grp, pool_scale, li)
            pool_p.append(pp)
            pool_s.append(jnp.transpose(ps, (1, 0, 2)))
            y = _matmul(a, pool_w_out, li, D_MODEL, 512, TM_MATMUL, BF16, "pool_out_proj")
        else:
            proj = _matmul(h, att_w_in, li, ATT_IN_COLS, 1024, TM_MATMUL, F32, "att_in_proj", slabs=True)
            att_projs.append(proj)
            proj_s = jnp.transpose(proj[:, M_PROMPT:M_PROMPT + DEC_BATCH, :], (1, 0, 2))
            souts = _att_sample(proj_s, caches, bias_s, li)
            a_s = jnp.zeros((SAMPLE_ROWS, D_MODEL), F32).at[0:DEC_BATCH].set(souts[0].reshape(DEC_BATCH, D_MODEL))
            a = _att_prompt(proj, bias_mats, a_s.astype(BF16))
            for g in range(N_GROUPS):
                kv_s[g].append(souts[1 + g].reshape(DEC_BATCH, 1, 2, N_HEADS, HEAD_DIM))
            y = _matmul(a, att_w_out, li, D_MODEL, 1024, TM_MATMUL, BF16, "att_out_proj")
        last = i + 1 == DEPTH
        if i == 0:
            x, h = _norm_step(xp0, y, mod, gains_post, gains_pre, i, i + 1, x_sample=xs0)
        elif not last:
            x, h = _norm_step(x, y, mod, gains_post, gains_pre, i, i + 1)
        else:
            y_p, y_s = _norm_step(x, y, mod, gains_post, gains_pre, i, None, split_out=True)

    kv_p = [_kv_rows(att_projs[0], att_projs[1], g) for g in range(N_GROUPS)]
    y_prompt = y_p.reshape(BATCH, SEQ, D_MODEL)
    y_sample = y_s[0:DEC_BATCH].reshape(DEC_BATCH, 1, D_MODEL)
    return (y_prompt, y_sample, kv_p[0], kv_p[1], kv_p[2], jnp.stack(pool_p),
            jnp.stack(kv_s[0]), jnp.stack(kv_s[1]), jnp.stack(kv_s[2]), jnp.stack(pool_s))
```

```python
import functools

import numpy as np
import jax
import jax.numpy as jnp
from jax import lax
from jax.experimental import pallas as pl
from jax.experimental.pallas import tpu as pltpu

D_MODEL = 2048
BATCH = 4
SEQ = 2048
DEPTH = 4
DEC_BATCH = 8
HEAD_DIM = 128
N_HEADS = 16
DILATIONS = (1, 4, 16)
WINDOWS = (128, 512, 2048)
N_GROUPS = 3
QKV_WIDTH = N_GROUPS * D_MODEL
ATT_IN_COLS = 3 * QKV_WIDTH + D_MODEL
BAND = 128
ATT_SCALE = HEAD_DIM ** -0.5
LOG2E = 1.4426950408889634
POOL_WINDOWS = (2, 4, 8, 16)
POOL_WIDTH = 2 * D_MODEL
POOL_GROUP = POOL_WIDTH // 4
POOL_BUF = 15
N_BUCKETS = 32
T5_MAX_DIST = 2048
RMS_EPS = 1e-6
NEG_INF = -1e30

M_PROMPT = BATCH * SEQ
SAMPLE_ROWS = 64
M_ALL = M_PROMPT + SAMPLE_ROWS
C_ROWS = 2 * SAMPLE_ROWS
C_PROMPT_ROWS = 8
C_SAMPLE_ROW0 = SAMPLE_ROWS

TM_MATMUL = 2064
MXU_COLS = 256
TE_NORM = 512
TP_POOL = 1024
POOL_HALO = 24
ATT_UNITS = 16
N_BLOCKS = SEQ // BAND
TR_KV = 256
ROW_CHUNK = 256
KV_PITCH_PAD = 8
VMEM_LIMIT = 58 * 1024 * 1024

F32 = jnp.float32
BF16 = jnp.bfloat16


def _silu(x):
    half = 0.5 * x
    return half + half * jnp.tanh(half)


def _cast_rows_to_bf16(src_ref, dst_ref, rows, chunk=ROW_CHUNK):
    def body(i, c):
        r = pl.multiple_of(i * chunk, chunk)
        dst_ref[pl.ds(r, chunk), :] = src_ref[pl.ds(r, chunk), :].astype(BF16)
        return c
    lax.fori_loop(0, rows // chunk, body, 0)


def _ada_kernel(c_ref, w_ref, b_ref, o_ref):
    a = _silu(c_ref[...]).astype(BF16)
    kc = 512
    acc = jnp.zeros(o_ref.shape, F32)
    for k0 in range(0, D_MODEL, kc):
        acc = acc + jnp.dot(a[:, k0:k0 + kc], w_ref[k0:k0 + kc, :].astype(BF16),
                            preferred_element_type=F32)
    o_ref[...] = acc + b_ref[...]


def _ada_all(c_all, ada_w, ada_b):
    tn = 1024
    n = 3 * D_MODEL
    return pl.pallas_call(
        _ada_kernel,
        grid=(DEPTH, n // tn),
        in_specs=[pl.BlockSpec((C_ROWS, D_MODEL), lambda l, j: (0, 0)),
                  pl.BlockSpec((None, D_MODEL, tn), lambda l, j: (l, 0, j)),
                  pl.BlockSpec((None, 1, tn), lambda l, j: (l, 0, j))],
        out_specs=pl.BlockSpec((None, C_ROWS, tn), lambda l, j: (l, 0, j)),
        out_shape=jax.ShapeDtypeStruct((DEPTH, C_ROWS, n), F32),
        compiler_params=pltpu.CompilerParams(
            dimension_semantics=("arbitrary", "arbitrary"), vmem_limit_bytes=VMEM_LIMIT),
        name="ada_mod",
    )(c_all, ada_w, ada_b.reshape(DEPTH, 1, n))


def _t5_bucket(dist):
    dist = np.asarray(dist, dtype=np.int64)
    max_exact = N_BUCKETS // 2
    ratio = np.log(np.maximum(dist, 1) / max_exact) / np.log(T5_MAX_DIST / max_exact)
    large = np.minimum(max_exact + (ratio * (N_BUCKETS - max_exact)).astype(np.int64), N_BUCKETS - 1)
    return np.where(dist < max_exact, dist, large).astype(np.int32)


def _bucket_index_table():
    rel = np.arange(BAND)[:, None] + BAND - np.arange(2 * BAND)[None, :]
    inband = (rel >= 0) & (rel <= BAND)
    out = []
    for dil in DILATIONS:
        bucket = _t5_bucket(np.clip(rel, 0, BAND) * dil)
        out.append(np.where(inband, bucket, -1))
    return np.stack(out).astype(np.int32)


def _bias_kernel(tab_ref, idx_ref, o_ref):
    g = pl.program_id(0)
    idx = idx_ref[...]

    def head(h, c):
        acc = jnp.full(idx.shape, NEG_INF, F32)
        for b in range(N_BUCKETS):
            acc = jnp.where(idx == b, tab_ref[b, g * N_HEADS + h], acc)
        o_ref[h] = acc
        return c
    lax.fori_loop(0, N_HEADS, head, 0)


def _bias_matrices(t5_bias):
    idx = jnp.asarray(_bucket_index_table())
    n_sub = N_GROUPS * N_HEADS
    return pl.pallas_call(
        _bias_kernel,
        grid=(N_GROUPS,),
        in_specs=[pl.BlockSpec(memory_space=pltpu.SMEM),
                  pl.BlockSpec((None, BAND, 2 * BAND), lambda g: (g, 0, 0))],
        out_specs=pl.BlockSpec((N_HEADS, BAND, 2 * BAND), lambda g: (g, 0, 0)),
        out_shape=jax.ShapeDtypeStruct((n_sub, BAND, 2 * BAND), F32),
        name="t5_bias_mats",
    )(t5_bias, idx)


def _norm_kernel(*refs, has_post, has_pre, split_in, split_out, n_prompt_tiles, tiles_per_batch):
    refs = list(refs)
    xp_ref = refs.pop(0)
    xs_ref = refs.pop(0) if split_in else xp_ref
    if has_post:
        y_ref, gpost_ref, gate_p_ref, gate_s_ref = refs[:4]
        refs = refs[4:]
    if has_pre:
        gpre_ref, shift_p_ref, scale_p_ref, shift_s_ref, scale_s_ref = refs[:5]
        refs = refs[5:]
    if has_post:
        xop_ref = refs.pop(0)
        xos_ref = refs.pop(0) if split_out else xop_ref
    if has_pre:
        h_ref = refs.pop(0)

    def rms(v, g):
        return v * lax.rsqrt(jnp.mean(v * v, axis=-1, keepdims=True) + RMS_EPS) * g

    def body(x_ref, xo_ref, rows, gate, shift, scale):
        x = x_ref[rows, :]
        if has_post:
            x = x + gate * rms(y_ref[rows, :].astype(F32), gpost_ref[...])
            xo_ref[rows, :] = x
        if has_pre:
            h = rms(x, gpre_ref[...]) * (1.0 + scale) + shift
            h_ref[rows, :] = h.astype(BF16)

    t = pl.program_id(0)

    @pl.when(t < n_prompt_tiles)
    def _():
        row = pl.ds(t // tiles_per_batch, 1)
        body(xp_ref, xop_ref if has_post else None, slice(None),
             gate_p_ref[row, :] if has_post else None,
             shift_p_ref[row, :] if has_pre else None,
             scale_p_ref[row, :] if has_pre else None)

    @pl.when(t == n_prompt_tiles)
    def _():
        body(xs_ref, xos_ref if has_post else None, slice(0, SAMPLE_ROWS),
             gate_s_ref[...] if has_post else None,
             shift_s_ref[...] if has_pre else None,
             scale_s_ref[...] if has_pre else None)


def _norm_step(x, y, mod, norm_post, norm_pre, post_layer, pre_layer, x_sample=None, split_out=False):
    has_post = post_layer is not None
    has_pre = pre_layer is not None
    split_in = x_sample is not None
    te = TE_NORM
    npt = M_PROMPT // te
    tiles_per_batch = SEQ // te
    row_spec = pl.BlockSpec((te, D_MODEL), lambda t: (t, 0))
    prompt_row_spec = pl.BlockSpec((te, D_MODEL), lambda t: (jnp.minimum(t, npt - 1), 0))
    sample_row_spec = pl.BlockSpec((SAMPLE_ROWS, D_MODEL), lambda t: (0, 0))

    def mod_p_spec(layer, part):
        return pl.BlockSpec((None, C_PROMPT_ROWS, D_MODEL), lambda t: (layer, 0, part))

    def mod_s_spec(layer, part):
        return pl.BlockSpec((None, SAMPLE_ROWS, D_MODEL),
                            lambda t: (layer, C_SAMPLE_ROW0 // SAMPLE_ROWS, part))

    def gain_spec(layer):
        return pl.BlockSpec((None, 1, D_MODEL), lambda t: (layer, 0, 0))

    if split_in:
        args, in_specs = [x, x_sample], [prompt_row_spec, sample_row_spec]
    else:
        args, in_specs = [x], [row_spec]
    out_shapes, out_specs = [], []
    if has_post:
        args += [y, norm_post, mod, mod]
        in_specs += [row_spec, gain_spec(post_layer), mod_p_spec(post_layer, 2), mod_s_spec(post_layer, 2)]
        if split_out:
            out_shapes += [jax.ShapeDtypeStruct((M_PROMPT, D_MODEL), F32),
                           jax.ShapeDtypeStruct((SAMPLE_ROWS, D_MODEL), F32)]
            out_specs += [prompt_row_spec, sample_row_spec]
        else:
            out_shapes.append(jax.ShapeDtypeStruct((M_ALL, D_MODEL), F32))
            out_specs.append(row_spec)
    if has_pre:
        args += [norm_pre, mod, mod, mod, mod]
        in_specs += [gain_spec(pre_layer), mod_p_spec(pre_layer, 0), mod_p_spec(pre_layer, 1),
                     mod_s_spec(pre_layer, 0), mod_s_spec(pre_layer, 1)]
        out_shapes.append(jax.ShapeDtypeStruct((M_ALL, D_MODEL), BF16))
        out_specs.append(row_spec)
    outs = pl.pallas_call(
        functools.partial(_norm_kernel, has_post=has_post, has_pre=has_pre, split_in=split_in,
                          split_out=split_out, n_prompt_tiles=npt, tiles_per_batch=tiles_per_batch),
        grid=(npt + 1,),
        in_specs=in_specs,
        out_specs=out_specs,
        out_shape=out_shapes,
        compiler_params=pltpu.CompilerParams(
            dimension_semantics=("arbitrary",), vmem_limit_bytes=VMEM_LIMIT),
        name="norm_step",
    )(*args)
    return outs


def _mm_kernel(a_ref, w_ref, o_ref, *, slabs, gate_tile0):
    tn = w_ref.shape[1]

    def body(apply_silu):
        for c0 in range(0, tn, MXU_COLS):
            acc = jnp.dot(a_ref[...], w_ref[:, c0:c0 + MXU_COLS].astype(BF16), preferred_element_type=F32)
            if apply_silu:
                acc = _silu(acc)
            acc = acc.astype(o_ref.dtype)
            if slabs:
                for c in range(MXU_COLS // HEAD_DIM):
                    o_ref[c0 // HEAD_DIM + c] = acc[:, c * HEAD_DIM:(c + 1) * HEAD_DIM]
            else:
                o_ref[:, c0:c0 + MXU_COLS] = acc

    if gate_tile0 is None:
        body(False)
    else:
        @pl.when(pl.program_id(0) < gate_tile0)
        def _():
            body(False)

        @pl.when(pl.program_id(0) >= gate_tile0)
        def _():
            body(True)


def _matmul(a, w, layer, n_out, tn, tm, out_dtype, name, slabs=False, gate_col0=None):
    m, k = a.shape
    if slabs:
        out_spec = pl.BlockSpec((tn // HEAD_DIM, tm, HEAD_DIM), lambda j, i: (j, i, 0))
        out_shape = jax.ShapeDtypeStruct((n_out // HEAD_DIM, m, HEAD_DIM), out_dtype)
    else:
        out_spec = pl.BlockSpec((tm, tn), lambda j, i: (i, j))
        out_shape = jax.ShapeDtypeStruct((m, n_out), out_dtype)
    return pl.pallas_call(
        functools.partial(_mm_kernel, slabs=slabs, gate_tile0=None if gate_col0 is None else gate_col0 // tn),
        grid=(n_out // tn, m // tm),
        in_specs=[pl.BlockSpec((tm, k), lambda j, i: (i, 0)),
                  pl.BlockSpec((None, k, tn), lambda j, i: (layer, 0, j))],
        out_specs=out_spec,
        out_shape=out_shape,
        compiler_params=pltpu.CompilerParams(
            dimension_semantics=("arbitrary", "arbitrary"), vmem_limit_bytes=VMEM_LIMIT),
        name=name,
    )(a, w)


def _pool_kernel(u_ref, halo_ref, z_ref, st_ref, w_ref, sc_ref,
                 a_ref, pp_ref, ps_ref, wb_ref, buf_a, buf_b, *, n_prompt_tiles, tiles_per_batch):
    g = pl.program_id(0)
    t = pl.program_id(1)
    tp = TP_POOL
    h0 = POOL_HALO

    @pl.when(t == 0)
    def _():
        _cast_rows_to_bf16(w_ref, wb_ref, POOL_GROUP)

    def finish(r, z, rows):
        y = jnp.dot(r.astype(BF16), wb_ref[...], preferred_element_type=F32) * sc_ref[...]
        a_ref[rows, :] = (y * z).astype(BF16)

    def prompt_tile(n_steps):
        w = 2 ** n_steps
        first = (t % tiles_per_batch) == 0
        buf_a[0:8, :] = jnp.zeros((8, POOL_GROUP), F32)
        buf_b[0:8, :] = jnp.zeros((8, POOL_GROUP), F32)
        buf_a[8:h0, :] = jnp.where(first, 0.0, halo_ref[...])
        buf_a[h0:h0 + tp, :] = u_ref[...]
        src, dst = buf_a, buf_b
        n = tp + h0 - 8
        for s in range(n_steps):
            sh = 2 ** s
            dst[8:8 + n, :] = src[8:8 + n, :] + src[8 - sh:8 - sh + n, :]
            src, dst = dst, src
        pos = (t % tiles_per_batch) * tp + lax.broadcasted_iota(jnp.int32, (tp, 1), 0)
        inv_cnt = 1.0 / jnp.minimum(pos + 1, w).astype(F32)
        u = u_ref[...]
        r = src[h0:h0 + tp, :] * inv_cnt - u
        finish(r, z_ref[...], slice(None))

        @pl.when((t % tiles_per_batch) == tiles_per_batch - 1)
        def _():
            pp_ref[...] = u_ref[tp - POOL_BUF:tp, :]

    def sample_tile(n_steps):
        w = 2 ** n_steps
        u_new = u_ref[0:DEC_BATCH, :]
        acc = u_new
        for k in range(1, w):
            acc = acc + st_ref[POOL_BUF - k]
        r = acc / float(w) - u_new
        buf_a[0:DEC_BATCH, :] = r
        buf_a[DEC_BATCH:SAMPLE_ROWS, :] = jnp.zeros((SAMPLE_ROWS - DEC_BATCH, POOL_GROUP), F32)
        finish(buf_a[0:SAMPLE_ROWS, :], z_ref[0:SAMPLE_ROWS, :], slice(0, SAMPLE_ROWS))
        for k in range(POOL_BUF - 1):
            ps_ref[k] = st_ref[k + 1]
        ps_ref[POOL_BUF - 1] = u_new

    for gi in range(len(POOL_WINDOWS)):
        @pl.when((g == gi) & (t < n_prompt_tiles))
        def _(gi=gi):
            prompt_tile(gi + 1)

        @pl.when((g == gi) & (t == n_prompt_tiles))
        def _(gi=gi):
            sample_tile(gi + 1)


def _pool_mix(proj, state_t, w_grp, scale, layer):
    tp = TP_POOL
    npt = M_PROMPT // tp
    tpb = SEQ // tp
    ng = len(POOL_WINDOWS)
    halo_blocks = tp // 16
    outs = pl.pallas_call(
        functools.partial(_pool_kernel, n_prompt_tiles=npt, tiles_per_batch=tpb),
        grid=(ng, npt + 1),
        in_specs=[
            pl.BlockSpec((tp, POOL_GROUP), lambda g, t: (t, g)),
            pl.BlockSpec((16, POOL_GROUP), lambda g, t: (jnp.maximum(t * halo_blocks - 1, 0), g)),
            pl.BlockSpec((tp, POOL_GROUP), lambda g, t: (t, ng + g)),
            pl.BlockSpec((None, POOL_BUF, DEC_BATCH, POOL_GROUP), lambda g, t: (layer, 0, 0, g)),
            pl.BlockSpec((None, None, POOL_GROUP, POOL_GROUP), lambda g, t: (layer, g, 0, 0)),
            pl.BlockSpec((None, 1, POOL_GROUP), lambda g, t: (layer, 0, g)),
        ],
        out_specs=[
            pl.BlockSpec((tp, POOL_GROUP), lambda g, t: (t, g)),
            pl.BlockSpec((None, POOL_BUF, POOL_GROUP),
                         lambda g, t: (jnp.minimum(t // tpb, BATCH - 1), 0, g)),
            pl.BlockSpec((POOL_BUF, DEC_BATCH, POOL_GROUP), lambda g, t: (0, 0, g)),
        ],
        out_shape=[
            jax.ShapeDtypeStruct((M_ALL, POOL_WIDTH), BF16),
            jax.ShapeDtypeStruct((BATCH, POOL_BUF, POOL_WIDTH), F32),
            jax.ShapeDtypeStruct((POOL_BUF, DEC_BATCH, POOL_WIDTH), F32),
        ],
        scratch_shapes=[pltpu.VMEM((POOL_GROUP, POOL_GROUP), BF16),
                        pltpu.VMEM((tp + POOL_HALO, POOL_GROUP), F32),
                        pltpu.VMEM((tp + POOL_HALO, POOL_GROUP), F32)],
        compiler_params=pltpu.CompilerParams(
            dimension_semantics=("arbitrary", "arbitrary"), vmem_limit_bytes=VMEM_LIMIT),
        name="pool_mix",
    )(proj, proj, proj, state_t, w_grp, scale.reshape(-1, 1, POOL_WIDTH))
    return outs


def _att_prompt_kernel(x_ref, bias_ref, as_ref, a_ref, *scratch):
    @pl.when(pl.program_id(0) < BATCH)
    def _():
        _att_prompt_tile(x_ref, bias_ref, a_ref, *scratch)

    @pl.when(pl.program_id(0) == BATCH)
    def _():
        a_ref[0:SAMPLE_ROWS, :] = as_ref[...]


def _att_prompt_tile(x_ref, bias_ref, a_ref, qd, kd, vd, od, ld, on, ln, tmp3):
    tmp = tmp3.at[0]
    q_refs = tuple(x_ref.at[g] for g in range(N_GROUPS))
    k_refs = tuple(x_ref.at[N_GROUPS + g] for g in range(N_GROUPS))
    v_refs = tuple(x_ref.at[2 * N_GROUPS + g] for g in range(N_GROUPS))
    z_ref = x_ref.at[3 * N_GROUPS]
    bias_refs = tuple(bias_ref.at[g] for g in range(N_GROUPS))
    nu = ATT_UNITS

    for g, dil in enumerate(DILATIONS):
        n = SEQ // dil
        nb = n // BAND
        zero_blk = jnp.zeros((BAND, HEAD_DIM), BF16)
        kd[g, 0:BAND, :] = zero_blk
        vd[g, 0:BAND, :] = zero_blk
        for src, dst, off, mul in ((q_refs[g], qd, 0, ATT_SCALE * LOG2E), (k_refs[g], kd, BAND, None),
                                   (v_refs[g], vd, BAND, None)):
            def to_bf16(x, mul=mul):
                return (x if mul is None else x * mul).astype(BF16)

            if dil == 1:
                dst[g, off:off + SEQ, :] = to_bf16(src[...])
            elif dil == 4:
                for r in range(dil):
                    dst[g, off + r * n:off + (r + 1) * n, :] = to_bf16(src[pl.ds(r, n, stride=dil), :])
            else:
                stage = tmp3.at[(off > 0) + (dst is vd)]
                quarter = SEQ // 4
                for r_lo in range(4):
                    stage[r_lo * quarter:(r_lo + 1) * quarter, :] = src[pl.ds(r_lo, quarter, stride=4), :]
                for r in range(dil):
                    r_lo, r_hi = r % 4, r // 4
                    dst[g, off + r * n:off + (r + 1) * n, :] = to_bf16(
                        stage[pl.ds(r_lo * quarter + r_hi, n, stride=4), :])

    def unit_batch(g, u0, seq_blocks):
        with_prev = seq_blocks > 1
        rows = nu * BAND
        row = pl.multiple_of(u0 * BAND, rows)
        shape3 = (nu, BAND, HEAD_DIM)
        q = qd[g, pl.ds(row, rows), :].reshape(shape3)
        kc = kd[g, pl.ds(row + BAND, rows), :].reshape(shape3)
        vc = vd[g, pl.ds(row + BAND, rows), :].reshape(shape3)
        bias = bias_refs[g]
        s_c = jnp.einsum('uqe,uke->uqk', q, kc, preferred_element_type=F32) \
            + (bias[:, BAND:2 * BAND] * LOG2E)[None]
        if with_prev:
            kp = kd[g, pl.ds(row, rows), :].reshape(shape3)
            vp = vd[g, pl.ds(row, rows), :].reshape(shape3)
            bias_p = jnp.broadcast_to((bias[:, 0:BAND] * LOG2E)[None], (nu, BAND, BAND))
            blk = u0 + lax.broadcasted_iota(jnp.int32, (nu, BAND, BAND), 0)
            bias_p = jnp.where((blk & (seq_blocks - 1)) == 0, NEG_INF, bias_p)
            s_p = jnp.einsum('uqe,uke->uqk', q, kp, preferred_element_type=F32) + bias_p
            m = jnp.max(jnp.maximum(s_c, s_p), axis=-1, keepdims=True)
            p_c = jnp.exp2(s_c - m)
            p_p = jnp.exp2(s_p - m)
            l = jnp.sum(p_c + p_p, axis=-1, keepdims=True)
            acc = jnp.einsum('uqk,uke->uqe', p_c.astype(BF16), vc, preferred_element_type=F32) \
                + jnp.einsum('uqk,uke->uqe', p_p.astype(BF16), vp, preferred_element_type=F32)
        else:
            m = jnp.max(s_c, axis=-1, keepdims=True)
            p_c = jnp.exp2(s_c - m)
            l = jnp.sum(p_c, axis=-1, keepdims=True)
            acc = jnp.einsum('uqk,uke->uqe', p_c.astype(BF16), vc, preferred_element_type=F32)
        od[g, pl.ds(row, rows), :] = (acc / l).reshape(rows, HEAD_DIM)
        ld[g, pl.ds(row, rows), :] = jnp.broadcast_to(m + jnp.log(l) * LOG2E, shape3).reshape(rows, HEAD_DIM)

    for g, dil in enumerate(DILATIONS):
        nb = (SEQ // dil) // BAND

        def batch_body(i, c, g=g, nb=nb):
            unit_batch(g, i * nu, nb)
            return c
        lax.fori_loop(0, N_BLOCKS // nu, batch_body, 0)

    def interleave4(src, dst, n_rows):
        quarter = n_rows // 4
        for r in range(4):
            dst[pl.ds(r, quarter, stride=4), :] = src[r * quarter:(r + 1) * quarter, :]

    for src_all, dst_all in ((od, on), (ld, ln)):
        interleave4(src_all.at[1], dst_all.at[0], SEQ)
        for r_lo in range(4):
            for r_hi in range(4):
                r = r_lo + 4 * r_hi
                tmp[pl.ds(r_lo * (SEQ // 4) + r_hi, BAND, stride=4), :] = src_all[2, r * BAND:(r + 1) * BAND, :]
        interleave4(tmp, dst_all.at[1], SEQ)

    def comb(i, c):
        rs = pl.ds(pl.multiple_of(i * ROW_CHUNK, ROW_CHUNK), ROW_CHUNK)
        l0, l1, l2 = ld[0, rs, :], ln[0, rs, :], ln[1, rs, :]
        mx = jnp.maximum(jnp.maximum(l0, l1), l2)
        w0, w1, w2 = jnp.exp2(l0 - mx), jnp.exp2(l1 - mx), jnp.exp2(l2 - mx)
        o = (w0 * od[0, rs, :] + w1 * on[0, rs, :] + w2 * on[1, rs, :]) / (w0 + w1 + w2)
        a_ref[rs, :] = (o * z_ref[rs, :]).astype(BF16)
        return c
    lax.fori_loop(0, SEQ // ROW_CHUNK, comb, 0)


def _att_prompt(proj, bias_mats, a_sample):
    hb = N_HEADS

    def head_blk(b, h):
        return jnp.where(b < BATCH, h, hb - 1)

    n_pg = ATT_IN_COLS // D_MODEL
    in_specs = [
        pl.BlockSpec((n_pg, None, SEQ, HEAD_DIM),
                     lambda b, h: (0, head_blk(b, h), jnp.minimum(b, BATCH - 1), 0)),
        pl.BlockSpec((N_GROUPS, None, BAND, 2 * BAND), lambda b, h: (0, head_blk(b, h), 0, 0)),
        pl.BlockSpec((SAMPLE_ROWS, HEAD_DIM), lambda b, h: (0, h)),
    ]
    scratch = [pltpu.VMEM((N_GROUPS, SEQ, HEAD_DIM), BF16)] + \
              [pltpu.VMEM((N_GROUPS, SEQ + BAND, HEAD_DIM), BF16)] * 2 + \
              [pltpu.VMEM((N_GROUPS, SEQ, HEAD_DIM), F32)] * 2 + \
              [pltpu.VMEM((N_GROUPS - 1, SEQ, HEAD_DIM), F32)] * 2 + \
              [pltpu.VMEM((3, SEQ, HEAD_DIM), F32)]
    return pl.pallas_call(
        _att_prompt_kernel,
        grid=(BATCH + 1, N_HEADS),
        in_specs=in_specs,
        out_specs=pl.BlockSpec((SEQ, HEAD_DIM), lambda b, h: (b, h)),
        out_shape=jax.ShapeDtypeStruct((M_ALL, D_MODEL), BF16),
        scratch_shapes=scratch,
        compiler_params=pltpu.CompilerParams(
            dimension_semantics=("arbitrary", "arbitrary"), vmem_limit_bytes=VMEM_LIMIT),
        name="att_prompt",
    )(proj.reshape(n_pg, hb, M_ALL, HEAD_DIM), bias_mats.reshape(N_GROUPS, hb, BAND, 2 * BAND), a_sample)


def _kv_rows_kernel(k_a, v_a, k_b, v_b, o_ref, *, tr):
    pitch = tr + KV_PITCH_PAD

    def emit(srcs):
        flats = [src.reshape(N_HEADS * pitch, HEAD_DIM) for src in srcs]

        def body(i, c):
            t0 = i * 8
            for k in range(8):
                for part, flat in enumerate(flats):
                    o_ref[t0 + k, part] = flat[pl.ds(t0 + k, N_HEADS, stride=pitch), :]
            return c
        lax.fori_loop(0, tr // 8, body, 0)

    @pl.when(pl.program_id(0) == 0)
    def _():
        emit((k_a, v_a))

    @pl.when(pl.program_id(0) == 1)
    def _():
        emit((k_b, v_b))


def _kv_rows(proj_a, proj_b, g):
    keep = WINDOWS[g]
    tr = min(keep, TR_KV)
    nt = keep // tr
    first_blk = (SEQ - keep) // tr
    blks_per_batch = SEQ // tr

    def row_blk(b, t):
        return b * blks_per_batch + first_blk + t

    def spec(layer, part):
        parked = row_blk(BATCH - 1, nt - 1) if layer == 0 else row_blk(0, 0)
        col0 = ((1 + part) * N_GROUPS + g) * N_HEADS
        return pl.BlockSpec((pl.Element(N_HEADS), pl.Element(tr + KV_PITCH_PAD), pl.Element(HEAD_DIM)),
                            lambda l, b, t: (col0, jnp.where(l == layer, row_blk(b, t), parked) * tr, 0))

    return pl.pallas_call(
        functools.partial(_kv_rows_kernel, tr=tr),
        grid=(2, BATCH, nt),
        in_specs=[spec(0, 0), spec(0, 1), spec(1, 0), spec(1, 1)],
        out_specs=pl.BlockSpec((None, None, tr, 2, N_HEADS, HEAD_DIM), lambda l, b, t: (l, b, t, 0, 0, 0)),
        out_shape=jax.ShapeDtypeStruct((2, BATCH, keep, 2, N_HEADS, HEAD_DIM), F32),
        compiler_params=pltpu.CompilerParams(
            dimension_semantics=("arbitrary",) * 3, vmem_limit_bytes=VMEM_LIMIT),
        name="kv_rows",
    )(proj_a, proj_a, proj_b, proj_b)


def _att_sample_kernel(x_ref, c0, c1, c2, bias_ref, a_ref, kv0, kv1, kv2):
    caches = (c0, c1, c2)
    kv_outs = (kv0, kv1, kv2)
    hb = N_HEADS
    outs, lses = [], []
    for g in range(N_GROUPS):
        q = x_ref[g * hb:(g + 1) * hb, :]
        kn = x_ref[(3 + g) * hb:(4 + g) * hb, :]
        vn = x_ref[(6 + g) * hb:(7 + g) * hb, :]
        kv_outs[g][0] = kn
        kv_outs[g][1] = vn
        kc = caches[g][:, 0]
        vc = caches[g][:, 1]
        s_c = jnp.sum(kc * q[None], axis=-1, keepdims=True) * ATT_SCALE + bias_ref[g, 0:BAND]
        s_n = jnp.sum(kn * q, axis=-1, keepdims=True) * ATT_SCALE + bias_ref[g, BAND]
        m = jnp.maximum(jnp.max(s_c, axis=0), s_n)
        p_c = jnp.exp(s_c - m[None])
        p_n = jnp.exp(s_n - m)
        l = jnp.sum(p_c, axis=0) + p_n
        o = (jnp.sum(p_c * vc, axis=0) + p_n * vn) / l
        outs.append(o)
        lses.append(m + jnp.log(l))
    mx = jnp.maximum(jnp.maximum(lses[0], lses[1]), lses[2])
    ws = [jnp.exp(ls - mx) for ls in lses]
    o = (ws[0] * outs[0] + ws[1] * outs[1] + ws[2] * outs[2]) / (ws[0] + ws[1] + ws[2])
    z = x_ref[9 * hb:10 * hb, :]
    a_ref[...] = o * z


def _att_sample(proj_s, caches, bias_s, layer):
    hb = N_HEADS
    in_specs = [pl.BlockSpec((None, ATT_IN_COLS // HEAD_DIM, HEAD_DIM), lambda b: (b, 0, 0))]
    for g in range(N_GROUPS):
        in_specs.append(pl.BlockSpec((None, None, BAND, None, 2, hb, HEAD_DIM),
                                     lambda b: (layer, b, 0, 0, 0, 0, 0)))
    in_specs.append(pl.BlockSpec((N_GROUPS, BAND + 1, hb, 1), lambda b: (0, 0, 0, 0)))
    out_specs = [pl.BlockSpec((None, hb, HEAD_DIM), lambda b: (b, 0, 0))]
    out_shapes = [jax.ShapeDtypeStruct((DEC_BATCH, hb, HEAD_DIM), F32)]
    for g in range(N_GROUPS):
        out_specs.append(pl.BlockSpec((None, 2, hb, HEAD_DIM), lambda b: (b, 0, 0, 0)))
        out_shapes.append(jax.ShapeDtypeStruct((DEC_BATCH, 2, hb, HEAD_DIM), F32))
    return pl.pallas_call(
        _att_sample_kernel,
        grid=(DEC_BATCH,),
        in_specs=in_specs,
        out_specs=out_specs,
        out_shape=out_shapes,
        compiler_params=pltpu.CompilerParams(
            dimension_semantics=("arbitrary",), vmem_limit_bytes=VMEM_LIMIT),
        name="att_sample",
    )(proj_s, *caches, bias_s)


def kernel(x_prompt, x_sample, c_prompt, c_sample, cache_kv0, cache_kv1, cache_kv2, state_pool,
           norm_pre, norm_post, ada_w, ada_b, t5_bias, pool_w_in, pool_w_grp, pool_scale,
           pool_w_out, att_w_in, att_w_out):
    n_att = DEPTH // 2
    xp0 = x_prompt.reshape(M_PROMPT, D_MODEL)
    xs0 = jnp.zeros((SAMPLE_ROWS, D_MODEL), F32).at[0:DEC_BATCH].set(x_sample.reshape(DEC_BATCH, D_MODEL))
    c_all = jnp.zeros((C_ROWS, D_MODEL), F32)
    c_all = c_all.at[0:BATCH].set(c_prompt).at[C_SAMPLE_ROW0:C_SAMPLE_ROW0 + DEC_BATCH].set(c_sample)

    mod = _ada_all(c_all, ada_w, ada_b)
    gains_pre = norm_pre.reshape(DEPTH, 1, D_MODEL)
    gains_post = norm_post.reshape(DEPTH, 1, D_MODEL)

    bias_mats = _bias_matrices(t5_bias)
    bias_s = bias_mats[:, 0, 0:BAND + 1].reshape(N_GROUPS, N_HEADS, BAND + 1)
    bias_s = jnp.transpose(bias_s, (0, 2, 1))[..., None]
    caches = [c.reshape(n_att, DEC_BATCH, BAND, dil, 2, N_HEADS, HEAD_DIM)
              for c, dil in zip((cache_kv0, cache_kv1, cache_kv2), DILATIONS)]
    state_t = jnp.transpose(state_pool, (0, 2, 1, 3))

    kv_s = [[] for _ in range(N_GROUPS)]
    pool_p, pool_s, att_projs = [], [], []

    (h,) = _norm_step(xp0, None, mod, gains_post, gains_pre, None, 0, x_sample=xs0)
    x = None
    for i in range(DEPTH):
        li = i // 2
        if i % 2 == 0:
            proj = _matmul(h, pool_w_in, li, 2 * POOL_WIDTH, 1024, TM_MATMUL, F32, "pool_in_proj",
                           gate_col0=POOL_WIDTH)
            a, pp, ps = _pool_mix(proj, state_t, pool_w_grp, pool_scale, li)
            pool_p.append(pp)
            pool_s.append(jnp.transpose(ps, (1, 0, 2)))
            y = _matmul(a, pool_w_out, li, D_MODEL, 512, TM_MATMUL, BF16, "pool_out_proj")
        else:
            proj = _matmul(h, att_w_in, li, ATT_IN_COLS, 1024, TM_MATMUL, F32, "att_in_proj", slabs=True,
                           gate_col0=3 * QKV_WIDTH)
            att_projs.append(proj)
            proj_s = jnp.transpose(proj[:, M_PROMPT:M_PROMPT + DEC_BATCH, :], (1, 0, 2))
            souts = _att_sample(proj_s, caches, bias_s, li)
            a_s = jnp.zeros((SAMPLE_ROWS, D_MODEL), F32).at[0:DEC_BATCH].set(souts[0].reshape(DEC_BATCH, D_MODEL))
            a = _att_prompt(proj, bias_mats, a_s.astype(BF16))
            for g in range(N_GROUPS):
                kv_s[g].append(souts[1 + g].reshape(DEC_BATCH, 1, 2, N_HEADS, HEAD_DIM))
            y = _matmul(a, att_w_out, li, D_MODEL, 1024, TM_MATMUL, BF16, "att_out_proj")
        last = i + 1 == DEPTH
        if i == 0:
            x, h = _norm_step(xp0, y, mod, gains_post, gains_pre, i, i + 1, x_sample=xs0)
        elif not last:
            x, h = _norm_step(x, y, mod, gains_post, gains_pre, i, i + 1)
        else:
            y_p, y_s = _norm_step(x, y, mod, gains_post, gains_pre, i, None, split_out=True)

    kv_p = [_kv_rows(att_projs[0], att_projs[1], g) for g in range(N_GROUPS)]
    y_prompt = y_p.reshape(BATCH, SEQ, D_MODEL)
    y_sample = y_s[0:DEC_BATCH].reshape(DEC_BATCH, 1, D_MODEL)
    return (y_prompt, y_sample, kv_p[0], kv_p[1], kv_p[2], jnp.stack(pool_p),
            jnp.stack(kv_s[0]), jnp.stack(kv_s[1]), jnp.stack(kv_s[2]), jnp.stack(pool_s))
```

```python
import functools

import numpy as np
import jax
import jax.numpy as jnp
from jax import lax
from jax.experimental import pallas as pl
from jax.experimental.pallas import tpu as pltpu

D_MODEL = 2048
BATCH = 4
SEQ = 2048
DEPTH = 4
DEC_BATCH = 8
HEAD_DIM = 128
N_HEADS = 16
DILATIONS = (1, 4, 16)
WINDOWS = (128, 512, 2048)
N_GROUPS = 3
QKV_WIDTH = N_GROUPS * D_MODEL
ATT_IN_COLS = 3 * QKV_WIDTH + D_MODEL
BAND = 128
ATT_SCALE = HEAD_DIM ** -0.5
LOG2E = 1.4426950408889634
POOL_WINDOWS = (2, 4, 8, 16)
POOL_WIDTH = 2 * D_MODEL
POOL_GROUP = POOL_WIDTH // 4
POOL_BUF = 15
N_BUCKETS = 32
T5_MAX_DIST = 2048
RMS_EPS = 1e-6
NEG_INF = -1e30

M_PROMPT = BATCH * SEQ
SAMPLE_ROWS = 64
M_ALL = M_PROMPT + SAMPLE_ROWS
C_ROWS = 2 * SAMPLE_ROWS
C_PROMPT_ROWS = 8
C_SAMPLE_ROW0 = SAMPLE_ROWS

TM_MATMUL = 2064
MXU_COLS = 256
TE_NORM = 512
NORM_BUFS = 3
TP_POOL = 1024
POOL_HALO = 24
ATT_UNITS = 16
N_BLOCKS = SEQ // BAND
TR_KV = 256
ROW_CHUNK = 256
KV_PITCH_PAD = 8
VMEM_LIMIT = 58 * 1024 * 1024

F32 = jnp.float32
BF16 = jnp.bfloat16


def _silu(x):
    half = 0.5 * x
    return half + half * jnp.tanh(half)


def _cast_rows_to_bf16(src_ref, dst_ref, rows, chunk=ROW_CHUNK):
    def body(i, c):
        r = pl.multiple_of(i * chunk, chunk)
        dst_ref[pl.ds(r, chunk), :] = src_ref[pl.ds(r, chunk), :].astype(BF16)
        return c
    lax.fori_loop(0, rows // chunk, body, 0)


def _ada_kernel(c_ref, w_ref, b_ref, o_ref):
    a = _silu(c_ref[...]).astype(BF16)
    kc = 512
    acc = jnp.zeros(o_ref.shape, F32)
    for k0 in range(0, D_MODEL, kc):
        acc = acc + jnp.dot(a[:, k0:k0 + kc], w_ref[k0:k0 + kc, :].astype(BF16),
                            preferred_element_type=F32)
    o_ref[...] = acc + b_ref[...]


def _ada_all(c_all, ada_w, ada_b):
    tn = 1024
    n = 3 * D_MODEL
    return pl.pallas_call(
        _ada_kernel,
        grid=(DEPTH, n // tn),
        in_specs=[pl.BlockSpec((C_ROWS, D_MODEL), lambda l, j: (0, 0)),
                  pl.BlockSpec((None, D_MODEL, tn), lambda l, j: (l, 0, j)),
                  pl.BlockSpec((None, 1, tn), lambda l, j: (l, 0, j))],
        out_specs=pl.BlockSpec((None, C_ROWS, tn), lambda l, j: (l, 0, j)),
        out_shape=jax.ShapeDtypeStruct((DEPTH, C_ROWS, n), F32),
        compiler_params=pltpu.CompilerParams(
            dimension_semantics=("arbitrary", "arbitrary"), vmem_limit_bytes=VMEM_LIMIT),
        name="ada_mod",
    )(c_all, ada_w, ada_b.reshape(DEPTH, 1, n))


def _t5_bucket(dist):
    dist = np.asarray(dist, dtype=np.int64)
    max_exact = N_BUCKETS // 2
    ratio = np.log(np.maximum(dist, 1) / max_exact) / np.log(T5_MAX_DIST / max_exact)
    large = np.minimum(max_exact + (ratio * (N_BUCKETS - max_exact)).astype(np.int64), N_BUCKETS - 1)
    return np.where(dist < max_exact, dist, large).astype(np.int32)


def _bucket_index_table():
    rel = np.arange(BAND)[:, None] + BAND - np.arange(2 * BAND)[None, :]
    inband = (rel >= 0) & (rel <= BAND)
    out = []
    for dil in DILATIONS:
        bucket = _t5_bucket(np.clip(rel, 0, BAND) * dil)
        out.append(np.where(inband, bucket, -1))
    return np.stack(out).astype(np.int32)


def _bias_kernel(tab_ref, idx_ref, o_ref):
    g = pl.program_id(0)
    idx = idx_ref[...]

    def head(h, c):
        acc = jnp.full(idx.shape, NEG_INF, F32)
        for b in range(N_BUCKETS):
            acc = jnp.where(idx == b, tab_ref[b, g * N_HEADS + h], acc)
        o_ref[h] = acc
        return c
    lax.fori_loop(0, N_HEADS, head, 0)


def _bias_matrices(t5_bias):
    idx = jnp.asarray(_bucket_index_table())
    n_sub = N_GROUPS * N_HEADS
    return pl.pallas_call(
        _bias_kernel,
        grid=(N_GROUPS,),
        in_specs=[pl.BlockSpec(memory_space=pltpu.SMEM),
                  pl.BlockSpec((None, BAND, 2 * BAND), lambda g: (g, 0, 0))],
        out_specs=pl.BlockSpec((N_HEADS, BAND, 2 * BAND), lambda g: (g, 0, 0)),
        out_shape=jax.ShapeDtypeStruct((n_sub, BAND, 2 * BAND), F32),
        name="t5_bias_mats",
    )(t5_bias, idx)


def _norm_kernel(*refs, has_post, has_pre, split_in, split_out, n_prompt_tiles, tiles_per_batch, te):
    refs = list(refs)
    x_hbm = refs.pop(0)
    xs_ref = refs.pop(0) if split_in else None
    if has_post:
        y_hbm, gpost_ref, gate_p_ref, gate_s_ref = refs[:4]
        refs = refs[4:]
    if has_pre:
        gpre_ref, shift_p_ref, scale_p_ref, shift_s_ref, scale_s_ref = refs[:5]
        refs = refs[5:]
    if has_post:
        xop_ref = refs.pop(0)
        xos_ref = refs.pop(0) if split_out else xop_ref
    if has_pre:
        h_ref = refs.pop(0)
    xbuf, semx = refs[:2]
    if has_post:
        ybuf, semy = refs[2:4]
    npt = n_prompt_tiles

    def tile_copies(k, slot, sample):
        if sample:
            rows_src, rows_dst = pl.ds(M_PROMPT, SAMPLE_ROWS), pl.ds(0, SAMPLE_ROWS)
        else:
            rows_src, rows_dst = pl.ds(pl.multiple_of(k * te, te), te), pl.ds(0, te)
        out = []
        if not (sample and split_in):
            out.append(pltpu.make_async_copy(x_hbm.at[rows_src], xbuf.at[slot, rows_dst], semx.at[slot]))
        if has_post:
            out.append(pltpu.make_async_copy(y_hbm.at[rows_src], ybuf.at[slot, rows_dst], semy.at[slot]))
        return out

    def for_tile(k, action):
        slot = k % NORM_BUFS

        @pl.when(k < npt)
        def _():
            for c in tile_copies(k, slot, False):
                action(c)

        @pl.when(k == npt)
        def _():
            for c in tile_copies(k, slot, True):
                action(c)

    def rms(v, g):
        return v * lax.rsqrt(jnp.mean(v * v, axis=-1, keepdims=True) + RMS_EPS) * g

    def body(x, y, xo_ref, rows, gate, shift, scale):
        if has_post:
            x = x + gate * rms(y.astype(F32), gpost_ref[...])
            xo_ref[rows, :] = x
        if has_pre:
            h = rms(x, gpre_ref[...]) * (1.0 + scale) + shift
            h_ref[rows, :] = h.astype(BF16)

    t = pl.program_id(0)
    slot = t % NORM_BUFS

    @pl.when(t == 0)
    def _():
        for k in range(NORM_BUFS - 1):
            for_tile(jnp.int32(k), lambda c: c.start())

    @pl.when(t + NORM_BUFS - 1 <= npt)
    def _():
        for_tile(t + NORM_BUFS - 1, lambda c: c.start())

    for_tile(t, lambda c: c.wait())

    @pl.when(t < npt)
    def _():
        row = pl.ds(t // tiles_per_batch, 1)
        body(xbuf[slot], ybuf[slot] if has_post else None, xop_ref if has_post else None, slice(None),
             gate_p_ref[row, :] if has_post else None,
             shift_p_ref[row, :] if has_pre else None,
             scale_p_ref[row, :] if has_pre else None)

    @pl.when(t == npt)
    def _():
        rows = slice(0, SAMPLE_ROWS)
        body(xs_ref[...] if split_in else xbuf[slot, rows, :],
             ybuf[slot, rows, :] if has_post else None, xos_ref if has_post else None, rows,
             gate_s_ref[...] if has_post else None,
             shift_s_ref[...] if has_pre else None,
             scale_s_ref[...] if has_pre else None)


def _norm_step(x, y, mod, norm_post, norm_pre, post_layer, pre_layer, x_sample=None, split_out=False):
    has_post = post_layer is not None
    has_pre = pre_layer is not None
    split_in = x_sample is not None
    te = TE_NORM
    npt = M_PROMPT // te
    tiles_per_batch = SEQ // te
    row_spec = pl.BlockSpec((te, D_MODEL), lambda t: (t, 0))
    prompt_row_spec = pl.BlockSpec((te, D_MODEL), lambda t: (jnp.minimum(t, npt - 1), 0))
    sample_row_spec = pl.BlockSpec((SAMPLE_ROWS, D_MODEL), lambda t: (0, 0))

    def mod_p_spec(layer, part):
        return pl.BlockSpec((None, C_PROMPT_ROWS, D_MODEL), lambda t: (layer, 0, part))

    def mod_s_spec(layer, part):
        return pl.BlockSpec((None, SAMPLE_ROWS, D_MODEL),
                            lambda t: (layer, C_SAMPLE_ROW0 // SAMPLE_ROWS, part))

    def gain_spec(layer):
        return pl.BlockSpec((None, 1, D_MODEL), lambda t: (layer, 0, 0))

    hbm_spec = pl.BlockSpec(memory_space=pl.ANY)
    if split_in:
        args, in_specs = [x, x_sample], [hbm_spec, sample_row_spec]
    else:
        args, in_specs = [x], [hbm_spec]
    out_shapes, out_specs = [], []
    scratch = [pltpu.VMEM((NORM_BUFS, te, D_MODEL), F32), pltpu.SemaphoreType.DMA((NORM_BUFS,))]
    if has_post:
        args += [y, norm_post, mod, mod]
        in_specs += [hbm_spec, gain_spec(post_layer), mod_p_spec(post_layer, 2), mod_s_spec(post_layer, 2)]
        scratch += [pltpu.VMEM((NORM_BUFS, te, D_MODEL), y.dtype), pltpu.SemaphoreType.DMA((NORM_BUFS,))]
        if split_out:
            out_shapes += [jax.ShapeDtypeStruct((M_PROMPT, D_MODEL), F32),
                           jax.ShapeDtypeStruct((SAMPLE_ROWS, D_MODEL), F32)]
            out_specs += [prompt_row_spec, sample_row_spec]
        else:
            out_shapes.append(jax.ShapeDtypeStruct((M_ALL, D_MODEL), F32))
            out_specs.append(row_spec)
    if has_pre:
        args += [norm_pre, mod, mod, mod, mod]
        in_specs += [gain_spec(pre_layer), mod_p_spec(pre_layer, 0), mod_p_spec(pre_layer, 1),
                     mod_s_spec(pre_layer, 0), mod_s_spec(pre_layer, 1)]
        out_shapes.append(jax.ShapeDtypeStruct((M_ALL, D_MODEL), BF16))
        out_specs.append(row_spec)
    outs = pl.pallas_call(
        functools.partial(_norm_kernel, has_post=has_post, has_pre=has_pre, split_in=split_in,
                          split_out=split_out, n_prompt_tiles=npt, tiles_per_batch=tiles_per_batch, te=te),
        grid=(npt + 1,),
        in_specs=in_specs,
        out_specs=out_specs,
        out_shape=out_shapes,
        scratch_shapes=scratch,
        compiler_params=pltpu.CompilerParams(
            dimension_semantics=("arbitrary",), vmem_limit_bytes=VMEM_LIMIT),
        name="norm_step",
    )(*args)
    return outs


def _mm_kernel(a_ref, w_ref, o_ref, *, slabs):
    tn = w_ref.shape[1]
    for c0 in range(0, tn, MXU_COLS):
        acc = jnp.dot(a_ref[...], w_ref[:, c0:c0 + MXU_COLS].astype(BF16),
                      preferred_element_type=F32).astype(o_ref.dtype)
        if slabs:
            for c in range(MXU_COLS // HEAD_DIM):
                o_ref[c0 // HEAD_DIM + c] = acc[:, c * HEAD_DIM:(c + 1) * HEAD_DIM]
        else:
            o_ref[:, c0:c0 + MXU_COLS] = acc


def _matmul(a, w, layer, n_out, tn, tm, out_dtype, name, slabs=False):
    m, k = a.shape
    if slabs:
        out_spec = pl.BlockSpec((tn // HEAD_DIM, tm, HEAD_DIM), lambda j, i: (j, i, 0))
        out_shape = jax.ShapeDtypeStruct((n_out // HEAD_DIM, m, HEAD_DIM), out_dtype)
    else:
        out_spec = pl.BlockSpec((tm, tn), lambda j, i: (i, j))
        out_shape = jax.ShapeDtypeStruct((m, n_out), out_dtype)
    return pl.pallas_call(
        functools.partial(_mm_kernel, slabs=slabs),
        grid=(n_out // tn, m // tm),
        in_specs=[pl.BlockSpec((tm, k), lambda j, i: (i, 0)),
                  pl.BlockSpec((None, k, tn), lambda j, i: (layer, 0, j))],
        out_specs=out_spec,
        out_shape=out_shape,
        compiler_params=pltpu.CompilerParams(
            dimension_semantics=("arbitrary", "arbitrary"), vmem_limit_bytes=VMEM_LIMIT),
        name=name,
    )(a, w)


def _pool_kernel(u_ref, halo_ref, z_ref, st_ref, w_ref, sc_ref,
                 a_ref, pp_ref, ps_ref, wb_ref, buf_a, buf_b, *, n_prompt_tiles, tiles_per_batch):
    g = pl.program_id(0)
    t = pl.program_id(1)
    tp = TP_POOL
    h0 = POOL_HALO

    @pl.when(t == 0)
    def _():
        _cast_rows_to_bf16(w_ref, wb_ref, POOL_GROUP)

    def finish(r, z, rows):
        y = jnp.dot(r.astype(BF16), wb_ref[...], preferred_element_type=F32) * sc_ref[...]
        a_ref[rows, :] = (y * _silu(z)).astype(BF16)

    def prompt_tile(n_steps):
        w = 2 ** n_steps
        first = (t % tiles_per_batch) == 0
        buf_a[0:8, :] = jnp.zeros((8, POOL_GROUP), F32)
        buf_b[0:8, :] = jnp.zeros((8, POOL_GROUP), F32)
        buf_a[8:h0, :] = jnp.where(first, 0.0, halo_ref[...])
        buf_a[h0:h0 + tp, :] = u_ref[...]
        src, dst = buf_a, buf_b
        n = tp + h0 - 8
        for s in range(n_steps):
            sh = 2 ** s
            dst[8:8 + n, :] = src[8:8 + n, :] + src[8 - sh:8 - sh + n, :]
            src, dst = dst, src
        pos = (t % tiles_per_batch) * tp + lax.broadcasted_iota(jnp.int32, (tp, 1), 0)
        inv_cnt = 1.0 / jnp.minimum(pos + 1, w).astype(F32)
        u = u_ref[...]
        r = src[h0:h0 + tp, :] * inv_cnt - u
        finish(r, z_ref[...], slice(None))

        @pl.when((t % tiles_per_batch) == tiles_per_batch - 1)
        def _():
            pp_ref[...] = u_ref[tp - POOL_BUF:tp, :]

    def sample_tile(n_steps):
        w = 2 ** n_steps
        u_new = u_ref[0:DEC_BATCH, :]
        acc = u_new
        for k in range(1, w):
            acc = acc + st_ref[POOL_BUF - k]
        r = acc / float(w) - u_new
        buf_a[0:DEC_BATCH, :] = r
        buf_a[DEC_BATCH:SAMPLE_ROWS, :] = jnp.zeros((SAMPLE_ROWS - DEC_BATCH, POOL_GROUP), F32)
        finish(buf_a[0:SAMPLE_ROWS, :], z_ref[0:SAMPLE_ROWS, :], slice(0, SAMPLE_ROWS))
        for k in range(POOL_BUF - 1):
            ps_ref[k] = st_ref[k + 1]
        ps_ref[POOL_BUF - 1] = u_new

    for gi in range(len(POOL_WINDOWS)):
        @pl.when((g == gi) & (t < n_prompt_tiles))
        def _(gi=gi):
            prompt_tile(gi + 1)

        @pl.when((g == gi) & (t == n_prompt_tiles))
        def _(gi=gi):
            sample_tile(gi + 1)


def _pool_mix(proj, state_t, w_grp, scale, layer):
    tp = TP_POOL
    npt = M_PROMPT // tp
    tpb = SEQ // tp
    ng = len(POOL_WINDOWS)
    halo_blocks = tp // 16
    outs = pl.pallas_call(
        functools.partial(_pool_kernel, n_prompt_tiles=npt, tiles_per_batch=tpb),
        grid=(ng, npt + 1),
        in_specs=[
            pl.BlockSpec((tp, POOL_GROUP), lambda g, t: (t, g)),
            pl.BlockSpec((16, POOL_GROUP), lambda g, t: (jnp.maximum(t * halo_blocks - 1, 0), g)),
            pl.BlockSpec((tp, POOL_GROUP), lambda g, t: (t, ng + g)),
            pl.BlockSpec((None, POOL_BUF, DEC_BATCH, POOL_GROUP), lambda g, t: (layer, 0, 0, g)),
            pl.BlockSpec((None, None, POOL_GROUP, POOL_GROUP), lambda g, t: (layer, g, 0, 0)),
            pl.BlockSpec((None, 1, POOL_GROUP), lambda g, t: (layer, 0, g)),
        ],
        out_specs=[
            pl.BlockSpec((tp, POOL_GROUP), lambda g, t: (t, g)),
            pl.BlockSpec((None, POOL_BUF, POOL_GROUP),
                         lambda g, t: (jnp.minimum(t // tpb, BATCH - 1), 0, g)),
            pl.BlockSpec((POOL_BUF, DEC_BATCH, POOL_GROUP), lambda g, t: (0, 0, g)),
        ],
        out_shape=[
            jax.ShapeDtypeStruct((M_ALL, POOL_WIDTH), BF16),
            jax.ShapeDtypeStruct((BATCH, POOL_BUF, POOL_WIDTH), F32),
            jax.ShapeDtypeStruct((POOL_BUF, DEC_BATCH, POOL_WIDTH), F32),
        ],
        scratch_shapes=[pltpu.VMEM((POOL_GROUP, POOL_GROUP), BF16),
                        pltpu.VMEM((tp + POOL_HALO, POOL_GROUP), F32),
                        pltpu.VMEM((tp + POOL_HALO, POOL_GROUP), F32)],
        compiler_params=pltpu.CompilerParams(
            dimension_semantics=("arbitrary", "arbitrary"), vmem_limit_bytes=VMEM_LIMIT),
        name="pool_mix",
    )(proj, proj, proj, state_t, w_grp, scale.reshape(-1, 1, POOL_WIDTH))
    return outs


def _att_prompt_kernel(x_ref, bias_ref, as_ref, a_ref, *scratch):
    @pl.when(pl.program_id(0) < BATCH)
    def _():
        _att_prompt_tile(x_ref, bias_ref, a_ref, *scratch)

    @pl.when(pl.program_id(0) == BATCH)
    def _():
        a_ref[0:SAMPLE_ROWS, :] = as_ref[...]


def _att_prompt_tile(x_ref, bias_ref, a_ref, qd, kd, vd, od, ld, on, ln, tmp3):
    tmp = tmp3.at[0]
    q_refs = tuple(x_ref.at[g] for g in range(N_GROUPS))
    k_refs = tuple(x_ref.at[N_GROUPS + g] for g in range(N_GROUPS))
    v_refs = tuple(x_ref.at[2 * N_GROUPS + g] for g in range(N_GROUPS))
    z_ref = x_ref.at[3 * N_GROUPS]
    bias_refs = tuple(bias_ref.at[g] for g in range(N_GROUPS))
    nu = ATT_UNITS

    for g, dil in enumerate(DILATIONS):
        n = SEQ // dil
        nb = n // BAND
        zero_blk = jnp.zeros((BAND, HEAD_DIM), BF16)
        kd[g, 0:BAND, :] = zero_blk
        vd[g, 0:BAND, :] = zero_blk
        for src, dst, off, mul in ((q_refs[g], qd, 0, ATT_SCALE * LOG2E), (k_refs[g], kd, BAND, None),
                                   (v_refs[g], vd, BAND, None)):
            def to_bf16(x, mul=mul):
                return (x if mul is None else x * mul).astype(BF16)

            if dil == 1:
                dst[g, off:off + SEQ, :] = to_bf16(src[...])
            elif dil == 4:
                for r in range(dil):
                    dst[g, off + r * n:off + (r + 1) * n, :] = to_bf16(src[pl.ds(r, n, stride=dil), :])
            else:
                stage = tmp3.at[(off > 0) + (dst is vd)]
                quarter = SEQ // 4
                for r_lo in range(4):
                    stage[r_lo * quarter:(r_lo + 1) * quarter, :] = src[pl.ds(r_lo, quarter, stride=4), :]
                for r in range(dil):
                    r_lo, r_hi = r % 4, r // 4
                    dst[g, off + r * n:off + (r + 1) * n, :] = to_bf16(
                        stage[pl.ds(r_lo * quarter + r_hi, n, stride=4), :])

    def unit_batch(g, u0, seq_blocks):
        with_prev = seq_blocks > 1
        rows = nu * BAND
        row = pl.multiple_of(u0 * BAND, rows)
        shape3 = (nu, BAND, HEAD_DIM)
        q = qd[g, pl.ds(row, rows), :].reshape(shape3)
        kc = kd[g, pl.ds(row + BAND, rows), :].reshape(shape3)
        vc = vd[g, pl.ds(row + BAND, rows), :].reshape(shape3)
        bias = bias_refs[g]
        s_c = jnp.einsum('uqe,uke->uqk', q, kc, preferred_element_type=F32) \
            + (bias[:, BAND:2 * BAND] * LOG2E)[None]
        if with_prev:
            kp = kd[g, pl.ds(row, rows), :].reshape(shape3)
            vp = vd[g, pl.ds(row, rows), :].reshape(shape3)
            bias_p = jnp.broadcast_to((bias[:, 0:BAND] * LOG2E)[None], (nu, BAND, BAND))
            blk = u0 + lax.broadcasted_iota(jnp.int32, (nu, BAND, BAND), 0)
            bias_p = jnp.where((blk & (seq_blocks - 1)) == 0, NEG_INF, bias_p)
            s_p = jnp.einsum('uqe,uke->uqk', q, kp, preferred_element_type=F32) + bias_p
            m = jnp.max(jnp.maximum(s_c, s_p), axis=-1, keepdims=True)
            p_c = jnp.exp2(s_c - m)
            p_p = jnp.exp2(s_p - m)
            l = jnp.sum(p_c + p_p, axis=-1, keepdims=True)
            acc = jnp.einsum('uqk,uke->uqe', p_c.astype(BF16), vc, preferred_element_type=F32) \
                + jnp.einsum('uqk,uke->uqe', p_p.astype(BF16), vp, preferred_element_type=F32)
        else:
            m = jnp.max(s_c, axis=-1, keepdims=True)
            p_c = jnp.exp2(s_c - m)
            l = jnp.sum(p_c, axis=-1, keepdims=True)
            acc = jnp.einsum('uqk,uke->uqe', p_c.astype(BF16), vc, preferred_element_type=F32)
        od[g, pl.ds(row, rows), :] = (acc / l).reshape(rows, HEAD_DIM)
        ld[g, pl.ds(row, rows), :] = jnp.broadcast_to(m + jnp.log(l) * LOG2E, shape3).reshape(rows, HEAD_DIM)

    for g, dil in enumerate(DILATIONS):
        nb = (SEQ // dil) // BAND

        def batch_body(i, c, g=g, nb=nb):
            unit_batch(g, i * nu, nb)
            return c
        lax.fori_loop(0, N_BLOCKS // nu, batch_body, 0)

    def interleave4(src, dst, n_rows):
        quarter = n_rows // 4
        for r in range(4):
            dst[pl.ds(r, quarter, stride=4), :] = src[r * quarter:(r + 1) * quarter, :]

    for src_all, dst_all in ((od, on), (ld, ln)):
        interleave4(src_all.at[1], dst_all.at[0], SEQ)
        for r_lo in range(4):
            for r_hi in range(4):
                r = r_lo + 4 * r_hi
                tmp[pl.ds(r_lo * (SEQ // 4) + r_hi, BAND, stride=4), :] = src_all[2, r * BAND:(r + 1) * BAND, :]
        interleave4(tmp, dst_all.at[1], SEQ)

    def comb(i, c):
        rs = pl.ds(pl.multiple_of(i * ROW_CHUNK, ROW_CHUNK), ROW_CHUNK)
        l0, l1, l2 = ld[0, rs, :], ln[0, rs, :], ln[1, rs, :]
        mx = jnp.maximum(jnp.maximum(l0, l1), l2)
        w0, w1, w2 = jnp.exp2(l0 - mx), jnp.exp2(l1 - mx), jnp.exp2(l2 - mx)
        o = (w0 * od[0, rs, :] + w1 * on[0, rs, :] + w2 * on[1, rs, :]) / (w0 + w1 + w2)
        a_ref[rs, :] = (o * _silu(z_ref[rs, :])).astype(BF16)
        return c
    lax.fori_loop(0, SEQ // ROW_CHUNK, comb, 0)


def _att_prompt(proj, bias_mats, a_sample):
    hb = N_HEADS

    def head_blk(b, h):
        return jnp.where(b < BATCH, h, hb - 1)

    n_pg = ATT_IN_COLS // D_MODEL
    in_specs = [
        pl.BlockSpec((n_pg, None, SEQ, HEAD_DIM),
                     lambda b, h: (0, head_blk(b, h), jnp.minimum(b, BATCH - 1), 0)),
        pl.BlockSpec((N_GROUPS, None, BAND, 2 * BAND), lambda b, h: (0, head_blk(b, h), 0, 0)),
        pl.BlockSpec((SAMPLE_ROWS, HEAD_DIM), lambda b, h: (0, h)),
    ]
    scratch = [pltpu.VMEM((N_GROUPS, SEQ, HEAD_DIM), BF16)] + \
              [pltpu.VMEM((N_GROUPS, SEQ + BAND, HEAD_DIM), BF16)] * 2 + \
              [pltpu.VMEM((N_GROUPS, SEQ, HEAD_DIM), F32)] * 2 + \
              [pltpu.VMEM((N_GROUPS - 1, SEQ, HEAD_DIM), F32)] * 2 + \
              [pltpu.VMEM((3, SEQ, HEAD_DIM), F32)]
    return pl.pallas_call(
        _att_prompt_kernel,
        grid=(BATCH + 1, N_HEADS),
        in_specs=in_specs,
        out_specs=pl.BlockSpec((SEQ, HEAD_DIM), lambda b, h: (b, h)),
        out_shape=jax.ShapeDtypeStruct((M_ALL, D_MODEL), BF16),
        scratch_shapes=scratch,
        compiler_params=pltpu.CompilerParams(
            dimension_semantics=("arbitrary", "arbitrary"), vmem_limit_bytes=VMEM_LIMIT),
        name="att_prompt",
    )(proj.reshape(n_pg, hb, M_ALL, HEAD_DIM), bias_mats.reshape(N_GROUPS, hb, BAND, 2 * BAND), a_sample)


def _kv_rows_kernel(k_a, v_a, k_b, v_b, o_ref, *, tr):
    pitch = tr + KV_PITCH_PAD

    def emit(srcs):
        flats = [src.reshape(N_HEADS * pitch, HEAD_DIM) for src in srcs]

        def body(i, c):
            t0 = i * 8
            for k in range(8):
                for part, flat in enumerate(flats):
                    o_ref[t0 + k, part] = flat[pl.ds(t0 + k, N_HEADS, stride=pitch), :]
            return c
        lax.fori_loop(0, tr // 8, body, 0)

    @pl.when(pl.program_id(0) == 0)
    def _():
        emit((k_a, v_a))

    @pl.when(pl.program_id(0) == 1)
    def _():
        emit((k_b, v_b))


def _kv_rows(proj_a, proj_b, g):
    keep = WINDOWS[g]
    tr = min(keep, TR_KV)
    nt = keep // tr
    first_blk = (SEQ - keep) // tr
    blks_per_batch = SEQ // tr

    def row_blk(b, t):
        return b * blks_per_batch + first_blk + t

    def spec(layer, part):
        parked = row_blk(BATCH - 1, nt - 1) if layer == 0 else row_blk(0, 0)
        col0 = ((1 + part) * N_GROUPS + g) * N_HEADS
        return pl.BlockSpec((pl.Element(N_HEADS), pl.Element(tr + KV_PITCH_PAD), pl.Element(HEAD_DIM)),
                            lambda l, b, t: (col0, jnp.where(l == layer, row_blk(b, t), parked) * tr, 0))

    return pl.pallas_call(
        functools.partial(_kv_rows_kernel, tr=tr),
        grid=(2, BATCH, nt),
        in_specs=[spec(0, 0), spec(0, 1), spec(1, 0), spec(1, 1)],
        out_specs=pl.BlockSpec((None, None, tr, 2, N_HEADS, HEAD_DIM), lambda l, b, t: (l, b, t, 0, 0, 0)),
        out_shape=jax.ShapeDtypeStruct((2, BATCH, keep, 2, N_HEADS, HEAD_DIM), F32),
        compiler_params=pltpu.CompilerParams(
            dimension_semantics=("arbitrary",) * 3, vmem_limit_bytes=VMEM_LIMIT),
        name="kv_rows",
    )(proj_a, proj_a, proj_b, proj_b)


def _att_sample_kernel(x_ref, c0, c1, c2, bias_ref, a_ref, kv0, kv1, kv2):
    caches = (c0, c1, c2)
    kv_outs = (kv0, kv1, kv2)
    hb = N_HEADS
    outs, lses = [], []
    for g in range(N_GROUPS):
        q = x_ref[g * hb:(g + 1) * hb, :]
        kn = x_ref[(3 + g) * hb:(4 + g) * hb, :]
        vn = x_ref[(6 + g) * hb:(7 + g) * hb, :]
        kv_outs[g][0] = kn
        kv_outs[g][1] = vn
        kc = caches[g][:, 0]
        vc = caches[g][:, 1]
        s_c = jnp.sum(kc * q[None], axis=-1, keepdims=True) * ATT_SCALE + bias_ref[g, 0:BAND]
        s_n = jnp.sum(kn * q, axis=-1, keepdims=True) * ATT_SCALE + bias_ref[g, BAND]
        m = jnp.maximum(jnp.max(s_c, axis=0), s_n)
        p_c = jnp.exp(s_c - m[None])
        p_n = jnp.exp(s_n - m)
        l = jnp.sum(p_c, axis=0) + p_n
        o = (jnp.sum(p_c * vc, axis=0) + p_n * vn) / l
        outs.append(o)
        lses.append(m + jnp.log(l))
    mx = jnp.maximum(jnp.maximum(lses[0], lses[1]), lses[2])
    ws = [jnp.exp(ls - mx) for ls in lses]
    o = (ws[0] * outs[0] + ws[1] * outs[1] + ws[2] * outs[2]) / (ws[0] + ws[1] + ws[2])
    z = x_ref[9 * hb:10 * hb, :]
    a_ref[...] = o * _silu(z)


def _att_sample(proj_s, caches, bias_s, layer):
    hb = N_HEADS
    in_specs = [pl.BlockSpec((None, ATT_IN_COLS // HEAD_DIM, HEAD_DIM), lambda b: (b, 0, 0))]
    for g in range(N_GROUPS):
        in_specs.append(pl.BlockSpec((None, None, BAND, None, 2, hb, HEAD_DIM),
                                     lambda b: (layer, b, 0, 0, 0, 0, 0)))
    in_specs.append(pl.BlockSpec((N_GROUPS, BAND + 1, hb, 1), lambda b: (0, 0, 0, 0)))
    out_specs = [pl.BlockSpec((None, hb, HEAD_DIM), lambda b: (b, 0, 0))]
    out_shapes = [jax.ShapeDtypeStruct((DEC_BATCH, hb, HEAD_DIM), F32)]
    for g in range(N_GROUPS):
        out_specs.append(pl.BlockSpec((None, 2, hb, HEAD_DIM), lambda b: (b, 0, 0, 0)))
        out_shapes.append(jax.ShapeDtypeStruct((DEC_BATCH, 2, hb, HEAD_DIM), F32))
    return pl.pallas_call(
        _att_sample_kernel,
        grid=(DEC_BATCH,),
        in_specs=in_specs,
        out_specs=out_specs,
        out_shape=out_shapes,
        compiler_params=pltpu.CompilerParams(
            dimension_semantics=("arbitrary",), vmem_limit_bytes=VMEM_LIMIT),
        name="att_sample",
    )(proj_s, *caches, bias_s)


def kernel(x_prompt, x_sample, c_prompt, c_sample, cache_kv0, cache_kv1, cache_kv2, state_pool,
           norm_pre, norm_post, ada_w, ada_b, t5_bias, pool_w_in, pool_w_grp, pool_scale,
           pool_w_out, att_w_in, att_w_out):
    n_att = DEPTH // 2
    xp0 = x_prompt.reshape(M_PROMPT, D_MODEL)
    xs0 = jnp.zeros((SAMPLE_ROWS, D_MODEL), F32).at[0:DEC_BATCH].set(x_sample.reshape(DEC_BATCH, D_MODEL))
    c_all = jnp.zeros((C_ROWS, D_MODEL), F32)
    c_all = c_all.at[0:BATCH].set(c_prompt).at[C_SAMPLE_ROW0:C_SAMPLE_ROW0 + DEC_BATCH].set(c_sample)

    mod = _ada_all(c_all, ada_w, ada_b)
    gains_pre = norm_pre.reshape(DEPTH, 1, D_MODEL)
    gains_post = norm_post.reshape(DEPTH, 1, D_MODEL)

    bias_mats = _bias_matrices(t5_bias)
    bias_s = bias_mats[:, 0, 0:BAND + 1].reshape(N_GROUPS, N_HEADS, BAND + 1)
    bias_s = jnp.transpose(bias_s, (0, 2, 1))[..., None]
    caches = [c.reshape(n_att, DEC_BATCH, BAND, dil, 2, N_HEADS, HEAD_DIM)
              for c, dil in zip((cache_kv0, cache_kv1, cache_kv2), DILATIONS)]
    state_t = jnp.transpose(state_pool, (0, 2, 1, 3))

    kv_s = [[] for _ in range(N_GROUPS)]
    pool_p, pool_s, att_projs = [], [], []

    (h,) = _norm_step(xp0, None, mod, gains_post, gains_pre, None, 0, x_sample=xs0)
    x = None
    for i in range(DEPTH):
        li = i // 2
        if i % 2 == 0:
            proj = _matmul(h, pool_w_in, li, 2 * POOL_WIDTH, 1024, TM_MATMUL, F32, "pool_in_proj")
            a, pp, ps = _pool_mix(proj, state_t, pool_w_grp, pool_scale, li)
            pool_p.append(pp)
            pool_s.append(jnp.transpose(ps, (1, 0, 2)))
            y = _matmul(a, pool_w_out, li, D_MODEL, 512, TM_MATMUL, BF16, "pool_out_proj")
        else:
            proj = _matmul(h, att_w_in, li, ATT_IN_COLS, 1024, TM_MATMUL, F32, "att_in_proj", slabs=True)
            att_projs.append(proj)
            proj_s = jnp.transpose(proj[:, M_PROMPT:M_PROMPT + DEC_BATCH, :], (1, 0, 2))
            souts = _att_sample(proj_s, caches, bias_s, li)
            a_s = jnp.zeros((SAMPLE_ROWS, D_MODEL), F32).at[0:DEC_BATCH].set(souts[0].reshape(DEC_BATCH, D_MODEL))
            a = _att_prompt(proj, bias_mats, a_s.astype(BF16))
            for g in range(N_GROUPS):
                kv_s[g].append(souts[1 + g].reshape(DEC_BATCH, 1, 2, N_HEADS, HEAD_DIM))
            y = _matmul(a, att_w_out, li, D_MODEL, 1024, TM_MATMUL, BF16, "att_out_proj")
        last = i + 1 == DEPTH
        if i == 0:
            x, h = _norm_step(xp0, y, mod, gains_post, gains_pre, i, i + 1, x_sample=xs0)
        elif not last:
            x, h = _norm_step(x, y, mod, gains_post, gains_pre, i, i + 1)
        else:
            y_p, y_s = _norm_step(x, y, mod, gains_post, gains_pre, i, None, split_out=True)

    kv_p = [_kv_rows(att_projs[0], att_projs[1], g) for g in range(N_GROUPS)]
    y_prompt = y_p.reshape(BATCH, SEQ, D_MODEL)
    y_sample = y_s[0:DEC_BATCH].reshape(DEC_BATCH, 1, D_MODEL)
    return (y_prompt, y_sample, kv_p[0], kv_p[1], kv_p[2], jnp.stack(pool_p),
            jnp.stack(kv_s[0]), jnp.stack(kv_s[1]), jnp.stack(kv_s[2]), jnp.stack(pool_s))
```

```python
import functools

import numpy as np
import jax
import jax.numpy as jnp
from jax import lax
from jax.experimental import pallas as pl
from jax.experimental.pallas import tpu as pltpu

D_MODEL = 2048
BATCH = 4
SEQ = 2048
DEPTH = 4
DEC_BATCH = 8
HEAD_DIM = 128
N_HEADS = 16
DILATIONS = (1, 4, 16)
WINDOWS = (128, 512, 2048)
N_GROUPS = 3
QKV_WIDTH = N_GROUPS * D_MODEL
ATT_IN_COLS = 3 * QKV_WIDTH + D_MODEL
BAND = 128
ATT_SCALE = HEAD_DIM ** -0.5
LOG2E = 1.4426950408889634
POOL_WINDOWS = (2, 4, 8, 16)
POOL_WIDTH = 2 * D_MODEL
POOL_GROUP = POOL_WIDTH // 4
POOL_BUF = 15
N_BUCKETS = 32
T5_MAX_DIST = 2048
RMS_EPS = 1e-6
NEG_INF = -1e30

M_PROMPT = BATCH * SEQ
SAMPLE_ROWS = 64
M_ALL = M_PROMPT + SAMPLE_ROWS
C_ROWS = 2 * SAMPLE_ROWS
C_PROMPT_ROWS = 8
C_SAMPLE_ROW0 = SAMPLE_ROWS

TM_MATMUL = 2064
MXU_COLS = 256
TE_NORM = 512
NORM_BUFS = 3
SAMPLE_BUFS = 4
TP_POOL = 1024
POOL_HALO = 24
ATT_UNITS = 16
N_BLOCKS = SEQ // BAND
TR_KV = 256
ROW_CHUNK = 256
KV_PITCH_PAD = 8
VMEM_LIMIT = 58 * 1024 * 1024

F32 = jnp.float32
BF16 = jnp.bfloat16


def _silu(x):
    half = 0.5 * x
    return half + half * jnp.tanh(half)


def _cast_rows_to_bf16(src_ref, dst_ref, rows, chunk=ROW_CHUNK):
    def body(i, c):
        r = pl.multiple_of(i * chunk, chunk)
        dst_ref[pl.ds(r, chunk), :] = src_ref[pl.ds(r, chunk), :].astype(BF16)
        return c
    lax.fori_loop(0, rows // chunk, body, 0)


def _ada_kernel(c_ref, w_ref, b_ref, o_ref):
    a = _silu(c_ref[...]).astype(BF16)
    kc = 512
    acc = jnp.zeros(o_ref.shape, F32)
    for k0 in range(0, D_MODEL, kc):
        acc = acc + jnp.dot(a[:, k0:k0 + kc], w_ref[k0:k0 + kc, :].astype(BF16),
                            preferred_element_type=F32)
    o_ref[...] = acc + b_ref[...]


def _ada_all(c_all, ada_w, ada_b):
    tn = 1024
    n = 3 * D_MODEL
    return pl.pallas_call(
        _ada_kernel,
        grid=(DEPTH, n // tn),
        in_specs=[pl.BlockSpec((C_ROWS, D_MODEL), lambda l, j: (0, 0)),
                  pl.BlockSpec((None, D_MODEL, tn), lambda l, j: (l, 0, j)),
                  pl.BlockSpec((None, 1, tn), lambda l, j: (l, 0, j))],
        out_specs=pl.BlockSpec((None, C_ROWS, tn), lambda l, j: (l, 0, j)),
        out_shape=jax.ShapeDtypeStruct((DEPTH, C_ROWS, n), F32),
        compiler_params=pltpu.CompilerParams(
            dimension_semantics=("arbitrary", "arbitrary"), vmem_limit_bytes=VMEM_LIMIT),
        name="ada_mod",
    )(c_all, ada_w, ada_b.reshape(DEPTH, 1, n))


def _t5_bucket(dist):
    dist = np.asarray(dist, dtype=np.int64)
    max_exact = N_BUCKETS // 2
    ratio = np.log(np.maximum(dist, 1) / max_exact) / np.log(T5_MAX_DIST / max_exact)
    large = np.minimum(max_exact + (ratio * (N_BUCKETS - max_exact)).astype(np.int64), N_BUCKETS - 1)
    return np.where(dist < max_exact, dist, large).astype(np.int32)


def _bucket_index_table():
    rel = np.arange(BAND)[:, None] + BAND - np.arange(2 * BAND)[None, :]
    inband = (rel >= 0) & (rel <= BAND)
    out = []
    for dil in DILATIONS:
        bucket = _t5_bucket(np.clip(rel, 0, BAND) * dil)
        out.append(np.where(inband, bucket, -1))
    return np.stack(out).astype(np.int32)


def _bias_kernel(tab_ref, idx_ref, o_ref):
    g = pl.program_id(0)
    idx = idx_ref[...]

    def head(h, c):
        acc = jnp.full(idx.shape, NEG_INF, F32)
        for b in range(N_BUCKETS):
            acc = jnp.where(idx == b, tab_ref[b, g * N_HEADS + h], acc)
        o_ref[h] = acc
        return c
    lax.fori_loop(0, N_HEADS, head, 0)


def _bias_matrices(t5_bias):
    idx = jnp.asarray(_bucket_index_table())
    n_sub = N_GROUPS * N_HEADS
    return pl.pallas_call(
        _bias_kernel,
        grid=(N_GROUPS,),
        in_specs=[pl.BlockSpec(memory_space=pltpu.SMEM),
                  pl.BlockSpec((None, BAND, 2 * BAND), lambda g: (g, 0, 0))],
        out_specs=pl.BlockSpec((N_HEADS, BAND, 2 * BAND), lambda g: (g, 0, 0)),
        out_shape=jax.ShapeDtypeStruct((n_sub, BAND, 2 * BAND), F32),
        name="t5_bias_mats",
    )(t5_bias, idx)


def _norm_kernel(*refs, has_post, has_pre, split_in, split_out, n_prompt_tiles, tiles_per_batch, te):
    refs = list(refs)
    x_hbm = refs.pop(0)
    xs_ref = refs.pop(0) if split_in else None
    if has_post:
        y_hbm, gpost_ref, gate_p_ref, gate_s_ref = refs[:4]
        refs = refs[4:]
    if has_pre:
        gpre_ref, shift_p_ref, scale_p_ref, shift_s_ref, scale_s_ref = refs[:5]
        refs = refs[5:]
    if has_post:
        xop_ref = refs.pop(0)
        xos_ref = refs.pop(0) if split_out else xop_ref
    if has_pre:
        h_ref = refs.pop(0)
    xbuf, semx = refs[:2]
    if has_post:
        ybuf, semy = refs[2:4]
    npt = n_prompt_tiles

    def tile_copies(k, slot, sample):
        if sample:
            rows_src, rows_dst = pl.ds(M_PROMPT, SAMPLE_ROWS), pl.ds(0, SAMPLE_ROWS)
        else:
            rows_src, rows_dst = pl.ds(pl.multiple_of(k * te, te), te), pl.ds(0, te)
        out = []
        if not (sample and split_in):
            out.append(pltpu.make_async_copy(x_hbm.at[rows_src], xbuf.at[slot, rows_dst], semx.at[slot]))
        if has_post:
            out.append(pltpu.make_async_copy(y_hbm.at[rows_src], ybuf.at[slot, rows_dst], semy.at[slot]))
        return out

    def for_tile(k, action):
        slot = k % NORM_BUFS

        @pl.when(k < npt)
        def _():
            for c in tile_copies(k, slot, False):
                action(c)

        @pl.when(k == npt)
        def _():
            for c in tile_copies(k, slot, True):
                action(c)

    def rms(v, g):
        return v * lax.rsqrt(jnp.mean(v * v, axis=-1, keepdims=True) + RMS_EPS) * g

    def body(x, y, xo_ref, rows, gate, shift, scale):
        if has_post:
            x = x + gate * rms(y.astype(F32), gpost_ref[...])
            xo_ref[rows, :] = x
        if has_pre:
            h = rms(x, gpre_ref[...]) * (1.0 + scale) + shift
            h_ref[rows, :] = h.astype(BF16)

    t = pl.program_id(0)
    slot = t % NORM_BUFS

    @pl.when(t == 0)
    def _():
        for k in range(NORM_BUFS - 1):
            for_tile(jnp.int32(k), lambda c: c.start())

    @pl.when(t + NORM_BUFS - 1 <= npt)
    def _():
        for_tile(t + NORM_BUFS - 1, lambda c: c.start())

    for_tile(t, lambda c: c.wait())

    @pl.when(t < npt)
    def _():
        row = pl.ds(t // tiles_per_batch, 1)
        body(xbuf[slot], ybuf[slot] if has_post else None, xop_ref if has_post else None, slice(None),
             gate_p_ref[row, :] if has_post else None,
             shift_p_ref[row, :] if has_pre else None,
             scale_p_ref[row, :] if has_pre else None)

    @pl.when(t == npt)
    def _():
        rows = slice(0, SAMPLE_ROWS)
        body(xs_ref[...] if split_in else xbuf[slot, rows, :],
             ybuf[slot, rows, :] if has_post else None, xos_ref if has_post else None, rows,
             gate_s_ref[...] if has_post else None,
             shift_s_ref[...] if has_pre else None,
             scale_s_ref[...] if has_pre else None)


def _norm_step(x, y, mod, norm_post, norm_pre, post_layer, pre_layer, x_sample=None, split_out=False):
    has_post = post_layer is not None
    has_pre = pre_layer is not None
    split_in = x_sample is not None
    te = TE_NORM
    npt = M_PROMPT // te
    tiles_per_batch = SEQ // te
    row_spec = pl.BlockSpec((te, D_MODEL), lambda t: (t, 0))
    prompt_row_spec = pl.BlockSpec((te, D_MODEL), lambda t: (jnp.minimum(t, npt - 1), 0))
    sample_row_spec = pl.BlockSpec((SAMPLE_ROWS, D_MODEL), lambda t: (0, 0))

    def mod_p_spec(layer, part):
        return pl.BlockSpec((None, C_PROMPT_ROWS, D_MODEL), lambda t: (layer, 0, part))

    def mod_s_spec(layer, part):
        return pl.BlockSpec((None, SAMPLE_ROWS, D_MODEL),
                            lambda t: (layer, C_SAMPLE_ROW0 // SAMPLE_ROWS, part))

    def gain_spec(layer):
        return pl.BlockSpec((None, 1, D_MODEL), lambda t: (layer, 0, 0))

    hbm_spec = pl.BlockSpec(memory_space=pl.ANY)
    if split_in:
        args, in_specs = [x, x_sample], [hbm_spec, sample_row_spec]
    else:
        args, in_specs = [x], [hbm_spec]
    out_shapes, out_specs = [], []
    scratch = [pltpu.VMEM((NORM_BUFS, te, D_MODEL), F32), pltpu.SemaphoreType.DMA((NORM_BUFS,))]
    if has_post:
        args += [y, norm_post, mod, mod]
        in_specs += [hbm_spec, gain_spec(post_layer), mod_p_spec(post_layer, 2), mod_s_spec(post_layer, 2)]
        scratch += [pltpu.VMEM((NORM_BUFS, te, D_MODEL), y.dtype), pltpu.SemaphoreType.DMA((NORM_BUFS,))]
        if split_out:
            out_shapes += [jax.ShapeDtypeStruct((M_PROMPT, D_MODEL), F32),
                           jax.ShapeDtypeStruct((SAMPLE_ROWS, D_MODEL), F32)]
            out_specs += [prompt_row_spec, sample_row_spec]
        else:
            out_shapes.append(jax.ShapeDtypeStruct((M_ALL, D_MODEL), F32))
            out_specs.append(row_spec)
    if has_pre:
        args += [norm_pre, mod, mod, mod, mod]
        in_specs += [gain_spec(pre_layer), mod_p_spec(pre_layer, 0), mod_p_spec(pre_layer, 1),
                     mod_s_spec(pre_layer, 0), mod_s_spec(pre_layer, 1)]
        out_shapes.append(jax.ShapeDtypeStruct((M_ALL, D_MODEL), BF16))
        out_specs.append(row_spec)
    outs = pl.pallas_call(
        functools.partial(_norm_kernel, has_post=has_post, has_pre=has_pre, split_in=split_in,
                          split_out=split_out, n_prompt_tiles=npt, tiles_per_batch=tiles_per_batch, te=te),
        grid=(npt + 1,),
        in_specs=in_specs,
        out_specs=out_specs,
        out_shape=out_shapes,
        scratch_shapes=scratch,
        compiler_params=pltpu.CompilerParams(
            dimension_semantics=("arbitrary",), vmem_limit_bytes=VMEM_LIMIT),
        name="norm_step",
    )(*args)
    return outs


def _mm_kernel(a_ref, w_ref, o_ref, *, slabs):
    tn = w_ref.shape[1]
    for c0 in range(0, tn, MXU_COLS):
        acc = jnp.dot(a_ref[...], w_ref[:, c0:c0 + MXU_COLS].astype(BF16),
                      preferred_element_type=F32).astype(o_ref.dtype)
        if slabs:
            for c in range(MXU_COLS // HEAD_DIM):
                o_ref[c0 // HEAD_DIM + c] = acc[:, c * HEAD_DIM:(c + 1) * HEAD_DIM]
        else:
            o_ref[:, c0:c0 + MXU_COLS] = acc


def _matmul(a, w, layer, n_out, tn, tm, out_dtype, name, slabs=False):
    m, k = a.shape
    if slabs:
        out_spec = pl.BlockSpec((tn // HEAD_DIM, tm, HEAD_DIM), lambda j, i: (j, i, 0))
        out_shape = jax.ShapeDtypeStruct((n_out // HEAD_DIM, m, HEAD_DIM), out_dtype)
    else:
        out_spec = pl.BlockSpec((tm, tn), lambda j, i: (i, j))
        out_shape = jax.ShapeDtypeStruct((m, n_out), out_dtype)
    return pl.pallas_call(
        functools.partial(_mm_kernel, slabs=slabs),
        grid=(n_out // tn, m // tm),
        in_specs=[pl.BlockSpec((tm, k), lambda j, i: (i, 0)),
                  pl.BlockSpec((None, k, tn), lambda j, i: (layer, 0, j))],
        out_specs=out_spec,
        out_shape=out_shape,
        compiler_params=pltpu.CompilerParams(
            dimension_semantics=("arbitrary", "arbitrary"), vmem_limit_bytes=VMEM_LIMIT),
        name=name,
    )(a, w)


def _pool_kernel(u_ref, halo_ref, z_ref, st_ref, w_ref, sc_ref,
                 a_ref, pp_ref, ps_ref, wb_ref, buf_a, buf_b, *, n_prompt_tiles, tiles_per_batch):
    g = pl.program_id(0)
    t = pl.program_id(1)
    tp = TP_POOL
    h0 = POOL_HALO

    @pl.when(t == 0)
    def _():
        _cast_rows_to_bf16(w_ref, wb_ref, POOL_GROUP)

    def finish(r, z, rows):
        y = jnp.dot(r.astype(BF16), wb_ref[...], preferred_element_type=F32) * sc_ref[...]
        a_ref[rows, :] = (y * _silu(z)).astype(BF16)

    def prompt_tile(n_steps):
        w = 2 ** n_steps
        first = (t % tiles_per_batch) == 0
        buf_a[0:8, :] = jnp.zeros((8, POOL_GROUP), F32)
        buf_b[0:8, :] = jnp.zeros((8, POOL_GROUP), F32)
        buf_a[8:h0, :] = jnp.where(first, 0.0, halo_ref[...])
        buf_a[h0:h0 + tp, :] = u_ref[...]
        src, dst = buf_a, buf_b
        n = tp + h0 - 8
        for s in range(n_steps):
            sh = 2 ** s
            dst[8:8 + n, :] = src[8:8 + n, :] + src[8 - sh:8 - sh + n, :]
            src, dst = dst, src
        pos = (t % tiles_per_batch) * tp + lax.broadcasted_iota(jnp.int32, (tp, 1), 0)
        inv_cnt = 1.0 / jnp.minimum(pos + 1, w).astype(F32)
        u = u_ref[...]
        r = src[h0:h0 + tp, :] * inv_cnt - u
        finish(r, z_ref[...], slice(None))

        @pl.when((t % tiles_per_batch) == tiles_per_batch - 1)
        def _():
            pp_ref[...] = u_ref[tp - POOL_BUF:tp, :]

    def sample_tile(n_steps):
        w = 2 ** n_steps
        u_new = u_ref[0:DEC_BATCH, :]
        acc = u_new
        for k in range(1, w):
            acc = acc + st_ref[POOL_BUF - k]
        r = acc / float(w) - u_new
        buf_a[0:DEC_BATCH, :] = r
        buf_a[DEC_BATCH:SAMPLE_ROWS, :] = jnp.zeros((SAMPLE_ROWS - DEC_BATCH, POOL_GROUP), F32)
        finish(buf_a[0:SAMPLE_ROWS, :], z_ref[0:SAMPLE_ROWS, :], slice(0, SAMPLE_ROWS))
        for k in range(POOL_BUF - 1):
            ps_ref[k] = st_ref[k + 1]
        ps_ref[POOL_BUF - 1] = u_new

    for gi in range(len(POOL_WINDOWS)):
        @pl.when((g == gi) & (t < n_prompt_tiles))
        def _(gi=gi):
            prompt_tile(gi + 1)

        @pl.when((g == gi) & (t == n_prompt_tiles))
        def _(gi=gi):
            sample_tile(gi + 1)


def _pool_mix(proj, state_t, w_grp, scale, layer):
    tp = TP_POOL
    npt = M_PROMPT // tp
    tpb = SEQ // tp
    ng = len(POOL_WINDOWS)
    halo_blocks = tp // 16
    outs = pl.pallas_call(
        functools.partial(_pool_kernel, n_prompt_tiles=npt, tiles_per_batch=tpb),
        grid=(ng, npt + 1),
        in_specs=[
            pl.BlockSpec((tp, POOL_GROUP), lambda g, t: (t, g)),
            pl.BlockSpec((16, POOL_GROUP), lambda g, t: (jnp.maximum(t * halo_blocks - 1, 0), g)),
            pl.BlockSpec((tp, POOL_GROUP), lambda g, t: (t, ng + g)),
            pl.BlockSpec((None, POOL_BUF, DEC_BATCH, POOL_GROUP), lambda g, t: (layer, 0, 0, g)),
            pl.BlockSpec((None, None, POOL_GROUP, POOL_GROUP), lambda g, t: (layer, g, 0, 0)),
            pl.BlockSpec((None, 1, POOL_GROUP), lambda g, t: (layer, 0, g)),
        ],
        out_specs=[
            pl.BlockSpec((tp, POOL_GROUP), lambda g, t: (t, g)),
            pl.BlockSpec((None, POOL_BUF, POOL_GROUP),
                         lambda g, t: (jnp.minimum(t // tpb, BATCH - 1), 0, g)),
            pl.BlockSpec((POOL_BUF, DEC_BATCH, POOL_GROUP), lambda g, t: (0, 0, g)),
        ],
        out_shape=[
            jax.ShapeDtypeStruct((M_ALL, POOL_WIDTH), BF16),
            jax.ShapeDtypeStruct((BATCH, POOL_BUF, POOL_WIDTH), F32),
            jax.ShapeDtypeStruct((POOL_BUF, DEC_BATCH, POOL_WIDTH), F32),
        ],
        scratch_shapes=[pltpu.VMEM((POOL_GROUP, POOL_GROUP), BF16),
                        pltpu.VMEM((tp + POOL_HALO, POOL_GROUP), F32),
                        pltpu.VMEM((tp + POOL_HALO, POOL_GROUP), F32)],
        compiler_params=pltpu.CompilerParams(
            dimension_semantics=("arbitrary", "arbitrary"), vmem_limit_bytes=VMEM_LIMIT),
        name="pool_mix",
    )(proj, proj, proj, state_t, w_grp, scale.reshape(-1, 1, POOL_WIDTH))
    return outs


def _att_prompt_kernel(x_ref, bias_ref, as_ref, a_ref, *scratch):
    @pl.when(pl.program_id(0) < BATCH)
    def _():
        _att_prompt_tile(x_ref, bias_ref, a_ref, *scratch)

    @pl.when(pl.program_id(0) == BATCH)
    def _():
        a_ref[0:SAMPLE_ROWS, :] = as_ref[...]


def _att_prompt_tile(x_ref, bias_ref, a_ref, qd, kd, vd, od, ld, on, ln, tmp3):
    tmp = tmp3.at[0]
    q_refs = tuple(x_ref.at[g] for g in range(N_GROUPS))
    k_refs = tuple(x_ref.at[N_GROUPS + g] for g in range(N_GROUPS))
    v_refs = tuple(x_ref.at[2 * N_GROUPS + g] for g in range(N_GROUPS))
    z_ref = x_ref.at[3 * N_GROUPS]
    bias_refs = tuple(bias_ref.at[g] for g in range(N_GROUPS))
    nu = ATT_UNITS

    for g, dil in enumerate(DILATIONS):
        n = SEQ // dil
        nb = n // BAND
        zero_blk = jnp.zeros((BAND, HEAD_DIM), BF16)
        kd[g, 0:BAND, :] = zero_blk
        vd[g, 0:BAND, :] = zero_blk
        for src, dst, off, mul in ((q_refs[g], qd, 0, ATT_SCALE * LOG2E), (k_refs[g], kd, BAND, None),
                                   (v_refs[g], vd, BAND, None)):
            def to_bf16(x, mul=mul):
                return (x if mul is None else x * mul).astype(BF16)

            if dil == 1:
                dst[g, off:off + SEQ, :] = to_bf16(src[...])
            elif dil == 4:
                for r in range(dil):
                    dst[g, off + r * n:off + (r + 1) * n, :] = to_bf16(src[pl.ds(r, n, stride=dil), :])
            else:
                stage = tmp3.at[(off > 0) + (dst is vd)]
                quarter = SEQ // 4
                for r_lo in range(4):
                    stage[r_lo * quarter:(r_lo + 1) * quarter, :] = src[pl.ds(r_lo, quarter, stride=4), :]
                for r in range(dil):
                    r_lo, r_hi = r % 4, r // 4
                    dst[g, off + r * n:off + (r + 1) * n, :] = to_bf16(
                        stage[pl.ds(r_lo * quarter + r_hi, n, stride=4), :])

    def unit_batch(g, u0, seq_blocks):
        with_prev = seq_blocks > 1
        rows = nu * BAND
        row = pl.multiple_of(u0 * BAND, rows)
        shape3 = (nu, BAND, HEAD_DIM)
        q = qd[g, pl.ds(row, rows), :].reshape(shape3)
        kc = kd[g, pl.ds(row + BAND, rows), :].reshape(shape3)
        vc = vd[g, pl.ds(row + BAND, rows), :].reshape(shape3)
        bias = bias_refs[g]
        s_c = jnp.einsum('uqe,uke->uqk', q, kc, preferred_element_type=F32) \
            + (bias[:, BAND:2 * BAND] * LOG2E)[None]
        if with_prev:
            kp = kd[g, pl.ds(row, rows), :].reshape(shape3)
            vp = vd[g, pl.ds(row, rows), :].reshape(shape3)
            bias_p = jnp.broadcast_to((bias[:, 0:BAND] * LOG2E)[None], (nu, BAND, BAND))
            blk = u0 + lax.broadcasted_iota(jnp.int32, (nu, BAND, BAND), 0)
            bias_p = jnp.where((blk & (seq_blocks - 1)) == 0, NEG_INF, bias_p)
            s_p = jnp.einsum('uqe,uke->uqk', q, kp, preferred_element_type=F32) + bias_p
            m = jnp.max(jnp.maximum(s_c, s_p), axis=-1, keepdims=True)
            p_c = jnp.exp2(s_c - m)
            p_p = jnp.exp2(s_p - m)
            l = jnp.sum(p_c + p_p, axis=-1, keepdims=True)
            acc = jnp.einsum('uqk,uke->uqe', p_c.astype(BF16), vc, preferred_element_type=F32) \
                + jnp.einsum('uqk,uke->uqe', p_p.astype(BF16), vp, preferred_element_type=F32)
        else:
            m = jnp.max(s_c, axis=-1, keepdims=True)
            p_c = jnp.exp2(s_c - m)
            l = jnp.sum(p_c, axis=-1, keepdims=True)
            acc = jnp.einsum('uqk,uke->uqe', p_c.astype(BF16), vc, preferred_element_type=F32)
        od[g, pl.ds(row, rows), :] = (acc / l).reshape(rows, HEAD_DIM)
        ld[g, pl.ds(row, rows), :] = jnp.broadcast_to(m + jnp.log(l) * LOG2E, shape3).reshape(rows, HEAD_DIM)

    for g, dil in enumerate(DILATIONS):
        nb = (SEQ // dil) // BAND

        def batch_body(i, c, g=g, nb=nb):
            unit_batch(g, i * nu, nb)
            return c
        lax.fori_loop(0, N_BLOCKS // nu, batch_body, 0)

    def interleave4(src, dst, n_rows):
        quarter = n_rows // 4
        for r in range(4):
            dst[pl.ds(r, quarter, stride=4), :] = src[r * quarter:(r + 1) * quarter, :]

    for src_all, dst_all in ((od, on), (ld, ln)):
        interleave4(src_all.at[1], dst_all.at[0], SEQ)
        for r_lo in range(4):
            for r_hi in range(4):
                r = r_lo + 4 * r_hi
                tmp[pl.ds(r_lo * (SEQ // 4) + r_hi, BAND, stride=4), :] = src_all[2, r * BAND:(r + 1) * BAND, :]
        interleave4(tmp, dst_all.at[1], SEQ)

    def comb(i, c):
        rs = pl.ds(pl.multiple_of(i * ROW_CHUNK, ROW_CHUNK), ROW_CHUNK)
        l0, l1, l2 = ld[0, rs, :], ln[0, rs, :], ln[1, rs, :]
        mx = jnp.maximum(jnp.maximum(l0, l1), l2)
        w0, w1, w2 = jnp.exp2(l0 - mx), jnp.exp2(l1 - mx), jnp.exp2(l2 - mx)
        o = (w0 * od[0, rs, :] + w1 * on[0, rs, :] + w2 * on[1, rs, :]) / (w0 + w1 + w2)
        a_ref[rs, :] = (o * _silu(z_ref[rs, :])).astype(BF16)
        return c
    lax.fori_loop(0, SEQ // ROW_CHUNK, comb, 0)


def _att_prompt(proj, bias_mats, a_sample):
    hb = N_HEADS

    def head_blk(b, h):
        return jnp.where(b < BATCH, h, hb - 1)

    n_pg = ATT_IN_COLS // D_MODEL
    in_specs = [
        pl.BlockSpec((n_pg, None, SEQ, HEAD_DIM),
                     lambda b, h: (0, head_blk(b, h), jnp.minimum(b, BATCH - 1), 0)),
        pl.BlockSpec((N_GROUPS, None, BAND, 2 * BAND), lambda b, h: (0, head_blk(b, h), 0, 0)),
        pl.BlockSpec((SAMPLE_ROWS, HEAD_DIM), lambda b, h: (0, h)),
    ]
    scratch = [pltpu.VMEM((N_GROUPS, SEQ, HEAD_DIM), BF16)] + \
              [pltpu.VMEM((N_GROUPS, SEQ + BAND, HEAD_DIM), BF16)] * 2 + \
              [pltpu.VMEM((N_GROUPS, SEQ, HEAD_DIM), F32)] * 2 + \
              [pltpu.VMEM((N_GROUPS - 1, SEQ, HEAD_DIM), F32)] * 2 + \
              [pltpu.VMEM((3, SEQ, HEAD_DIM), F32)]
    return pl.pallas_call(
        _att_prompt_kernel,
        grid=(BATCH + 1, N_HEADS),
        in_specs=in_specs,
        out_specs=pl.BlockSpec((SEQ, HEAD_DIM), lambda b, h: (b, h)),
        out_shape=jax.ShapeDtypeStruct((M_ALL, D_MODEL), BF16),
        scratch_shapes=scratch,
        compiler_params=pltpu.CompilerParams(
            dimension_semantics=("arbitrary", "arbitrary"), vmem_limit_bytes=VMEM_LIMIT),
        name="att_prompt",
    )(proj.reshape(n_pg, hb, M_ALL, HEAD_DIM), bias_mats.reshape(N_GROUPS, hb, BAND, 2 * BAND), a_sample)


def _kv_rows_kernel(k_a, v_a, k_b, v_b, o_ref, *, tr):
    pitch = tr + KV_PITCH_PAD

    def emit(srcs):
        flats = [src.reshape(N_HEADS * pitch, HEAD_DIM) for src in srcs]

        def body(i, c):
            t0 = i * 8
            for k in range(8):
                for part, flat in enumerate(flats):
                    o_ref[t0 + k, part] = flat[pl.ds(t0 + k, N_HEADS, stride=pitch), :]
            return c
        lax.fori_loop(0, tr // 8, body, 0)

    @pl.when(pl.program_id(0) == 0)
    def _():
        emit((k_a, v_a))

    @pl.when(pl.program_id(0) == 1)
    def _():
        emit((k_b, v_b))


def _kv_rows(proj_a, proj_b, g):
    keep = WINDOWS[g]
    tr = min(keep, TR_KV)
    nt = keep // tr
    first_blk = (SEQ - keep) // tr
    blks_per_batch = SEQ // tr

    def row_blk(b, t):
        return b * blks_per_batch + first_blk + t

    def spec(layer, part):
        parked = row_blk(BATCH - 1, nt - 1) if layer == 0 else row_blk(0, 0)
        col0 = ((1 + part) * N_GROUPS + g) * N_HEADS
        return pl.BlockSpec((pl.Element(N_HEADS), pl.Element(tr + KV_PITCH_PAD), pl.Element(HEAD_DIM)),
                            lambda l, b, t: (col0, jnp.where(l == layer, row_blk(b, t), parked) * tr, 0))

    return pl.pallas_call(
        functools.partial(_kv_rows_kernel, tr=tr),
        grid=(2, BATCH, nt),
        in_specs=[spec(0, 0), spec(0, 1), spec(1, 0), spec(1, 1)],
        out_specs=pl.BlockSpec((None, None, tr, 2, N_HEADS, HEAD_DIM), lambda l, b, t: (l, b, t, 0, 0, 0)),
        out_shape=jax.ShapeDtypeStruct((2, BATCH, keep, 2, N_HEADS, HEAD_DIM), F32),
        compiler_params=pltpu.CompilerParams(
            dimension_semantics=("arbitrary",) * 3, vmem_limit_bytes=VMEM_LIMIT),
        name="kv_rows",
    )(proj_a, proj_a, proj_b, proj_b)


def _att_sample_kernel(x_ref, c0, c1, c2, bias_ref, a_ref, kv0, kv1, kv2, buf0, buf1, buf2, sems, *, layer):
    cache_hbm = (c0, c1, c2)
    bufs = (buf0, buf1, buf2)
    kv_outs = (kv0, kv1, kv2)
    hb = N_HEADS

    def row_copies(b):
        slot = b % SAMPLE_BUFS
        return [pltpu.make_async_copy(cache_hbm[g].at[layer, b, pl.ds(0, BAND), 0], bufs[g].at[slot],
                                      sems.at[g, slot]) for g in range(N_GROUPS)]

    b = pl.program_id(0)

    @pl.when(b == 0)
    def _():
        for k in range(SAMPLE_BUFS - 1):
            for c in row_copies(jnp.int32(k)):
                c.start()

    @pl.when(b + SAMPLE_BUFS - 1 < DEC_BATCH)
    def _():
        for c in row_copies(b + SAMPLE_BUFS - 1):
            c.start()

    for c in row_copies(b):
        c.wait()
    caches = tuple(buf.at[b % SAMPLE_BUFS] for buf in bufs)
    outs, lses = [], []
    for g in range(N_GROUPS):
        q = x_ref[g * hb:(g + 1) * hb, :]
        kn = x_ref[(3 + g) * hb:(4 + g) * hb, :]
        vn = x_ref[(6 + g) * hb:(7 + g) * hb, :]
        kv_outs[g][0] = kn
        kv_outs[g][1] = vn
        kc = caches[g][:, 0]
        vc = caches[g][:, 1]
        s_c = jnp.sum(kc * q[None], axis=-1, keepdims=True) * ATT_SCALE + bias_ref[g, 0:BAND]
        s_n = jnp.sum(kn * q, axis=-1, keepdims=True) * ATT_SCALE + bias_ref[g, BAND]
        m = jnp.maximum(jnp.max(s_c, axis=0), s_n)
        p_c = jnp.exp(s_c - m[None])
        p_n = jnp.exp(s_n - m)
        l = jnp.sum(p_c, axis=0) + p_n
        o = (jnp.sum(p_c * vc, axis=0) + p_n * vn) / l
        outs.append(o)
        lses.append(m + jnp.log(l))
    mx = jnp.maximum(jnp.maximum(lses[0], lses[1]), lses[2])
    ws = [jnp.exp(ls - mx) for ls in lses]
    o = (ws[0] * outs[0] + ws[1] * outs[1] + ws[2] * outs[2]) / (ws[0] + ws[1] + ws[2])
    z = x_ref[9 * hb:10 * hb, :]
    a_ref[...] = o * _silu(z)


def _att_sample(proj_s, caches, bias_s, layer):
    hb = N_HEADS
    in_specs = [pl.BlockSpec((None, ATT_IN_COLS // HEAD_DIM, HEAD_DIM), lambda b: (b, 0, 0))]
    in_specs += [pl.BlockSpec(memory_space=pl.ANY)] * N_GROUPS
    in_specs.append(pl.BlockSpec((N_GROUPS, BAND + 1, hb, 1), lambda b: (0, 0, 0, 0)))
    out_specs = [pl.BlockSpec((None, hb, HEAD_DIM), lambda b: (b, 0, 0))]
    out_shapes = [jax.ShapeDtypeStruct((DEC_BATCH, hb, HEAD_DIM), F32)]
    for g in range(N_GROUPS):
        out_specs.append(pl.BlockSpec((None, 2, hb, HEAD_DIM), lambda b: (b, 0, 0, 0)))
        out_shapes.append(jax.ShapeDtypeStruct((DEC_BATCH, 2, hb, HEAD_DIM), F32))
    return pl.pallas_call(
        functools.partial(_att_sample_kernel, layer=layer),
        grid=(DEC_BATCH,),
        in_specs=in_specs,
        out_specs=out_specs,
        out_shape=out_shapes,
        scratch_shapes=[pltpu.VMEM((SAMPLE_BUFS, BAND, 2, hb, HEAD_DIM), F32)] * N_GROUPS
        + [pltpu.SemaphoreType.DMA((N_GROUPS, SAMPLE_BUFS))],
        compiler_params=pltpu.CompilerParams(
            dimension_semantics=("arbitrary",), vmem_limit_bytes=VMEM_LIMIT),
        name="att_sample",
    )(proj_s, *caches, bias_s)


def kernel(x_prompt, x_sample, c_prompt, c_sample, cache_kv0, cache_kv1, cache_kv2, state_pool,
           norm_pre, norm_post, ada_w, ada_b, t5_bias, pool_w_in, pool_w_grp, pool_scale,
           pool_w_out, att_w_in, att_w_out):
    n_att = DEPTH // 2
    xp0 = x_prompt.reshape(M_PROMPT, D_MODEL)
    xs0 = jnp.zeros((SAMPLE_ROWS, D_MODEL), F32).at[0:DEC_BATCH].set(x_sample.reshape(DEC_BATCH, D_MODEL))
    c_all = jnp.zeros((C_ROWS, D_MODEL), F32)
    c_all = c_all.at[0:BATCH].set(c_prompt).at[C_SAMPLE_ROW0:C_SAMPLE_ROW0 + DEC_BATCH].set(c_sample)

    mod = _ada_all(c_all, ada_w, ada_b)
    gains_pre = norm_pre.reshape(DEPTH, 1, D_MODEL)
    gains_post = norm_post.reshape(DEPTH, 1, D_MODEL)

    bias_mats = _bias_matrices(t5_bias)
    bias_s = bias_mats[:, 0, 0:BAND + 1].reshape(N_GROUPS, N_HEADS, BAND + 1)
    bias_s = jnp.transpose(bias_s, (0, 2, 1))[..., None]
    caches = [c.reshape(n_att, DEC_BATCH, BAND, dil, 2, N_HEADS, HEAD_DIM)
              for c, dil in zip((cache_kv0, cache_kv1, cache_kv2), DILATIONS)]
    state_t = jnp.transpose(state_pool, (0, 2, 1, 3))

    kv_s = [[] for _ in range(N_GROUPS)]
    pool_p, pool_s, att_projs = [], [], []

    (h,) = _norm_step(xp0, None, mod, gains_post, gains_pre, None, 0, x_sample=xs0)
    x = None
    for i in range(DEPTH):
        li = i // 2
        if i % 2 == 0:
            proj = _matmul(h, pool_w_in, li, 2 * POOL_WIDTH, 1024, TM_MATMUL, F32, "pool_in_proj")
            a, pp, ps = _pool_mix(proj, state_t, pool_w_grp, pool_scale, li)
            pool_p.append(pp)
            pool_s.append(jnp.transpose(ps, (1, 0, 2)))
            y = _matmul(a, pool_w_out, li, D_MODEL, 512, TM_MATMUL, BF16, "pool_out_proj")
        else:
            proj = _matmul(h, att_w_in, li, ATT_IN_COLS, 1024, TM_MATMUL, F32, "att_in_proj", slabs=True)
            att_projs.append(proj)
            proj_s = jnp.transpose(proj[:, M_PROMPT:M_PROMPT + DEC_BATCH, :], (1, 0, 2))
            souts = _att_sample(proj_s, caches, bias_s, li)
            a_s = jnp.zeros((SAMPLE_ROWS, D_MODEL), F32).at[0:DEC_BATCH].set(souts[0].reshape(DEC_BATCH, D_MODEL))
            a = _att_prompt(proj, bias_mats, a_s.astype(BF16))
            for g in range(N_GROUPS):
                kv_s[g].append(souts[1 + g].reshape(DEC_BATCH, 1, 2, N_HEADS, HEAD_DIM))
            y = _matmul(a, att_w_out, li, D_MODEL, 1024, TM_MATMUL, BF16, "att_out_proj")
        last = i + 1 == DEPTH
        if i == 0:
            x, h = _norm_step(xp0, y, mod, gains_post, gains_pre, i, i + 1, x_sample=xs0)
        elif not last:
            x, h = _norm_step(x, y, mod, gains_post, gains_pre, i, i + 1)
        else:
            y_p, y_s = _norm_step(x, y, mod, gains_post, gains_pre, i, None, split_out=True)

    kv_p = [_kv_rows(att_projs[0], att_projs[1], g) for g in range(N_GROUPS)]
    y_prompt = y_p.reshape(BATCH, SEQ, D_MODEL)
    y_sample = y_s[0:DEC_BATCH].reshape(DEC_BATCH, 1, D_MODEL)
    return (y_prompt, y_sample, kv_p[0], kv_p[1], kv_p[2], jnp.stack(pool_p),
            jnp.stack(kv_s[0]), jnp.stack(kv_s[1]), jnp.stack(kv_s[2]), jnp.stack(pool_s))
```

```python
import functools

import numpy as np
import jax
import jax.numpy as jnp
from jax import lax
from jax.experimental import pallas as pl
from jax.experimental.pallas import tpu as pltpu

D_MODEL = 2048
BATCH = 4
SEQ = 2048
DEPTH = 4
DEC_BATCH = 8
HEAD_DIM = 128
N_HEADS = 16
DILATIONS = (1, 4, 16)
WINDOWS = (128, 512, 2048)
N_GROUPS = 3
QKV_WIDTH = N_GROUPS * D_MODEL
ATT_IN_COLS = 3 * QKV_WIDTH + D_MODEL
BAND = 128
ATT_SCALE = HEAD_DIM ** -0.5
LOG2E = 1.4426950408889634
POOL_WINDOWS = (2, 4, 8, 16)
POOL_WIDTH = 2 * D_MODEL
POOL_GROUP = POOL_WIDTH // 4
POOL_BUF = 15
N_BUCKETS = 32
T5_MAX_DIST = 2048
RMS_EPS = 1e-6
NEG_INF = -1e30

M_PROMPT = BATCH * SEQ
SAMPLE_ROWS = 64
M_ALL = M_PROMPT + SAMPLE_ROWS
C_ROWS = 2 * SAMPLE_ROWS
C_PROMPT_ROWS = 8
C_SAMPLE_ROW0 = SAMPLE_ROWS

TM_MATMUL = 2064
MXU_COLS = 256
TE_NORM = 512
NORM_BUFS = 4
TP_POOL = 1024
POOL_HALO = 24
ATT_UNITS = 16
N_BLOCKS = SEQ // BAND
TR_KV = 256
ROW_CHUNK = 256
KV_PITCH_PAD = 8
VMEM_LIMIT = 58 * 1024 * 1024

F32 = jnp.float32
BF16 = jnp.bfloat16


def _silu(x):
    half = 0.5 * x
    return half + half * jnp.tanh(half)


def _cast_rows_to_bf16(src_ref, dst_ref, rows, chunk=ROW_CHUNK):
    def body(i, c):
        r = pl.multiple_of(i * chunk, chunk)
        dst_ref[pl.ds(r, chunk), :] = src_ref[pl.ds(r, chunk), :].astype(BF16)
        return c
    lax.fori_loop(0, rows // chunk, body, 0)


def _ada_kernel(c_ref, w_ref, b_ref, o_ref):
    a = _silu(c_ref[...]).astype(BF16)
    kc = 512
    acc = jnp.zeros(o_ref.shape, F32)
    for k0 in range(0, D_MODEL, kc):
        acc = acc + jnp.dot(a[:, k0:k0 + kc], w_ref[k0:k0 + kc, :].astype(BF16),
                            preferred_element_type=F32)
    o_ref[...] = acc + b_ref[...]


def _ada_all(c_all, ada_w, ada_b):
    tn = 1024
    n = 3 * D_MODEL
    return pl.pallas_call(
        _ada_kernel,
        grid=(DEPTH, n // tn),
        in_specs=[pl.BlockSpec((C_ROWS, D_MODEL), lambda l, j: (0, 0)),
                  pl.BlockSpec((None, D_MODEL, tn), lambda l, j: (l, 0, j)),
                  pl.BlockSpec((None, 1, tn), lambda l, j: (l, 0, j))],
        out_specs=pl.BlockSpec((None, C_ROWS, tn), lambda l, j: (l, 0, j)),
        out_shape=jax.ShapeDtypeStruct((DEPTH, C_ROWS, n), F32),
        compiler_params=pltpu.CompilerParams(
            dimension_semantics=("arbitrary", "arbitrary"), vmem_limit_bytes=VMEM_LIMIT),
        name="ada_mod",
    )(c_all, ada_w, ada_b.reshape(DEPTH, 1, n))


def _t5_bucket(dist):
    dist = np.asarray(dist, dtype=np.int64)
    max_exact = N_BUCKETS // 2
    ratio = np.log(np.maximum(dist, 1) / max_exact) / np.log(T5_MAX_DIST / max_exact)
    large = np.minimum(max_exact + (ratio * (N_BUCKETS - max_exact)).astype(np.int64), N_BUCKETS - 1)
    return np.where(dist < max_exact, dist, large).astype(np.int32)


def _bucket_index_table():
    rel = np.arange(BAND)[:, None] + BAND - np.arange(2 * BAND)[None, :]
    inband = (rel >= 0) & (rel <= BAND)
    out = []
    for dil in DILATIONS:
        bucket = _t5_bucket(np.clip(rel, 0, BAND) * dil)
        out.append(np.where(inband, bucket, -1))
    return np.stack(out).astype(np.int32)


def _bias_kernel(tab_ref, idx_ref, o_ref):
    g = pl.program_id(0)
    idx = idx_ref[...]

    def head(h, c):
        acc = jnp.full(idx.shape, NEG_INF, F32)
        for b in range(N_BUCKETS):
            acc = jnp.where(idx == b, tab_ref[b, g * N_HEADS + h], acc)
        o_ref[h] = acc
        return c
    lax.fori_loop(0, N_HEADS, head, 0)


def _bias_matrices(t5_bias):
    idx = jnp.asarray(_bucket_index_table())
    n_sub = N_GROUPS * N_HEADS
    return pl.pallas_call(
        _bias_kernel,
        grid=(N_GROUPS,),
        in_specs=[pl.BlockSpec(memory_space=pltpu.SMEM),
                  pl.BlockSpec((None, BAND, 2 * BAND), lambda g: (g, 0, 0))],
        out_specs=pl.BlockSpec((N_HEADS, BAND, 2 * BAND), lambda g: (g, 0, 0)),
        out_shape=jax.ShapeDtypeStruct((n_sub, BAND, 2 * BAND), F32),
        name="t5_bias_mats",
    )(t5_bias, idx)


def _norm_kernel(*refs, has_post, has_pre, split_in, split_out, n_prompt_tiles, tiles_per_batch, te):
    refs = list(refs)
    x_hbm = refs.pop(0)
    xs_ref = refs.pop(0) if split_in else None
    if has_post:
        y_hbm, gpost_ref, gate_p_ref, gate_s_ref = refs[:4]
        refs = refs[4:]
    if has_pre:
        gpre_ref, shift_p_ref, scale_p_ref, shift_s_ref, scale_s_ref = refs[:5]
        refs = refs[5:]
    if has_post:
        xop_ref = refs.pop(0)
        xos_ref = refs.pop(0) if split_out else xop_ref
    if has_pre:
        h_ref = refs.pop(0)
    xbuf, semx = refs[:2]
    if has_post:
        ybuf, semy = refs[2:4]
    npt = n_prompt_tiles

    def tile_copies(k, slot, sample):
        if sample:
            rows_src, rows_dst = pl.ds(M_PROMPT, SAMPLE_ROWS), pl.ds(0, SAMPLE_ROWS)
        else:
            rows_src, rows_dst = pl.ds(pl.multiple_of(k * te, te), te), pl.ds(0, te)
        out = []
        if not (sample and split_in):
            out.append(pltpu.make_async_copy(x_hbm.at[rows_src], xbuf.at[slot, rows_dst], semx.at[slot]))
        if has_post:
            out.append(pltpu.make_async_copy(y_hbm.at[rows_src], ybuf.at[slot, rows_dst], semy.at[slot]))
        return out

    def for_tile(k, action):
        slot = k % NORM_BUFS

        @pl.when(k < npt)
        def _():
            for c in tile_copies(k, slot, False):
                action(c)

        @pl.when(k == npt)
        def _():
            for c in tile_copies(k, slot, True):
                action(c)

    def rms(v, g):
        return v * lax.rsqrt(jnp.mean(v * v, axis=-1, keepdims=True) + RMS_EPS) * g

    def body(x, y, xo_ref, rows, gate, shift, scale):
        if has_post:
            x = x + gate * rms(y.astype(F32), gpost_ref[...])
            xo_ref[rows, :] = x
        if has_pre:
            h = rms(x, gpre_ref[...]) * (1.0 + scale) + shift
            h_ref[rows, :] = h.astype(BF16)

    t = pl.program_id(0)
    slot = t % NORM_BUFS

    @pl.when(t == 0)
    def _():
        for k in range(NORM_BUFS - 1):
            for_tile(jnp.int32(k), lambda c: c.start())

    @pl.when(t + NORM_BUFS - 1 <= npt)
    def _():
        for_tile(t + NORM_BUFS - 1, lambda c: c.start())

    for_tile(t, lambda c: c.wait())

    @pl.when(t < npt)
    def _():
        row = pl.ds(t // tiles_per_batch, 1)
        body(xbuf[slot], ybuf[slot] if has_post else None, xop_ref if has_post else None, slice(None),
             gate_p_ref[row, :] if has_post else None,
             shift_p_ref[row, :] if has_pre else None,
             scale_p_ref[row, :] if has_pre else None)

    @pl.when(t == npt)
    def _():
        rows = slice(0, SAMPLE_ROWS)
        body(xs_ref[...] if split_in else xbuf[slot, rows, :],
             ybuf[slot, rows, :] if has_post else None, xos_ref if has_post else None, rows,
             gate_s_ref[...] if has_post else None,
             shift_s_ref[...] if has_pre else None,
             scale_s_ref[...] if has_pre else None)


def _norm_step(x, y, mod, norm_post, norm_pre, post_layer, pre_layer, x_sample=None, split_out=False):
    has_post = post_layer is not None
    has_pre = pre_layer is not None
    split_in = x_sample is not None
    te = TE_NORM
    npt = M_PROMPT // te
    tiles_per_batch = SEQ // te
    row_spec = pl.BlockSpec((te, D_MODEL), lambda t: (t, 0))
    prompt_row_spec = pl.BlockSpec((te, D_MODEL), lambda t: (jnp.minimum(t, npt - 1), 0))
    sample_row_spec = pl.BlockSpec((SAMPLE_ROWS, D_MODEL), lambda t: (0, 0))

    def mod_p_spec(layer, part):
        return pl.BlockSpec((None, C_PROMPT_ROWS, D_MODEL), lambda t: (layer, 0, part))

    def mod_s_spec(layer, part):
        return pl.BlockSpec((None, SAMPLE_ROWS, D_MODEL),
                            lambda t: (layer, C_SAMPLE_ROW0 // SAMPLE_ROWS, part))

    def gain_spec(layer):
        return pl.BlockSpec((None, 1, D_MODEL), lambda t: (layer, 0, 0))

    hbm_spec = pl.BlockSpec(memory_space=pl.ANY)
    if split_in:
        args, in_specs = [x, x_sample], [hbm_spec, sample_row_spec]
    else:
        args, in_specs = [x], [hbm_spec]
    out_shapes, out_specs = [], []
    scratch = [pltpu.VMEM((NORM_BUFS, te, D_MODEL), F32), pltpu.SemaphoreType.DMA((NORM_BUFS,))]
    if has_post:
        args += [y, norm_post, mod, mod]
        in_specs += [hbm_spec, gain_spec(post_layer), mod_p_spec(post_layer, 2), mod_s_spec(post_layer, 2)]
        scratch += [pltpu.VMEM((NORM_BUFS, te, D_MODEL), y.dtype), pltpu.SemaphoreType.DMA((NORM_BUFS,))]
        if split_out:
            out_shapes += [jax.ShapeDtypeStruct((M_PROMPT, D_MODEL), F32),
                           jax.ShapeDtypeStruct((SAMPLE_ROWS, D_MODEL), F32)]
            out_specs += [prompt_row_spec, sample_row_spec]
        else:
            out_shapes.append(jax.ShapeDtypeStruct((M_ALL, D_MODEL), F32))
            out_specs.append(row_spec)
    if has_pre:
        args += [norm_pre, mod, mod, mod, mod]
        in_specs += [gain_spec(pre_layer), mod_p_spec(pre_layer, 0), mod_p_spec(pre_layer, 1),
                     mod_s_spec(pre_layer, 0), mod_s_spec(pre_layer, 1)]
        out_shapes.append(jax.ShapeDtypeStruct((M_ALL, D_MODEL), BF16))
        out_specs.append(row_spec)
    outs = pl.pallas_call(
        functools.partial(_norm_kernel, has_post=has_post, has_pre=has_pre, split_in=split_in,
                          split_out=split_out, n_prompt_tiles=npt, tiles_per_batch=tiles_per_batch, te=te),
        grid=(npt + 1,),
        in_specs=in_specs,
        out_specs=out_specs,
        out_shape=out_shapes,
        scratch_shapes=scratch,
        compiler_params=pltpu.CompilerParams(
            dimension_semantics=("arbitrary",), vmem_limit_bytes=VMEM_LIMIT),
        name="norm_step",
    )(*args)
    return outs


def _mm_kernel(a_ref, w_ref, o_ref, *, slabs):
    tn = w_ref.shape[1]
    for c0 in range(0, tn, MXU_COLS):
        acc = jnp.dot(a_ref[...], w_ref[:, c0:c0 + MXU_COLS].astype(BF16),
                      preferred_element_type=F32).astype(o_ref.dtype)
        if slabs:
            for c in range(MXU_COLS // HEAD_DIM):
                o_ref[c0 // HEAD_DIM + c] = acc[:, c * HEAD_DIM:(c + 1) * HEAD_DIM]
        else:
            o_ref[:, c0:c0 + MXU_COLS] = acc


def _matmul(a, w, layer, n_out, tn, tm, out_dtype, name, slabs=False):
    m, k = a.shape
    if slabs:
        out_spec = pl.BlockSpec((tn // HEAD_DIM, tm, HEAD_DIM), lambda j, i: (j, i, 0))
        out_shape = jax.ShapeDtypeStruct((n_out // HEAD_DIM, m, HEAD_DIM), out_dtype)
    else:
        out_spec = pl.BlockSpec((tm, tn), lambda j, i: (i, j))
        out_shape = jax.ShapeDtypeStruct((m, n_out), out_dtype)
    return pl.pallas_call(
        functools.partial(_mm_kernel, slabs=slabs),
        grid=(n_out // tn, m // tm),
        in_specs=[pl.BlockSpec((tm, k), lambda j, i: (i, 0)),
                  pl.BlockSpec((None, k, tn), lambda j, i: (layer, 0, j))],
        out_specs=out_spec,
        out_shape=out_shape,
        compiler_params=pltpu.CompilerParams(
            dimension_semantics=("arbitrary", "arbitrary"), vmem_limit_bytes=VMEM_LIMIT),
        name=name,
    )(a, w)


def _pool_kernel(u_ref, halo_ref, z_ref, st_ref, w_ref, sc_ref,
                 a_ref, pp_ref, ps_ref, wb_ref, buf_a, buf_b, *, n_prompt_tiles, tiles_per_batch):
    g = pl.program_id(0)
    t = pl.program_id(1)
    tp = TP_POOL
    h0 = POOL_HALO

    @pl.when(t == 0)
    def _():
        _cast_rows_to_bf16(w_ref, wb_ref, POOL_GROUP)

    def finish(r, z, rows):
        y = jnp.dot(r.astype(BF16), wb_ref[...], preferred_element_type=F32) * sc_ref[...]
        a_ref[rows, :] = (y * _silu(z)).astype(BF16)

    def prompt_tile(n_steps):
        w = 2 ** n_steps
        first = (t % tiles_per_batch) == 0
        buf_a[0:8, :] = jnp.zeros((8, POOL_GROUP), F32)
        buf_b[0:8, :] = jnp.zeros((8, POOL_GROUP), F32)
        buf_a[8:h0, :] = jnp.where(first, 0.0, halo_ref[...])
        buf_a[h0:h0 + tp, :] = u_ref[...]
        src, dst = buf_a, buf_b
        n = tp + h0 - 8
        for s in range(n_steps):
            sh = 2 ** s
            dst[8:8 + n, :] = src[8:8 + n, :] + src[8 - sh:8 - sh + n, :]
            src, dst = dst, src
        pos = (t % tiles_per_batch) * tp + lax.broadcasted_iota(jnp.int32, (tp, 1), 0)
        inv_cnt = 1.0 / jnp.minimum(pos + 1, w).astype(F32)
        u = u_ref[...]
        r = src[h0:h0 + tp, :] * inv_cnt - u
        finish(r, z_ref[...], slice(None))

        @pl.when((t % tiles_per_batch) == tiles_per_batch - 1)
        def _():
            pp_ref[...] = u_ref[tp - POOL_BUF:tp, :]

    def sample_tile(n_steps):
        w = 2 ** n_steps
        u_new = u_ref[0:DEC_BATCH, :]
        acc = u_new
        for k in range(1, w):
            acc = acc + st_ref[POOL_BUF - k]
        r = acc / float(w) - u_new
        buf_a[0:DEC_BATCH, :] = r
        buf_a[DEC_BATCH:SAMPLE_ROWS, :] = jnp.zeros((SAMPLE_ROWS - DEC_BATCH, POOL_GROUP), F32)
        finish(buf_a[0:SAMPLE_ROWS, :], z_ref[0:SAMPLE_ROWS, :], slice(0, SAMPLE_ROWS))
        for k in range(POOL_BUF - 1):
            ps_ref[k] = st_ref[k + 1]
        ps_ref[POOL_BUF - 1] = u_new

    for gi in range(len(POOL_WINDOWS)):
        @pl.when((g == gi) & (t < n_prompt_tiles))
        def _(gi=gi):
            prompt_tile(gi + 1)

        @pl.when((g == gi) & (t == n_prompt_tiles))
        def _(gi=gi):
            sample_tile(gi + 1)


def _pool_mix(proj, state_t, w_grp, scale, layer):
    tp = TP_POOL
    npt = M_PROMPT // tp
    tpb = SEQ // tp
    ng = len(POOL_WINDOWS)
    halo_blocks = tp // 16
    outs = pl.pallas_call(
        functools.partial(_pool_kernel, n_prompt_tiles=npt, tiles_per_batch=tpb),
        grid=(ng, npt + 1),
        in_specs=[
            pl.BlockSpec((tp, POOL_GROUP), lambda g, t: (t, g)),
            pl.BlockSpec((16, POOL_GROUP), lambda g, t: (jnp.maximum(t * halo_blocks - 1, 0), g)),
            pl.BlockSpec((tp, POOL_GROUP), lambda g, t: (t, ng + g)),
            pl.BlockSpec((None, POOL_BUF, DEC_BATCH, POOL_GROUP), lambda g, t: (layer, 0, 0, g)),
            pl.BlockSpec((None, None, POOL_GROUP, POOL_GROUP), lambda g, t: (layer, g, 0, 0)),
            pl.BlockSpec((None, 1, POOL_GROUP), lambda g, t: (layer, 0, g)),
        ],
        out_specs=[
            pl.BlockSpec((tp, POOL_GROUP), lambda g, t: (t, g)),
            pl.BlockSpec((None, POOL_BUF, POOL_GROUP),
                         lambda g, t: (jnp.minimum(t // tpb, BATCH - 1), 0, g)),
            pl.BlockSpec((POOL_BUF, DEC_BATCH, POOL_GROUP), lambda g, t: (0, 0, g)),
        ],
        out_shape=[
            jax.ShapeDtypeStruct((M_ALL, POOL_WIDTH), BF16),
            jax.ShapeDtypeStruct((BATCH, POOL_BUF, POOL_WIDTH), F32),
            jax.ShapeDtypeStruct((POOL_BUF, DEC_BATCH, POOL_WIDTH), F32),
        ],
        scratch_shapes=[pltpu.VMEM((POOL_GROUP, POOL_GROUP), BF16),
                        pltpu.VMEM((tp + POOL_HALO, POOL_GROUP), F32),
                        pltpu.VMEM((tp + POOL_HALO, POOL_GROUP), F32)],
        compiler_params=pltpu.CompilerParams(
            dimension_semantics=("arbitrary", "arbitrary"), vmem_limit_bytes=VMEM_LIMIT),
        name="pool_mix",
    )(proj, proj, proj, state_t, w_grp, scale.reshape(-1, 1, POOL_WIDTH))
    return outs


def _att_prompt_kernel(x_ref, bias_ref, as_ref, a_ref, *scratch):
    @pl.when(pl.program_id(0) < BATCH)
    def _():
        _att_prompt_tile(x_ref, bias_ref, a_ref, *scratch)

    @pl.when(pl.program_id(0) == BATCH)
    def _():
        a_ref[0:SAMPLE_ROWS, :] = as_ref[...]


def _att_prompt_tile(x_ref, bias_ref, a_ref, qd, kd, vd, od, ld, on, ln, tmp3):
    tmp = tmp3.at[0]
    q_refs = tuple(x_ref.at[g] for g in range(N_GROUPS))
    k_refs = tuple(x_ref.at[N_GROUPS + g] for g in range(N_GROUPS))
    v_refs = tuple(x_ref.at[2 * N_GROUPS + g] for g in range(N_GROUPS))
    z_ref = x_ref.at[3 * N_GROUPS]
    bias_refs = tuple(bias_ref.at[g] for g in range(N_GROUPS))
    nu = ATT_UNITS

    for g, dil in enumerate(DILATIONS):
        n = SEQ // dil
        nb = n // BAND
        zero_blk = jnp.zeros((BAND, HEAD_DIM), BF16)
        kd[g, 0:BAND, :] = zero_blk
        vd[g, 0:BAND, :] = zero_blk
        for src, dst, off, mul in ((q_refs[g], qd, 0, ATT_SCALE * LOG2E), (k_refs[g], kd, BAND, None),
                                   (v_refs[g], vd, BAND, None)):
            def to_bf16(x, mul=mul):
                return (x if mul is None else x * mul).astype(BF16)

            if dil == 1:
                dst[g, off:off + SEQ, :] = to_bf16(src[...])
            elif dil == 4:
                for r in range(dil):
                    dst[g, off + r * n:off + (r + 1) * n, :] = to_bf16(src[pl.ds(r, n, stride=dil), :])
            else:
                stage = tmp3.at[(off > 0) + (dst is vd)]
                quarter = SEQ // 4
                for r_lo in range(4):
                    stage[r_lo * quarter:(r_lo + 1) * quarter, :] = src[pl.ds(r_lo, quarter, stride=4), :]
                for r in range(dil):
                    r_lo, r_hi = r % 4, r // 4
                    dst[g, off + r * n:off + (r + 1) * n, :] = to_bf16(
                        stage[pl.ds(r_lo * quarter + r_hi, n, stride=4), :])

    def unit_batch(g, u0, seq_blocks):
        with_prev = seq_blocks > 1
        rows = nu * BAND
        row = pl.multiple_of(u0 * BAND, rows)
        shape3 = (nu, BAND, HEAD_DIM)
        q = qd[g, pl.ds(row, rows), :].reshape(shape3)
        kc = kd[g, pl.ds(row + BAND, rows), :].reshape(shape3)
        vc = vd[g, pl.ds(row + BAND, rows), :].reshape(shape3)
        bias = bias_refs[g]
        s_c = jnp.einsum('uqe,uke->uqk', q, kc, preferred_element_type=F32) \
            + (bias[:, BAND:2 * BAND] * LOG2E)[None]
        if with_prev:
            kp = kd[g, pl.ds(row, rows), :].reshape(shape3)
            vp = vd[g, pl.ds(row, rows), :].reshape(shape3)
            bias_p = jnp.broadcast_to((bias[:, 0:BAND] * LOG2E)[None], (nu, BAND, BAND))
            blk = u0 + lax.broadcasted_iota(jnp.int32, (nu, BAND, BAND), 0)
            bias_p = jnp.where((blk & (seq_blocks - 1)) == 0, NEG_INF, bias_p)
            s_p = jnp.einsum('uqe,uke->uqk', q, kp, preferred_element_type=F32) + bias_p
            m = jnp.max(jnp.maximum(s_c, s_p), axis=-1, keepdims=True)
            p_c = jnp.exp2(s_c - m)
            p_p = jnp.exp2(s_p - m)
            l = jnp.sum(p_c + p_p, axis=-1, keepdims=True)
            acc = jnp.einsum('uqk,uke->uqe', p_c.astype(BF16), vc, preferred_element_type=F32) \
                + jnp.einsum('uqk,uke->uqe', p_p.astype(BF16), vp, preferred_element_type=F32)
        else:
            m = jnp.max(s_c, axis=-1, keepdims=True)
            p_c = jnp.exp2(s_c - m)
            l = jnp.sum(p_c, axis=-1, keepdims=True)
            acc = jnp.einsum('uqk,uke->uqe', p_c.astype(BF16), vc, preferred_element_type=F32)
        od[g, pl.ds(row, rows), :] = (acc / l).reshape(rows, HEAD_DIM)
        ld[g, pl.ds(row, rows), :] = jnp.broadcast_to(m + jnp.log(l) * LOG2E, shape3).reshape(rows, HEAD_DIM)

    for g, dil in enumerate(DILATIONS):
        nb = (SEQ // dil) // BAND

        def batch_body(i, c, g=g, nb=nb):
            unit_batch(g, i * nu, nb)
            return c
        lax.fori_loop(0, N_BLOCKS // nu, batch_body, 0)

    def interleave4(src, dst, n_rows):
        quarter = n_rows // 4
        for r in range(4):
            dst[pl.ds(r, quarter, stride=4), :] = src[r * quarter:(r + 1) * quarter, :]

    for src_all, dst_all in ((od, on), (ld, ln)):
        interleave4(src_all.at[1], dst_all.at[0], SEQ)
        for r_lo in range(4):
            for r_hi in range(4):
                r = r_lo + 4 * r_hi
                tmp[pl.ds(r_lo * (SEQ // 4) + r_hi, BAND, stride=4), :] = src_all[2, r * BAND:(r + 1) * BAND, :]
        interleave4(tmp, dst_all.at[1], SEQ)

    def comb(i, c):
        rs = pl.ds(pl.multiple_of(i * ROW_CHUNK, ROW_CHUNK), ROW_CHUNK)
        l0, l1, l2 = ld[0, rs, :], ln[0, rs, :], ln[1, rs, :]
        mx = jnp.maximum(jnp.maximum(l0, l1), l2)
        w0, w1, w2 = jnp.exp2(l0 - mx), jnp.exp2(l1 - mx), jnp.exp2(l2 - mx)
        o = (w0 * od[0, rs, :] + w1 * on[0, rs, :] + w2 * on[1, rs, :]) / (w0 + w1 + w2)
        a_ref[rs, :] = (o * _silu(z_ref[rs, :])).astype(BF16)
        return c
    lax.fori_loop(0, SEQ // ROW_CHUNK, comb, 0)


def _att_prompt(proj, bias_mats, a_sample):
    hb = N_HEADS

    def head_blk(b, h):
        return jnp.where(b < BATCH, h, hb - 1)

    n_pg = ATT_IN_COLS // D_MODEL
    in_specs = [
        pl.BlockSpec((n_pg, None, SEQ, HEAD_DIM),
                     lambda b, h: (0, head_blk(b, h), jnp.minimum(b, BATCH - 1), 0)),
        pl.BlockSpec((N_GROUPS, None, BAND, 2 * BAND), lambda b, h: (0, head_blk(b, h), 0, 0)),
        pl.BlockSpec((SAMPLE_ROWS, HEAD_DIM), lambda b, h: (0, h)),
    ]
    scratch = [pltpu.VMEM((N_GROUPS, SEQ, HEAD_DIM), BF16)] + \
              [pltpu.VMEM((N_GROUPS, SEQ + BAND, HEAD_DIM), BF16)] * 2 + \
              [pltpu.VMEM((N_GROUPS, SEQ, HEAD_DIM), F32)] * 2 + \
              [pltpu.VMEM((N_GROUPS - 1, SEQ, HEAD_DIM), F32)] * 2 + \
              [pltpu.VMEM((3, SEQ, HEAD_DIM), F32)]
    return pl.pallas_call(
        _att_prompt_kernel,
        grid=(BATCH + 1, N_HEADS),
        in_specs=in_specs,
        out_specs=pl.BlockSpec((SEQ, HEAD_DIM), lambda b, h: (b, h)),
        out_shape=jax.ShapeDtypeStruct((M_ALL, D_MODEL), BF16),
        scratch_shapes=scratch,
        compiler_params=pltpu.CompilerParams(
            dimension_semantics=("arbitrary", "arbitrary"), vmem_limit_bytes=VMEM_LIMIT),
        name="att_prompt",
    )(proj.reshape(n_pg, hb, M_ALL, HEAD_DIM), bias_mats.reshape(N_GROUPS, hb, BAND, 2 * BAND), a_sample)


def _kv_rows_kernel(k_a, v_a, k_b, v_b, o_ref, *, tr):
    pitch = tr + KV_PITCH_PAD

    def emit(srcs):
        flats = [src.reshape(N_HEADS * pitch, HEAD_DIM) for src in srcs]

        def body(i, c):
            t0 = i * 8
            for k in range(8):
                for part, flat in enumerate(flats):
                    o_ref[t0 + k, part] = flat[pl.ds(t0 + k, N_HEADS, stride=pitch), :]
            return c
        lax.fori_loop(0, tr // 8, body, 0)

    @pl.when(pl.program_id(0) == 0)
    def _():
        emit((k_a, v_a))

    @pl.when(pl.program_id(0) == 1)
    def _():
        emit((k_b, v_b))


def _kv_rows(proj_a, proj_b, g):
    keep = WINDOWS[g]
    tr = min(keep, TR_KV)
    nt = keep // tr
    first_blk = (SEQ - keep) // tr
    blks_per_batch = SEQ // tr

    def row_blk(b, t):
        return b * blks_per_batch + first_blk + t

    def spec(layer, part):
        parked = row_blk(BATCH - 1, nt - 1) if layer == 0 else row_blk(0, 0)
        col0 = ((1 + part) * N_GROUPS + g) * N_HEADS
        return pl.BlockSpec((pl.Element(N_HEADS), pl.Element(tr + KV_PITCH_PAD), pl.Element(HEAD_DIM)),
                            lambda l, b, t: (col0, jnp.where(l == layer, row_blk(b, t), parked) * tr, 0))

    return pl.pallas_call(
        functools.partial(_kv_rows_kernel, tr=tr),
        grid=(2, BATCH, nt),
        in_specs=[spec(0, 0), spec(0, 1), spec(1, 0), spec(1, 1)],
        out_specs=pl.BlockSpec((None, None, tr, 2, N_HEADS, HEAD_DIM), lambda l, b, t: (l, b, t, 0, 0, 0)),
        out_shape=jax.ShapeDtypeStruct((2, BATCH, keep, 2, N_HEADS, HEAD_DIM), F32),
        compiler_params=pltpu.CompilerParams(
            dimension_semantics=("arbitrary",) * 3, vmem_limit_bytes=VMEM_LIMIT),
        name="kv_rows",
    )(proj_a, proj_a, proj_b, proj_b)


def _att_sample_kernel(x_ref, c0, c1, c2, bias_ref, a_ref, kv0, kv1, kv2):
    caches = (c0, c1, c2)
    kv_outs = (kv0, kv1, kv2)
    hb = N_HEADS
    outs, lses = [], []
    for g in range(N_GROUPS):
        q = x_ref[g * hb:(g + 1) * hb, :]
        kn = x_ref[(3 + g) * hb:(4 + g) * hb, :]
        vn = x_ref[(6 + g) * hb:(7 + g) * hb, :]
        kv_outs[g][0] = kn
        kv_outs[g][1] = vn
        kc = caches[g][:, 0]
        vc = caches[g][:, 1]
        s_c = jnp.sum(kc * q[None], axis=-1, keepdims=True) * ATT_SCALE + bias_ref[g, 0:BAND]
        s_n = jnp.sum(kn * q, axis=-1, keepdims=True) * ATT_SCALE + bias_ref[g, BAND]
        m = jnp.maximum(jnp.max(s_c, axis=0), s_n)
        p_c = jnp.exp(s_c - m[None])
        p_n = jnp.exp(s_n - m)
        l = jnp.sum(p_c, axis=0) + p_n
        o = (jnp.sum(p_c * vc, axis=0) + p_n * vn) / l
        outs.append(o)
        lses.append(m + jnp.log(l))
    mx = jnp.maximum(jnp.maximum(lses[0], lses[1]), lses[2])
    ws = [jnp.exp(ls - mx) for ls in lses]
    o = (ws[0] * outs[0] + ws[1] * outs[1] + ws[2] * outs[2]) / (ws[0] + ws[1] + ws[2])
    z = x_ref[9 * hb:10 * hb, :]
    a_ref[...] = o * _silu(z)


def _att_sample(proj_s, caches, bias_s, layer):
    hb = N_HEADS
    in_specs = [pl.BlockSpec((None, ATT_IN_COLS // HEAD_DIM, HEAD_DIM), lambda b: (b, 0, 0))]
    for g in range(N_GROUPS):
        in_specs.append(pl.BlockSpec((None, None, BAND, None, 2, hb, HEAD_DIM),
                                     lambda b: (layer, b, 0, 0, 0, 0, 0)))
    in_specs.append(pl.BlockSpec((N_GROUPS, BAND + 1, hb, 1), lambda b: (0, 0, 0, 0)))
    out_specs = [pl.BlockSpec((None, hb, HEAD_DIM), lambda b: (b, 0, 0))]
    out_shapes = [jax.ShapeDtypeStruct((DEC_BATCH, hb, HEAD_DIM), F32)]
    for g in range(N_GROUPS):
        out_specs.append(pl.BlockSpec((None, 2, hb, HEAD_DIM), lambda b: (b, 0, 0, 0)))
        out_shapes.append(jax.ShapeDtypeStruct((DEC_BATCH, 2, hb, HEAD_DIM), F32))
    return pl.pallas_call(
        _att_sample_kernel,
        grid=(DEC_BATCH,),
        in_specs=in_specs,
        out_specs=out_specs,
        out_shape=out_shapes,
        compiler_params=pltpu.CompilerParams(
            dimension_semantics=("arbitrary",), vmem_limit_bytes=VMEM_LIMIT),
        name="att_sample",
    )(proj_s, *caches, bias_s)


def kernel(x_prompt, x_sample, c_prompt, c_sample, cache_kv0, cache_kv1, cache_kv2, state_pool,
           norm_pre, norm_post, ada_w, ada_b, t5_bias, pool_w_in, pool_w_grp, pool_scale,
           pool_w_out, att_w_in, att_w_out):
    n_att = DEPTH // 2
    xp0 = x_prompt.reshape(M_PROMPT, D_MODEL)
    xs0 = jnp.zeros((SAMPLE_ROWS, D_MODEL), F32).at[0:DEC_BATCH].set(x_sample.reshape(DEC_BATCH, D_MODEL))
    c_all = jnp.zeros((C_ROWS, D_MODEL), F32)
    c_all = c_all.at[0:BATCH].set(c_prompt).at[C_SAMPLE_ROW0:C_SAMPLE_ROW0 + DEC_BATCH].set(c_sample)

    mod = _ada_all(c_all, ada_w, ada_b)
    gains_pre = norm_pre.reshape(DEPTH, 1, D_MODEL)
    gains_post = norm_post.reshape(DEPTH, 1, D_MODEL)

    bias_mats = _bias_matrices(t5_bias)
    bias_s = bias_mats[:, 0, 0:BAND + 1].reshape(N_GROUPS, N_HEADS, BAND + 1)
    bias_s = jnp.transpose(bias_s, (0, 2, 1))[..., None]
    caches = [c.reshape(n_att, DEC_BATCH, BAND, dil, 2, N_HEADS, HEAD_DIM)
              for c, dil in zip((cache_kv0, cache_kv1, cache_kv2), DILATIONS)]
    state_t = jnp.transpose(state_pool, (0, 2, 1, 3))

    kv_s = [[] for _ in range(N_GROUPS)]
    pool_p, pool_s, att_projs = [], [], []

    (h,) = _norm_step(xp0, None, mod, gains_post, gains_pre, None, 0, x_sample=xs0)
    x = None
    for i in range(DEPTH):
        li = i // 2
        if i % 2 == 0:
            proj = _matmul(h, pool_w_in, li, 2 * POOL_WIDTH, 1024, TM_MATMUL, F32, "pool_in_proj")
            a, pp, ps = _pool_mix(proj, state_t, pool_w_grp, pool_scale, li)
            pool_p.append(pp)
            pool_s.append(jnp.transpose(ps, (1, 0, 2)))
            y = _matmul(a, pool_w_out, li, D_MODEL, 512, TM_MATMUL, BF16, "pool_out_proj")
        else:
            proj = _matmul(h, att_w_in, li, ATT_IN_COLS, 1024, TM_MATMUL, F32, "att_in_proj", slabs=True)
            att_projs.append(proj)
            proj_s = jnp.transpose(proj[:, M_PROMPT:M_PROMPT + DEC_BATCH, :], (1, 0, 2))
            souts = _att_sample(proj_s, caches, bias_s, li)
            a_s = jnp.zeros((SAMPLE_ROWS, D_MODEL), F32).at[0:DEC_BATCH].set(souts[0].reshape(DEC_BATCH, D_MODEL))
            a = _att_prompt(proj, bias_mats, a_s.astype(BF16))
            for g in range(N_GROUPS):
                kv_s[g].append(souts[1 + g].reshape(DEC_BATCH, 1, 2, N_HEADS, HEAD_DIM))
            y = _matmul(a, att_w_out, li, D_MODEL, 1024, TM_MATMUL, BF16, "att_out_proj")
        last = i + 1 == DEPTH
        if i == 0:
            x, h = _norm_step(xp0, y, mod, gains_post, gains_pre, i, i + 1, x_sample=xs0)
        elif not last:
            x, h = _norm_step(x, y, mod, gains_post, gains_pre, i, i + 1)
        else:
            y_p, y_s = _norm_step(x, y, mod, gains_post, gains_pre, i, None, split_out=True)

    kv_p = [_kv_rows(att_projs[0], att_projs[1], g) for g in range(N_GROUPS)]
    y_prompt = y_p.reshape(BATCH, SEQ, D_MODEL)
    y_sample = y_s[0:DEC_BATCH].reshape(DEC_BATCH, 1, D_MODEL)
    return (y_prompt, y_sample, kv_p[0], kv_p[1], kv_p[2], jnp.stack(pool_p),
            jnp.stack(kv_s[0]), jnp.stack(kv_s[1]), jnp.stack(kv_s[2]), jnp.stack(pool_s))
```
